```python
import math
import jax, jax.numpy as jnp
from jax import lax
import numpy as np

D_MODEL = 2048
BATCH = 8
SEQ = 8192
DEPTH = 4

N_MIXERS = 2
N_MEM = 256
MIX_WIDTH = 2 * D_MODEL
MEM_WIDTH = MIX_WIDTH // 4
TOK_WIDTH = MIX_WIDTH - MEM_WIDTH
MEM_HEADS = 4
MEM_HEAD_DIM = MEM_WIDTH // MEM_HEADS

SSD_HEAD_DIM = 64
SSD_HEADS = TOK_WIDTH // SSD_HEAD_DIM
SSD_GROUPS = 8
SSD_HEADS_PER_GROUP = SSD_HEADS // SSD_GROUPS
SSD_STATE = 128
SSD_CONV = 4
SSD_CHUNK = 128
SSD_CONV_DIM = TOK_WIDTH + 2 * SSD_GROUPS * SSD_STATE
SSD_IN_COLS = SSD_CONV_DIM + SSD_HEADS + MEM_WIDTH + MIX_WIDTH

ATTN_HEAD_DIM = 128
ATTN_HEADS_PER_GROUP = TOK_WIDTH // ATTN_HEAD_DIM
DILATED_GROUPS = ((128, 1), (512, 4), (2048, 16))
N_DIL = len(DILATED_GROUPS)
N_ALIBI_HEADS = N_DIL * ATTN_HEADS_PER_GROUP
ALIBI_MAX_EXP = 8.0
ATTN_GROUP_COLS = 3 * TOK_WIDTH
ATTN_IN_COLS = N_DIL * ATTN_GROUP_COLS + MEM_WIDTH + MIX_WIDTH
ATTN_BLOCK = 128
EPS = 1e-6

kernel_name = 'hybrid_ssd_dilated_memory_trunk'


def _rmsnorm(x, g):
    xf = x.astype(jnp.float32)
    xf = xf * lax.rsqrt(jnp.mean(xf * xf, axis=-1, keepdims=True) + EPS)
    return xf.astype(x.dtype) * g


def _grouped_rmsnorm(x, g, groups):
    shp = x.shape
    xg = x.reshape(shp[:-1] + (groups, shp[-1] // groups)).astype(jnp.float32)
    xg = xg * lax.rsqrt(jnp.mean(xg * xg, axis=-1, keepdims=True) + EPS)
    return xg.reshape(shp).astype(x.dtype) * g


def _causal_depthwise_conv(u, w, b):
    y = lax.conv_general_dilated(u, w[:, None, :], window_strides=(1,), padding=[(SSD_CONV - 1, 0)],
                                 dimension_numbers=('NWC', 'WIO', 'NWC'), feature_group_count=u.shape[-1])
    return y + b


def _memory_cross_attention(q_mem, mem_n, w_mem_kv):
    b_, t_, _ = q_mem.shape
    mk, mv = jnp.split(mem_n @ w_mem_kv, 2, axis=-1)
    q = q_mem.reshape(b_, t_, MEM_HEADS, MEM_HEAD_DIM)
    mk = mk.reshape(b_, -1, MEM_HEADS, MEM_HEAD_DIM)
    mv = mv.reshape(b_, -1, MEM_HEADS, MEM_HEAD_DIM)
    s = jnp.einsum('bthe,bmhe->bhtm', q, mk).astype(jnp.float32) * (MEM_HEAD_DIM ** -0.5)
    p = jax.nn.softmax(s, axis=-1).astype(mv.dtype)
    return jnp.einsum('bhtm,bmhe->bthe', p, mv).reshape(b_, t_, MEM_WIDTH)


def _ssd_chunked(xh, dt, a, bm, cm):
    b_, t_ = xh.shape[:2]
    nc = t_ // SSD_CHUNK
    def chunk(v):
        return v.reshape((b_, nc, SSD_CHUNK) + v.shape[2:])
    xc, dtc, bc, cc = chunk(xh), chunk(dt), chunk(bm), chunk(cm)
    a_cs = jnp.cumsum(dtc * a, axis=2)
    xdt = xc * dtc[..., None]
    pos = jnp.arange(SSD_CHUNK)
    causal = (pos[:, None] >= pos[None, :])[None, None, :, :, None, None]
    seg = a_cs[:, :, :, None] - a_cs[:, :, None, :]
    decay_ls = jnp.exp(jnp.where(causal, seg, -jnp.inf))
    cb = jnp.einsum('bclgn,bcsgn->bclsg', cc, bc)
    y_diag = jnp.einsum('bclsg,bclsgh,bcsghp->bclghp', cb, decay_ls, xdt)
    decay_to_end = jnp.exp(a_cs[:, :, -1:] - a_cs)
    chunk_states = jnp.einsum('bcsgn,bcsgh,bcsghp->bcghpn', bc, decay_to_end, xdt)
    chunk_decay = jnp.exp(a_cs[:, :, -1])

    def step(h, inp):
        st, dec = inp
        return dec[..., None, None] * h + st, h

    h0 = jnp.zeros((b_,) + chunk_states.shape[2:], chunk_states.dtype)
    _, h_prev = lax.scan(step, h0, (jnp.moveaxis(chunk_states, 1, 0), jnp.moveaxis(chunk_decay, 1, 0)))
    h_prev = jnp.moveaxis(h_prev, 0, 1)
    y_off = jnp.einsum('bclgn,bcghpn,bclgh->bclghp', cc, h_prev, jnp.exp(a_cs))
    return (y_diag + y_off).reshape(xh.shape)


def _dilated_group_attention(q, k, v, window, dilation, slopes):
    b_, t_, nh, e_ = q.shape
    n_sub = t_ // dilation
    w = window // dilation
    c = min(ATTN_BLOCK, n_sub)
    nb = -(-n_sub // c)
    lp = nb * c
    tail = lp - n_sub
    def strided(arr):
        return arr.reshape(b_, n_sub, dilation, nh, e_)
    qs = jnp.pad(strided(q), ((0, 0), (0, tail), (0, 0), (0, 0), (0, 0)))
    ks = jnp.pad(strided(k), ((0, 0), (w, tail), (0, 0), (0, 0), (0, 0)))
    vs = jnp.pad(strided(v), ((0, 0), (w, tail), (0, 0), (0, 0), (0, 0)))
    rel = jnp.arange(c)[:, None] + w - jnp.arange(c + w)[None, :]
    band = (rel >= 0) & (rel <= w)
    bias = -slopes[:, None, None] * (dilation * rel).astype(jnp.float32)[None]
    scale = e_ ** -0.5

    def block(n):
        start = n * c
        qb = lax.dynamic_slice_in_dim(qs, start, c, axis=1)
        kb = lax.dynamic_slice_in_dim(ks, start, c + w, axis=1)
        vb = lax.dynamic_slice_in_dim(vs, start, c + w, axis=1)
        key_pos = start - w + jnp.arange(c + w)
        valid = band & (key_pos >= 0)[None, :]
        s = jnp.einsum('bqrhe,bkrhe->brhqk', qb, kb).astype(jnp.float32) * scale + bias
        s = jnp.where(valid, s, -jnp.inf)
        m = jnp.max(s, axis=-1, keepdims=True)
        p = jnp.exp(s - m)
        den = jnp.transpose(jnp.sum(p, axis=-1), (0, 3, 1, 2))
        o = jnp.einsum('brhqk,bkrhe->bqrhe', p.astype(vb.dtype), vb).astype(jnp.float32)
        lse = jnp.transpose(m[..., 0], (0, 3, 1, 2)) + jnp.log(den)
        return o / den[..., None], lse

    o, lse = lax.map(block, jnp.arange(nb))
    o = jnp.moveaxis(o, 0, 1).reshape(b_, lp, dilation, nh, e_)[:, :n_sub].reshape(b_, t_, nh, e_)
    lse = jnp.moveaxis(lse, 0, 1).reshape(b_, lp, dilation, nh)[:, :n_sub].reshape(b_, t_, nh)
    return o, lse


def _ssd_layer(x, mem_n, norm_g, w_in, conv_w, conv_b, dt_bias, a_log, d_skip, ssd_norm_g, w_mem_kv, w_out):
    b_, t_, _ = x.shape
    h = _rmsnorm(x, norm_g)
    proj = h @ w_in
    xbc, dt_raw, q_mem, z = jnp.split(
        proj, [SSD_CONV_DIM, SSD_CONV_DIM + SSD_HEADS, SSD_CONV_DIM + SSD_HEADS + MEM_WIDTH], axis=-1)
    xbc = jax.nn.silu(_causal_depthwise_conv(xbc, conv_w, conv_b))
    xs, bm, cm = jnp.split(xbc, [TOK_WIDTH, TOK_WIDTH + SSD_GROUPS * SSD_STATE], axis=-1)
    xs = xs.reshape(b_, t_, SSD_GROUPS, SSD_HEADS_PER_GROUP, SSD_HEAD_DIM)
    bm = bm.reshape(b_, t_, SSD_GROUPS, SSD_STATE)
    cm = cm.reshape(b_, t_, SSD_GROUPS, SSD_STATE)
    dt = jax.nn.softplus(dt_raw.astype(jnp.float32) + dt_bias.astype(jnp.float32))
    dt = dt.reshape(b_, t_, SSD_GROUPS, SSD_HEADS_PER_GROUP)
    a = -jnp.exp(a_log.astype(jnp.float32)).reshape(SSD_GROUPS, SSD_HEADS_PER_GROUP)
    y = _ssd_chunked(xs, dt, a, bm, cm) + d_skip.reshape(SSD_GROUPS, SSD_HEADS_PER_GROUP, 1) * xs
    y_tok = y.reshape(b_, t_, TOK_WIDTH).astype(x.dtype)
    y_mem = _memory_cross_attention(q_mem, mem_n, w_mem_kv)
    gated = jnp.concatenate([y_tok, y_mem], axis=-1) * jax.nn.silu(z)
    gated = jnp.concatenate([_grouped_rmsnorm(gated[..., :TOK_WIDTH], ssd_norm_g, SSD_GROUPS),
                             gated[..., TOK_WIDTH:]], axis=-1)
    return x + gated @ w_out


def _dilated_attention_layer(x, mem_n, norm_g, w_in, w_mem_kv, w_out):
    b_, t_, _ = x.shape
    h = _rmsnorm(x, norm_g)
    slopes = jnp.exp2(-ALIBI_MAX_EXP * jnp.arange(1, N_ALIBI_HEADS + 1, dtype=jnp.float32) / N_ALIBI_HEADS)
    slopes = slopes.reshape(N_DIL, ATTN_HEADS_PER_GROUP)
    outs, lses = [], []
    for g, (window, dilation) in enumerate(DILATED_GROUPS):
        qkv = (h @ w_in[:, g * ATTN_GROUP_COLS:(g + 1) * ATTN_GROUP_COLS])
        qkv = qkv.reshape(b_, t_, 3, ATTN_HEADS_PER_GROUP, ATTN_HEAD_DIM)
        o, lse = _dilated_group_attention(qkv[:, :, 0], qkv[:, :, 1], qkv[:, :, 2], window, dilation, slopes[g])
        outs.append(o)
        lses.append(lse)
    wts = jax.nn.softmax(jnp.stack(lses), axis=0)
    y_tok = jnp.einsum('gbth,gbthe->bthe', wts, jnp.stack(outs)).reshape(b_, t_, TOK_WIDTH).astype(x.dtype)
    q_mem, z = jnp.split(h @ w_in[:, N_DIL * ATTN_GROUP_COLS:], [MEM_WIDTH], axis=-1)
    y_mem = _memory_cross_attention(q_mem, mem_n, w_mem_kv)
    gated = jnp.concatenate([y_tok, y_mem], axis=-1) * jax.nn.silu(z)
    return x + gated @ w_out


def _fwd_setup_inputs(seed: int = 0) -> dict:
    key = jax.random.key(seed)
    keys = iter(jax.random.split(key, 64))
    f32 = jnp.float32

    def nrm(shape, scale):
        return scale * jax.random.normal(next(keys), shape, f32)

    def gain(n):
        return 1.0 + nrm((n,), 0.02)

    inp = {
        'x': nrm((BATCH, SEQ, D_MODEL), 1.0),
        'mem': nrm((BATCH, N_MEM, D_MODEL), 1.0),
        'mem_norm_g': gain(D_MODEL),
        'final_norm_g': gain(D_MODEL),
    }
    for i in range(DEPTH):
        inp[f'norm_g_{i}'] = gain(D_MODEL)
        if i % N_MIXERS == 0:
            inp[f'w_in_{i}'] = nrm((D_MODEL, SSD_IN_COLS), D_MODEL ** -0.5)
            inp[f'conv_w_{i}'] = nrm((SSD_CONV, SSD_CONV_DIM), SSD_CONV ** -0.5)
            inp[f'conv_b_{i}'] = nrm((SSD_CONV_DIM,), 0.01)
            dt0 = jnp.exp(jax.random.uniform(next(keys), (SSD_HEADS,), f32, math.log(1e-3), math.log(1e-1)))
            inp[f'dt_bias_{i}'] = dt0 + jnp.log(-jnp.expm1(-dt0))
            inp[f'a_log_{i}'] = jnp.log(jax.random.uniform(next(keys), (SSD_HEADS,), f32, 1.0, 16.0))
            inp[f'd_skip_{i}'] = gain(SSD_HEADS)
            inp[f'ssd_norm_g_{i}'] = gain(TOK_WIDTH)
        else:
            inp[f'w_in_{i}'] = nrm((D_MODEL, ATTN_IN_COLS), D_MODEL ** -0.5)
        inp[f'w_mem_kv_{i}'] = nrm((D_MODEL, 2 * MEM_WIDTH), D_MODEL ** -0.5)
        inp[f'w_out_{i}'] = nrm((MIX_WIDTH, D_MODEL), MIX_WIDTH ** -0.5)
    return inp


def _fwd_reference(x, mem, mem_norm_g, final_norm_g,
              norm_g_0, w_in_0, conv_w_0, conv_b_0, dt_bias_0, a_log_0, d_skip_0, ssd_norm_g_0, w_mem_kv_0, w_out_0,
              norm_g_1, w_in_1, w_mem_kv_1, w_out_1,
              norm_g_2, w_in_2, conv_w_2, conv_b_2, dt_bias_2, a_log_2, d_skip_2, ssd_norm_g_2, w_mem_kv_2, w_out_2,
              norm_g_3, w_in_3, w_mem_kv_3, w_out_3):
    mem_n = _rmsnorm(mem, mem_norm_g)
    params = [
        (norm_g_0, w_in_0, conv_w_0, conv_b_0, dt_bias_0, a_log_0, d_skip_0, ssd_norm_g_0, w_mem_kv_0, w_out_0),
        (norm_g_1, w_in_1, w_mem_kv_1, w_out_1),
        (norm_g_2, w_in_2, conv_w_2, conv_b_2, dt_bias_2, a_log_2, d_skip_2, ssd_norm_g_2, w_mem_kv_2, w_out_2),
        (norm_g_3, w_in_3, w_mem_kv_3, w_out_3),
    ]
    for i in range(DEPTH):
        layer_fn = _ssd_layer if i % N_MIXERS == 0 else _dilated_attention_layer
        x = layer_fn(x, mem_n, *params[i])
    return _rmsnorm(x, final_norm_g)


import jax as _jax
import jax.numpy as _jnp

TWIN_FORMAT = 'train_step'
FWD_PARAMS = ['x', 'mem', 'mem_norm_g', 'final_norm_g', 'norm_g_0', 'w_in_0', 'conv_w_0', 'conv_b_0', 'dt_bias_0', 'a_log_0', 'd_skip_0', 'ssd_norm_g_0', 'w_mem_kv_0', 'w_out_0', 'norm_g_1', 'w_in_1', 'w_mem_kv_1', 'w_out_1', 'norm_g_2', 'w_in_2', 'conv_w_2', 'conv_b_2', 'dt_bias_2', 'a_log_2', 'd_skip_2', 'ssd_norm_g_2', 'w_mem_kv_2', 'w_out_2', 'norm_g_3', 'w_in_3', 'w_mem_kv_3', 'w_out_3']
TWIN_WEIGHTS = ['mem_norm_g', 'final_norm_g', 'norm_g_0', 'w_in_0', 'conv_w_0', 'conv_b_0', 'dt_bias_0', 'a_log_0', 'd_skip_0', 'ssd_norm_g_0', 'w_mem_kv_0', 'w_out_0', 'norm_g_1', 'w_in_1', 'w_mem_kv_1', 'w_out_1', 'norm_g_2', 'w_in_2', 'conv_w_2', 'conv_b_2', 'dt_bias_2', 'a_log_2', 'd_skip_2', 'ssd_norm_g_2', 'w_mem_kv_2', 'w_out_2', 'norm_g_3', 'w_in_3', 'w_mem_kv_3', 'w_out_3']
TWIN_DIFF_INPUT = 'x'
TWIN_INPUTS = ['x', 'mem', 'mem_norm_g', 'final_norm_g', 'norm_g_0', 'w_in_0', 'conv_w_0', 'conv_b_0', 'dt_bias_0', 'a_log_0', 'd_skip_0', 'ssd_norm_g_0', 'w_mem_kv_0', 'w_out_0', 'norm_g_1', 'w_in_1', 'w_mem_kv_1', 'w_out_1', 'norm_g_2', 'w_in_2', 'conv_w_2', 'conv_b_2', 'dt_bias_2', 'a_log_2', 'd_skip_2', 'ssd_norm_g_2', 'w_mem_kv_2', 'w_out_2', 'norm_g_3', 'w_in_3', 'w_mem_kv_3', 'w_out_3', 'loss_target', 'm_mem_norm_g', 'm_final_norm_g', 'm_norm_g_0', 'm_w_in_0', 'm_conv_w_0', 'm_conv_b_0', 'm_dt_bias_0', 'm_a_log_0', 'm_d_skip_0', 'm_ssd_norm_g_0', 'm_w_mem_kv_0', 'm_w_out_0', 'm_norm_g_1', 'm_w_in_1', 'm_w_mem_kv_1', 'm_w_out_1', 'm_norm_g_2', 'm_w_in_2', 'm_conv_w_2', 'm_conv_b_2', 'm_dt_bias_2', 'm_a_log_2', 'm_d_skip_2', 'm_ssd_norm_g_2', 'm_w_mem_kv_2', 'm_w_out_2', 'm_norm_g_3', 'm_w_in_3', 'm_w_mem_kv_3', 'm_w_out_3', 'v_mem_norm_g', 'v_final_norm_g', 'v_norm_g_0', 'v_w_in_0', 'v_conv_w_0', 'v_conv_b_0', 'v_dt_bias_0', 'v_a_log_0', 'v_d_skip_0', 'v_ssd_norm_g_0', 'v_w_mem_kv_0', 'v_w_out_0', 'v_norm_g_1', 'v_w_in_1', 'v_w_mem_kv_1', 'v_w_out_1', 'v_norm_g_2', 'v_w_in_2', 'v_conv_w_2', 'v_conv_b_2', 'v_dt_bias_2', 'v_a_log_2', 'v_d_skip_2', 'v_ssd_norm_g_2', 'v_w_mem_kv_2', 'v_w_out_2', 'v_norm_g_3', 'v_w_in_3', 'v_w_mem_kv_3', 'v_w_out_3']
TWIN_OUTPUTS = ['loss', 'grad_x', 'grad_mem_norm_g', 'grad_final_norm_g', 'grad_norm_g_0', 'grad_w_in_0', 'grad_conv_w_0', 'grad_conv_b_0', 'grad_dt_bias_0', 'grad_a_log_0', 'grad_d_skip_0', 'grad_ssd_norm_g_0', 'grad_w_mem_kv_0', 'grad_w_out_0', 'grad_norm_g_1', 'grad_w_in_1', 'grad_w_mem_kv_1', 'grad_w_out_1', 'grad_norm_g_2', 'grad_w_in_2', 'grad_conv_w_2', 'grad_conv_b_2', 'grad_dt_bias_2', 'grad_a_log_2', 'grad_d_skip_2', 'grad_ssd_norm_g_2', 'grad_w_mem_kv_2', 'grad_w_out_2', 'grad_norm_g_3', 'grad_w_in_3', 'grad_w_mem_kv_3', 'grad_w_out_3', 'delta_mem_norm_g', 'delta_final_norm_g', 'delta_norm_g_0', 'delta_w_in_0', 'delta_conv_w_0', 'delta_conv_b_0', 'delta_dt_bias_0', 'delta_a_log_0', 'delta_d_skip_0', 'delta_ssd_norm_g_0', 'delta_w_mem_kv_0', 'delta_w_out_0', 'delta_norm_g_1', 'delta_w_in_1', 'delta_w_mem_kv_1', 'delta_w_out_1', 'delta_norm_g_2', 'delta_w_in_2', 'delta_conv_w_2', 'delta_conv_b_2', 'delta_dt_bias_2', 'delta_a_log_2', 'delta_d_skip_2', 'delta_ssd_norm_g_2', 'delta_w_mem_kv_2', 'delta_w_out_2', 'delta_norm_g_3', 'delta_w_in_3', 'delta_w_mem_kv_3', 'delta_w_out_3', 'new_m_mem_norm_g', 'new_m_final_norm_g', 'new_m_norm_g_0', 'new_m_w_in_0', 'new_m_conv_w_0', 'new_m_conv_b_0', 'new_m_dt_bias_0', 'new_m_a_log_0', 'new_m_d_skip_0', 'new_m_ssd_norm_g_0', 'new_m_w_mem_kv_0', 'new_m_w_out_0', 'new_m_norm_g_1', 'new_m_w_in_1', 'new_m_w_mem_kv_1', 'new_m_w_out_1', 'new_m_norm_g_2', 'new_m_w_in_2', 'new_m_conv_w_2', 'new_m_conv_b_2', 'new_m_dt_bias_2', 'new_m_a_log_2', 'new_m_d_skip_2', 'new_m_ssd_norm_g_2', 'new_m_w_mem_kv_2', 'new_m_w_out_2', 'new_m_norm_g_3', 'new_m_w_in_3', 'new_m_w_mem_kv_3', 'new_m_w_out_3', 'new_v_mem_norm_g', 'new_v_final_norm_g', 'new_v_norm_g_0', 'new_v_w_in_0', 'new_v_conv_w_0', 'new_v_conv_b_0', 'new_v_dt_bias_0', 'new_v_a_log_0', 'new_v_d_skip_0', 'new_v_ssd_norm_g_0', 'new_v_w_mem_kv_0', 'new_v_w_out_0', 'new_v_norm_g_1', 'new_v_w_in_1', 'new_v_w_mem_kv_1', 'new_v_w_out_1', 'new_v_norm_g_2', 'new_v_w_in_2', 'new_v_conv_w_2', 'new_v_conv_b_2', 'new_v_dt_bias_2', 'new_v_a_log_2', 'new_v_d_skip_2', 'new_v_ssd_norm_g_2', 'new_v_w_mem_kv_2', 'new_v_w_out_2', 'new_v_norm_g_3', 'new_v_w_in_3', 'new_v_w_mem_kv_3', 'new_v_w_out_3']
TWIN_LEAF_KINDS = {'loss': 'loss', 'grad_x': 'grad_x', 'grad_mem_norm_g': 'grad_w', 'grad_final_norm_g': 'grad_w', 'grad_norm_g_0': 'grad_w', 'grad_w_in_0': 'grad_w', 'grad_conv_w_0': 'grad_w', 'grad_conv_b_0': 'grad_w', 'grad_dt_bias_0': 'grad_w', 'grad_a_log_0': 'grad_w', 'grad_d_skip_0': 'grad_w', 'grad_ssd_norm_g_0': 'grad_w', 'grad_w_mem_kv_0': 'grad_w', 'grad_w_out_0': 'grad_w', 'grad_norm_g_1': 'grad_w', 'grad_w_in_1': 'grad_w', 'grad_w_mem_kv_1': 'grad_w', 'grad_w_out_1': 'grad_w', 'grad_norm_g_2': 'grad_w', 'grad_w_in_2': 'grad_w', 'grad_conv_w_2': 'grad_w', 'grad_conv_b_2': 'grad_w', 'grad_dt_bias_2': 'grad_w', 'grad_a_log_2': 'grad_w', 'grad_d_skip_2': 'grad_w', 'grad_ssd_norm_g_2': 'grad_w', 'grad_w_mem_kv_2': 'grad_w', 'grad_w_out_2': 'grad_w', 'grad_norm_g_3': 'grad_w', 'grad_w_in_3': 'grad_w', 'grad_w_mem_kv_3': 'grad_w', 'grad_w_out_3': 'grad_w', 'delta_mem_norm_g': 'delta_w', 'delta_final_norm_g': 'delta_w', 'delta_norm_g_0': 'delta_w', 'delta_w_in_0': 'delta_w', 'delta_conv_w_0': 'delta_w', 'delta_conv_b_0': 'delta_w', 'delta_dt_bias_0': 'delta_w', 'delta_a_log_0': 'delta_w', 'delta_d_skip_0': 'delta_w', 'delta_ssd_norm_g_0': 'delta_w', 'delta_w_mem_kv_0': 'delta_w', 'delta_w_out_0': 'delta_w', 'delta_norm_g_1': 'delta_w', 'delta_w_in_1': 'delta_w', 'delta_w_mem_kv_1': 'delta_w', 'delta_w_out_1': 'delta_w', 'delta_norm_g_2': 'delta_w', 'delta_w_in_2': 'delta_w', 'delta_conv_w_2': 'delta_w', 'delta_conv_b_2': 'delta_w', 'delta_dt_bias_2': 'delta_w', 'delta_a_log_2': 'delta_w', 'delta_d_skip_2': 'delta_w', 'delta_ssd_norm_g_2': 'delta_w', 'delta_w_mem_kv_2': 'delta_w', 'delta_w_out_2': 'delta_w', 'delta_norm_g_3': 'delta_w', 'delta_w_in_3': 'delta_w', 'delta_w_mem_kv_3': 'delta_w', 'delta_w_out_3': 'delta_w', 'new_m_mem_norm_g': 'new_m', 'new_m_final_norm_g': 'new_m', 'new_m_norm_g_0': 'new_m', 'new_m_w_in_0': 'new_m', 'new_m_conv_w_0': 'new_m', 'new_m_conv_b_0': 'new_m', 'new_m_dt_bias_0': 'new_m', 'new_m_a_log_0': 'new_m', 'new_m_d_skip_0': 'new_m', 'new_m_ssd_norm_g_0': 'new_m', 'new_m_w_mem_kv_0': 'new_m', 'new_m_w_out_0': 'new_m', 'new_m_norm_g_1': 'new_m', 'new_m_w_in_1': 'new_m', 'new_m_w_mem_kv_1': 'new_m', 'new_m_w_out_1': 'new_m', 'new_m_norm_g_2': 'new_m', 'new_m_w_in_2': 'new_m', 'new_m_conv_w_2': 'new_m', 'new_m_conv_b_2': 'new_m', 'new_m_dt_bias_2': 'new_m', 'new_m_a_log_2': 'new_m', 'new_m_d_skip_2': 'new_m', 'new_m_ssd_norm_g_2': 'new_m', 'new_m_w_mem_kv_2': 'new_m', 'new_m_w_out_2': 'new_m', 'new_m_norm_g_3': 'new_m', 'new_m_w_in_3': 'new_m', 'new_m_w_mem_kv_3': 'new_m', 'new_m_w_out_3': 'new_m', 'new_v_mem_norm_g': 'new_v', 'new_v_final_norm_g': 'new_v', 'new_v_norm_g_0': 'new_v', 'new_v_w_in_0': 'new_v', 'new_v_conv_w_0': 'new_v', 'new_v_conv_b_0': 'new_v', 'new_v_dt_bias_0': 'new_v', 'new_v_a_log_0': 'new_v', 'new_v_d_skip_0': 'new_v', 'new_v_ssd_norm_g_0': 'new_v', 'new_v_w_mem_kv_0': 'new_v', 'new_v_w_out_0': 'new_v', 'new_v_norm_g_1': 'new_v', 'new_v_w_in_1': 'new_v', 'new_v_w_mem_kv_1': 'new_v', 'new_v_w_out_1': 'new_v', 'new_v_norm_g_2': 'new_v', 'new_v_w_in_2': 'new_v', 'new_v_conv_w_2': 'new_v', 'new_v_conv_b_2': 'new_v', 'new_v_dt_bias_2': 'new_v', 'new_v_a_log_2': 'new_v', 'new_v_d_skip_2': 'new_v', 'new_v_ssd_norm_g_2': 'new_v', 'new_v_w_mem_kv_2': 'new_v', 'new_v_w_out_2': 'new_v', 'new_v_norm_g_3': 'new_v', 'new_v_w_in_3': 'new_v', 'new_v_w_mem_kv_3': 'new_v', 'new_v_w_out_3': 'new_v'}


def _forward(args):
    return _fwd_reference(*[args[k] for k in FWD_PARAMS])


def _output_shape():
    def fwd():
        inp = _fwd_setup_inputs(0)
        return _fwd_reference(*[inp[k] for k in FWD_PARAMS])
    out = _jax.eval_shape(fwd)
    return out.shape, out.dtype

N_MICROBATCH = 1
ADAM_LR = 0.001
ADAM_B1 = 0.9
ADAM_B2 = 0.999
ADAM_EPS = 1e-08
ADAM_WD = 0.01
ADAM_STEP = 10
PER_EXAMPLE_BATCH_AXIS = {'x': 0, 'mem': 0, 'loss_target': 0}
SHARED_INPUTS = []
_WEIGHT_DTYPES = {'mem_norm_g': _jnp.float32, 'final_norm_g': _jnp.float32, 'norm_g_0': _jnp.float32, 'w_in_0': _jnp.float32, 'conv_w_0': _jnp.float32, 'conv_b_0': _jnp.float32, 'dt_bias_0': _jnp.float32, 'a_log_0': _jnp.float32, 'd_skip_0': _jnp.float32, 'ssd_norm_g_0': _jnp.float32, 'w_mem_kv_0': _jnp.float32, 'w_out_0': _jnp.float32, 'norm_g_1': _jnp.float32, 'w_in_1': _jnp.float32, 'w_mem_kv_1': _jnp.float32, 'w_out_1': _jnp.float32, 'norm_g_2': _jnp.float32, 'w_in_2': _jnp.float32, 'conv_w_2': _jnp.float32, 'conv_b_2': _jnp.float32, 'dt_bias_2': _jnp.float32, 'a_log_2': _jnp.float32, 'd_skip_2': _jnp.float32, 'ssd_norm_g_2': _jnp.float32, 'w_mem_kv_2': _jnp.float32, 'w_out_2': _jnp.float32, 'norm_g_3': _jnp.float32, 'w_in_3': _jnp.float32, 'w_mem_kv_3': _jnp.float32, 'w_out_3': _jnp.float32}
MOMENT_SCALE = {'mem_norm_g': 9.042017e-03, 'final_norm_g': 3.200243e+01, 'norm_g_0': 1.482465e-01, 'w_in_0': 6.577502e-02, 'conv_w_0': 6.601237e-02, 'conv_b_0': 1.043172e-01, 'dt_bias_0': 1.600823e-01, 'a_log_0': 2.837829e-01, 'd_skip_0': 4.250704e-01, 'ssd_norm_g_0': 8.204180e-02, 'w_mem_kv_0': 4.997708e-03, 'w_out_0': 9.933146e-02, 'norm_g_1': 3.880119e-02, 'w_in_1': 9.273853e-03, 'w_mem_kv_1': 4.859464e-03, 'w_out_1': 2.045946e-02, 'norm_g_2': 1.043516e-01, 'w_in_2': 4.596879e-02, 'conv_w_2': 4.626510e-02, 'conv_b_2': 6.644489e-02, 'dt_bias_2': 8.976959e-02, 'a_log_2': 3.464738e-01, 'd_skip_2': 3.364002e-01, 'ssd_norm_g_2': 5.861533e-02, 'w_mem_kv_2': 3.495709e-03, 'w_out_2': 7.030890e-02, 'norm_g_3': 2.628382e-02, 'w_in_3': 6.557055e-03, 'w_mem_kv_3': 3.419531e-03, 'w_out_3': 1.450196e-02}


def _to_microbatches(a, axis):
    t = _jnp.moveaxis(a, axis, 0)
    t = t.reshape((N_MICROBATCH, t.shape[0] // N_MICROBATCH) + t.shape[1:])
    return _jnp.moveaxis(t, 1, axis + 1)


def setup_inputs(seed: int = 0) -> dict:
    inp = _fwd_setup_inputs(seed)
    key = _jax.random.fold_in(_jax.random.key(seed), 7919)
    shape, _ = _output_shape()
    out = dict(inp)
    out["loss_target"] = _jax.random.normal(_jax.random.fold_in(key, 0), shape, _jnp.float32)
    for i, name in enumerate(TWIN_WEIGHTS):
        w = inp[name].astype(_jnp.float32)
        if MOMENT_SCALE is None:
            s = _jnp.sqrt(_jnp.mean(_jnp.square(w)) + 1e-30)
        else:
            s = MOMENT_SCALE[name]
        km, kv = _jax.random.split(_jax.random.fold_in(key, i + 1))
        out[name] = w
        out["m_" + name] = s * _jax.random.normal(km, w.shape, _jnp.float32)
        out["v_" + name] = (s * s) * _jax.random.uniform(kv, w.shape, _jnp.float32, 0.5, 1.5)
    if N_MICROBATCH > 1:
        for name, axis in PER_EXAMPLE_BATCH_AXIS.items():
            out[name] = _to_microbatches(out[name], axis)
    return {'x': out['x'], 'mem': out['mem'], 'mem_norm_g': out['mem_norm_g'], 'final_norm_g': out['final_norm_g'], 'norm_g_0': out['norm_g_0'], 'w_in_0': out['w_in_0'], 'conv_w_0': out['conv_w_0'], 'conv_b_0': out['conv_b_0'], 'dt_bias_0': out['dt_bias_0'], 'a_log_0': out['a_log_0'], 'd_skip_0': out['d_skip_0'], 'ssd_norm_g_0': out['ssd_norm_g_0'], 'w_mem_kv_0': out['w_mem_kv_0'], 'w_out_0': out['w_out_0'], 'norm_g_1': out['norm_g_1'], 'w_in_1': out['w_in_1'], 'w_mem_kv_1': out['w_mem_kv_1'], 'w_out_1': out['w_out_1'], 'norm_g_2': out['norm_g_2'], 'w_in_2': out['w_in_2'], 'conv_w_2': out['conv_w_2'], 'conv_b_2': out['conv_b_2'], 'dt_bias_2': out['dt_bias_2'], 'a_log_2': out['a_log_2'], 'd_skip_2': out['d_skip_2'], 'ssd_norm_g_2': out['ssd_norm_g_2'], 'w_mem_kv_2': out['w_mem_kv_2'], 'w_out_2': out['w_out_2'], 'norm_g_3': out['norm_g_3'], 'w_in_3': out['w_in_3'], 'w_mem_kv_3': out['w_mem_kv_3'], 'w_out_3': out['w_out_3'], 'loss_target': out['loss_target'], 'm_mem_norm_g': out['m_mem_norm_g'], 'm_final_norm_g': out['m_final_norm_g'], 'm_norm_g_0': out['m_norm_g_0'], 'm_w_in_0': out['m_w_in_0'], 'm_conv_w_0': out['m_conv_w_0'], 'm_conv_b_0': out['m_conv_b_0'], 'm_dt_bias_0': out['m_dt_bias_0'], 'm_a_log_0': out['m_a_log_0'], 'm_d_skip_0': out['m_d_skip_0'], 'm_ssd_norm_g_0': out['m_ssd_norm_g_0'], 'm_w_mem_kv_0': out['m_w_mem_kv_0'], 'm_w_out_0': out['m_w_out_0'], 'm_norm_g_1': out['m_norm_g_1'], 'm_w_in_1': out['m_w_in_1'], 'm_w_mem_kv_1': out['m_w_mem_kv_1'], 'm_w_out_1': out['m_w_out_1'], 'm_norm_g_2': out['m_norm_g_2'], 'm_w_in_2': out['m_w_in_2'], 'm_conv_w_2': out['m_conv_w_2'], 'm_conv_b_2': out['m_conv_b_2'], 'm_dt_bias_2': out['m_dt_bias_2'], 'm_a_log_2': out['m_a_log_2'], 'm_d_skip_2': out['m_d_skip_2'], 'm_ssd_norm_g_2': out['m_ssd_norm_g_2'], 'm_w_mem_kv_2': out['m_w_mem_kv_2'], 'm_w_out_2': out['m_w_out_2'], 'm_norm_g_3': out['m_norm_g_3'], 'm_w_in_3': out['m_w_in_3'], 'm_w_mem_kv_3': out['m_w_mem_kv_3'], 'm_w_out_3': out['m_w_out_3'], 'v_mem_norm_g': out['v_mem_norm_g'], 'v_final_norm_g': out['v_final_norm_g'], 'v_norm_g_0': out['v_norm_g_0'], 'v_w_in_0': out['v_w_in_0'], 'v_conv_w_0': out['v_conv_w_0'], 'v_conv_b_0': out['v_conv_b_0'], 'v_dt_bias_0': out['v_dt_bias_0'], 'v_a_log_0': out['v_a_log_0'], 'v_d_skip_0': out['v_d_skip_0'], 'v_ssd_norm_g_0': out['v_ssd_norm_g_0'], 'v_w_mem_kv_0': out['v_w_mem_kv_0'], 'v_w_out_0': out['v_w_out_0'], 'v_norm_g_1': out['v_norm_g_1'], 'v_w_in_1': out['v_w_in_1'], 'v_w_mem_kv_1': out['v_w_mem_kv_1'], 'v_w_out_1': out['v_w_out_1'], 'v_norm_g_2': out['v_norm_g_2'], 'v_w_in_2': out['v_w_in_2'], 'v_conv_w_2': out['v_conv_w_2'], 'v_conv_b_2': out['v_conv_b_2'], 'v_dt_bias_2': out['v_dt_bias_2'], 'v_a_log_2': out['v_a_log_2'], 'v_d_skip_2': out['v_d_skip_2'], 'v_ssd_norm_g_2': out['v_ssd_norm_g_2'], 'v_w_mem_kv_2': out['v_w_mem_kv_2'], 'v_w_out_2': out['v_w_out_2'], 'v_norm_g_3': out['v_norm_g_3'], 'v_w_in_3': out['v_w_in_3'], 'v_w_mem_kv_3': out['v_w_mem_kv_3'], 'v_w_out_3': out['v_w_out_3']}


def _loss(weights, diff, rest, loss_target):
    with _jax.named_scope("forward"):
        args = {**rest, TWIN_DIFF_INPUT: diff, **{k: w.astype(_WEIGHT_DTYPES[k]) for k, w in weights.items()}}
        y = _forward(args)
    with _jax.named_scope("loss_head"):
        err = _jnp.square(y.astype(_jnp.float32) - loss_target)
        return 0.5 * _jnp.sum(_jnp.mean(err, axis=-1)) if err.ndim else 0.5 * err


def _adamw(w, g, m, v):
    m = ADAM_B1 * m + (1.0 - ADAM_B1) * g
    v = ADAM_B2 * v + (1.0 - ADAM_B2) * _jnp.square(g)
    m_hat = m / (1.0 - ADAM_B1 ** ADAM_STEP)
    v_hat = v / (1.0 - ADAM_B2 ** ADAM_STEP)
    delta = -ADAM_LR * (m_hat / (_jnp.sqrt(v_hat) + ADAM_EPS) + ADAM_WD * w)
    return delta, m, v


def reference(x, mem, mem_norm_g, final_norm_g, norm_g_0, w_in_0, conv_w_0, conv_b_0, dt_bias_0, a_log_0, d_skip_0, ssd_norm_g_0, w_mem_kv_0, w_out_0, norm_g_1, w_in_1, w_mem_kv_1, w_out_1, norm_g_2, w_in_2, conv_w_2, conv_b_2, dt_bias_2, a_log_2, d_skip_2, ssd_norm_g_2, w_mem_kv_2, w_out_2, norm_g_3, w_in_3, w_mem_kv_3, w_out_3, loss_target, m_mem_norm_g, m_final_norm_g, m_norm_g_0, m_w_in_0, m_conv_w_0, m_conv_b_0, m_dt_bias_0, m_a_log_0, m_d_skip_0, m_ssd_norm_g_0, m_w_mem_kv_0, m_w_out_0, m_norm_g_1, m_w_in_1, m_w_mem_kv_1, m_w_out_1, m_norm_g_2, m_w_in_2, m_conv_w_2, m_conv_b_2, m_dt_bias_2, m_a_log_2, m_d_skip_2, m_ssd_norm_g_2, m_w_mem_kv_2, m_w_out_2, m_norm_g_3, m_w_in_3, m_w_mem_kv_3, m_w_out_3, v_mem_norm_g, v_final_norm_g, v_norm_g_0, v_w_in_0, v_conv_w_0, v_conv_b_0, v_dt_bias_0, v_a_log_0, v_d_skip_0, v_ssd_norm_g_0, v_w_mem_kv_0, v_w_out_0, v_norm_g_1, v_w_in_1, v_w_mem_kv_1, v_w_out_1, v_norm_g_2, v_w_in_2, v_conv_w_2, v_conv_b_2, v_dt_bias_2, v_a_log_2, v_d_skip_2, v_ssd_norm_g_2, v_w_mem_kv_2, v_w_out_2, v_norm_g_3, v_w_in_3, v_w_mem_kv_3, v_w_out_3):
    given = dict(x=x, mem=mem, mem_norm_g=mem_norm_g, final_norm_g=final_norm_g, norm_g_0=norm_g_0, w_in_0=w_in_0, conv_w_0=conv_w_0, conv_b_0=conv_b_0, dt_bias_0=dt_bias_0, a_log_0=a_log_0, d_skip_0=d_skip_0, ssd_norm_g_0=ssd_norm_g_0, w_mem_kv_0=w_mem_kv_0, w_out_0=w_out_0, norm_g_1=norm_g_1, w_in_1=w_in_1, w_mem_kv_1=w_mem_kv_1, w_out_1=w_out_1, norm_g_2=norm_g_2, w_in_2=w_in_2, conv_w_2=conv_w_2, conv_b_2=conv_b_2, dt_bias_2=dt_bias_2, a_log_2=a_log_2, d_skip_2=d_skip_2, ssd_norm_g_2=ssd_norm_g_2, w_mem_kv_2=w_mem_kv_2, w_out_2=w_out_2, norm_g_3=norm_g_3, w_in_3=w_in_3, w_mem_kv_3=w_mem_kv_3, w_out_3=w_out_3, loss_target=loss_target, m_mem_norm_g=m_mem_norm_g, m_final_norm_g=m_final_norm_g, m_norm_g_0=m_norm_g_0, m_w_in_0=m_w_in_0, m_conv_w_0=m_conv_w_0, m_conv_b_0=m_conv_b_0, m_dt_bias_0=m_dt_bias_0, m_a_log_0=m_a_log_0, m_d_skip_0=m_d_skip_0, m_ssd_norm_g_0=m_ssd_norm_g_0, m_w_mem_kv_0=m_w_mem_kv_0, m_w_out_0=m_w_out_0, m_norm_g_1=m_norm_g_1, m_w_in_1=m_w_in_1, m_w_mem_kv_1=m_w_mem_kv_1, m_w_out_1=m_w_out_1, m_norm_g_2=m_norm_g_2, m_w_in_2=m_w_in_2, m_conv_w_2=m_conv_w_2, m_conv_b_2=m_conv_b_2, m_dt_bias_2=m_dt_bias_2, m_a_log_2=m_a_log_2, m_d_skip_2=m_d_skip_2, m_ssd_norm_g_2=m_ssd_norm_g_2, m_w_mem_kv_2=m_w_mem_kv_2, m_w_out_2=m_w_out_2, m_norm_g_3=m_norm_g_3, m_w_in_3=m_w_in_3, m_w_mem_kv_3=m_w_mem_kv_3, m_w_out_3=m_w_out_3, v_mem_norm_g=v_mem_norm_g, v_final_norm_g=v_final_norm_g, v_norm_g_0=v_norm_g_0, v_w_in_0=v_w_in_0, v_conv_w_0=v_conv_w_0, v_conv_b_0=v_conv_b_0, v_dt_bias_0=v_dt_bias_0, v_a_log_0=v_a_log_0, v_d_skip_0=v_d_skip_0, v_ssd_norm_g_0=v_ssd_norm_g_0, v_w_mem_kv_0=v_w_mem_kv_0, v_w_out_0=v_w_out_0, v_norm_g_1=v_norm_g_1, v_w_in_1=v_w_in_1, v_w_mem_kv_1=v_w_mem_kv_1, v_w_out_1=v_w_out_1, v_norm_g_2=v_norm_g_2, v_w_in_2=v_w_in_2, v_conv_w_2=v_conv_w_2, v_conv_b_2=v_conv_b_2, v_dt_bias_2=v_dt_bias_2, v_a_log_2=v_a_log_2, v_d_skip_2=v_d_skip_2, v_ssd_norm_g_2=v_ssd_norm_g_2, v_w_mem_kv_2=v_w_mem_kv_2, v_w_out_2=v_w_out_2, v_norm_g_3=v_norm_g_3, v_w_in_3=v_w_in_3, v_w_mem_kv_3=v_w_mem_kv_3, v_w_out_3=v_w_out_3)
    weights = {n: given[n] for n in TWIN_WEIGHTS}
    shared = {n: given[n] for n in SHARED_INPUTS}
    per_example = {n: given[n] for n in ['x', 'mem']}
    grad_fn = _jax.value_and_grad(_loss, argnums=(0, 1))

    def one_microbatch(ex, loss_target):
        ex = dict(ex)
        diff = ex.pop(TWIN_DIFF_INPUT)
        return grad_fn(weights, diff, {**shared, **ex}, loss_target)

    if N_MICROBATCH == 1:
        loss, (grad_w, grad_x) = one_microbatch(per_example, given["loss_target"])
    else:
        def body(carry, xs):
            loss_sum, grad_sum = carry
            l_k, (gw_k, gx_k) = one_microbatch(xs[0], xs[1])
            with _jax.named_scope("update"):
                return (loss_sum + l_k, _jax.tree.map(_jnp.add, grad_sum, gw_k)), gx_k

        init = (_jnp.zeros((), _jnp.float32), _jax.tree.map(_jnp.zeros_like, weights))
        (loss, grad_w), grad_x = _jax.lax.scan(body, init, (per_example, given["loss_target"]))
    with _jax.named_scope("update"):
        delta_w, new_m, new_v = {}, {}, {}
        for n in TWIN_WEIGHTS:
            delta_w[n], new_m[n], new_v[n] = _adamw(weights[n], grad_w[n], given["m_" + n], given["v_" + n])
    return (loss, grad_x, *[grad_w[n] for n in TWIN_WEIGHTS], *[delta_w[n] for n in TWIN_WEIGHTS],
            *[new_m[n] for n in TWIN_WEIGHTS], *[new_v[n] for n in TWIN_WEIGHTS])
```

```python
import functools
import math

import jax
import jax.numpy as jnp
from jax import lax
from jax.experimental import pallas as pl
from jax.experimental.pallas import tpu as pltpu

F32 = jnp.float32
BF16 = jnp.bfloat16

EPS = 1e-6
MEM_HEADS = 4
SSD_HEAD_DIM = 64
SSD_GROUPS = 8
SSD_STATE = 128
SSD_CONV = 4
SSD_CHUNK = 128
ATTN_HEAD_DIM = 128
ATTN_BLOCK = 128
DILATED_GROUPS = ((128, 1), (512, 4), (2048, 16))
ALIBI_MAX_EXP = 8.0
DEPTH = 4

ADAM_LR = 0.001
ADAM_B1 = 0.9
ADAM_B2 = 0.999
ADAM_EPS = 1e-08
ADAM_WD = 0.01
ADAM_STEP = 10

LANES = 128
VMEM_LIMIT_BYTES = 48 * 1024 * 1024
NEG = -1e30
MESH = pl.DeviceIdType.MESH


def _cparams(*sem):
    return pltpu.CompilerParams(dimension_semantics=tuple(sem), vmem_limit_bytes=VMEM_LIMIT_BYTES)


def _tile(dim, pref, unit=LANES):
    if dim <= pref:
        return dim
    t = (pref // unit) * unit
    while t >= unit:
        if dim % t == 0:
            return t
        t -= unit
    return dim


def _ew_tiles(rows, cols):
    tc = cols if (cols % LANES != 0 or cols <= 2048) else _tile(cols, 2048)
    tr = rows
    while tr * tc > 256 * 1024 and tr % 2 == 0 and (tr // 2) % 8 == 0:
        tr //= 2
    return tr, tc


def _sigmoid(v):
    return 1.0 / (1.0 + jnp.exp(-v))


def _dot(a, b):
    return jnp.dot(a, b, preferred_element_type=F32)


def _dot_nt(a, b):
    return lax.dot_general(a, b, (((1,), (1,)), ((), ())), preferred_element_type=F32)


def _dot_tn(a, b):
    return lax.dot_general(a, b, (((0,), (0,)), ((), ())), preferred_element_type=F32)


def _split3(v):
    hi = v.astype(BF16)
    r = v - hi.astype(F32)
    mid = r.astype(BF16)
    lo = (r - mid.astype(F32)).astype(BF16)
    return hi, mid, lo


def _xdot(v, onehot):
    hi, mid, lo = _split3(v)
    return _dot(hi, onehot) + _dot(mid, onehot) + _dot(lo, onehot)


def _xdot_l(onehot, v):
    hi, mid, lo = _split3(v)
    return _dot(onehot, hi) + _dot(onehot, mid) + _dot(onehot, lo)


def _mm(a, b, mode, out_dtype, name, add=None, tm=1024, tn=1024, tk=512):
    if mode == "nn":
        M, K = a.shape
        N = b.shape[1]
    elif mode == "nt":
        M, K = a.shape
        N = b.shape[0]
    else:
        K, M = a.shape
        N = b.shape[1]
    tm, tn, tk = _tile(M, tm, 8 if M < LANES else LANES), _tile(N, tn), _tile(K, tk)
    nk = K // tk
    has_add = add is not None

    def body(*refs):
        if has_add:
            a_ref, b_ref, add_ref, o_ref, acc = refs
        else:
            a_ref, b_ref, o_ref, acc = refs
        k = pl.program_id(2)

        @pl.when(k == 0)
        def _():
            acc[...] = jnp.zeros_like(acc)

        av = a_ref[...].astype(BF16)
        bv = b_ref[...].astype(BF16)
        if mode == "nn":
            acc[...] += _dot(av, bv)
        elif mode == "nt":
            acc[...] += _dot_nt(av, bv)
        else:
            acc[...] += _dot_tn(av, bv)

        @pl.when(k == nk - 1)
        def _():
            r = acc[...]
            if has_add:
                r = r + add_ref[...]
            o_ref[...] = r.astype(out_dtype)

    if mode == "nn":
        a_spec = pl.BlockSpec((tm, tk), lambda i, j, k: (i, k))
        b_spec = pl.BlockSpec((tk, tn), lambda i, j, k: (k, j))
    elif mode == "nt":
        a_spec = pl.BlockSpec((tm, tk), lambda i, j, k: (i, k))
        b_spec = pl.BlockSpec((tn, tk), lambda i, j, k: (j, k))
    else:
        a_spec = pl.BlockSpec((tk, tm), lambda i, j, k: (k, i))
        b_spec = pl.BlockSpec((tk, tn), lambda i, j, k: (k, j))
    o_spec = pl.BlockSpec((tm, tn), lambda i, j, k: (i, j))
    in_specs = [a_spec, b_spec] + ([o_spec] if has_add else [])
    args = (a, b) + ((add,) if has_add else ())
    return pl.pallas_call(
        body, name=name, grid=(M // tm, N // tn, nk), in_specs=in_specs, out_specs=o_spec,
        out_shape=jax.ShapeDtypeStruct((M, N), out_dtype),
        scratch_shapes=[pltpu.VMEM((tm, tn), F32)],
        compiler_params=_cparams("parallel", "parallel", "arbitrary"),
    )(*args)


def _rms_fwd(x, g, name):
    R, Dm = x.shape
    tr = _tile(R, 256, 8)

    def body(x_ref, g_ref, o_ref):
        xv = x_ref[...]
        r = lax.rsqrt(jnp.mean(xv * xv, axis=-1, keepdims=True) + EPS)
        o_ref[...] = (xv * r * g_ref[...]).astype(BF16)

    return pl.pallas_call(
        body, name=name, grid=(R // tr,),
        in_specs=[pl.BlockSpec((tr, Dm), lambda i: (i, 0)), pl.BlockSpec((1, Dm), lambda i: (0, 0))],
        out_specs=pl.BlockSpec((tr, Dm), lambda i: (i, 0)),
        out_shape=jax.ShapeDtypeStruct((R, Dm), BF16),
        compiler_params=_cparams("parallel"),
    )(x, g.reshape(1, Dm))


def _rms_bwd(x, g, dh, dres, name):
    R, Dm = x.shape
    tr = _tile(R, 256, 8)
    has_res = dres is not None

    def body(*refs):
        if has_res:
            x_ref, g_ref, dh_ref, dres_ref, dx_ref, dg_ref = refs
        else:
            x_ref, g_ref, dh_ref, dx_ref, dg_ref = refs
        xv = x_ref[...]
        r = lax.rsqrt(jnp.mean(xv * xv, axis=-1, keepdims=True) + EPS)
        xhat = xv * r
        dhv = dh_ref[...]
        dxhat = dhv * g_ref[...]
        dx = r * (dxhat - xhat * jnp.mean(dxhat * xhat, axis=-1, keepdims=True))
        if has_res:
            dx = dx + dres_ref[...]
        dx_ref[...] = dx

        @pl.when(pl.program_id(0) == 0)
        def _():
            dg_ref[...] = jnp.zeros_like(dg_ref)

        dg_ref[...] += jnp.sum(dhv * xhat, axis=0, keepdims=True)

    row = pl.BlockSpec((tr, Dm), lambda i: (i, 0))
    vec = pl.BlockSpec((1, Dm), lambda i: (0, 0))
    in_specs = [row, vec, row] + ([row] if has_res else [])
    args = (x, g.reshape(1, Dm), dh) + ((dres,) if has_res else ())
    dx, dg = pl.pallas_call(
        body, name=name, grid=(R // tr,), in_specs=in_specs, out_specs=[row, vec],
        out_shape=[jax.ShapeDtypeStruct((R, Dm), F32), jax.ShapeDtypeStruct((1, Dm), F32)],
        compiler_params=_cparams("arbitrary"),
    )(*args)
    return dx, dg.reshape(Dm)


def _loss_head(x, g, tgt, name):
    R, Dm = x.shape
    tr = _tile(R, 256, 8)

    def body(x_ref, g_ref, t_ref, loss_ref, dx_ref, dg_ref):
        xv = x_ref[...]
        gv = g_ref[...]
        r = lax.rsqrt(jnp.mean(xv * xv, axis=-1, keepdims=True) + EPS)
        xhat = xv * r
        e = xhat * gv - t_ref[...]
        part = jnp.sum(jnp.mean(e * e, axis=-1, keepdims=True), axis=0, keepdims=True) * 0.5
        dy = e * (1.0 / Dm)
        dxhat = dy * gv
        dx_ref[...] = r * (dxhat - xhat * jnp.mean(dxhat * xhat, axis=-1, keepdims=True))

        @pl.when(pl.program_id(0) == 0)
        def _():
            dg_ref[...] = jnp.zeros_like(dg_ref)
            loss_ref[...] = jnp.zeros_like(loss_ref)

        dg_ref[...] += jnp.sum(dy * xhat, axis=0, keepdims=True)
        loss_ref[...] += jnp.broadcast_to(part, loss_ref.shape)

    row = pl.BlockSpec((tr, Dm), lambda i: (i, 0))
    vec = pl.BlockSpec((1, Dm), lambda i: (0, 0))
    lsp = pl.BlockSpec((1, LANES), lambda i: (0, 0))
    loss, dx, dg = pl.pallas_call(
        body, name=name, grid=(R // tr,), in_specs=[row, vec, row], out_specs=[lsp, row, vec],
        out_shape=[jax.ShapeDtypeStruct((1, LANES), F32), jax.ShapeDtypeStruct((R, Dm), F32),
                   jax.ShapeDtypeStruct((1, Dm), F32)],
        compiler_params=_cparams("arbitrary"),
    )(x, g.reshape(1, Dm), tgt)
    return loss[0, 0], dx, dg.reshape(Dm)


def _elementwise(fn, mats, vecs, out_dtypes, name):
    R, C = mats[0].shape
    tr, tc = _ew_tiles(R, C)
    nm, nv, no = len(mats), len(vecs), len(out_dtypes)

    def body(*refs):
        ins = [r[...] for r in refs[:nm + nv]]
        outs = fn(*ins)
        for o_ref, o in zip(refs[nm + nv:], outs):
            o_ref[...] = o.astype(o_ref.dtype)

    blk = pl.BlockSpec((tr, tc), lambda i, j: (i, j))
    vblk = pl.BlockSpec((1, tc), lambda i, j: (0, j))
    res = pl.pallas_call(
        body, name=name, grid=(R // tr, C // tc),
        in_specs=[blk] * nm + [vblk] * nv, out_specs=[blk] * no,
        out_shape=[jax.ShapeDtypeStruct((R, C), dt) for dt in out_dtypes],
        compiler_params=_cparams("parallel", "parallel"),
    )(*mats, *[v.reshape(1, C) for v in vecs])
    return res


def _sum_lead(arr, name):
    P_, R, C = arr.shape
    tr, tc = _ew_tiles(R, C)

    def body(a_ref, o_ref):
        s = a_ref[0].astype(F32)
        for p in range(1, P_):
            s = s + a_ref[p].astype(F32)
        o_ref[...] = s

    return pl.pallas_call(
        body, name=name, grid=(R // tr, C // tc),
        in_specs=[pl.BlockSpec((P_, tr, tc), lambda i, j: (0, i, j))],
        out_specs=pl.BlockSpec((tr, tc), lambda i, j: (i, j)),
        out_shape=jax.ShapeDtypeStruct((R, C), F32),
        compiler_params=_cparams("parallel", "parallel"),
    )(arr)


def _adamw(w, m, v, gparts, name):
    P_, R, C = gparts.shape
    tr, tc = _ew_tiles(R, C)
    c1 = 1.0 / (1.0 - ADAM_B1 ** ADAM_STEP)
    c2 = 1.0 / (1.0 - ADAM_B2 ** ADAM_STEP)

    def body(w_ref, m_ref, v_ref, g_ref, go_ref, d_ref, mo_ref, vo_ref):
        g = g_ref[0]
        for p in range(1, P_):
            g = g + g_ref[p]
        mn = ADAM_B1 * m_ref[...] + (1.0 - ADAM_B1) * g
        vn = ADAM_B2 * v_ref[...] + (1.0 - ADAM_B2) * (g * g)
        m_hat = mn * c1
        v_hat = vn * c2
        d_ref[...] = -ADAM_LR * (m_hat / (jnp.sqrt(v_hat) + ADAM_EPS) + ADAM_WD * w_ref[...])
        go_ref[...] = g
        mo_ref[...] = mn
        vo_ref[...] = vn

    blk = pl.BlockSpec((tr, tc), lambda i, j: (i, j))
    sds = jax.ShapeDtypeStruct((R, C), F32)
    return pl.pallas_call(
        body, name=name, grid=(R // tr, C // tc),
        in_specs=[blk, blk, blk, pl.BlockSpec((P_, tr, tc), lambda i, j: (0, i, j))],
        out_specs=[blk] * 4, out_shape=[sds] * 4,
        compiler_params=_cparams("parallel", "parallel"),
    )(w, m, v, gparts)


def _conv_pre(u, up, w_ref, b, first):
    tr = u.shape[0]
    rows = lax.broadcasted_iota(jnp.int32, u.shape, 0)
    keep = 1.0 - first.astype(F32)
    acc = b + w_ref[SSD_CONV - 1:SSD_CONV, :] * u
    shifted = []
    for j in range(1, SSD_CONV):
        su = pltpu.roll(u, j, 0)
        sp = pltpu.roll(up, j, 0) * keep
        sh = jnp.where(rows < j, sp, su)
        shifted.append(sh)
        acc = acc + w_ref[SSD_CONV - 1 - j:SSD_CONV - j, :] * sh
    return acc, shifted


def _conv_fwd(u, w, b, name):
    T, C = u.shape
    tr, tc = _tile(T, 256, 8), _tile(C, 1024)

    def body(u_ref, up_ref, w_ref, b_ref, o_ref):
        pre, _ = _conv_pre(u_ref[...], up_ref[...], w_ref, b_ref[...], pl.program_id(0) == 0)
        o_ref[...] = pre * _sigmoid(pre)

    return pl.pallas_call(
        body, name=name, grid=(T // tr, C // tc),
        in_specs=[pl.BlockSpec((tr, tc), lambda i, j: (i, j)),
                  pl.BlockSpec((tr, tc), lambda i, j: (jnp.maximum(i - 1, 0), j)),
                  pl.BlockSpec((SSD_CONV, tc), lambda i, j: (0, j)),
                  pl.BlockSpec((1, tc), lambda i, j: (0, j))],
        out_specs=pl.BlockSpec((tr, tc), lambda i, j: (i, j)),
        out_shape=jax.ShapeDtypeStruct((T, C), F32),
        compiler_params=_cparams("parallel", "parallel"),
    )(u, u, w, b.reshape(1, C))


def _conv_bwd_pre(u, w, b, dact, name):
    T, C = u.shape
    tr, tc = _tile(T, 256, 8), _tile(C, 1024)

    def body(u_ref, up_ref, w_ref, b_ref, da_ref, dp_ref, dw_ref, db_ref):
        i = pl.program_id(1)
        uv = u_ref[...]
        pre, shifted = _conv_pre(uv, up_ref[...], w_ref, b_ref[...], i == 0)
        sg = _sigmoid(pre)
        dpre = da_ref[...] * (sg * (1.0 + pre * (1.0 - sg)))
        dp_ref[...] = dpre

        @pl.when(i == 0)
        def _():
            dw_ref[...] = jnp.zeros_like(dw_ref)
            db_ref[...] = jnp.zeros_like(db_ref)

        db_ref[...] += jnp.sum(dpre, axis=0, keepdims=True)
        dw_ref[SSD_CONV - 1:SSD_CONV, :] += jnp.sum(dpre * uv, axis=0, keepdims=True)
        for j in range(1, SSD_CONV):
            dw_ref[SSD_CONV - 1 - j:SSD_CONV - j, :] += jnp.sum(dpre * shifted[j - 1], axis=0, keepdims=True)

    blk = pl.BlockSpec((tr, tc), lambda j, i: (i, j))
    dpre, dw, db = pl.pallas_call(
        body, name=name, grid=(C // tc, T // tr),
        in_specs=[blk, pl.BlockSpec((tr, tc), lambda j, i: (jnp.maximum(i - 1, 0), j)),
                  pl.BlockSpec((SSD_CONV, tc), lambda j, i: (0, j)),
                  pl.BlockSpec((1, tc), lambda j, i: (0, j)), blk],
        out_specs=[blk, pl.BlockSpec((SSD_CONV, tc), lambda j, i: (0, j)), pl.BlockSpec((1, tc), lambda j, i: (0, j))],
        out_shape=[jax.ShapeDtypeStruct((T, C), F32), jax.ShapeDtypeStruct((SSD_CONV, C), F32),
                   jax.ShapeDtypeStruct((1, C), F32)],
        compiler_params=_cparams("parallel", "arbitrary"),
    )(u, u, w, b.reshape(1, C), dact)
    return dpre, dw, db.reshape(C)


def _conv_bwd_in(dpre, w, name):
    T, C = dpre.shape
    tr, tc = _tile(T, 256, 8), _tile(C, 1024)
    nb = T // tr

    def body(d_ref, dn_ref, w_ref, o_ref):
        d = d_ref[...]
        keep = 1.0 - (pl.program_id(0) == nb - 1).astype(F32)
        dn = dn_ref[...] * keep
        rows = lax.broadcasted_iota(jnp.int32, d.shape, 0)
        acc = w_ref[SSD_CONV - 1:SSD_CONV, :] * d
        for j in range(1, SSD_CONV):
            sd = pltpu.roll(d, tr - j, 0)
            sn = pltpu.roll(dn, tr - j, 0)
            acc = acc + w_ref[SSD_CONV - 1 - j:SSD_CONV - j, :] * jnp.where(rows >= tr - j, sn, sd)
        o_ref[...] = acc

    return pl.pallas_call(
        body, name=name, grid=(nb, C // tc),
        in_specs=[pl.BlockSpec((tr, tc), lambda i, j: (i, j)),
                  pl.BlockSpec((tr, tc), lambda i, j: (jnp.minimum(i + 1, nb - 1), j)),
                  pl.BlockSpec((SSD_CONV, tc), lambda i, j: (0, j))],
        out_specs=pl.BlockSpec((tr, tc), lambda i, j: (i, j)),
        out_shape=jax.ShapeDtypeStruct((T, C), F32),
        compiler_params=_cparams("parallel", "parallel"),
    )(dpre, dpre, w)


def _mem_probs(qh, kh, scale):
    s = _dot_nt(qh.astype(BF16), kh.astype(BF16)) * scale
    m = jnp.max(s, axis=-1, keepdims=True)
    p = jnp.exp(s - m)
    return p / jnp.sum(p, axis=-1, keepdims=True)


def _memattn_fwd(q, kv, name):
    T, MW = q.shape
    NM = kv.shape[0]
    hd = MW // MEM_HEADS
    scale = hd ** -0.5
    tq = _tile(T, 512, 8)

    def body(q_ref, kv_ref, o_ref):
        for h in range(MEM_HEADS):
            sl = slice(h * hd, (h + 1) * hd)
            p = _mem_probs(q_ref[:, sl], kv_ref[:, sl], scale)
            vh = kv_ref[:, MW + h * hd:MW + (h + 1) * hd]
            o_ref[:, sl] = _dot(p.astype(BF16), vh.astype(BF16))

    return pl.pallas_call(
        body, name=name, grid=(T // tq,),
        in_specs=[pl.BlockSpec((tq, MW), lambda i: (i, 0)), pl.BlockSpec((NM, 2 * MW), lambda i: (0, 0))],
        out_specs=pl.BlockSpec((tq, MW), lambda i: (i, 0)),
        out_shape=jax.ShapeDtypeStruct((T, MW), F32),
        compiler_params=_cparams("parallel"),
    )(q, kv)


def _memattn_bwd(q, kv, dy, name):
    T, MW = q.shape
    NM = kv.shape[0]
    hd = MW // MEM_HEADS
    scale = hd ** -0.5
    tq = _tile(T, 512, 8)

    def body(q_ref, kv_ref, dy_ref, dq_ref, dkv_ref):
        @pl.when(pl.program_id(0) == 0)
        def _():
            dkv_ref[...] = jnp.zeros_like(dkv_ref)

        for h in range(MEM_HEADS):
            sl = slice(h * hd, (h + 1) * hd)
            vsl = slice(MW + h * hd, MW + (h + 1) * hd)
            qh = q_ref[:, sl]
            kh = kv_ref[:, sl]
            vh = kv_ref[:, vsl]
            dyh = dy_ref[:, sl].astype(BF16)
            p = _mem_probs(qh, kh, scale)
            dp = _dot_nt(dyh, vh.astype(BF16))
            ds = p * (dp - jnp.sum(dp * p, axis=-1, keepdims=True)) * scale
            dq_ref[:, sl] = _dot(ds.astype(BF16), kh.astype(BF16))
            dkv_ref[:, sl] += _dot(ds.T.astype(BF16), qh.astype(BF16))
            dkv_ref[:, vsl] += _dot(p.T.astype(BF16), dyh)

    return pl.pallas_call(
        body, name=name, grid=(T // tq,),
        in_specs=[pl.BlockSpec((tq, MW), lambda i: (i, 0)), pl.BlockSpec((NM, 2 * MW), lambda i: (0, 0)),
                  pl.BlockSpec((tq, MW), lambda i: (i, 0))],
        out_specs=[pl.BlockSpec((tq, MW), lambda i: (i, 0)), pl.BlockSpec((NM, 2 * MW), lambda i: (0, 0))],
        out_shape=[jax.ShapeDtypeStruct((T, MW), F32), jax.ShapeDtypeStruct((NM, 2 * MW), F32)],
        compiler_params=_cparams("arbitrary"),
    )(q, kv, dy)


def _silu_parts(z):
    sg = _sigmoid(z)
    return z * sg, sg * (1.0 + z * (1.0 - sg))


def _gate_fwd(a, z, name):
    return _elementwise(lambda av, zv: (av * _silu_parts(zv)[0],), [a, z], [], [BF16], name)[0]


def _gate_bwd(a, z, d, name):
    def fn(av, zv, dv):
        s, ds = _silu_parts(zv)
        return dv * s, dv * av * ds
    return _elementwise(fn, [a, z, d], [], [F32, F32], name)


def _gate_norm_fwd(y, z, g, groups, name):
    T, C = y.shape
    gw = C // groups
    tr = _tile(T, 128, 8)

    def body(y_ref, z_ref, g_ref, o_ref):
        for k in range(groups):
            sl = slice(k * gw, (k + 1) * gw)
            u = y_ref[:, sl] * _silu_parts(z_ref[:, sl])[0]
            r = lax.rsqrt(jnp.mean(u * u, axis=-1, keepdims=True) + EPS)
            o_ref[:, sl] = (u * r * g_ref[:, sl]).astype(BF16)

    row = pl.BlockSpec((tr, C), lambda i: (i, 0))
    return pl.pallas_call(
        body, name=name, grid=(T // tr,), in_specs=[row, row, pl.BlockSpec((1, C), lambda i: (0, 0))],
        out_specs=row, out_shape=jax.ShapeDtypeStruct((T, C), BF16),
        compiler_params=_cparams("parallel"),
    )(y, z, g.reshape(1, C))


def _gate_norm_bwd(y, z, g, d, groups, name):
    T, C = y.shape
    gw = C // groups
    tr = _tile(T, 128, 8)

    def body(y_ref, z_ref, g_ref, d_ref, dy_ref, dz_ref, dg_ref):
        @pl.when(pl.program_id(0) == 0)
        def _():
            dg_ref[...] = jnp.zeros_like(dg_ref)

        for k in range(groups):
            sl = slice(k * gw, (k + 1) * gw)
            yv = y_ref[:, sl]
            s, ds = _silu_parts(z_ref[:, sl])
            u = yv * s
            r = lax.rsqrt(jnp.mean(u * u, axis=-1, keepdims=True) + EPS)
            uhat = u * r
            dv = d_ref[:, sl]
            dg_ref[:, sl] += jnp.sum(dv * uhat, axis=0, keepdims=True)
            duhat = dv * g_ref[:, sl]
            du = r * (duhat - uhat * jnp.mean(duhat * uhat, axis=-1, keepdims=True))
            dy_ref[:, sl] = du * s
            dz_ref[:, sl] = du * yv * ds

    row = pl.BlockSpec((tr, C), lambda i: (i, 0))
    vec = pl.BlockSpec((1, C), lambda i: (0, 0))
    dy, dz, dg = pl.pallas_call(
        body, name=name, grid=(T // tr,), in_specs=[row, row, vec, row], out_specs=[row, row, vec],
        out_shape=[jax.ShapeDtypeStruct((T, C), F32), jax.ShapeDtypeStruct((T, C), F32),
                   jax.ShapeDtypeStruct((1, C), F32)],
        compiler_params=_cparams("arbitrary"),
    )(y, z, g.reshape(1, C), d)
    return dy, dz, dg.reshape(C)


def _ssd_common(raw, bias, alog, Q, HP, HPG):
    P_ = SSD_HEAD_DIM
    dt_in = raw + bias
    dt = jnp.maximum(dt_in, 0.0) + jnp.log(1.0 + jnp.exp(-jnp.abs(dt_in)))
    a = -jnp.exp(alog)
    r_q = lax.broadcasted_iota(jnp.int32, (Q, Q), 0)
    c_q = lax.broadcasted_iota(jnp.int32, (Q, Q), 1)
    causal = r_q >= c_q
    tril = causal.astype(BF16)
    A = _xdot_l(tril, dt * a)
    e_r = lax.broadcasted_iota(jnp.int32, (LANES, HP), 0)
    e_c = lax.broadcasted_iota(jnp.int32, (LANES, HP), 1)
    E = ((e_c >= e_r * P_) & (e_c < (e_r + 1) * P_) & (e_r < HPG)).astype(BF16)
    return dt, a, A, causal, E


def _head_cols(v, vt, j):
    lane = lax.broadcasted_iota(jnp.int32, v.shape, 1)
    sub = lax.broadcasted_iota(jnp.int32, vt.shape, 0)
    col = jnp.sum(jnp.where(lane == j, v, 0.0), axis=-1, keepdims=True)
    row = jnp.sum(jnp.where(sub == j, vt, 0.0), axis=0, keepdims=True)
    return col, row


def _ssd_fwd(act, raw_g, bias_g, alog_g, dsk_g, TOK, name):
    T = act.shape[0]
    G, N, Q, P_ = SSD_GROUPS, SSD_STATE, SSD_CHUNK, SSD_HEAD_DIM
    HP = TOK // G
    HPG = HP // P_
    NC = T // Q

    def body(x_ref, b_ref, c_ref, raw_ref, bias_ref, alog_ref, dsk_ref, y_ref, hp_ref, hT):
        @pl.when(pl.program_id(1) == 0)
        def _():
            hT[...] = jnp.zeros_like(hT)

        xs = x_ref[...]
        Bb = b_ref[...].astype(BF16)
        Cb = c_ref[...].astype(BF16)
        dt, a, A, causal, E = _ssd_common(raw_ref[...], bias_ref[...], alog_ref[...], Q, HP, HPG)
        AT = A.T
        dt_e = _xdot(dt, E)
        A_e = _xdot(A, E)
        dsk_e = _xdot(jnp.broadcast_to(dsk_ref[...], (Q, LANES)), E)
        rows = lax.broadcasted_iota(jnp.int32, (Q, HP), 0)
        cols = lax.broadcasted_iota(jnp.int32, (Q, HP), 1)
        Al_e = jnp.sum(jnp.where(rows == Q - 1, A_e, 0.0), axis=0, keepdims=True)
        xdt = xs * dt_e
        hprev = hT[...]
        hp_ref[...] = hprev
        CB = _dot_nt(Cb, Bb)
        y = _dot(Cb, hprev.astype(BF16)) * jnp.exp(A_e) + dsk_e * xs
        for j in range(HPG):
            a_col, a_row = _head_cols(A, AT, j)
            L = jnp.exp(jnp.where(causal, a_col - a_row, NEG))
            xj = jnp.where((cols >= j * P_) & (cols < (j + 1) * P_), xdt, 0.0)
            y = y + _dot((CB * L).astype(BF16), xj.astype(BF16))
        y_ref[...] = y
        dte = jnp.exp(Al_e - A_e)
        hT[...] = jnp.exp(Al_e) * hprev + _dot(b_ref[...].T.astype(BF16), (xdt * dte).astype(BF16))

    nbx = TOK // N
    par = pl.BlockSpec((None, 1, LANES), lambda g, c: (g, 0, 0))
    return pl.pallas_call(
        body, name=name, grid=(G, NC),
        in_specs=[pl.BlockSpec((Q, HP), lambda g, c: (c, g)),
                  pl.BlockSpec((Q, N), lambda g, c: (c, nbx + g)),
                  pl.BlockSpec((Q, N), lambda g, c: (c, nbx + G + g)),
                  pl.BlockSpec((None, Q, LANES), lambda g, c: (g, c, 0)), par, par, par],
        out_specs=[pl.BlockSpec((Q, HP), lambda g, c: (c, g)),
                   pl.BlockSpec((None, None, N, HP), lambda g, c: (g, c, 0, 0))],
        out_shape=[jax.ShapeDtypeStruct((T, TOK), F32), jax.ShapeDtypeStruct((G, NC, N, HP), F32)],
        scratch_shapes=[pltpu.VMEM((N, HP), F32)],
        compiler_params=_cparams("parallel", "arbitrary"),
    )(act, act, act, raw_g, bias_g, alog_g, dsk_g)


def _ssd_bwd(act, raw_g, bias_g, alog_g, dsk_g, hprev, dy, TOK, name):
    T = act.shape[0]
    G, N, Q, P_ = SSD_GROUPS, SSD_STATE, SSD_CHUNK, SSD_HEAD_DIM
    HP = TOK // G
    HPG = HP // P_
    NC = T // Q

    def body(x_ref, b_ref, c_ref, raw_ref, bias_ref, alog_ref, dsk_ref, hp_ref, dy_ref,
             dx_ref, db_ref, dc_ref, draw_ref, dalog_ref, dbias_ref, ddsk_ref, dHT):
        @pl.when(pl.program_id(1) == 0)
        def _():
            dHT[...] = jnp.zeros_like(dHT)
            dalog_ref[...] = jnp.zeros_like(dalog_ref)
            dbias_ref[...] = jnp.zeros_like(dbias_ref)
            ddsk_ref[...] = jnp.zeros_like(ddsk_ref)

        xs = x_ref[...]
        dyv = dy_ref[...]
        Bm = b_ref[...]
        Cm = c_ref[...]
        Bb = Bm.astype(BF16)
        Cb = Cm.astype(BF16)
        raw_in = raw_ref[...] + bias_ref[...]
        dt, a, A, causal, E = _ssd_common(raw_ref[...], bias_ref[...], alog_ref[...], Q, HP, HPG)
        AT = A.T
        t_r = lax.broadcasted_iota(jnp.int32, (HP, LANES), 0)
        t_c = lax.broadcasted_iota(jnp.int32, (HP, LANES), 1)
        ET = ((t_r >= t_c * P_) & (t_r < (t_c + 1) * P_) & (t_c < HPG)).astype(BF16)
        dt_e = _xdot(dt, E)
        A_e = _xdot(A, E)
        dsk_e = _xdot(jnp.broadcast_to(dsk_ref[...], (Q, LANES)), E)
        rows = lax.broadcasted_iota(jnp.int32, (Q, HP), 0)
        cols = lax.broadcasted_iota(jnp.int32, (Q, HP), 1)
        last = rows == Q - 1
        Al_e = jnp.sum(jnp.where(last, A_e, 0.0), axis=0, keepdims=True)
        eA = jnp.exp(A_e)
        eAl = jnp.exp(Al_e)
        dte = jnp.exp(Al_e - A_e)
        xdt = xs * dt_e
        xdt_b = xdt.astype(BF16)
        CB = _dot_nt(Cb, Bb)
        HT = hp_ref[...]
        HTb = HT.astype(BF16)
        dH = dHT[...]
        dHb = dH.astype(BF16)
        dys = (dyv * eA).astype(BF16)
        CH = _dot(Cb, HTb)
        dC = _dot_nt(dys, HTb)
        dH_prev = _dot(Cm.T.astype(BF16), dys) + eAl * dH
        dAe = dyv * CH * eA
        dAl = eAl * jnp.sum(dH * HT, axis=0, keepdims=True)
        W = _dot(Bb, dHb)
        dxdt = W * dte
        dd = W * xdt * dte
        dB = _dot_nt((xdt * dte).astype(BF16), dHb)
        dAl = dAl + jnp.sum(dd, axis=0, keepdims=True)
        dAe = dAe - dd + jnp.where(last, dAl, 0.0)
        lane = lax.broadcasted_iota(jnp.int32, (Q, LANES), 1)
        sub = lax.broadcasted_iota(jnp.int32, (LANES, Q), 0)
        dCB = jnp.zeros((Q, Q), F32)
        dA_col = jnp.zeros((Q, LANES), F32)
        dA_row = jnp.zeros((LANES, Q), F32)
        for j in range(HPG):
            a_col, a_row = _head_cols(A, AT, j)
            L = jnp.exp(jnp.where(causal, a_col - a_row, NEG))
            hmask = (cols >= j * P_) & (cols < (j + 1) * P_)
            dyj = jnp.where(hmask, dyv, 0.0).astype(BF16)
            GL = _dot_nt(dyj, xdt_b) * L
            dCB = dCB + GL
            dLL = GL * CB
            dA_col = dA_col + jnp.where(lane == j, jnp.sum(dLL, axis=-1, keepdims=True), 0.0)
            dA_row = dA_row + jnp.where(sub == j, jnp.sum(dLL, axis=0, keepdims=True), 0.0)
            dxdt = dxdt + _dot((CB * L).T.astype(BF16), dyj)
        dC = dC + _dot(dCB.astype(BF16), Bb)
        dB = dB + _dot(dCB.T.astype(BF16), Cb)
        dA = dA_col - dA_row.T + _xdot(dAe, ET)
        triu = (lax.broadcasted_iota(jnp.int32, (Q, Q), 1) >= lax.broadcasted_iota(jnp.int32, (Q, Q), 0)).astype(BF16)
        rcs = _xdot_l(triu, dA)
        ddt = a * rcs + _xdot(dxdt * xs, ET)
        draw = ddt * _sigmoid(raw_in)
        draw_ref[...] = draw
        dalog_ref[...] += jnp.sum(dt * rcs, axis=0, keepdims=True) * a
        dbias_ref[...] += jnp.sum(draw, axis=0, keepdims=True)
        ddsk_ref[...] += jnp.sum(_xdot(dyv * xs, ET), axis=0, keepdims=True)
        dx_ref[...] = dxdt * dt_e + dsk_e * dyv
        db_ref[...] = dB
        dc_ref[...] = dC
        dHT[...] = dH_prev

    nbx = TOK // N
    rv = lambda c: NC - 1 - c
    par = pl.BlockSpec((None, 1, LANES), lambda g, c: (g, 0, 0))
    xsp = pl.BlockSpec((Q, HP), lambda g, c: (rv(c), g))
    outs = pl.pallas_call(
        body, name=name, grid=(G, NC),
        in_specs=[xsp,
                  pl.BlockSpec((Q, N), lambda g, c: (rv(c), nbx + g)),
                  pl.BlockSpec((Q, N), lambda g, c: (rv(c), nbx + G + g)),
                  pl.BlockSpec((None, Q, LANES), lambda g, c: (g, rv(c), 0)), par, par, par,
                  pl.BlockSpec((None, None, N, HP), lambda g, c: (g, rv(c), 0, 0)), xsp],
        out_specs=[xsp, pl.BlockSpec((Q, N), lambda g, c: (rv(c), g)), pl.BlockSpec((Q, N), lambda g, c: (rv(c), g)),
                   pl.BlockSpec((None, Q, LANES), lambda g, c: (g, rv(c), 0)), par, par, par],
        out_shape=[jax.ShapeDtypeStruct((T, TOK), F32), jax.ShapeDtypeStruct((T, G * N), F32),
                   jax.ShapeDtypeStruct((T, G * N), F32), jax.ShapeDtypeStruct((G, T, LANES), F32),
                   jax.ShapeDtypeStruct((G, 1, LANES), F32), jax.ShapeDtypeStruct((G, 1, LANES), F32),
                   jax.ShapeDtypeStruct((G, 1, LANES), F32)],
        scratch_shapes=[pltpu.VMEM((N, HP), F32)],
        compiler_params=_cparams("parallel", "arbitrary"),
    )(act, act, act, raw_g, bias_g, alog_g, dsk_g, hprev, dy)
    return outs


def _heads_per_block(H):
    for hb in (8, 6, 4, 3, 2, 1):
        if H % hb == 0:
            return hb
    return 1


def _alibi_slope(head_index, n_alibi):
    c = -ALIBI_MAX_EXP * math.log(2.0) / n_alibi
    return jnp.exp(jnp.full((1, 1), c, F32) * (head_index + 1).astype(F32))


def _attn_fwd(q, k, v, gi, window, d, name):
    T, TOK = q.shape
    E_ = ATTN_HEAD_DIM
    H = TOK // E_
    n_alibi = len(DILATED_GROUPS) * H
    assert window // d == ATTN_BLOCK and (T // d) % ATTN_BLOCK == 0
    n_sub = T // d
    nb = n_sub // ATTN_BLOCK
    HB = _heads_per_block(H)
    NHB = H // HB
    hbw = HB * E_
    scale = E_ ** -0.5
    Bq = ATTN_BLOCK

    def body(q_ref, kc_ref, kp_ref, vc_ref, vp_ref, o_ref, l_ref):
        b = pl.program_id(2)
        hb = pl.program_id(1)
        iq = lax.broadcasted_iota(jnp.int32, (Bq, Bq), 0)
        ik = lax.broadcasted_iota(jnp.int32, (Bq, Bq), 1)
        rel_c = iq - ik
        rel_p = rel_c + Bq
        mask_c = rel_c >= 0
        mask_p = (rel_p + jnp.where(b > 0, 0, 4 * Bq)) <= Bq
        for hh in range(HB):
            sl = slice(hh * E_, (hh + 1) * E_)
            slope = _alibi_slope(gi * H + hb * HB + hh, n_alibi) * float(d)
            qh = q_ref[:, sl].astype(BF16)
            s_c = jnp.where(mask_c, _dot_nt(qh, kc_ref[:, sl].astype(BF16)) * scale - slope * rel_c.astype(F32), NEG)
            s_p = jnp.where(mask_p, _dot_nt(qh, kp_ref[:, sl].astype(BF16)) * scale - slope * rel_p.astype(F32), NEG)
            m = jnp.maximum(jnp.max(s_c, axis=-1, keepdims=True), jnp.max(s_p, axis=-1, keepdims=True))
            p_c = jnp.exp(s_c - m)
            p_p = jnp.exp(s_p - m)
            den = jnp.sum(p_c, axis=-1, keepdims=True) + jnp.sum(p_p, axis=-1, keepdims=True)
            o = _dot(p_c.astype(BF16), vc_ref[:, sl].astype(BF16)) + _dot(p_p.astype(BF16), vp_ref[:, sl].astype(BF16))
            o_ref[:, sl] = o / den
            l_ref[:, sl] = jnp.broadcast_to(m + jnp.log(den), (Bq, E_))

    cur = pl.BlockSpec((Bq, hbw), lambda r, h, b: (b, r * NHB + h))
    prev = pl.BlockSpec((Bq, hbw), lambda r, h, b: (jnp.maximum(b - 1, 0), r * NHB + h))
    view = lambda t: t.reshape(n_sub, d * TOK)
    sds = jax.ShapeDtypeStruct((n_sub, d * TOK), F32)
    o, l = pl.pallas_call(
        body, name=name, grid=(d, NHB, nb), in_specs=[cur, cur, prev, cur, prev], out_specs=[cur, cur],
        out_shape=[sds, sds], compiler_params=_cparams("parallel", "parallel", "parallel"),
    )(view(q), view(k), view(k), view(v), view(v))
    return o.reshape(T, TOK), l.reshape(T, TOK)


def _attn_combine(os_, ls_, name):
    def fn(*v):
        n = len(v) // 2
        o, l = v[:n], v[n:]
        m = l[0]
        for t in l[1:]:
            m = jnp.maximum(m, t)
        e = [jnp.exp(t - m) for t in l]
        den = e[0]
        for t in e[1:]:
            den = den + t
        y = e[0] * o[0]
        for t, u in zip(e[1:], o[1:]):
            y = y + t * u
        return y / den, m + jnp.log(den)
    return _elementwise(fn, list(os_) + list(ls_), [], [F32, F32], name)


def _attn_bwd(q, k, v, y, lse, dy, gi, window, d, name):
    T, TOK = q.shape
    E_ = ATTN_HEAD_DIM
    H = TOK // E_
    n_alibi = len(DILATED_GROUPS) * H
    n_sub = T // d
    nb = n_sub // ATTN_BLOCK
    HB = _heads_per_block(H)
    NHB = H // HB
    hbw = HB * E_
    scale = E_ ** -0.5
    Bq = ATTN_BLOCK

    def body(q0_ref, q1_ref, kp_ref, k0_ref, vp_ref, v0_ref, do0_ref, do1_ref, y0_ref, y1_ref, l0_ref, l1_ref,
             dq_ref, dk_ref, dv_ref):
        b = pl.program_id(2)
        hb = pl.program_id(1)
        iq = lax.broadcasted_iota(jnp.int32, (Bq, Bq), 0)
        ik = lax.broadcasted_iota(jnp.int32, (Bq, Bq), 1)
        rel_c = iq - ik
        rel_p = rel_c + Bq
        mask_c = rel_c >= 0
        mask_p = (rel_p + jnp.where(b > 0, 0, 4 * Bq)) <= Bq
        mask_n = (rel_p + jnp.where(b < nb - 1, 0, 4 * Bq)) <= Bq
        relc_f = rel_c.astype(F32)
        relp_f = rel_p.astype(F32)
        for hh in range(HB):
            sl = slice(hh * E_, (hh + 1) * E_)
            slope = _alibi_slope(gi * H + hb * HB + hh, n_alibi) * float(d)
            q0 = q0_ref[:, sl].astype(BF16)
            q1 = q1_ref[:, sl].astype(BF16)
            k0 = k0_ref[:, sl].astype(BF16)
            kp = kp_ref[:, sl].astype(BF16)
            v0 = v0_ref[:, sl].astype(BF16)
            vp = vp_ref[:, sl].astype(BF16)
            do0 = do0_ref[:, sl]
            do1 = do1_ref[:, sl]
            lse0 = jnp.max(l0_ref[:, sl], axis=-1, keepdims=True)
            lse1 = jnp.max(l1_ref[:, sl], axis=-1, keepdims=True)
            delta0 = jnp.sum(do0 * y0_ref[:, sl], axis=-1, keepdims=True)
            delta1 = jnp.sum(do1 * y1_ref[:, sl], axis=-1, keepdims=True)
            do0b = do0.astype(BF16)
            do1b = do1.astype(BF16)
            p_cc = jnp.exp(jnp.where(mask_c, _dot_nt(q0, k0) * scale - slope * relc_f, NEG) - lse0)
            ds_cc = p_cc * (_dot_nt(do0b, v0) - delta0)
            p_cp = jnp.exp(jnp.where(mask_p, _dot_nt(q0, kp) * scale - slope * relp_f, NEG) - lse0)
            ds_cp = p_cp * (_dot_nt(do0b, vp) - delta0)
            p_nc = jnp.exp(jnp.where(mask_n, _dot_nt(q1, k0) * scale - slope * relp_f, NEG) - lse1)
            ds_nc = p_nc * (_dot_nt(do1b, v0) - delta1)
            dq_ref[:, sl] = scale * (_dot(ds_cc.astype(BF16), k0) + _dot(ds_cp.astype(BF16), kp))
            dk_ref[:, sl] = scale * (_dot(ds_cc.T.astype(BF16), q0) + _dot(ds_nc.T.astype(BF16), q1))
            dv_ref[:, sl] = _dot(p_cc.T.astype(BF16), do0b) + _dot(p_nc.T.astype(BF16), do1b)

    cur = pl.BlockSpec((Bq, hbw), lambda r, h, b: (b, r * NHB + h))
    prev = pl.BlockSpec((Bq, hbw), lambda r, h, b: (jnp.maximum(b - 1, 0), r * NHB + h))
    nxt = pl.BlockSpec((Bq, hbw), lambda r, h, b: (jnp.minimum(b + 1, nb - 1), r * NHB + h))
    view = lambda t: t.reshape(n_sub, d * TOK)
    sds = jax.ShapeDtypeStruct((n_sub, d * TOK), F32)
    dq, dk, dv = pl.pallas_call(
        body, name=name, grid=(d, NHB, nb),
        in_specs=[cur, nxt, prev, cur, prev, cur, cur, nxt, cur, nxt, cur, nxt],
        out_specs=[cur, cur, cur], out_shape=[sds, sds, sds],
        compiler_params=_cparams("parallel", "parallel", "parallel"),
    )(view(q), view(q), view(k), view(k), view(v), view(v), view(dy), view(dy), view(y), view(y), view(lse), view(lse))
    return dq.reshape(T, TOK), dk.reshape(T, TOK), dv.reshape(T, TOK)


_FLIPS = {
    "xy": [(1, 0, 0), (0, 1, 0), (1, 1, 0)],
    "c": [(0, 0, 1)],
    "xyc": [(dx, dy, dc) for dx in (0, 1) for dy in (0, 1) for dc in (0, 1) if (dx, dy, dc) != (0, 0, 0)],
}


def _comm(name, group, srcs, modes):
    flips = _FLIPS[group]
    F_ = len(flips)
    P_ = F_ + 1
    n = len(srcs)

    def gidx(px, py, pc):
        if group == "xy":
            return 2 * px + py
        if group == "c":
            return pc
        return 4 * px + 2 * py + pc

    def body(*refs):
        src_refs, out_refs = refs[:n], refs[n:2 * n]
        send_sems, recv_sems, loc_sems = refs[2 * n:]
        x, y, c = lax.axis_index("x"), lax.axis_index("y"), lax.axis_index("c")
        me = gidx(x, y, c)
        local, remote = [], []
        for i in range(n):
            gather = modes[i] == "gather"
            mine = pltpu.make_async_copy(src_refs[i] if gather else src_refs[i].at[me], out_refs[i].at[me], loc_sems.at[i])
            mine.start()
            local.append(mine)
            for f, (dx, dy, dc) in enumerate(flips):
                px = 1 - x if dx else x
                py = 1 - y if dy else y
                pc = 1 - c if dc else c
                cp = pltpu.make_async_remote_copy(
                    src_ref=src_refs[i] if gather else src_refs[i].at[gidx(px, py, pc)],
                    dst_ref=out_refs[i].at[me],
                    send_sem=send_sems.at[i * F_ + f], recv_sem=recv_sems.at[i * F_ + f],
                    device_id=(px, py, pc), device_id_type=MESH)
                cp.start()
                remote.append(cp)
        for cp in local:
            cp.wait()
        for cp in remote:
            cp.wait()

    out_shape = []
    for s, mode in zip(srcs, modes):
        shp = (P_,) + tuple(s.shape) if mode == "gather" else tuple(s.shape)
        out_shape.append(jax.ShapeDtypeStruct(shp, s.dtype))
    anyspec = pl.BlockSpec(memory_space=pl.ANY)
    return pl.pallas_call(
        body, name=name, in_specs=[anyspec] * n, out_specs=[anyspec] * n, out_shape=out_shape,
        scratch_shapes=[pltpu.SemaphoreType.DMA((n * F_,)), pltpu.SemaphoreType.DMA((n * F_,)),
                        pltpu.SemaphoreType.DMA((n,))],
    )(*srcs)


def _dims(D):
    MIX = 2 * D
    MW = MIX // 4
    TOK = MIX - MW
    H = TOK // SSD_HEAD_DIM
    CONV = TOK + 2 * SSD_GROUPS * SSD_STATE
    return dict(MIX=MIX, MW=MW, TOK=TOK, H=H, CONV=CONV)


def _proj_chain(dsegs, wsegs, name):
    acc = None
    for n, (ds, ws) in enumerate(zip(dsegs, wsegs)):
        acc = _mm(ds, ws, "nt", F32, f"{name}_dh{n}", add=acc)
    return acc


def _pad_lanes(a, width=LANES):
    return jnp.pad(a, [(0, 0)] * (a.ndim - 1) + [(0, width - a.shape[-1])])


def _ssd_layer_fwd(x, kv, p, li):
    T, D = x.shape
    dm = _dims(D)
    TOK, MW, H, CONV = dm["TOK"], dm["MW"], dm["H"], dm["CONV"]
    G = SSD_GROUPS
    HPG = H // G
    w = p["w_in"]
    segs = dict(xbc=w[:, :CONV], dt=_pad_lanes(w[:, CONV:CONV + H]), qm=w[:, CONV + H:CONV + H + MW],
                zt=w[:, CONV + H + MW:CONV + H + MW + TOK], zm=w[:, CONV + H + MW + TOK:])
    h = _rms_fwd(x, p["norm_g"], f"l{li}_rms")
    pr = {k: _mm(h, ws, "nn", F32, f"l{li}_in_{k}") for k, ws in segs.items()}
    act = _conv_fwd(pr["xbc"], p["conv_w"], p["conv_b"], f"l{li}_conv")
    raw_g = _pad_lanes(pr["dt"][:, :H].reshape(T, G, HPG).transpose(1, 0, 2))
    hp = lambda a: _pad_lanes(a.reshape(G, 1, HPG))
    bias_g, alog_g, dsk_g = hp(p["dt_bias"]), hp(p["a_log"]), hp(p["d_skip"])
    y, hprev = _ssd_fwd(act, raw_g, bias_g, alog_g, dsk_g, TOK, f"l{li}_ssd")
    ymem = _memattn_fwd(pr["qm"], kv, f"l{li}_mem")
    gt = _gate_norm_fwd(y, pr["zt"], p["ssd_norm_g"], G, f"l{li}_gate_tok")
    gm = _gate_fwd(ymem, pr["zm"], f"l{li}_gate_mem")
    wo = p["w_out"]
    out = _mm(gt, wo[:TOK], "nn", F32, f"l{li}_out_tok", add=x)
    out = _mm(gm, wo[TOK:], "nn", F32, f"l{li}_out_mem", add=out)
    saved = dict(x=x, h=h, pr=pr, act=act, raw_g=raw_g, par=(bias_g, alog_g, dsk_g), y=y, hprev=hprev, ymem=ymem,
                 gt=gt, gm=gm, segs=segs)
    return out, saved


def _ssd_layer_bwd(dout, kv, p, s, li):
    x = s["x"]
    T, D = x.shape
    dm = _dims(D)
    TOK, MW, H, CONV = dm["TOK"], dm["MW"], dm["H"], dm["CONV"]
    G = SSD_GROUPS
    HPG = H // G
    wo = p["w_out"]
    pr = s["pr"]
    dgt = _mm(dout, wo[:TOK], "nt", F32, f"l{li}_dgt")
    dgm = _mm(dout, wo[TOK:], "nt", F32, f"l{li}_dgm")
    dwo = jnp.concatenate([_mm(s["gt"], dout, "tn", BF16, f"l{li}_dwo_tok"),
                           _mm(s["gm"], dout, "tn", BF16, f"l{li}_dwo_mem")], axis=0)
    dy, dzt, dng = _gate_norm_bwd(s["y"], pr["zt"], p["ssd_norm_g"], dgt, G, f"l{li}_gate_tok_b")
    dymem, dzm = _gate_bwd(s["ymem"], pr["zm"], dgm, f"l{li}_gate_mem_b")
    dqm, dkv = _memattn_bwd(pr["qm"], kv, dymem, f"l{li}_mem_b")
    bias_g, alog_g, dsk_g = s["par"]
    dxs, dB, dC, draw_g, dalog, dbias, ddsk = _ssd_bwd(s["act"], s["raw_g"], bias_g, alog_g, dsk_g, s["hprev"], dy, TOK,
                                                      f"l{li}_ssd_b")
    dact = jnp.concatenate([dxs, dB, dC], axis=1)
    dpre, dconv_w, dconv_b = _conv_bwd_pre(pr["xbc"], p["conv_w"], p["conv_b"], dact, f"l{li}_conv_b1")
    dxbc = _conv_bwd_in(dpre, p["conv_w"], f"l{li}_conv_b2")
    draw = _pad_lanes(draw_g[:, :, :HPG].transpose(1, 0, 2).reshape(T, H))
    dsegs = dict(xbc=dxbc, dt=draw, qm=dqm, zt=dzt, zm=dzm)
    keys = ["xbc", "dt", "qm", "zt", "zm"]
    dh = _proj_chain([dsegs[k] for k in keys], [s["segs"][k] for k in keys], f"l{li}")
    dws = {k: _mm(s["h"], dsegs[k], "tn", BF16, f"l{li}_dwin_{k}") for k in keys}
    dws["dt"] = dws["dt"][:, :H]
    dwin = jnp.concatenate([dws[k] for k in keys], axis=1)
    dx, dnorm = _rms_bwd(x, p["norm_g"], dh, dout, f"l{li}_rms_b")
    unhead = lambda a: a[:, 0, :HPG].reshape(H)
    grads = dict(norm_g=dnorm, w_in=dwin, conv_w=dconv_w, conv_b=dconv_b, dt_bias=unhead(dbias), a_log=unhead(dalog),
                 d_skip=unhead(ddsk), ssd_norm_g=dng, w_out=dwo)
    return dx, dkv, grads


def _attn_layer_fwd(x, kv, p, li):
    T, D = x.shape
    dm = _dims(D)
    TOK, MW = dm["TOK"], dm["MW"]
    w = p["w_in"]
    ng = len(DILATED_GROUPS)
    segs = {}
    for g in range(ng):
        for n, nm in enumerate("qkv"):
            c0 = g * 3 * TOK + n * TOK
            segs[f"{nm}{g}"] = w[:, c0:c0 + TOK]
    c0 = ng * 3 * TOK
    segs["qm"] = w[:, c0:c0 + MW]
    segs["zt"] = w[:, c0 + MW:c0 + MW + TOK]
    segs["zm"] = w[:, c0 + MW + TOK:]
    h = _rms_fwd(x, p["norm_g"], f"l{li}_rms")
    pr = {k: _mm(h, ws, "nn", F32, f"l{li}_in_{k}") for k, ws in segs.items()}
    os_, ls_ = [], []
    for g, (window, d) in enumerate(DILATED_GROUPS):
        o, l = _attn_fwd(pr[f"q{g}"], pr[f"k{g}"], pr[f"v{g}"], g, window, d, f"l{li}_attn{g}")
        os_.append(o)
        ls_.append(l)
    ytok, lse = _attn_combine(os_, ls_, f"l{li}_combine")
    ymem = _memattn_fwd(pr["qm"], kv, f"l{li}_mem")
    gt = _gate_fwd(ytok, pr["zt"], f"l{li}_gate_tok")
    gm = _gate_fwd(ymem, pr["zm"], f"l{li}_gate_mem")
    wo = p["w_out"]
    out = _mm(gt, wo[:TOK], "nn", F32, f"l{li}_out_tok", add=x)
    out = _mm(gm, wo[TOK:], "nn", F32, f"l{li}_out_mem", add=out)
    saved = dict(x=x, h=h, pr=pr, ytok=ytok, lse=lse, ymem=ymem, gt=gt, gm=gm, segs=segs)
    return out, saved


def _attn_layer_bwd(dout, kv, p, s, li):
    x = s["x"]
    T, D = x.shape
    dm = _dims(D)
    TOK, MW = dm["TOK"], dm["MW"]
    wo = p["w_out"]
    pr = s["pr"]
    dgt = _mm(dout, wo[:TOK], "nt", F32, f"l{li}_dgt")
    dgm = _mm(dout, wo[TOK:], "nt", F32, f"l{li}_dgm")
    dwo = jnp.concatenate([_mm(s["gt"], dout, "tn", BF16, f"l{li}_dwo_tok"),
                           _mm(s["gm"], dout, "tn", BF16, f"l{li}_dwo_mem")], axis=0)
    dytok, dzt = _gate_bwd(s["ytok"], pr["zt"], dgt, f"l{li}_gate_tok_b")
    dymem, dzm = _gate_bwd(s["ymem"], pr["zm"], dgm, f"l{li}_gate_mem_b")
    dqm, dkv = _memattn_bwd(pr["qm"], kv, dymem, f"l{li}_mem_b")
    dsegs = {}
    for g, (window, d) in enumerate(DILATED_GROUPS):
        dq, dk, dv = _attn_bwd(pr[f"q{g}"], pr[f"k{g}"], pr[f"v{g}"], s["ytok"], s["lse"], dytok, g, window, d,
                               f"l{li}_attn{g}_b")
        dsegs[f"q{g}"], dsegs[f"k{g}"], dsegs[f"v{g}"] = dq, dk, dv
    dsegs["qm"], dsegs["zt"], dsegs["zm"] = dqm, dzt, dzm
    keys = list(s["segs"].keys())
    dh = _proj_chain([dsegs[k] for k in keys], [s["segs"][k] for k in keys], f"l{li}")
    dwin = jnp.concatenate([_mm(s["h"], dsegs[k], "tn", BF16, f"l{li}_dwin_{k}") for k in keys], axis=1)
    dx, dnorm = _rms_bwd(x, p["norm_g"], dh, dout, f"l{li}_rms_b")
    return dx, dkv, dict(norm_g=dnorm, w_in=dwin, w_out=dwo)


def _local_step(x, mem, tgt, mem_norm_g, final_norm_g, layers):
    mem_n = _rms_fwd(mem, mem_norm_g, "mem_rms")
    kvs = [_mm(mem_n, p["w_mem_kv"], "nn", F32, f"l{li}_kv") for li, p in enumerate(layers)]
    saved = []
    for li, p in enumerate(layers):
        fwd = _ssd_layer_fwd if li % 2 == 0 else _attn_layer_fwd
        x, s = fwd(x, kvs[li], p, li)
        saved.append(s)
    loss, dx, dfinal = _loss_head(x, final_norm_g, tgt, "loss_head")
    grads = [None] * len(layers)
    dmem_n = None
    for li in reversed(range(len(layers))):
        p = layers[li]
        bwd = _ssd_layer_bwd if li % 2 == 0 else _attn_layer_bwd
        dx, dkv, g = bwd(dx, kvs[li], p, saved[li], li)
        g["w_mem_kv"] = _mm(mem_n, dkv, "tn", BF16, f"l{li}_dwkv")
        dmem_n = _mm(dkv, p["w_mem_kv"], "nt", F32, f"l{li}_dmem", add=dmem_n)
        grads[li] = g
    _, dmem_g = _rms_bwd(mem, mem_norm_g, dmem_n, None, "mem_rms_b")
    return loss, dx, dmem_g, dfinal, grads


_SSD_SMALL = ["norm_g", "conv_w", "conv_b", "dt_bias", "a_log", "d_skip", "ssd_norm_g"]
_ATTN_SMALL = ["norm_g"]
_SSD_ORDER = ["norm_g", "w_in", "conv_w", "conv_b", "dt_bias", "a_log", "d_skip", "ssd_norm_g", "w_mem_kv", "w_out"]
_ATTN_ORDER = ["norm_g", "w_in", "w_mem_kv", "w_out"]


def _pack(arrs):
    flat = jnp.concatenate([a.reshape(-1).astype(F32) for a in arrs])
    n = flat.shape[0]
    pad = (-n) % (8 * LANES)
    return jnp.pad(flat, (0, pad)).reshape(-1, LANES)


def _unpack(mat, shapes):
    flat = mat.reshape(-1)
    out, o = [], 0
    for shp in shapes:
        n = math.prod(shp)
        out.append(flat[o:o + n].reshape(shp))
        o += n
    return out


def kernel(x, mem, mem_norm_g, final_norm_g, norm_g_0, w_in_0, conv_w_0, conv_b_0, dt_bias_0, a_log_0, d_skip_0, ssd_norm_g_0, w_mem_kv_0, w_out_0, norm_g_1, w_in_1, w_mem_kv_1, w_out_1, norm_g_2, w_in_2, conv_w_2, conv_b_2, dt_bias_2, a_log_2, d_skip_2, ssd_norm_g_2, w_mem_kv_2, w_out_2, norm_g_3, w_in_3, w_mem_kv_3, w_out_3, loss_target, m_mem_norm_g, m_final_norm_g, m_norm_g_0, m_w_in_0, m_conv_w_0, m_conv_b_0, m_dt_bias_0, m_a_log_0, m_d_skip_0, m_ssd_norm_g_0, m_w_mem_kv_0, m_w_out_0, m_norm_g_1, m_w_in_1, m_w_mem_kv_1, m_w_out_1, m_norm_g_2, m_w_in_2, m_conv_w_2, m_conv_b_2, m_dt_bias_2, m_a_log_2, m_d_skip_2, m_ssd_norm_g_2, m_w_mem_kv_2, m_w_out_2, m_norm_g_3, m_w_in_3, m_w_mem_kv_3, m_w_out_3, v_mem_norm_g, v_final_norm_g, v_norm_g_0, v_w_in_0, v_conv_w_0, v_conv_b_0, v_dt_bias_0, v_a_log_0, v_d_skip_0, v_ssd_norm_g_0, v_w_mem_kv_0, v_w_out_0, v_norm_g_1, v_w_in_1, v_w_mem_kv_1, v_w_out_1, v_norm_g_2, v_w_in_2, v_conv_w_2, v_conv_b_2, v_dt_bias_2, v_a_log_2, v_d_skip_2, v_ssd_norm_g_2, v_w_mem_kv_2, v_w_out_2, v_norm_g_3, v_w_in_3, v_w_mem_kv_3, v_w_out_3):
    a = dict(locals())
    names = ["mem_norm_g", "final_norm_g"]
    for li in range(DEPTH):
        names += [f"{k}_{li}" for k in (_SSD_ORDER if li % 2 == 0 else _ATTN_ORDER)]
    W = {n: a[n] for n in names}
    Mo = {n: a["m_" + n] for n in names}
    Vo = {n: a["v_" + n] for n in names}
    NX = 4
    chip = 2 * lax.axis_index("x") + lax.axis_index("y")

    layers = []
    for li in range(DEPTH):
        ssd = li % 2 == 0
        srcs = [W[f"w_in_{li}"].astype(BF16), W[f"w_mem_kv_{li}"].astype(BF16), W[f"w_out_{li}"].astype(BF16)]
        if ssd:
            srcs.append(W[f"conv_w_{li}"])
        got = _comm(f"gather_w{li}", "xy", srcs, ["gather"] * len(srcs))
        rows = lambda g: g.reshape((-1,) + g.shape[2:])
        colcat = lambda g: g.transpose(1, 0, 2).reshape(g.shape[1], -1)
        p = dict(w_in=colcat(got[0]), w_mem_kv=rows(got[1]), w_out=rows(got[2]), norm_g=W[f"norm_g_{li}"])
        if ssd:
            p.update(conv_w=colcat(got[3]), conv_b=W[f"conv_b_{li}"], dt_bias=W[f"dt_bias_{li}"], a_log=W[f"a_log_{li}"],
                     d_skip=W[f"d_skip_{li}"], ssd_norm_g=W[f"ssd_norm_g_{li}"])
        layers.append(p)

    loss_l, dx, dmem_g, dfinal, grads = _local_step(x[0], mem[0], loss_target[0], W["mem_norm_g"], W["final_norm_g"], layers)
    loss = lax.psum(loss_l, ("x", "y", "c"))

    G, Dl, Mn, Vn = {}, {}, {}, {}

    for li in range(DEPTH):
        g = grads[li]
        dwin = g["w_in"]
        Dm, N = dwin.shape
        chunks = [dwin.reshape(Dm, NX, N // NX).transpose(1, 0, 2),
                  g["w_mem_kv"].reshape((NX, -1) + g["w_mem_kv"].shape[1:]),
                  g["w_out"].reshape((NX, -1) + g["w_out"].shape[1:])]
        got = _comm(f"xchg_g{li}", "xy", chunks, ["a2a"] * 3)
        parts = [_sum_lead(t, f"l{li}_gsum{n}") for n, t in enumerate(got)]
        both = _comm(f"swap_g{li}", "c", parts, ["gather"] * 3)
        for nm, pp in zip(["w_in", "w_mem_kv", "w_out"], both):
            key = f"{nm}_{li}"
            G[key], Dl[key], Mn[key], Vn[key] = _adamw(W[key], Mo[key], Vo[key], pp, f"adamw_{key}")

    small_names = ["mem_norm_g", "final_norm_g"]
    small_grads = [dmem_g, dfinal]
    for li in range(DEPTH):
        for k in (_SSD_SMALL if li % 2 == 0 else _ATTN_SMALL):
            small_names.append(f"{k}_{li}")
            small_grads.append(grads[li][k])
    shapes = [tuple(t.shape) for t in small_grads]
    allg = _comm("gather_small", "xyc", [_pack(small_grads)], ["gather"])[0]
    gsum = _unpack(_sum_lead(allg, "small_gsum"), shapes)
    small_w, small_m, small_v, small_g = [], [], [], []
    for nme, gv in zip(small_names, gsum):
        if nme.startswith("conv_w"):
            cw = W[nme].shape[1]
            gv = lax.dynamic_slice_in_dim(gv, chip * cw, cw, axis=1)
        small_g.append(gv)
        small_w.append(W[nme])
        small_m.append(Mo[nme])
        small_v.append(Vo[nme])
    sshapes = [tuple(t.shape) for t in small_g]
    res = _adamw(_pack(small_w), _pack(small_m), _pack(small_v), _pack(small_g)[None], "adamw_small")
    for dst, mat in zip((G, Dl, Mn, Vn), res):
        for nme, t in zip(small_names, _unpack(mat, sshapes)):
            dst[nme] = t

    return (loss, dx[None], *[G[n] for n in names], *[Dl[n] for n in names], *[Mn[n] for n in names],
            *[Vn[n] for n in names])
```

```python
import functools
import math

import jax
import jax.numpy as jnp
from jax import lax
from jax.experimental import pallas as pl
from jax.experimental.pallas import tpu as pltpu
from jax.experimental.pallas import tpu_sc as plsc

F32 = jnp.float32
BF16 = jnp.bfloat16

EPS = 1e-6
MEM_HEADS = 4
SSD_HEAD_DIM = 64
SSD_GROUPS = 8
SSD_STATE = 128
SSD_CONV = 4
SSD_CHUNK = 128
ATTN_HEAD_DIM = 128
ATTN_BLOCK = 128
DILATED_GROUPS = ((128, 1), (512, 4), (2048, 16))
ALIBI_MAX_EXP = 8.0
DEPTH = 4

ADAM_LR = 0.001
ADAM_B1 = 0.9
ADAM_B2 = 0.999
ADAM_EPS = 1e-08
ADAM_WD = 0.01
ADAM_STEP = 10

LANES = 128
VMEM_LIMIT_BYTES = 48 * 1024 * 1024
NEG = -1e30
MESH = pl.DeviceIdType.MESH


def _cparams(*sem):
    return pltpu.CompilerParams(dimension_semantics=tuple(sem), vmem_limit_bytes=VMEM_LIMIT_BYTES)


def _tile(dim, pref, unit=LANES):
    if dim <= pref:
        return dim
    t = (pref // unit) * unit
    while t >= unit:
        if dim % t == 0:
            return t
        t -= unit
    return dim


def _ew_tiles(rows, cols):
    tc = cols if (cols % LANES != 0 or cols <= 2048) else _tile(cols, 2048)
    tr = rows
    while tr * tc > 256 * 1024 and tr % 2 == 0 and (tr // 2) % 8 == 0:
        tr //= 2
    return tr, tc


def _sigmoid(v):
    return 1.0 / (1.0 + jnp.exp(-v))


def _dot(a, b):
    return jnp.dot(a, b, preferred_element_type=F32)


def _dot_nt(a, b):
    return lax.dot_general(a, b, (((1,), (1,)), ((), ())), preferred_element_type=F32)


def _dot_tn(a, b):
    return lax.dot_general(a, b, (((0,), (0,)), ((), ())), preferred_element_type=F32)


def _split3(v):
    hi = v.astype(BF16)
    r = v - hi.astype(F32)
    mid = r.astype(BF16)
    lo = (r - mid.astype(F32)).astype(BF16)
    return hi, mid, lo


def _xdot(v, onehot):
    hi, mid, lo = _split3(v)
    return _dot(hi, onehot) + _dot(mid, onehot) + _dot(lo, onehot)


def _xdot_l(onehot, v):
    hi, mid, lo = _split3(v)
    return _dot(onehot, hi) + _dot(onehot, mid) + _dot(onehot, lo)


def _mm(a, b, mode, out_dtype, name, add=None):
    if mode == "nn":
        M, K = a.shape
        N = b.shape[1]
    elif mode == "nt":
        M, K = a.shape
        N = b.shape[0]
    else:
        K, M = a.shape
        N = b.shape[1]
    tm, tn, tk = (2048, 1024, 1024) if mode == "tn" else (512, 1024, 3072)
    tm, tn, tk = _tile(M, tm, 8 if M < LANES else LANES), _tile(N, tn), _tile(K, tk)
    nk = K // tk
    has_add = add is not None

    def product(a_ref, b_ref):
        av = a_ref[...].astype(BF16)
        bv = b_ref[...].astype(BF16)
        if mode == "nn":
            return _dot(av, bv)
        if mode == "nt":
            return _dot_nt(av, bv)
        return _dot_tn(av, bv)

    def body(*refs):
        a_ref, b_ref = refs[:2]
        add_ref = refs[2] if has_add else None
        o_ref = refs[3] if has_add else refs[2]

        def finish(r):
            if has_add:
                r = r + add_ref[...]
            o_ref[...] = r.astype(out_dtype)

        if nk == 1:
            finish(product(a_ref, b_ref))
            return
        acc = refs[-1]
        k = pl.program_id(2)

        @pl.when(k == 0)
        def _():
            acc[...] = product(a_ref, b_ref)

        @pl.when((k > 0) & (k < nk - 1))
        def _():
            acc[...] += product(a_ref, b_ref)

        @pl.when(k == nk - 1)
        def _():
            finish(acc[...] + product(a_ref, b_ref))

    if mode == "nn":
        a_spec = pl.BlockSpec((tm, tk), lambda j, i, k: (i, k))
        b_spec = pl.BlockSpec((tk, tn), lambda j, i, k: (k, j))
    elif mode == "nt":
        a_spec = pl.BlockSpec((tm, tk), lambda j, i, k: (i, k))
        b_spec = pl.BlockSpec((tn, tk), lambda j, i, k: (j, k))
    else:
        a_spec = pl.BlockSpec((tk, tm), lambda j, i, k: (k, i))
        b_spec = pl.BlockSpec((tk, tn), lambda j, i, k: (k, j))
    o_spec = pl.BlockSpec((tm, tn), lambda j, i, k: (i, j))
    in_specs = [a_spec, b_spec] + ([o_spec] if has_add else [])
    args = (a, b) + ((add,) if has_add else ())
    return pl.pallas_call(
        body, name=name, grid=(N // tn, M // tm, nk), in_specs=in_specs, out_specs=o_spec,
        out_shape=jax.ShapeDtypeStruct((M, N), out_dtype),
        scratch_shapes=[pltpu.VMEM((tm, tn), F32)] if nk > 1 else [],
        compiler_params=_cparams("parallel", "parallel", "arbitrary"),
    )(*args)


def _rms_fwd(x, g, name):
    R, Dm = x.shape
    tr = _tile(R, 256, 8)

    def body(x_ref, g_ref, o_ref):
        xv = x_ref[...]
        r = lax.rsqrt(jnp.mean(xv * xv, axis=-1, keepdims=True) + EPS)
        o_ref[...] = (xv * r * g_ref[...]).astype(BF16)

    return pl.pallas_call(
        body, name=name, grid=(R // tr,),
        in_specs=[pl.BlockSpec((tr, Dm), lambda i: (i, 0)), pl.BlockSpec((1, Dm), lambda i: (0, 0))],
        out_specs=pl.BlockSpec((tr, Dm), lambda i: (i, 0)),
        out_shape=jax.ShapeDtypeStruct((R, Dm), BF16),
        compiler_params=_cparams("parallel"),
    )(x, g.reshape(1, Dm))


def _rms_bwd(x, g, dh, dres, name):
    R, Dm = x.shape
    tr = _tile(R, 256, 8)
    has_res = dres is not None

    def body(*refs):
        if has_res:
            x_ref, g_ref, dh_ref, dres_ref, dx_ref, dxb_ref, dg_ref = refs
        else:
            x_ref, g_ref, dh_ref, dx_ref, dxb_ref, dg_ref = refs
        xv = x_ref[...]
        r = lax.rsqrt(jnp.mean(xv * xv, axis=-1, keepdims=True) + EPS)
        xhat = xv * r
        dhv = dh_ref[...]
        dxhat = dhv * g_ref[...]
        dx = r * (dxhat - xhat * jnp.mean(dxhat * xhat, axis=-1, keepdims=True))
        if has_res:
            dx = dx + dres_ref[...]
        dx_ref[...] = dx
        dxb_ref[...] = dx.astype(BF16)

        @pl.when(pl.program_id(0) == 0)
        def _():
            dg_ref[...] = jnp.zeros_like(dg_ref)

        dg_ref[...] += jnp.sum(dhv * xhat, axis=0, keepdims=True)

    row = pl.BlockSpec((tr, Dm), lambda i: (i, 0))
    vec = pl.BlockSpec((1, Dm), lambda i: (0, 0))
    in_specs = [row, vec, row] + ([row] if has_res else [])
    args = (x, g.reshape(1, Dm), dh) + ((dres,) if has_res else ())
    dx, dxb, dg = pl.pallas_call(
        body, name=name, grid=(R // tr,), in_specs=in_specs, out_specs=[row, row, vec],
        out_shape=[jax.ShapeDtypeStruct((R, Dm), F32), jax.ShapeDtypeStruct((R, Dm), BF16),
                   jax.ShapeDtypeStruct((1, Dm), F32)],
        compiler_params=_cparams("arbitrary"),
    )(*args)
    return dx, dxb, dg.reshape(Dm)


def _loss_head(x, g, tgt, name):
    R, Dm = x.shape
    tr = _tile(R, 256, 8)

    def body(x_ref, g_ref, t_ref, loss_ref, dx_ref, dxb_ref, dg_ref):
        xv = x_ref[...]
        gv = g_ref[...]
        r = lax.rsqrt(jnp.mean(xv * xv, axis=-1, keepdims=True) + EPS)
        xhat = xv * r
        e = xhat * gv - t_ref[...]
        part = jnp.sum(jnp.mean(e * e, axis=-1, keepdims=True), axis=0, keepdims=True) * 0.5
        dy = e * (1.0 / Dm)
        dxhat = dy * gv
        dx = r * (dxhat - xhat * jnp.mean(dxhat * xhat, axis=-1, keepdims=True))
        dx_ref[...] = dx
        dxb_ref[...] = dx.astype(BF16)

        @pl.when(pl.program_id(0) == 0)
        def _():
            dg_ref[...] = jnp.zeros_like(dg_ref)
            loss_ref[...] = jnp.zeros_like(loss_ref)

        dg_ref[...] += jnp.sum(dy * xhat, axis=0, keepdims=True)
        loss_ref[...] += jnp.broadcast_to(part, loss_ref.shape)

    row = pl.BlockSpec((tr, Dm), lambda i: (i, 0))
    vec = pl.BlockSpec((1, Dm), lambda i: (0, 0))
    lsp = pl.BlockSpec((1, LANES), lambda i: (0, 0))
    loss, dx, dxb, dg = pl.pallas_call(
        body, name=name, grid=(R // tr,), in_specs=[row, vec, row], out_specs=[lsp, row, row, vec],
        out_shape=[jax.ShapeDtypeStruct((1, LANES), F32), jax.ShapeDtypeStruct((R, Dm), F32),
                   jax.ShapeDtypeStruct((R, Dm), BF16), jax.ShapeDtypeStruct((1, Dm), F32)],
        compiler_params=_cparams("arbitrary"),
    )(x, g.reshape(1, Dm), tgt)
    return loss[0, 0], dx, dxb, dg.reshape(Dm)


def _elementwise(fn, mats, vecs, out_dtypes, name):
    R, C = mats[0].shape
    tr, tc = _ew_tiles(R, C)
    nm, nv, no = len(mats), len(vecs), len(out_dtypes)

    def body(*refs):
        ins = [r[...] for r in refs[:nm + nv]]
        outs = fn(*ins)
        for o_ref, o in zip(refs[nm + nv:], outs):
            o_ref[...] = o.astype(o_ref.dtype)

    blk = pl.BlockSpec((tr, tc), lambda i, j: (i, j))
    vblk = pl.BlockSpec((1, tc), lambda i, j: (0, j))
    res = pl.pallas_call(
        body, name=name, grid=(R // tr, C // tc),
        in_specs=[blk] * nm + [vblk] * nv, out_specs=[blk] * no,
        out_shape=[jax.ShapeDtypeStruct((R, C), dt) for dt in out_dtypes],
        compiler_params=_cparams("parallel", "parallel"),
    )(*mats, *[v.reshape(1, C) for v in vecs])
    return res


def _sum_lead(arr, name):
    P_, R, C = arr.shape
    tr, tc = _ew_tiles(R, C)

    def body(a_ref, o_ref):
        s = a_ref[0].astype(F32)
        for p in range(1, P_):
            s = s + a_ref[p].astype(F32)
        o_ref[...] = s

    return pl.pallas_call(
        body, name=name, grid=(R // tr, C // tc),
        in_specs=[pl.BlockSpec((P_, tr, tc), lambda i, j: (0, i, j))],
        out_specs=pl.BlockSpec((tr, tc), lambda i, j: (i, j)),
        out_shape=jax.ShapeDtypeStruct((R, C), F32),
        compiler_params=_cparams("parallel", "parallel"),
    )(arr)


def _adamw(w, m, v, gparts, name):
    P_ = len(gparts)
    R, C = w.shape
    tr, tc = _ew_tiles(R, C)
    c1 = 1.0 / (1.0 - ADAM_B1 ** ADAM_STEP)
    c2 = 1.0 / (1.0 - ADAM_B2 ** ADAM_STEP)

    def body(w_ref, m_ref, v_ref, *rest):
        g_refs, (go_ref, d_ref, mo_ref, vo_ref) = rest[:P_], rest[P_:]
        g = g_refs[0][...]
        for g_ref in g_refs[1:]:
            g = g + g_ref[...]
        mn = ADAM_B1 * m_ref[...] + (1.0 - ADAM_B1) * g
        vn = ADAM_B2 * v_ref[...] + (1.0 - ADAM_B2) * (g * g)
        m_hat = mn * c1
        v_hat = vn * c2
        d_ref[...] = -ADAM_LR * (m_hat / (jnp.sqrt(v_hat) + ADAM_EPS) + ADAM_WD * w_ref[...])
        go_ref[...] = g
        mo_ref[...] = mn
        vo_ref[...] = vn

    blk = pl.BlockSpec((tr, tc), lambda i, j: (i, j))
    sds = jax.ShapeDtypeStruct((R, C), F32)
    return pl.pallas_call(
        body, name=name, grid=(R // tr, C // tc),
        in_specs=[blk] * (3 + P_), out_specs=[blk] * 4, out_shape=[sds] * 4,
        compiler_params=_cparams("parallel", "parallel"),
    )(w, m, v, *gparts)


def _conv_pre(u, up, w_ref, b, first):
    tr = u.shape[0]
    rows = lax.broadcasted_iota(jnp.int32, u.shape, 0)
    keep = 1.0 - first.astype(F32)
    acc = b + w_ref[SSD_CONV - 1:SSD_CONV, :] * u
    shifted = []
    for j in range(1, SSD_CONV):
        su = pltpu.roll(u, j, 0)
        sp = pltpu.roll(up, j, 0) * keep
        sh = jnp.where(rows < j, sp, su)
        shifted.append(sh)
        acc = acc + w_ref[SSD_CONV - 1 - j:SSD_CONV - j, :] * sh
    return acc, shifted


def _conv_fwd(u, w, b, name):
    T, C = u.shape
    tr, tc = _tile(T, 256, 8), _tile(C, 1024)

    def body(u_ref, up_ref, w_ref, b_ref, o_ref):
        pre, _ = _conv_pre(u_ref[...], up_ref[...], w_ref, b_ref[...], pl.program_id(0) == 0)
        o_ref[...] = pre * _sigmoid(pre)

    return pl.pallas_call(
        body, name=name, grid=(T // tr, C // tc),
        in_specs=[pl.BlockSpec((tr, tc), lambda i, j: (i, j)),
                  pl.BlockSpec((tr, tc), lambda i, j: (jnp.maximum(i - 1, 0), j)),
                  pl.BlockSpec((SSD_CONV, tc), lambda i, j: (0, j)),
                  pl.BlockSpec((1, tc), lambda i, j: (0, j))],
        out_specs=pl.BlockSpec((tr, tc), lambda i, j: (i, j)),
        out_shape=jax.ShapeDtypeStruct((T, C), F32),
        compiler_params=_cparams("parallel", "parallel"),
    )(u, u, w, b.reshape(1, C))


def _conv_bwd_pre(u, w, b, dact, name):
    T, C = u.shape
    tr, tc = _tile(T, 256, 8), _tile(C, 1024)

    def body(u_ref, up_ref, w_ref, b_ref, da_ref, dp_ref, dw_ref, db_ref):
        i = pl.program_id(1)
        uv = u_ref[...]
        pre, shifted = _conv_pre(uv, up_ref[...], w_ref, b_ref[...], i == 0)
        sg = _sigmoid(pre)
        dpre = da_ref[...] * (sg * (1.0 + pre * (1.0 - sg)))
        dp_ref[...] = dpre

        @pl.when(i == 0)
        def _():
            dw_ref[...] = jnp.zeros_like(dw_ref)
            db_ref[...] = jnp.zeros_like(db_ref)

        db_ref[...] += jnp.sum(dpre, axis=0, keepdims=True)
        dw_ref[SSD_CONV - 1:SSD_CONV, :] += jnp.sum(dpre * uv, axis=0, keepdims=True)
        for j in range(1, SSD_CONV):
            dw_ref[SSD_CONV - 1 - j:SSD_CONV - j, :] += jnp.sum(dpre * shifted[j - 1], axis=0, keepdims=True)

    blk = pl.BlockSpec((tr, tc), lambda j, i: (i, j))
    dpre, dw, db = pl.pallas_call(
        body, name=name, grid=(C // tc, T // tr),
        in_specs=[blk, pl.BlockSpec((tr, tc), lambda j, i: (jnp.maximum(i - 1, 0), j)),
                  pl.BlockSpec((SSD_CONV, tc), lambda j, i: (0, j)),
                  pl.BlockSpec((1, tc), lambda j, i: (0, j)), blk],
        out_specs=[blk, pl.BlockSpec((SSD_CONV, tc), lambda j, i: (0, j)), pl.BlockSpec((1, tc), lambda j, i: (0, j))],
        out_shape=[jax.ShapeDtypeStruct((T, C), F32), jax.ShapeDtypeStruct((SSD_CONV, C), F32),
                   jax.ShapeDtypeStruct((1, C), F32)],
        compiler_params=_cparams("parallel", "arbitrary"),
    )(u, u, w, b.reshape(1, C), dact)
    return dpre, dw, db.reshape(C)


def _conv_bwd_in(dpre, w, name):
    T, C = dpre.shape
    tr, tc = _tile(T, 256, 8), _tile(C, 1024)
    nb = T // tr

    def body(d_ref, dn_ref, w_ref, o_ref):
        d = d_ref[...]
        keep = 1.0 - (pl.program_id(0) == nb - 1).astype(F32)
        dn = dn_ref[...] * keep
        rows = lax.broadcasted_iota(jnp.int32, d.shape, 0)
        acc = w_ref[SSD_CONV - 1:SSD_CONV, :] * d
        for j in range(1, SSD_CONV):
            sd = pltpu.roll(d, tr - j, 0)
            sn = pltpu.roll(dn, tr - j, 0)
            acc = acc + w_ref[SSD_CONV - 1 - j:SSD_CONV - j, :] * jnp.where(rows >= tr - j, sn, sd)
        o_ref[...] = acc.astype(BF16)

    return pl.pallas_call(
        body, name=name, grid=(nb, C // tc),
        in_specs=[pl.BlockSpec((tr, tc), lambda i, j: (i, j)),
                  pl.BlockSpec((tr, tc), lambda i, j: (jnp.minimum(i + 1, nb - 1), j)),
                  pl.BlockSpec((SSD_CONV, tc), lambda i, j: (0, j))],
        out_specs=pl.BlockSpec((tr, tc), lambda i, j: (i, j)),
        out_shape=jax.ShapeDtypeStruct((T, C), BF16),
        compiler_params=_cparams("parallel", "parallel"),
    )(dpre, dpre, w)


def _mem_probs(qh, kh, scale):
    s = _dot_nt(qh.astype(BF16), kh.astype(BF16)) * scale
    m = jnp.max(s, axis=-1, keepdims=True)
    p = jnp.exp(s - m)
    return p / jnp.sum(p, axis=-1, keepdims=True)


def _memattn_fwd(q, kv, name):
    T, MW = q.shape
    NM = kv.shape[0]
    hd = MW // MEM_HEADS
    scale = hd ** -0.5
    tq = _tile(T, 512, 8)

    def body(q_ref, kv_ref, o_ref):
        for h in range(MEM_HEADS):
            sl = slice(h * hd, (h + 1) * hd)
            p = _mem_probs(q_ref[:, sl], kv_ref[:, sl], scale)
            vh = kv_ref[:, MW + h * hd:MW + (h + 1) * hd]
            o_ref[:, sl] = _dot(p.astype(BF16), vh.astype(BF16))

    return pl.pallas_call(
        body, name=name, grid=(T // tq,),
        in_specs=[pl.BlockSpec((tq, MW), lambda i: (i, 0)), pl.BlockSpec((NM, 2 * MW), lambda i: (0, 0))],
        out_specs=pl.BlockSpec((tq, MW), lambda i: (i, 0)),
        out_shape=jax.ShapeDtypeStruct((T, MW), F32),
        compiler_params=_cparams("parallel"),
    )(q, kv)


def _memattn_bwd(q, kv, dy, name):
    T, MW = q.shape
    NM = kv.shape[0]
    hd = MW // MEM_HEADS
    scale = hd ** -0.5
    tq = _tile(T, 512, 8)

    def body(q_ref, kv_ref, dy_ref, dq_ref, dkv_ref):
        @pl.when(pl.program_id(0) == 0)
        def _():
            dkv_ref[...] = jnp.zeros_like(dkv_ref)

        for h in range(MEM_HEADS):
            sl = slice(h * hd, (h + 1) * hd)
            vsl = slice(MW + h * hd, MW + (h + 1) * hd)
            qh = q_ref[:, sl]
            kh = kv_ref[:, sl]
            vh = kv_ref[:, vsl]
            dyh = dy_ref[:, sl].astype(BF16)
            p = _mem_probs(qh, kh, scale)
            dp = _dot_nt(dyh, vh.astype(BF16))
            ds = p * (dp - jnp.sum(dp * p, axis=-1, keepdims=True)) * scale
            dq_ref[:, sl] = _dot(ds.astype(BF16), kh.astype(BF16)).astype(BF16)
            dkv_ref[:, sl] += _dot(ds.T.astype(BF16), qh.astype(BF16))
            dkv_ref[:, vsl] += _dot(p.T.astype(BF16), dyh)

    return pl.pallas_call(
        body, name=name, grid=(T // tq,),
        in_specs=[pl.BlockSpec((tq, MW), lambda i: (i, 0)), pl.BlockSpec((NM, 2 * MW), lambda i: (0, 0)),
                  pl.BlockSpec((tq, MW), lambda i: (i, 0))],
        out_specs=[pl.BlockSpec((tq, MW), lambda i: (i, 0)), pl.BlockSpec((NM, 2 * MW), lambda i: (0, 0))],
        out_shape=[jax.ShapeDtypeStruct((T, MW), BF16), jax.ShapeDtypeStruct((NM, 2 * MW), F32)],
        compiler_params=_cparams("arbitrary"),
    )(q, kv, dy)


def _silu_parts(z):
    sg = _sigmoid(z)
    return z * sg, sg * (1.0 + z * (1.0 - sg))


def _gate_fwd(a, z, name):
    return _elementwise(lambda av, zv: (av * _silu_parts(zv)[0],), [a, z], [], [BF16], name)[0]


def _gate_bwd(a, z, d, name):
    def fn(av, zv, dv):
        s, ds = _silu_parts(zv)
        return dv * s, dv * av * ds
    return _elementwise(fn, [a, z, d], [], [F32, BF16], name)


def _gate_norm_fwd(y, z, g, groups, name):
    T, C = y.shape
    gw = C // groups
    tr = _tile(T, 128, 8)

    def body(y_ref, z_ref, g_ref, o_ref):
        for k in range(groups):
            sl = slice(k * gw, (k + 1) * gw)
            u = y_ref[:, sl] * _silu_parts(z_ref[:, sl])[0]
            r = lax.rsqrt(jnp.mean(u * u, axis=-1, keepdims=True) + EPS)
            o_ref[:, sl] = (u * r * g_ref[:, sl]).astype(BF16)

    row = pl.BlockSpec((tr, C), lambda i: (i, 0))
    return pl.pallas_call(
        body, name=name, grid=(T // tr,), in_specs=[row, row, pl.BlockSpec((1, C), lambda i: (0, 0))],
        out_specs=row, out_shape=jax.ShapeDtypeStruct((T, C), BF16),
        compiler_params=_cparams("parallel"),
    )(y, z, g.reshape(1, C))


def _gate_norm_bwd(y, z, g, d, groups, name):
    T, C = y.shape
    gw = C // groups
    tr = _tile(T, 128, 8)

    def body(y_ref, z_ref, g_ref, d_ref, dy_ref, dz_ref, dg_ref):
        @pl.when(pl.program_id(0) == 0)
        def _():
            dg_ref[...] = jnp.zeros_like(dg_ref)

        for k in range(groups):
            sl = slice(k * gw, (k + 1) * gw)
            yv = y_ref[:, sl]
            s, ds = _silu_parts(z_ref[:, sl])
            u = yv * s
            r = lax.rsqrt(jnp.mean(u * u, axis=-1, keepdims=True) + EPS)
            uhat = u * r
            dv = d_ref[:, sl]
            dg_ref[:, sl] += jnp.sum(dv * uhat, axis=0, keepdims=True)
            duhat = dv * g_ref[:, sl]
            du = r * (duhat - uhat * jnp.mean(duhat * uhat, axis=-1, keepdims=True))
            dy_ref[:, sl] = du * s
            dz_ref[:, sl] = (du * yv * ds).astype(BF16)

    row = pl.BlockSpec((tr, C), lambda i: (i, 0))
    vec = pl.BlockSpec((1, C), lambda i: (0, 0))
    dy, dz, dg = pl.pallas_call(
        body, name=name, grid=(T // tr,), in_specs=[row, row, vec, row], out_specs=[row, row, vec],
        out_shape=[jax.ShapeDtypeStruct((T, C), F32), jax.ShapeDtypeStruct((T, C), BF16),
                   jax.ShapeDtypeStruct((1, C), F32)],
        compiler_params=_cparams("arbitrary"),
    )(y, z, g.reshape(1, C), d)
    return dy, dz, dg.reshape(C)


def _ssd_common(raw, bias, alog, Q, HP, HPG):
    P_ = SSD_HEAD_DIM
    dt_in = raw + bias
    dt = jnp.maximum(dt_in, 0.0) + jnp.log(1.0 + jnp.exp(-jnp.abs(dt_in)))
    a = -jnp.exp(alog)
    r_q = lax.broadcasted_iota(jnp.int32, (Q, Q), 0)
    c_q = lax.broadcasted_iota(jnp.int32, (Q, Q), 1)
    causal = r_q >= c_q
    tril = causal.astype(BF16)
    A = _xdot_l(tril, dt * a)
    e_r = lax.broadcasted_iota(jnp.int32, (LANES, HP), 0)
    e_c = lax.broadcasted_iota(jnp.int32, (LANES, HP), 1)
    E = ((e_c >= e_r * P_) & (e_c < (e_r + 1) * P_) & (e_r < HPG)).astype(BF16)
    return dt, a, A, causal, E


def _head_cols(v, vt, j):
    lane = lax.broadcasted_iota(jnp.int32, v.shape, 1)
    sub = lax.broadcasted_iota(jnp.int32, vt.shape, 0)
    col = jnp.sum(jnp.where(lane == j, v, 0.0), axis=-1, keepdims=True)
    row = jnp.sum(jnp.where(sub == j, vt, 0.0), axis=0, keepdims=True)
    return col, row


def _ssd_fwd(act, raw_g, bias_g, alog_g, dsk_g, TOK, name):
    T = act.shape[0]
    G, N, Q, P_ = SSD_GROUPS, SSD_STATE, SSD_CHUNK, SSD_HEAD_DIM
    HP = TOK // G
    HPG = HP // P_
    NC = T // Q

    def body(x_ref, b_ref, c_ref, raw_ref, bias_ref, alog_ref, dsk_ref, y_ref, hp_ref, hT):
        @pl.when(pl.program_id(1) == 0)
        def _():
            hT[...] = jnp.zeros_like(hT)

        xs = x_ref[...]
        Bb = b_ref[...].astype(BF16)
        Cb = c_ref[...].astype(BF16)
        dt, a, A, causal, E = _ssd_common(raw_ref[...], bias_ref[...], alog_ref[...], Q, HP, HPG)
        AT = A.T
        dt_e = _xdot(dt, E)
        A_e = _xdot(A, E)
        dsk_e = _xdot(jnp.broadcast_to(dsk_ref[...], (Q, LANES)), E)
        rows = lax.broadcasted_iota(jnp.int32, (Q, HP), 0)
        cols = lax.broadcasted_iota(jnp.int32, (Q, HP), 1)
        Al_e = jnp.sum(jnp.where(rows == Q - 1, A_e, 0.0), axis=0, keepdims=True)
        xdt = xs * dt_e
        hprev = hT[...]
        hp_ref[...] = hprev
        CB = _dot_nt(Cb, Bb)
        y = _dot(Cb, hprev.astype(BF16)) * jnp.exp(A_e) + dsk_e * xs
        for j in range(HPG):
            a_col, a_row = _head_cols(A, AT, j)
            L = jnp.exp(jnp.where(causal, a_col - a_row, NEG))
            xj = jnp.where((cols >= j * P_) & (cols < (j + 1) * P_), xdt, 0.0)
            y = y + _dot((CB * L).astype(BF16), xj.astype(BF16))
        y_ref[...] = y
        dte = jnp.exp(Al_e - A_e)
        hT[...] = jnp.exp(Al_e) * hprev + _dot(b_ref[...].T.astype(BF16), (xdt * dte).astype(BF16))

    nbx = TOK // N
    par = pl.BlockSpec((None, 1, LANES), lambda g, c: (g, 0, 0))
    return pl.pallas_call(
        body, name=name, grid=(G, NC),
        in_specs=[pl.BlockSpec((Q, HP), lambda g, c: (c, g)),
                  pl.BlockSpec((Q, N), lambda g, c: (c, nbx + g)),
                  pl.BlockSpec((Q, N), lambda g, c: (c, nbx + G + g)),
                  pl.BlockSpec((None, Q, LANES), lambda g, c: (g, c, 0)), par, par, par],
        out_specs=[pl.BlockSpec((Q, HP), lambda g, c: (c, g)),
                   pl.BlockSpec((None, None, N, HP), lambda g, c: (g, c, 0, 0))],
        out_shape=[jax.ShapeDtypeStruct((T, TOK), F32), jax.ShapeDtypeStruct((G, NC, N, HP), F32)],
        scratch_shapes=[pltpu.VMEM((N, HP), F32)],
        compiler_params=_cparams("parallel", "arbitrary"),
    )(act, act, act, raw_g, bias_g, alog_g, dsk_g)


def _ssd_bwd(act, raw_g, bias_g, alog_g, dsk_g, hprev, dy, TOK, name):
    T = act.shape[0]
    G, N, Q, P_ = SSD_GROUPS, SSD_STATE, SSD_CHUNK, SSD_HEAD_DIM
    HP = TOK // G
    HPG = HP // P_
    NC = T // Q

    def body(x_ref, b_ref, c_ref, raw_ref, bias_ref, alog_ref, dsk_ref, hp_ref, dy_ref,
             dx_ref, db_ref, dc_ref, draw_ref, dalog_ref, dbias_ref, ddsk_ref, dHT):
        @pl.when(pl.program_id(1) == 0)
        def _():
            dHT[...] = jnp.zeros_like(dHT)
            dalog_ref[...] = jnp.zeros_like(dalog_ref)
            dbias_ref[...] = jnp.zeros_like(dbias_ref)
            ddsk_ref[...] = jnp.zeros_like(ddsk_ref)

        xs = x_ref[...]
        dyv = dy_ref[...]
        Bm = b_ref[...]
        Cm = c_ref[...]
        Bb = Bm.astype(BF16)
        Cb = Cm.astype(BF16)
        raw_in = raw_ref[...] + bias_ref[...]
        dt, a, A, causal, E = _ssd_common(raw_ref[...], bias_ref[...], alog_ref[...], Q, HP, HPG)
        AT = A.T
        t_r = lax.broadcasted_iota(jnp.int32, (HP, LANES), 0)
        t_c = lax.broadcasted_iota(jnp.int32, (HP, LANES), 1)
        ET = ((t_r >= t_c * P_) & (t_r < (t_c + 1) * P_) & (t_c < HPG)).astype(BF16)
        dt_e = _xdot(dt, E)
        A_e = _xdot(A, E)
        dsk_e = _xdot(jnp.broadcast_to(dsk_ref[...], (Q, LANES)), E)
        rows = lax.broadcasted_iota(jnp.int32, (Q, HP), 0)
        cols = lax.broadcasted_iota(jnp.int32, (Q, HP), 1)
        last = rows == Q - 1
        Al_e = jnp.sum(jnp.where(last, A_e, 0.0), axis=0, keepdims=True)
        eA = jnp.exp(A_e)
        eAl = jnp.exp(Al_e)
        dte = jnp.exp(Al_e - A_e)
        xdt = xs * dt_e
        xdt_b = xdt.astype(BF16)
        CB = _dot_nt(Cb, Bb)
        HT = hp_ref[...]
        HTb = HT.astype(BF16)
        dH = dHT[...]
        dHb = dH.astype(BF16)
        dys = (dyv * eA).astype(BF16)
        CH = _dot(Cb, HTb)
        dC = _dot_nt(dys, HTb)
        dH_prev = _dot(Cm.T.astype(BF16), dys) + eAl * dH
        dAe = dyv * CH * eA
        dAl = eAl * jnp.sum(dH * HT, axis=0, keepdims=True)
        W = _dot(Bb, dHb)
        dxdt = W * dte
        dd = W * xdt * dte
        dB = _dot_nt((xdt * dte).astype(BF16), dHb)
        dAl = dAl + jnp.sum(dd, axis=0, keepdims=True)
        dAe = dAe - dd + jnp.where(last, dAl, 0.0)
        lane = lax.broadcasted_iota(jnp.int32, (Q, LANES), 1)
        sub = lax.broadcasted_iota(jnp.int32, (LANES, Q), 0)
        dCB = jnp.zeros((Q, Q), F32)
        dA_col = jnp.zeros((Q, LANES), F32)
        dA_row = jnp.zeros((LANES, Q), F32)
        for j in range(HPG):
            a_col, a_row = _head_cols(A, AT, j)
            L = jnp.exp(jnp.where(causal, a_col - a_row, NEG))
            hmask = (cols >= j * P_) & (cols < (j + 1) * P_)
            dyj = jnp.where(hmask, dyv, 0.0).astype(BF16)
            GL = _dot_nt(dyj, xdt_b) * L
            dCB = dCB + GL
            dLL = GL * CB
            dA_col = dA_col + jnp.where(lane == j, jnp.sum(dLL, axis=-1, keepdims=True), 0.0)
            dA_row = dA_row + jnp.where(sub == j, jnp.sum(dLL, axis=0, keepdims=True), 0.0)
            dxdt = dxdt + _dot((CB * L).T.astype(BF16), dyj)
        dC = dC + _dot(dCB.astype(BF16), Bb)
        dB = dB + _dot(dCB.T.astype(BF16), Cb)
        dA = dA_col - dA_row.T + _xdot(dAe, ET)
        triu = (lax.broadcasted_iota(jnp.int32, (Q, Q), 1) >= lax.broadcasted_iota(jnp.int32, (Q, Q), 0)).astype(BF16)
        rcs = _xdot_l(triu, dA)
        ddt = a * rcs + _xdot(dxdt * xs, ET)
        draw = ddt * _sigmoid(raw_in)
        draw_ref[...] = draw
        dalog_ref[...] += jnp.sum(dt * rcs, axis=0, keepdims=True) * a
        dbias_ref[...] += jnp.sum(draw, axis=0, keepdims=True)
        ddsk_ref[...] += jnp.sum(_xdot(dyv * xs, ET), axis=0, keepdims=True)
        dx_ref[...] = dxdt * dt_e + dsk_e * dyv
        db_ref[...] = dB
        dc_ref[...] = dC
        dHT[...] = dH_prev

    nbx = TOK // N
    rv = lambda c: NC - 1 - c
    par = pl.BlockSpec((None, 1, LANES), lambda g, c: (g, 0, 0))
    xsp = pl.BlockSpec((Q, HP), lambda g, c: (rv(c), g))
    outs = pl.pallas_call(
        body, name=name, grid=(G, NC),
        in_specs=[xsp,
                  pl.BlockSpec((Q, N), lambda g, c: (rv(c), nbx + g)),
                  pl.BlockSpec((Q, N), lambda g, c: (rv(c), nbx + G + g)),
                  pl.BlockSpec((None, Q, LANES), lambda g, c: (g, rv(c), 0)), par, par, par,
                  pl.BlockSpec((None, None, N, HP), lambda g, c: (g, rv(c), 0, 0)), xsp],
        out_specs=[xsp, pl.BlockSpec((Q, N), lambda g, c: (rv(c), g)), pl.BlockSpec((Q, N), lambda g, c: (rv(c), g)),
                   pl.BlockSpec((None, Q, LANES), lambda g, c: (g, rv(c), 0)), par, par, par],
        out_shape=[jax.ShapeDtypeStruct((T, TOK), F32), jax.ShapeDtypeStruct((T, G * N), F32),
                   jax.ShapeDtypeStruct((T, G * N), F32), jax.ShapeDtypeStruct((G, T, LANES), F32),
                   jax.ShapeDtypeStruct((G, 1, LANES), F32), jax.ShapeDtypeStruct((G, 1, LANES), F32),
                   jax.ShapeDtypeStruct((G, 1, LANES), F32)],
        scratch_shapes=[pltpu.VMEM((N, HP), F32)],
        compiler_params=_cparams("parallel", "arbitrary"),
    )(act, act, act, raw_g, bias_g, alog_g, dsk_g, hprev, dy)
    return outs


def _heads_per_block(H):
    for hb in (8, 6, 4, 3, 2, 1):
        if H % hb == 0:
            return hb
    return 1


def _alibi_slope(head_index, n_alibi):
    c = -ALIBI_MAX_EXP * math.log(2.0) / n_alibi
    return jnp.exp(jnp.full((1, 1), c, F32) * (head_index + 1).astype(F32))


def _attn_masks(b, nb):
    Bq = ATTN_BLOCK
    iq = lax.broadcasted_iota(jnp.int32, (Bq, Bq), 0)
    ik = lax.broadcasted_iota(jnp.int32, (Bq, Bq), 1)
    rel_c = iq - ik
    rel_p = rel_c + Bq
    mask_c = rel_c >= 0
    mask_p = (rel_p + jnp.where(b > 0, 0, 4 * Bq)) <= Bq
    mask_n = (rel_p + jnp.where(b < nb - 1, 0, 4 * Bq)) <= Bq
    return rel_c.astype(F32), rel_p.astype(F32), mask_c, mask_p, mask_n


def _attn_head_fwd(q, kc, kp, vc, vp, slope, masks, scale):
    relc_f, relp_f, mask_c, mask_p, _ = masks
    s_c = jnp.where(mask_c, _dot_nt(q, kc) * scale - slope * relc_f, NEG)
    s_p = jnp.where(mask_p, _dot_nt(q, kp) * scale - slope * relp_f, NEG)
    m = jnp.maximum(jnp.max(s_c, axis=-1, keepdims=True), jnp.max(s_p, axis=-1, keepdims=True))
    p_c = jnp.exp(s_c - m)
    p_p = jnp.exp(s_p - m)
    den = jnp.sum(p_c, axis=-1, keepdims=True) + jnp.sum(p_p, axis=-1, keepdims=True)
    o = _dot(p_c.astype(BF16), vc) + _dot(p_p.astype(BF16), vp)
    return o / den, m + jnp.log(den)


def _attn_head_bwd(q0, q1, kp, k0, vp, v0, do0, do1, y0, y1, lse0, lse1, slope, masks, scale):
    relc_f, relp_f, mask_c, mask_p, mask_n = masks
    delta0 = jnp.sum(do0 * y0, axis=-1, keepdims=True)
    delta1 = jnp.sum(do1 * y1, axis=-1, keepdims=True)
    do0b = do0.astype(BF16)
    do1b = do1.astype(BF16)
    p_cc = jnp.exp(jnp.where(mask_c, _dot_nt(q0, k0) * scale - slope * relc_f, NEG) - lse0)
    ds_cc = p_cc * (_dot_nt(do0b, v0) - delta0)
    p_cp = jnp.exp(jnp.where(mask_p, _dot_nt(q0, kp) * scale - slope * relp_f, NEG) - lse0)
    ds_cp = p_cp * (_dot_nt(do0b, vp) - delta0)
    p_nc = jnp.exp(jnp.where(mask_n, _dot_nt(q1, k0) * scale - slope * relp_f, NEG) - lse1)
    ds_nc = p_nc * (_dot_nt(do1b, v0) - delta1)
    dq = scale * (_dot(ds_cc.astype(BF16), k0) + _dot(ds_cp.astype(BF16), kp))
    dk = scale * (_dot(ds_cc.T.astype(BF16), q0) + _dot(ds_nc.T.astype(BF16), q1))
    dv = _dot(p_cc.T.astype(BF16), do0b) + _dot(p_nc.T.astype(BF16), do1b)
    return dq, dk, dv


def _attn_fwd_strided(q, k, v, gi, d, name):
    T, TOK = q.shape
    E_ = ATTN_HEAD_DIM
    H = TOK // E_
    n_alibi = len(DILATED_GROUPS) * H
    Bq = ATTN_BLOCK
    RB = Bq * d
    nb = T // RB
    scale = E_ ** -0.5

    def body(q_ref, kc_ref, kp_ref, vc_ref, vp_ref, o_ref, l_ref):
        masks = _attn_masks(pl.program_id(1), nb)
        slope = _alibi_slope(gi * H + pl.program_id(0), n_alibi) * float(d)
        for r in range(d):
            rows = pl.ds(r, Bq, stride=d)
            ld = lambda ref: ref[rows, :].astype(BF16)
            o, lse = _attn_head_fwd(ld(q_ref), ld(kc_ref), ld(kp_ref), ld(vc_ref), ld(vp_ref), slope, masks, scale)
            o_ref[rows, :] = o
            l_ref[rows, :] = jnp.broadcast_to(lse, (Bq, E_))

    cur = pl.BlockSpec((RB, E_), lambda h, b: (b, h))
    prev = pl.BlockSpec((RB, E_), lambda h, b: (jnp.maximum(b - 1, 0), h))
    sds = jax.ShapeDtypeStruct((T, TOK), F32)
    return pl.pallas_call(
        body, name=name, grid=(H, nb), in_specs=[cur, cur, prev, cur, prev], out_specs=[cur, cur],
        out_shape=[sds, sds], compiler_params=_cparams("parallel", "parallel"),
    )(q, k, k, v, v)


def _attn_bwd_strided(q, k, v, y, lse, dy, gi, d, name):
    T, TOK = q.shape
    E_ = ATTN_HEAD_DIM
    H = TOK // E_
    n_alibi = len(DILATED_GROUPS) * H
    Bq = ATTN_BLOCK
    RB = Bq * d
    nb = T // RB
    scale = E_ ** -0.5

    def body(q0_ref, q1_ref, kp_ref, k0_ref, vp_ref, v0_ref, do0_ref, do1_ref, y0_ref, y1_ref, l0_ref, l1_ref,
             dq_ref, dk_ref, dv_ref):
        masks = _attn_masks(pl.program_id(1), nb)
        slope = _alibi_slope(gi * H + pl.program_id(0), n_alibi) * float(d)
        for r in range(d):
            rows = pl.ds(r, Bq, stride=d)
            ld = lambda ref: ref[rows, :]
            lb = lambda ref: ref[rows, :].astype(BF16)
            dq, dk, dv = _attn_head_bwd(
                lb(q0_ref), lb(q1_ref), lb(kp_ref), lb(k0_ref), lb(vp_ref), lb(v0_ref), ld(do0_ref), ld(do1_ref),
                ld(y0_ref), ld(y1_ref), jnp.max(ld(l0_ref), axis=-1, keepdims=True),
                jnp.max(ld(l1_ref), axis=-1, keepdims=True), slope, masks, scale)
            dq_ref[rows, :] = dq
            dk_ref[rows, :] = dk
            dv_ref[rows, :] = dv

    cur = pl.BlockSpec((RB, E_), lambda h, b: (b, h))
    prev = pl.BlockSpec((RB, E_), lambda h, b: (jnp.maximum(b - 1, 0), h))
    nxt = pl.BlockSpec((RB, E_), lambda h, b: (jnp.minimum(b + 1, nb - 1), h))
    sds = jax.ShapeDtypeStruct((T, TOK), F32)
    return pl.pallas_call(
        body, name=name, grid=(H, nb),
        in_specs=[cur, nxt, prev, cur, prev, cur, cur, nxt, cur, nxt, cur, nxt],
        out_specs=[cur, cur, cur], out_shape=[sds, sds, sds],
        compiler_params=_cparams("parallel", "parallel"),
    )(q, q, k, k, v, v, dy, dy, y, y, lse, lse)


def _attn_fwd(q, k, v, gi, window, d, name):
    T, TOK = q.shape
    E_ = ATTN_HEAD_DIM
    H = TOK // E_
    n_alibi = len(DILATED_GROUPS) * H
    assert window // d == ATTN_BLOCK and (T // d) % ATTN_BLOCK == 0
    if d > 1:
        return _attn_fwd_strided(q, k, v, gi, d, name)
    n_sub = T // d
    nb = n_sub // ATTN_BLOCK
    HB = _heads_per_block(H)
    NHB = H // HB
    hbw = HB * E_
    scale = E_ ** -0.5
    Bq = ATTN_BLOCK

    def body(q_ref, kc_ref, kp_ref, vc_ref, vp_ref, o_ref, l_ref):
        b = pl.program_id(2)
        hb = pl.program_id(1)
        iq = lax.broadcasted_iota(jnp.int32, (Bq, Bq), 0)
        ik = lax.broadcasted_iota(jnp.int32, (Bq, Bq), 1)
        rel_c = iq - ik
        rel_p = rel_c + Bq
        mask_c = rel_c >= 0
        mask_p = (rel_p + jnp.where(b > 0, 0, 4 * Bq)) <= Bq
        for hh in range(HB):
            sl = slice(hh * E_, (hh + 1) * E_)
            slope = _alibi_slope(gi * H + hb * HB + hh, n_alibi) * float(d)
            qh = q_ref[:, sl].astype(BF16)
            s_c = jnp.where(mask_c, _dot_nt(qh, kc_ref[:, sl].astype(BF16)) * scale - slope * rel_c.astype(F32), NEG)
            s_p = jnp.where(mask_p, _dot_nt(qh, kp_ref[:, sl].astype(BF16)) * scale - slope * rel_p.astype(F32), NEG)
            m = jnp.maximum(jnp.max(s_c, axis=-1, keepdims=True), jnp.max(s_p, axis=-1, keepdims=True))
            p_c = jnp.exp(s_c - m)
            p_p = jnp.exp(s_p - m)
            den = jnp.sum(p_c, axis=-1, keepdims=True) + jnp.sum(p_p, axis=-1, keepdims=True)
            o = _dot(p_c.astype(BF16), vc_ref[:, sl].astype(BF16)) + _dot(p_p.astype(BF16), vp_ref[:, sl].astype(BF16))
            o_ref[:, sl] = o / den
            l_ref[:, sl] = jnp.broadcast_to(m + jnp.log(den), (Bq, E_))

    cur = pl.BlockSpec((Bq, hbw), lambda r, h, b: (b, r * NHB + h))
    prev = pl.BlockSpec((Bq, hbw), lambda r, h, b: (jnp.maximum(b - 1, 0), r * NHB + h))
    view = lambda t: t.reshape(n_sub, d * TOK)
    sds = jax.ShapeDtypeStruct((n_sub, d * TOK), F32)
    o, l = pl.pallas_call(
        body, name=name, grid=(d, NHB, nb), in_specs=[cur, cur, prev, cur, prev], out_specs=[cur, cur],
        out_shape=[sds, sds], compiler_params=_cparams("parallel", "parallel", "parallel"),
    )(view(q), view(k), view(k), view(v), view(v))
    return o.reshape(T, TOK), l.reshape(T, TOK)


def _attn_combine(os_, ls_, name):
    def fn(*v):
        n = len(v) // 2
        o, l = v[:n], v[n:]
        m = l[0]
        for t in l[1:]:
            m = jnp.maximum(m, t)
        e = [jnp.exp(t - m) for t in l]
        den = e[0]
        for t in e[1:]:
            den = den + t
        y = e[0] * o[0]
        for t, u in zip(e[1:], o[1:]):
            y = y + t * u
        return y / den, m + jnp.log(den)
    return _elementwise(fn, list(os_) + list(ls_), [], [F32, F32], name)


def _attn_bwd(q, k, v, y, lse, dy, gi, window, d, name):
    T, TOK = q.shape
    E_ = ATTN_HEAD_DIM
    H = TOK // E_
    n_alibi = len(DILATED_GROUPS) * H
    if d > 1:
        return _attn_bwd_strided(q, k, v, y, lse, dy, gi, d, name)
    n_sub = T // d
    nb = n_sub // ATTN_BLOCK
    HB = _heads_per_block(H)
    NHB = H // HB
    hbw = HB * E_
    scale = E_ ** -0.5
    Bq = ATTN_BLOCK

    def body(q0_ref, q1_ref, kp_ref, k0_ref, vp_ref, v0_ref, do0_ref, do1_ref, y0_ref, y1_ref, l0_ref, l1_ref,
             dq_ref, dk_ref, dv_ref):
        b = pl.program_id(2)
        hb = pl.program_id(1)
        iq = lax.broadcasted_iota(jnp.int32, (Bq, Bq), 0)
        ik = lax.broadcasted_iota(jnp.int32, (Bq, Bq), 1)
        rel_c = iq - ik
        rel_p = rel_c + Bq
        mask_c = rel_c >= 0
        mask_p = (rel_p + jnp.where(b > 0, 0, 4 * Bq)) <= Bq
        mask_n = (rel_p + jnp.where(b < nb - 1, 0, 4 * Bq)) <= Bq
        relc_f = rel_c.astype(F32)
        relp_f = rel_p.astype(F32)
        for hh in range(HB):
            sl = slice(hh * E_, (hh + 1) * E_)
            slope = _alibi_slope(gi * H + hb * HB + hh, n_alibi) * float(d)
            q0 = q0_ref[:, sl].astype(BF16)
            q1 = q1_ref[:, sl].astype(BF16)
            k0 = k0_ref[:, sl].astype(BF16)
            kp = kp_ref[:, sl].astype(BF16)
            v0 = v0_ref[:, sl].astype(BF16)
            vp = vp_ref[:, sl].astype(BF16)
            do0 = do0_ref[:, sl]
            do1 = do1_ref[:, sl]
            lse0 = jnp.max(l0_ref[:, sl], axis=-1, keepdims=True)
            lse1 = jnp.max(l1_ref[:, sl], axis=-1, keepdims=True)
            delta0 = jnp.sum(do0 * y0_ref[:, sl], axis=-1, keepdims=True)
            delta1 = jnp.sum(do1 * y1_ref[:, sl], axis=-1, keepdims=True)
            do0b = do0.astype(BF16)
            do1b = do1.astype(BF16)
            p_cc = jnp.exp(jnp.where(mask_c, _dot_nt(q0, k0) * scale - slope * relc_f, NEG) - lse0)
            ds_cc = p_cc * (_dot_nt(do0b, v0) - delta0)
            p_cp = jnp.exp(jnp.where(mask_p, _dot_nt(q0, kp) * scale - slope * relp_f, NEG) - lse0)
            ds_cp = p_cp * (_dot_nt(do0b, vp) - delta0)
            p_nc = jnp.exp(jnp.where(mask_n, _dot_nt(q1, k0) * scale - slope * relp_f, NEG) - lse1)
            ds_nc = p_nc * (_dot_nt(do1b, v0) - delta1)
            dq_ref[:, sl] = (scale * (_dot(ds_cc.astype(BF16), k0) + _dot(ds_cp.astype(BF16), kp))).astype(BF16)
            dk_ref[:, sl] = (scale * (_dot(ds_cc.T.astype(BF16), q0) + _dot(ds_nc.T.astype(BF16), q1))).astype(BF16)
            dv_ref[:, sl] = (_dot(p_cc.T.astype(BF16), do0b) + _dot(p_nc.T.astype(BF16), do1b)).astype(BF16)

    cur = pl.BlockSpec((Bq, hbw), lambda r, h, b: (b, r * NHB + h))
    prev = pl.BlockSpec((Bq, hbw), lambda r, h, b: (jnp.maximum(b - 1, 0), r * NHB + h))
    nxt = pl.BlockSpec((Bq, hbw), lambda r, h, b: (jnp.minimum(b + 1, nb - 1), r * NHB + h))
    view = lambda t: t.reshape(n_sub, d * TOK)
    sds = jax.ShapeDtypeStruct((n_sub, d * TOK), BF16)
    dq, dk, dv = pl.pallas_call(
        body, name=name, grid=(d, NHB, nb),
        in_specs=[cur, nxt, prev, cur, prev, cur, cur, nxt, cur, nxt, cur, nxt],
        out_specs=[cur, cur, cur], out_shape=[sds, sds, sds],
        compiler_params=_cparams("parallel", "parallel", "parallel"),
    )(view(q), view(q), view(k), view(k), view(v), view(v), view(dy), view(dy), view(y), view(y), view(lse), view(lse))
    return dq.reshape(T, TOK), dk.reshape(T, TOK), dv.reshape(T, TOK)


_FLIPS = {
    "xy": [(1, 0, 0), (0, 1, 0), (1, 1, 0)],
    "c": [(0, 0, 1)],
    "xyc": [(dx, dy, dc) for dx in (0, 1) for dy in (0, 1) for dc in (0, 1) if (dx, dy, dc) != (0, 0, 0)],
}


def _comm_parts(group, srcs, modes, handshake):
    flips = _FLIPS[group]
    F_ = len(flips)
    P_ = F_ + 1
    n = len(srcs)

    def gidx(px, py, pc):
        if group == "xy":
            return 2 * px + py
        if group == "c":
            return pc
        return 4 * px + 2 * py + pc

    def body(*refs):
        src_refs, out_refs = refs[:n], refs[n:2 * n]
        send_sems, recv_sems, loc_sems = refs[2 * n:]
        x, y, c = lax.axis_index("x"), lax.axis_index("y"), lax.axis_index("c")
        me = gidx(x, y, c)
        peers = [(1 - x if dx else x, 1 - y if dy else y, 1 - c if dc else c) for dx, dy, dc in flips]
        if handshake:
            barrier = pltpu.get_barrier_semaphore()
            for peer in peers:
                pl.semaphore_signal(barrier, inc=1, device_id=peer, device_id_type=MESH)
            pl.semaphore_wait(barrier, F_)
        local, remote = [], []
        for i in range(n):
            mode = modes[i]
            if mode != "swap":
                mine = pltpu.make_async_copy(src_refs[i] if mode == "gather" else src_refs[i].at[me],
                                             out_refs[i].at[me], loc_sems.at[i])
                mine.start()
                local.append(mine)
            for f, peer in enumerate(peers):
                cp = pltpu.make_async_remote_copy(
                    src_ref=src_refs[i].at[gidx(*peer)] if mode == "a2a" else src_refs[i],
                    dst_ref=out_refs[i] if mode == "swap" else out_refs[i].at[me],
                    send_sem=send_sems.at[i * F_ + f], recv_sem=recv_sems.at[i * F_ + f],
                    device_id=peer, device_id_type=MESH)
                cp.start()
                remote.append(cp)
        for cp in local:
            cp.wait()
        for cp in remote:
            cp.wait()

    out_shape = []
    for s, mode in zip(srcs, modes):
        assert mode != "swap" or F_ == 1
        shp = (P_,) + tuple(s.shape) if mode == "gather" else tuple(s.shape)
        out_shape.append(jax.ShapeDtypeStruct(shp, s.dtype))
    sems = [pltpu.SemaphoreType.DMA((n * F_,)), pltpu.SemaphoreType.DMA((n * F_,)), pltpu.SemaphoreType.DMA((n,))]
    return body, out_shape, sems


def _comm(name, group, srcs, modes):
    n = len(srcs)
    body, out_shape, sems = _comm_parts(group, srcs, modes, handshake=False)
    anyspec = pl.BlockSpec(memory_space=pl.ANY)
    return pl.pallas_call(body, name=name, in_specs=[anyspec] * n, out_specs=[anyspec] * n, out_shape=out_shape,
                          scratch_shapes=sems)(*srcs)


def _comm_async(name, collective_id, group, srcs, modes):
    body, out_shape, sems = _comm_parts(group, srcs, modes, handshake=True)
    return pl.kernel(body, name=name, out_type=out_shape,
                     mesh=plsc.ScalarSubcoreMesh(axis_name="sequencer", num_cores=1), scratch_types=sems,
                     compiler_params=pltpu.CompilerParams(collective_id=collective_id))(*srcs)


def _dims(D):
    MIX = 2 * D
    MW = MIX // 4
    TOK = MIX - MW
    H = TOK // SSD_HEAD_DIM
    CONV = TOK + 2 * SSD_GROUPS * SSD_STATE
    return dict(MIX=MIX, MW=MW, TOK=TOK, H=H, CONV=CONV)


def _proj_chain(dsegs, wsegs, name):
    acc = None
    for n, (ds, ws) in enumerate(zip(dsegs, wsegs)):
        acc = _mm(ds, ws, "nt", F32, f"{name}_dh{n}", add=acc)
    return acc


def _pad_lanes(a, width=LANES):
    return jnp.pad(a, [(0, 0)] * (a.ndim - 1) + [(0, width - a.shape[-1])])


def _ssd_layer_fwd(x, kv, p, li):
    T, D = x.shape
    dm = _dims(D)
    TOK, MW, H, CONV = dm["TOK"], dm["MW"], dm["H"], dm["CONV"]
    G = SSD_GROUPS
    HPG = H // G
    w = p["w_in"]
    segs = dict(xbc=w[:, :CONV], dt=_pad_lanes(w[:, CONV:CONV + H]), qm=w[:, CONV + H:CONV + H + MW],
                zt=w[:, CONV + H + MW:CONV + H + MW + TOK], zm=w[:, CONV + H + MW + TOK:])
    h = _rms_fwd(x, p["norm_g"], f"l{li}_rms")
    pr = {k: _mm(h, ws, "nn", BF16 if k == "qm" else F32, f"l{li}_in_{k}") for k, ws in segs.items()}
    act = _conv_fwd(pr["xbc"], p["conv_w"], p["conv_b"], f"l{li}_conv")
    raw_g = _pad_lanes(pr["dt"][:, :H].reshape(T, G, HPG).transpose(1, 0, 2))
    hp = lambda a: _pad_lanes(a.reshape(G, 1, HPG))
    bias_g, alog_g, dsk_g = hp(p["dt_bias"]), hp(p["a_log"]), hp(p["d_skip"])
    y, hprev = _ssd_fwd(act, raw_g, bias_g, alog_g, dsk_g, TOK, f"l{li}_ssd")
    ymem = _memattn_fwd(pr["qm"], kv, f"l{li}_mem")
    gt = _gate_norm_fwd(y, pr["zt"], p["ssd_norm_g"], G, f"l{li}_gate_tok")
    gm = _gate_fwd(ymem, pr["zm"], f"l{li}_gate_mem")
    wo = p["w_out"]
    out = _mm(gt, wo[:TOK], "nn", F32, f"l{li}_out_tok", add=x)
    out = _mm(gm, wo[TOK:], "nn", F32, f"l{li}_out_mem", add=out)
    saved = dict(x=x, h=h, pr=pr, act=act, raw_g=raw_g, par=(bias_g, alog_g, dsk_g), y=y, hprev=hprev, ymem=ymem,
                 gt=gt, gm=gm, segs=segs)
    return out, saved


def _ssd_layer_bwd(dout, doutb, kv, p, s, li):
    x = s["x"]
    T, D = x.shape
    dm = _dims(D)
    TOK, MW, H, CONV = dm["TOK"], dm["MW"], dm["H"], dm["CONV"]
    G = SSD_GROUPS
    HPG = H // G
    wo = p["w_out"]
    pr = s["pr"]
    dgt = _mm(doutb, wo[:TOK], "nt", F32, f"l{li}_dgt")
    dgm = _mm(doutb, wo[TOK:], "nt", F32, f"l{li}_dgm")
    dwo = jnp.concatenate([_mm(s["gt"], doutb, "tn", BF16, f"l{li}_dwo_tok"),
                           _mm(s["gm"], doutb, "tn", BF16, f"l{li}_dwo_mem")], axis=0)
    dy, dzt, dng = _gate_norm_bwd(s["y"], pr["zt"], p["ssd_norm_g"], dgt, G, f"l{li}_gate_tok_b")
    dymem, dzm = _gate_bwd(s["ymem"], pr["zm"], dgm, f"l{li}_gate_mem_b")
    dqm, dkv = _memattn_bwd(pr["qm"], kv, dymem, f"l{li}_mem_b")
    bias_g, alog_g, dsk_g = s["par"]
    dxs, dB, dC, draw_g, dalog, dbias, ddsk = _ssd_bwd(s["act"], s["raw_g"], bias_g, alog_g, dsk_g, s["hprev"], dy, TOK,
                                                      f"l{li}_ssd_b")
    dact = jnp.concatenate([dxs, dB, dC], axis=1)
    dpre, dconv_w, dconv_b = _conv_bwd_pre(pr["xbc"], p["conv_w"], p["conv_b"], dact, f"l{li}_conv_b1")
    dxbc = _conv_bwd_in(dpre, p["conv_w"], f"l{li}_conv_b2")
    draw = _pad_lanes(draw_g[:, :, :HPG].transpose(1, 0, 2).reshape(T, H)).astype(BF16)
    dsegs = dict(xbc=dxbc, dt=draw, qm=dqm, zt=dzt, zm=dzm)
    keys = ["xbc", "dt", "qm", "zt", "zm"]
    dh = _proj_chain([dsegs[k] for k in keys], [s["segs"][k] for k in keys], f"l{li}")
    dws = {k: _mm(s["h"], dsegs[k], "tn", BF16, f"l{li}_dwin_{k}") for k in keys}
    dws["dt"] = dws["dt"][:, :H]
    dwin = jnp.concatenate([dws[k] for k in keys], axis=1)
    dx, dxb, dnorm = _rms_bwd(x, p["norm_g"], dh, dout, f"l{li}_rms_b")
    unhead = lambda a: a[:, 0, :HPG].reshape(H)
    grads = dict(norm_g=dnorm, w_in=dwin, conv_w=dconv_w, conv_b=dconv_b, dt_bias=unhead(dbias), a_log=unhead(dalog),
                 d_skip=unhead(ddsk), ssd_norm_g=dng, w_out=dwo)
    return dx, dxb, dkv, grads


def _attn_layer_fwd(x, kv, p, li):
    T, D = x.shape
    dm = _dims(D)
    TOK, MW = dm["TOK"], dm["MW"]
    w = p["w_in"]
    ng = len(DILATED_GROUPS)
    segs = {}
    for g in range(ng):
        for n, nm in enumerate("qkv"):
            c0 = g * 3 * TOK + n * TOK
            segs[f"{nm}{g}"] = w[:, c0:c0 + TOK]
    c0 = ng * 3 * TOK
    segs["qm"] = w[:, c0:c0 + MW]
    segs["zt"] = w[:, c0 + MW:c0 + MW + TOK]
    segs["zm"] = w[:, c0 + MW + TOK:]
    h = _rms_fwd(x, p["norm_g"], f"l{li}_rms")
    dense = {"qm"} | {f"{nm}{g}" for g, (_, d) in enumerate(DILATED_GROUPS) if d == 1 for nm in "qkv"}
    pr = {k: _mm(h, ws, "nn", BF16 if k in dense else F32, f"l{li}_in_{k}") for k, ws in segs.items()}
    os_, ls_ = [], []
    for g, (window, d) in enumerate(DILATED_GROUPS):
        o, l = _attn_fwd(pr[f"q{g}"], pr[f"k{g}"], pr[f"v{g}"], g, window, d, f"l{li}_attn{g}")
        os_.append(o)
        ls_.append(l)
    ytok, lse = _attn_combine(os_, ls_, f"l{li}_combine")
    ymem = _memattn_fwd(pr["qm"], kv, f"l{li}_mem")
    gt = _gate_fwd(ytok, pr["zt"], f"l{li}_gate_tok")
    gm = _gate_fwd(ymem, pr["zm"], f"l{li}_gate_mem")
    wo = p["w_out"]
    out = _mm(gt, wo[:TOK], "nn", F32, f"l{li}_out_tok", add=x)
    out = _mm(gm, wo[TOK:], "nn", F32, f"l{li}_out_mem", add=out)
    saved = dict(x=x, h=h, pr=pr, ytok=ytok, lse=lse, ymem=ymem, gt=gt, gm=gm, segs=segs)
    return out, saved


def _attn_layer_bwd(dout, doutb, kv, p, s, li):
    x = s["x"]
    T, D = x.shape
    dm = _dims(D)
    TOK, MW = dm["TOK"], dm["MW"]
    wo = p["w_out"]
    pr = s["pr"]
    dgt = _mm(doutb, wo[:TOK], "nt", F32, f"l{li}_dgt")
    dgm = _mm(doutb, wo[TOK:], "nt", F32, f"l{li}_dgm")
    dwo = jnp.concatenate([_mm(s["gt"], doutb, "tn", BF16, f"l{li}_dwo_tok"),
                           _mm(s["gm"], doutb, "tn", BF16, f"l{li}_dwo_mem")], axis=0)
    dytok, dzt = _gate_bwd(s["ytok"], pr["zt"], dgt, f"l{li}_gate_tok_b")
    dymem, dzm = _gate_bwd(s["ymem"], pr["zm"], dgm, f"l{li}_gate_mem_b")
    dqm, dkv = _memattn_bwd(pr["qm"], kv, dymem, f"l{li}_mem_b")
    dsegs = {}
    for g, (window, d) in enumerate(DILATED_GROUPS):
        dq, dk, dv = _attn_bwd(pr[f"q{g}"], pr[f"k{g}"], pr[f"v{g}"], s["ytok"], s["lse"], dytok, g, window, d,
                               f"l{li}_attn{g}_b")
        dsegs[f"q{g}"], dsegs[f"k{g}"], dsegs[f"v{g}"] = dq, dk, dv
    dsegs["qm"], dsegs["zt"], dsegs["zm"] = dqm, dzt, dzm
    keys = list(s["segs"].keys())
    dh = _proj_chain([dsegs[k] for k in keys], [s["segs"][k] for k in keys], f"l{li}")
    dwin = jnp.concatenate([_mm(s["h"], dsegs[k], "tn", BF16, f"l{li}_dwin_{k}") for k in keys], axis=1)
    dx, dxb, dnorm = _rms_bwd(x, p["norm_g"], dh, dout, f"l{li}_rms_b")
    return dx, dxb, dkv, dict(norm_g=dnorm, w_in=dwin, w_out=dwo)


def _local_step(x, mem, tgt, mem_norm_g, final_norm_g, layers, on_layer_grads):
    mem_n = _rms_fwd(mem, mem_norm_g, "mem_rms")
    kvs = [_mm(mem_n, p["w_mem_kv"], "nn", F32, f"l{li}_kv") for li, p in enumerate(layers)]
    saved = []
    for li, p in enumerate(layers):
        fwd = _ssd_layer_fwd if li % 2 == 0 else _attn_layer_fwd
        x, s = fwd(x, kvs[li], p, li)
        saved.append(s)
    loss, dx, dxb, dfinal = _loss_head(x, final_norm_g, tgt, "loss_head")
    dmem_n = None
    for li in reversed(range(len(layers))):
        p = layers[li]
        bwd = _ssd_layer_bwd if li % 2 == 0 else _attn_layer_bwd
        dx, dxb, dkv, g = bwd(dx, dxb, kvs[li], p, saved[li], li)
        g["w_mem_kv"] = _mm(mem_n, dkv, "tn", BF16, f"l{li}_dwkv")
        dmem_n = _mm(dkv, p["w_mem_kv"], "nt", F32, f"l{li}_dmem", add=dmem_n)
        on_layer_grads(li, g)
    _, _, dmem_g = _rms_bwd(mem, mem_norm_g, dmem_n, None, "mem_rms_b")
    return loss, dx, dmem_g, dfinal


_SSD_SMALL = ["norm_g", "conv_w", "conv_b", "dt_bias", "a_log", "d_skip", "ssd_norm_g"]
_ATTN_SMALL = ["norm_g"]
_SSD_ORDER = ["norm_g", "w_in", "conv_w", "conv_b", "dt_bias", "a_log", "d_skip", "ssd_norm_g", "w_mem_kv", "w_out"]
_ATTN_ORDER = ["norm_g", "w_in", "w_mem_kv", "w_out"]


def _pack(arrs):
    flat = jnp.concatenate([a.reshape(-1).astype(F32) for a in arrs])
    n = flat.shape[0]
    pad = (-n) % (8 * LANES)
    return jnp.pad(flat, (0, pad)).reshape(-1, LANES)


def _unpack(mat, shapes):
    flat = mat.reshape(-1)
    out, o = [], 0
    for shp in shapes:
        n = math.prod(shp)
        out.append(flat[o:o + n].reshape(shp))
        o += n
    return out


def kernel(x, mem, mem_norm_g, final_norm_g, norm_g_0, w_in_0, conv_w_0, conv_b_0, dt_bias_0, a_log_0, d_skip_0, ssd_norm_g_0, w_mem_kv_0, w_out_0, norm_g_1, w_in_1, w_mem_kv_1, w_out_1, norm_g_2, w_in_2, conv_w_2, conv_b_2, dt_bias_2, a_log_2, d_skip_2, ssd_norm_g_2, w_mem_kv_2, w_out_2, norm_g_3, w_in_3, w_mem_kv_3, w_out_3, loss_target, m_mem_norm_g, m_final_norm_g, m_norm_g_0, m_w_in_0, m_conv_w_0, m_conv_b_0, m_dt_bias_0, m_a_log_0, m_d_skip_0, m_ssd_norm_g_0, m_w_mem_kv_0, m_w_out_0, m_norm_g_1, m_w_in_1, m_w_mem_kv_1, m_w_out_1, m_norm_g_2, m_w_in_2, m_conv_w_2, m_conv_b_2, m_dt_bias_2, m_a_log_2, m_d_skip_2, m_ssd_norm_g_2, m_w_mem_kv_2, m_w_out_2, m_norm_g_3, m_w_in_3, m_w_mem_kv_3, m_w_out_3, v_mem_norm_g, v_final_norm_g, v_norm_g_0, v_w_in_0, v_conv_w_0, v_conv_b_0, v_dt_bias_0, v_a_log_0, v_d_skip_0, v_ssd_norm_g_0, v_w_mem_kv_0, v_w_out_0, v_norm_g_1, v_w_in_1, v_w_mem_kv_1, v_w_out_1, v_norm_g_2, v_w_in_2, v_conv_w_2, v_conv_b_2, v_dt_bias_2, v_a_log_2, v_d_skip_2, v_ssd_norm_g_2, v_w_mem_kv_2, v_w_out_2, v_norm_g_3, v_w_in_3, v_w_mem_kv_3, v_w_out_3):
    a = dict(locals())
    names = ["mem_norm_g", "final_norm_g"]
    for li in range(DEPTH):
        names += [f"{k}_{li}" for k in (_SSD_ORDER if li % 2 == 0 else _ATTN_ORDER)]
    W = {n: a[n] for n in names}
    Mo = {n: a["m_" + n] for n in names}
    Vo = {n: a["v_" + n] for n in names}
    NX = 4
    chip = 2 * lax.axis_index("x") + lax.axis_index("y")

    layers = []
    for li in range(DEPTH):
        ssd = li % 2 == 0
        srcs = [W[f"w_in_{li}"].astype(BF16), W[f"w_mem_kv_{li}"].astype(BF16), W[f"w_out_{li}"].astype(BF16)]
        if ssd:
            srcs.append(W[f"conv_w_{li}"])
        got = _comm_async(f"gather_w{li}", li, "xy", srcs, ["gather"] * len(srcs))
        rows = lambda g: g.reshape((-1,) + g.shape[2:])
        colcat = lambda g: g.transpose(1, 0, 2).reshape(g.shape[1], -1)
        p = dict(w_in=colcat(got[0]), w_mem_kv=rows(got[1]), w_out=rows(got[2]), norm_g=W[f"norm_g_{li}"])
        if ssd:
            p.update(conv_w=colcat(got[3]), conv_b=W[f"conv_b_{li}"], dt_bias=W[f"dt_bias_{li}"], a_log=W[f"a_log_{li}"],
                     d_skip=W[f"d_skip_{li}"], ssd_norm_g=W[f"ssd_norm_g_{li}"])
        layers.append(p)

    G, Dl, Mn, Vn = {}, {}, {}, {}
    grads = [None] * DEPTH

    def on_layer_grads(li, g):
        grads[li] = g
        dwin = g["w_in"]
        Dm, N = dwin.shape
        chunks = [dwin.reshape(Dm, NX, N // NX).transpose(1, 0, 2),
                  g["w_mem_kv"].reshape((NX, -1) + g["w_mem_kv"].shape[1:]),
                  g["w_out"].reshape((NX, -1) + g["w_out"].shape[1:])]
        got = _comm_async(f"xchg_g{li}", DEPTH + li, "xy", chunks, ["a2a"] * 3)
        parts = [_sum_lead(t, f"l{li}_gsum{n}") for n, t in enumerate(got)]
        theirs = _comm_async(f"swap_g{li}", 2 * DEPTH + li, "c", parts, ["swap"] * 3)
        for nm, mine, other in zip(["w_in", "w_mem_kv", "w_out"], parts, theirs):
            key = f"{nm}_{li}"
            G[key], Dl[key], Mn[key], Vn[key] = _adamw(W[key], Mo[key], Vo[key], [mine, other], f"adamw_{key}")

    loss_l, dx, dmem_g, dfinal = _local_step(x[0], mem[0], loss_target[0], W["mem_norm_g"], W["final_norm_g"], layers,
                                             on_layer_grads)
    loss = lax.psum(loss_l, ("x", "y", "c"))

    small_names = ["mem_norm_g", "final_norm_g"]
    small_grads = [dmem_g, dfinal]
    for li in range(DEPTH):
        for k in (_SSD_SMALL if li % 2 == 0 else _ATTN_SMALL):
            small_names.append(f"{k}_{li}")
            small_grads.append(grads[li][k])
    shapes = [tuple(t.shape) for t in small_grads]
    allg = _comm("gather_small", "xyc", [_pack(small_grads)], ["gather"])[0]
    gsum = _unpack(_sum_lead(allg, "small_gsum"), shapes)
    small_w, small_m, small_v, small_g = [], [], [], []
    for nme, gv in zip(small_names, gsum):
        if nme.startswith("conv_w"):
            cw = W[nme].shape[1]
            gv = lax.dynamic_slice_in_dim(gv, chip * cw, cw, axis=1)
        small_g.append(gv)
        small_w.append(W[nme])
        small_m.append(Mo[nme])
        small_v.append(Vo[nme])
    sshapes = [tuple(t.shape) for t in small_g]
    res = _adamw(_pack(small_w), _pack(small_m), _pack(small_v), [_pack(small_g)], "adamw_small")
    for dst, mat in zip((G, Dl, Mn, Vn), res):
        for nme, t in zip(small_names, _unpack(mat, sshapes)):
            dst[nme] = t

    return (loss, dx[None], *[G[n] for n in names], *[Dl[n] for n in names], *[Mn[n] for n in names],
            *[Vn[n] for n in names])
```

```python
import functools
import math

import jax
import jax.numpy as jnp
from jax import lax
from jax.experimental import pallas as pl
from jax.experimental.pallas import tpu as pltpu
from jax.experimental.pallas import tpu_sc as plsc

F32 = jnp.float32
BF16 = jnp.bfloat16

EPS = 1e-6
MEM_HEADS = 4
SSD_HEAD_DIM = 64
SSD_GROUPS = 8
SSD_STATE = 128
SSD_CONV = 4
SSD_CHUNK = 128
ATTN_HEAD_DIM = 128
ATTN_BLOCK = 128
DILATED_GROUPS = ((128, 1), (512, 4), (2048, 16))
ALIBI_MAX_EXP = 8.0
DEPTH = 4

ADAM_LR = 0.001
ADAM_B1 = 0.9
ADAM_B2 = 0.999
ADAM_EPS = 1e-08
ADAM_WD = 0.01
ADAM_STEP = 10

LANES = 128
VMEM_LIMIT_BYTES = 48 * 1024 * 1024
NEG = -1e30
MESH = pl.DeviceIdType.MESH


def _cparams(*sem):
    return pltpu.CompilerParams(dimension_semantics=tuple(sem), vmem_limit_bytes=VMEM_LIMIT_BYTES)


def _tile(dim, pref, unit=LANES):
    if dim <= pref:
        return dim
    t = (pref // unit) * unit
    while t >= unit:
        if dim % t == 0:
            return t
        t -= unit
    return dim


def _ew_tiles(rows, cols):
    tc = cols if (cols % LANES != 0 or cols <= 2048) else _tile(cols, 2048)
    tr = rows
    while tr * tc > 256 * 1024 and tr % 2 == 0 and (tr // 2) % 8 == 0:
        tr //= 2
    return tr, tc


def _sigmoid(v):
    return 1.0 / (1.0 + jnp.exp(-v))


def _dot(a, b):
    return jnp.dot(a, b, preferred_element_type=F32)


def _dot_nt(a, b):
    return lax.dot_general(a, b, (((1,), (1,)), ((), ())), preferred_element_type=F32)


def _dot_tn(a, b):
    return lax.dot_general(a, b, (((0,), (0,)), ((), ())), preferred_element_type=F32)


def _split3(v):
    hi = v.astype(BF16)
    r = v - hi.astype(F32)
    mid = r.astype(BF16)
    lo = (r - mid.astype(F32)).astype(BF16)
    return hi, mid, lo


def _xdot(v, onehot):
    hi, mid, lo = _split3(v)
    return _dot(hi, onehot) + _dot(mid, onehot) + _dot(lo, onehot)


def _xdot_l(onehot, v):
    hi, mid, lo = _split3(v)
    return _dot(onehot, hi) + _dot(onehot, mid) + _dot(onehot, lo)


def _mm(a, b, mode, out_dtype, name, add=None):
    if mode == "nn":
        M, K = a.shape
        N = b.shape[1]
    elif mode == "nt":
        M, K = a.shape
        N = b.shape[0]
    else:
        K, M = a.shape
        N = b.shape[1]
    tm, tn, tk = (2048, 1024, 1024) if mode == "tn" else (512, 1024, 3072)
    tm, tn, tk = _tile(M, tm, 8 if M < LANES else LANES), _tile(N, tn), _tile(K, tk)
    nk = K // tk
    has_add = add is not None

    def product(a_ref, b_ref):
        av = a_ref[...].astype(BF16)
        bv = b_ref[...].astype(BF16)
        if mode == "nn":
            return _dot(av, bv)
        if mode == "nt":
            return _dot_nt(av, bv)
        return _dot_tn(av, bv)

    def body(*refs):
        a_ref, b_ref = refs[:2]
        add_ref = refs[2] if has_add else None
        o_ref = refs[3] if has_add else refs[2]

        def finish(r):
            if has_add:
                r = r + add_ref[...]
            o_ref[...] = r.astype(out_dtype)

        if nk == 1:
            finish(product(a_ref, b_ref))
            return
        acc = refs[-1]
        k = pl.program_id(2)

        @pl.when(k == 0)
        def _():
            acc[...] = product(a_ref, b_ref)

        @pl.when((k > 0) & (k < nk - 1))
        def _():
            acc[...] += product(a_ref, b_ref)

        @pl.when(k == nk - 1)
        def _():
            finish(acc[...] + product(a_ref, b_ref))

    if mode == "nn":
        a_spec = pl.BlockSpec((tm, tk), lambda j, i, k: (i, k))
        b_spec = pl.BlockSpec((tk, tn), lambda j, i, k: (k, j))
    elif mode == "nt":
        a_spec = pl.BlockSpec((tm, tk), lambda j, i, k: (i, k))
        b_spec = pl.BlockSpec((tn, tk), lambda j, i, k: (j, k))
    else:
        a_spec = pl.BlockSpec((tk, tm), lambda j, i, k: (k, i))
        b_spec = pl.BlockSpec((tk, tn), lambda j, i, k: (k, j))
    o_spec = pl.BlockSpec((tm, tn), lambda j, i, k: (i, j))
    in_specs = [a_spec, b_spec] + ([o_spec] if has_add else [])
    args = (a, b) + ((add,) if has_add else ())
    return pl.pallas_call(
        body, name=name, grid=(N // tn, M // tm, nk), in_specs=in_specs, out_specs=o_spec,
        out_shape=jax.ShapeDtypeStruct((M, N), out_dtype),
        scratch_shapes=[pltpu.VMEM((tm, tn), F32)] if nk > 1 else [],
        compiler_params=_cparams("parallel", "parallel", "arbitrary"),
    )(*args)


def _rms_fwd(x, g, name):
    R, Dm = x.shape
    tr = _tile(R, 256, 8)

    def body(x_ref, g_ref, o_ref):
        xv = x_ref[...]
        r = lax.rsqrt(jnp.mean(xv * xv, axis=-1, keepdims=True) + EPS)
        o_ref[...] = (xv * r * g_ref[...]).astype(BF16)

    return pl.pallas_call(
        body, name=name, grid=(R // tr,),
        in_specs=[pl.BlockSpec((tr, Dm), lambda i: (i, 0)), pl.BlockSpec((1, Dm), lambda i: (0, 0))],
        out_specs=pl.BlockSpec((tr, Dm), lambda i: (i, 0)),
        out_shape=jax.ShapeDtypeStruct((R, Dm), BF16),
        compiler_params=_cparams("parallel"),
    )(x, g.reshape(1, Dm))


def _rms_bwd(x, g, dh, dres, name):
    R, Dm = x.shape
    tr = _tile(R, 256, 8)
    has_res = dres is not None

    def body(*refs):
        if has_res:
            x_ref, g_ref, dh_ref, dres_ref, dx_ref, dxb_ref, dg_ref = refs
        else:
            x_ref, g_ref, dh_ref, dx_ref, dxb_ref, dg_ref = refs
        xv = x_ref[...]
        r = lax.rsqrt(jnp.mean(xv * xv, axis=-1, keepdims=True) + EPS)
        xhat = xv * r
        dhv = dh_ref[...]
        dxhat = dhv * g_ref[...]
        dx = r * (dxhat - xhat * jnp.mean(dxhat * xhat, axis=-1, keepdims=True))
        if has_res:
            dx = dx + dres_ref[...]
        dx_ref[...] = dx
        dxb_ref[...] = dx.astype(BF16)

        @pl.when(pl.program_id(0) == 0)
        def _():
            dg_ref[...] = jnp.zeros_like(dg_ref)

        dg_ref[...] += jnp.sum(dhv * xhat, axis=0, keepdims=True)

    row = pl.BlockSpec((tr, Dm), lambda i: (i, 0))
    vec = pl.BlockSpec((1, Dm), lambda i: (0, 0))
    in_specs = [row, vec, row] + ([row] if has_res else [])
    args = (x, g.reshape(1, Dm), dh) + ((dres,) if has_res else ())
    dx, dxb, dg = pl.pallas_call(
        body, name=name, grid=(R // tr,), in_specs=in_specs, out_specs=[row, row, vec],
        out_shape=[jax.ShapeDtypeStruct((R, Dm), F32), jax.ShapeDtypeStruct((R, Dm), BF16),
                   jax.ShapeDtypeStruct((1, Dm), F32)],
        compiler_params=_cparams("arbitrary"),
    )(*args)
    return dx, dxb, dg.reshape(Dm)


def _loss_head(x, g, tgt, name):
    R, Dm = x.shape
    tr = _tile(R, 256, 8)

    def body(x_ref, g_ref, t_ref, loss_ref, dx_ref, dxb_ref, dg_ref):
        xv = x_ref[...]
        gv = g_ref[...]
        r = lax.rsqrt(jnp.mean(xv * xv, axis=-1, keepdims=True) + EPS)
        xhat = xv * r
        e = xhat * gv - t_ref[...]
        part = jnp.sum(jnp.mean(e * e, axis=-1, keepdims=True), axis=0, keepdims=True) * 0.5
        dy = e * (1.0 / Dm)
        dxhat = dy * gv
        dx = r * (dxhat - xhat * jnp.mean(dxhat * xhat, axis=-1, keepdims=True))
        dx_ref[...] = dx
        dxb_ref[...] = dx.astype(BF16)

        @pl.when(pl.program_id(0) == 0)
        def _():
            dg_ref[...] = jnp.zeros_like(dg_ref)
            loss_ref[...] = jnp.zeros_like(loss_ref)

        dg_ref[...] += jnp.sum(dy * xhat, axis=0, keepdims=True)
        loss_ref[...] += jnp.broadcast_to(part, loss_ref.shape)

    row = pl.BlockSpec((tr, Dm), lambda i: (i, 0))
    vec = pl.BlockSpec((1, Dm), lambda i: (0, 0))
    lsp = pl.BlockSpec((1, LANES), lambda i: (0, 0))
    loss, dx, dxb, dg = pl.pallas_call(
        body, name=name, grid=(R // tr,), in_specs=[row, vec, row], out_specs=[lsp, row, row, vec],
        out_shape=[jax.ShapeDtypeStruct((1, LANES), F32), jax.ShapeDtypeStruct((R, Dm), F32),
                   jax.ShapeDtypeStruct((R, Dm), BF16), jax.ShapeDtypeStruct((1, Dm), F32)],
        compiler_params=_cparams("arbitrary"),
    )(x, g.reshape(1, Dm), tgt)
    return loss[0, 0], dx, dxb, dg.reshape(Dm)


def _elementwise(fn, mats, vecs, out_dtypes, name):
    R, C = mats[0].shape
    tr, tc = _ew_tiles(R, C)
    nm, nv, no = len(mats), len(vecs), len(out_dtypes)

    def body(*refs):
        ins = [r[...] for r in refs[:nm + nv]]
        outs = fn(*ins)
        for o_ref, o in zip(refs[nm + nv:], outs):
            o_ref[...] = o.astype(o_ref.dtype)

    blk = pl.BlockSpec((tr, tc), lambda i, j: (i, j))
    vblk = pl.BlockSpec((1, tc), lambda i, j: (0, j))
    res = pl.pallas_call(
        body, name=name, grid=(R // tr, C // tc),
        in_specs=[blk] * nm + [vblk] * nv, out_specs=[blk] * no,
        out_shape=[jax.ShapeDtypeStruct((R, C), dt) for dt in out_dtypes],
        compiler_params=_cparams("parallel", "parallel"),
    )(*mats, *[v.reshape(1, C) for v in vecs])
    return res


def _sum_lead(arr, name):
    P_, R, C = arr.shape
    tr, tc = _ew_tiles(R, C)

    def body(a_ref, o_ref):
        s = a_ref[0].astype(F32)
        for p in range(1, P_):
            s = s + a_ref[p].astype(F32)
        o_ref[...] = s

    return pl.pallas_call(
        body, name=name, grid=(R // tr, C // tc),
        in_specs=[pl.BlockSpec((P_, tr, tc), lambda i, j: (0, i, j))],
        out_specs=pl.BlockSpec((tr, tc), lambda i, j: (i, j)),
        out_shape=jax.ShapeDtypeStruct((R, C), F32),
        compiler_params=_cparams("parallel", "parallel"),
    )(arr)


def _adamw(w, m, v, gparts, name):
    P_ = len(gparts)
    R, C = w.shape
    tr, tc = _ew_tiles(R, C)
    c1 = 1.0 / (1.0 - ADAM_B1 ** ADAM_STEP)
    c2 = 1.0 / (1.0 - ADAM_B2 ** ADAM_STEP)

    def body(w_ref, m_ref, v_ref, *rest):
        g_refs, (go_ref, d_ref, mo_ref, vo_ref) = rest[:P_], rest[P_:]
        g = g_refs[0][...]
        for g_ref in g_refs[1:]:
            g = g + g_ref[...]
        mn = ADAM_B1 * m_ref[...] + (1.0 - ADAM_B1) * g
        vn = ADAM_B2 * v_ref[...] + (1.0 - ADAM_B2) * (g * g)
        m_hat = mn * c1
        v_hat = vn * c2
        d_ref[...] = -ADAM_LR * (m_hat / (jnp.sqrt(v_hat) + ADAM_EPS) + ADAM_WD * w_ref[...])
        go_ref[...] = g
        mo_ref[...] = mn
        vo_ref[...] = vn

    blk = pl.BlockSpec((tr, tc), lambda i, j: (i, j))
    sds = jax.ShapeDtypeStruct((R, C), F32)
    return pl.pallas_call(
        body, name=name, grid=(R // tr, C // tc),
        in_specs=[blk] * (3 + P_), out_specs=[blk] * 4, out_shape=[sds] * 4,
        compiler_params=_cparams("parallel", "parallel"),
    )(w, m, v, *gparts)


def _conv_pre(u, up, w_ref, b, first):
    tr = u.shape[0]
    rows = lax.broadcasted_iota(jnp.int32, u.shape, 0)
    keep = 1.0 - first.astype(F32)
    acc = b + w_ref[SSD_CONV - 1:SSD_CONV, :] * u
    shifted = []
    for j in range(1, SSD_CONV):
        su = pltpu.roll(u, j, 0)
        sp = pltpu.roll(up, j, 0) * keep
        sh = jnp.where(rows < j, sp, su)
        shifted.append(sh)
        acc = acc + w_ref[SSD_CONV - 1 - j:SSD_CONV - j, :] * sh
    return acc, shifted


def _conv_fwd(u, w, b, name):
    T, C = u.shape
    tr, tc = _tile(T, 256, 8), _tile(C, 1024)

    def body(u_ref, up_ref, w_ref, b_ref, o_ref):
        pre, _ = _conv_pre(u_ref[...], up_ref[...], w_ref, b_ref[...], pl.program_id(0) == 0)
        o_ref[...] = pre * _sigmoid(pre)

    return pl.pallas_call(
        body, name=name, grid=(T // tr, C // tc),
        in_specs=[pl.BlockSpec((tr, tc), lambda i, j: (i, j)),
                  pl.BlockSpec((tr, tc), lambda i, j: (jnp.maximum(i - 1, 0), j)),
                  pl.BlockSpec((SSD_CONV, tc), lambda i, j: (0, j)),
                  pl.BlockSpec((1, tc), lambda i, j: (0, j))],
        out_specs=pl.BlockSpec((tr, tc), lambda i, j: (i, j)),
        out_shape=jax.ShapeDtypeStruct((T, C), F32),
        compiler_params=_cparams("parallel", "parallel"),
    )(u, u, w, b.reshape(1, C))


def _conv_bwd_pre(u, w, b, dact, name):
    T, C = u.shape
    tr, tc = _tile(T, 256, 8), _tile(C, 1024)

    def body(u_ref, up_ref, w_ref, b_ref, da_ref, dp_ref, dw_ref, db_ref):
        i = pl.program_id(1)
        uv = u_ref[...]
        pre, shifted = _conv_pre(uv, up_ref[...], w_ref, b_ref[...], i == 0)
        sg = _sigmoid(pre)
        dpre = da_ref[...] * (sg * (1.0 + pre * (1.0 - sg)))
        dp_ref[...] = dpre

        @pl.when(i == 0)
        def _():
            dw_ref[...] = jnp.zeros_like(dw_ref)
            db_ref[...] = jnp.zeros_like(db_ref)

        db_ref[...] += jnp.sum(dpre, axis=0, keepdims=True)
        dw_ref[SSD_CONV - 1:SSD_CONV, :] += jnp.sum(dpre * uv, axis=0, keepdims=True)
        for j in range(1, SSD_CONV):
            dw_ref[SSD_CONV - 1 - j:SSD_CONV - j, :] += jnp.sum(dpre * shifted[j - 1], axis=0, keepdims=True)

    blk = pl.BlockSpec((tr, tc), lambda j, i: (i, j))
    dpre, dw, db = pl.pallas_call(
        body, name=name, grid=(C // tc, T // tr),
        in_specs=[blk, pl.BlockSpec((tr, tc), lambda j, i: (jnp.maximum(i - 1, 0), j)),
                  pl.BlockSpec((SSD_CONV, tc), lambda j, i: (0, j)),
                  pl.BlockSpec((1, tc), lambda j, i: (0, j)), blk],
        out_specs=[blk, pl.BlockSpec((SSD_CONV, tc), lambda j, i: (0, j)), pl.BlockSpec((1, tc), lambda j, i: (0, j))],
        out_shape=[jax.ShapeDtypeStruct((T, C), F32), jax.ShapeDtypeStruct((SSD_CONV, C), F32),
                   jax.ShapeDtypeStruct((1, C), F32)],
        compiler_params=_cparams("parallel", "arbitrary"),
    )(u, u, w, b.reshape(1, C), dact)
    return dpre, dw, db.reshape(C)


def _conv_bwd_in(dpre, w, name):
    T, C = dpre.shape
    tr, tc = _tile(T, 256, 8), _tile(C, 1024)
    nb = T // tr

    def body(d_ref, dn_ref, w_ref, o_ref):
        d = d_ref[...]
        keep = 1.0 - (pl.program_id(0) == nb - 1).astype(F32)
        dn = dn_ref[...] * keep
        rows = lax.broadcasted_iota(jnp.int32, d.shape, 0)
        acc = w_ref[SSD_CONV - 1:SSD_CONV, :] * d
        for j in range(1, SSD_CONV):
            sd = pltpu.roll(d, tr - j, 0)
            sn = pltpu.roll(dn, tr - j, 0)
            acc = acc + w_ref[SSD_CONV - 1 - j:SSD_CONV - j, :] * jnp.where(rows >= tr - j, sn, sd)
        o_ref[...] = acc.astype(BF16)

    return pl.pallas_call(
        body, name=name, grid=(nb, C // tc),
        in_specs=[pl.BlockSpec((tr, tc), lambda i, j: (i, j)),
                  pl.BlockSpec((tr, tc), lambda i, j: (jnp.minimum(i + 1, nb - 1), j)),
                  pl.BlockSpec((SSD_CONV, tc), lambda i, j: (0, j))],
        out_specs=pl.BlockSpec((tr, tc), lambda i, j: (i, j)),
        out_shape=jax.ShapeDtypeStruct((T, C), BF16),
        compiler_params=_cparams("parallel", "parallel"),
    )(dpre, dpre, w)


def _mem_probs(qh, kh, scale):
    s = _dot_nt(qh.astype(BF16), kh.astype(BF16)) * scale
    m = jnp.max(s, axis=-1, keepdims=True)
    p = jnp.exp(s - m)
    return p / jnp.sum(p, axis=-1, keepdims=True)


def _memattn_fwd(q, kv, name):
    T, MW = q.shape
    NM = kv.shape[0]
    hd = MW // MEM_HEADS
    scale = hd ** -0.5
    tq = _tile(T, 512, 8)

    def body(q_ref, kv_ref, o_ref):
        for h in range(MEM_HEADS):
            sl = slice(h * hd, (h + 1) * hd)
            p = _mem_probs(q_ref[:, sl], kv_ref[:, sl], scale)
            vh = kv_ref[:, MW + h * hd:MW + (h + 1) * hd]
            o_ref[:, sl] = _dot(p.astype(BF16), vh.astype(BF16))

    return pl.pallas_call(
        body, name=name, grid=(T // tq,),
        in_specs=[pl.BlockSpec((tq, MW), lambda i: (i, 0)), pl.BlockSpec((NM, 2 * MW), lambda i: (0, 0))],
        out_specs=pl.BlockSpec((tq, MW), lambda i: (i, 0)),
        out_shape=jax.ShapeDtypeStruct((T, MW), F32),
        compiler_params=_cparams("parallel"),
    )(q, kv)


def _memattn_bwd(q, kv, dy, name):
    T, MW = q.shape
    NM = kv.shape[0]
    hd = MW // MEM_HEADS
    scale = hd ** -0.5
    tq = _tile(T, 512, 8)

    def body(q_ref, kv_ref, dy_ref, dq_ref, dkv_ref):
        @pl.when(pl.program_id(0) == 0)
        def _():
            dkv_ref[...] = jnp.zeros_like(dkv_ref)

        for h in range(MEM_HEADS):
            sl = slice(h * hd, (h + 1) * hd)
            vsl = slice(MW + h * hd, MW + (h + 1) * hd)
            qh = q_ref[:, sl]
            kh = kv_ref[:, sl]
            vh = kv_ref[:, vsl]
            dyh = dy_ref[:, sl].astype(BF16)
            p = _mem_probs(qh, kh, scale)
            dp = _dot_nt(dyh, vh.astype(BF16))
            ds = p * (dp - jnp.sum(dp * p, axis=-1, keepdims=True)) * scale
            dq_ref[:, sl] = _dot(ds.astype(BF16), kh.astype(BF16)).astype(BF16)
            dkv_ref[:, sl] += _dot(ds.T.astype(BF16), qh.astype(BF16))
            dkv_ref[:, vsl] += _dot(p.T.astype(BF16), dyh)

    return pl.pallas_call(
        body, name=name, grid=(T // tq,),
        in_specs=[pl.BlockSpec((tq, MW), lambda i: (i, 0)), pl.BlockSpec((NM, 2 * MW), lambda i: (0, 0)),
                  pl.BlockSpec((tq, MW), lambda i: (i, 0))],
        out_specs=[pl.BlockSpec((tq, MW), lambda i: (i, 0)), pl.BlockSpec((NM, 2 * MW), lambda i: (0, 0))],
        out_shape=[jax.ShapeDtypeStruct((T, MW), BF16), jax.ShapeDtypeStruct((NM, 2 * MW), F32)],
        compiler_params=_cparams("arbitrary"),
    )(q, kv, dy)


def _silu_parts(z):
    sg = _sigmoid(z)
    return z * sg, sg * (1.0 + z * (1.0 - sg))


def _gate_fwd(a, z, name):
    return _elementwise(lambda av, zv: (av * _silu_parts(zv)[0],), [a, z], [], [BF16], name)[0]


def _gate_bwd(a, z, d, name):
    def fn(av, zv, dv):
        s, ds = _silu_parts(zv)
        return dv * s, dv * av * ds
    return _elementwise(fn, [a, z, d], [], [F32, BF16], name)


def _gate_norm_fwd(y, z, g, groups, name):
    T, C = y.shape
    gw = C // groups
    tr = _tile(T, 128, 8)

    def body(y_ref, z_ref, g_ref, o_ref):
        for k in range(groups):
            sl = slice(k * gw, (k + 1) * gw)
            u = y_ref[:, sl] * _silu_parts(z_ref[:, sl])[0]
            r = lax.rsqrt(jnp.mean(u * u, axis=-1, keepdims=True) + EPS)
            o_ref[:, sl] = (u * r * g_ref[:, sl]).astype(BF16)

    row = pl.BlockSpec((tr, C), lambda i: (i, 0))
    return pl.pallas_call(
        body, name=name, grid=(T // tr,), in_specs=[row, row, pl.BlockSpec((1, C), lambda i: (0, 0))],
        out_specs=row, out_shape=jax.ShapeDtypeStruct((T, C), BF16),
        compiler_params=_cparams("parallel"),
    )(y, z, g.reshape(1, C))


def _gate_norm_bwd(y, z, g, d, groups, name):
    T, C = y.shape
    gw = C // groups
    tr = _tile(T, 128, 8)

    def body(y_ref, z_ref, g_ref, d_ref, dy_ref, dz_ref, dg_ref):
        @pl.when(pl.program_id(0) == 0)
        def _():
            dg_ref[...] = jnp.zeros_like(dg_ref)

        for k in range(groups):
            sl = slice(k * gw, (k + 1) * gw)
            yv = y_ref[:, sl]
            s, ds = _silu_parts(z_ref[:, sl])
            u = yv * s
            r = lax.rsqrt(jnp.mean(u * u, axis=-1, keepdims=True) + EPS)
            uhat = u * r
            dv = d_ref[:, sl]
            dg_ref[:, sl] += jnp.sum(dv * uhat, axis=0, keepdims=True)
            duhat = dv * g_ref[:, sl]
            du = r * (duhat - uhat * jnp.mean(duhat * uhat, axis=-1, keepdims=True))
            dy_ref[:, sl] = du * s
            dz_ref[:, sl] = (du * yv * ds).astype(BF16)

    row = pl.BlockSpec((tr, C), lambda i: (i, 0))
    vec = pl.BlockSpec((1, C), lambda i: (0, 0))
    dy, dz, dg = pl.pallas_call(
        body, name=name, grid=(T // tr,), in_specs=[row, row, vec, row], out_specs=[row, row, vec],
        out_shape=[jax.ShapeDtypeStruct((T, C), F32), jax.ShapeDtypeStruct((T, C), BF16),
                   jax.ShapeDtypeStruct((1, C), F32)],
        compiler_params=_cparams("arbitrary"),
    )(y, z, g.reshape(1, C), d)
    return dy, dz, dg.reshape(C)


def _ssd_common(raw, bias, alog, Q, HP, HPG):
    P_ = SSD_HEAD_DIM
    dt_in = raw + bias
    dt = jnp.maximum(dt_in, 0.0) + jnp.log(1.0 + jnp.exp(-jnp.abs(dt_in)))
    a = -jnp.exp(alog)
    r_q = lax.broadcasted_iota(jnp.int32, (Q, Q), 0)
    c_q = lax.broadcasted_iota(jnp.int32, (Q, Q), 1)
    causal = r_q >= c_q
    tril = causal.astype(BF16)
    A = _xdot_l(tril, dt * a)
    e_r = lax.broadcasted_iota(jnp.int32, (LANES, HP), 0)
    e_c = lax.broadcasted_iota(jnp.int32, (LANES, HP), 1)
    E = ((e_c >= e_r * P_) & (e_c < (e_r + 1) * P_) & (e_r < HPG)).astype(BF16)
    return dt, a, A, causal, E


def _head_cols(v, vt, j):
    lane = lax.broadcasted_iota(jnp.int32, v.shape, 1)
    sub = lax.broadcasted_iota(jnp.int32, vt.shape, 0)
    col = jnp.sum(jnp.where(lane == j, v, 0.0), axis=-1, keepdims=True)
    row = jnp.sum(jnp.where(sub == j, vt, 0.0), axis=0, keepdims=True)
    return col, row


def _ssd_fwd(act, raw_g, bias_g, alog_g, dsk_g, TOK, name):
    T = act.shape[0]
    G, N, Q, P_ = SSD_GROUPS, SSD_STATE, SSD_CHUNK, SSD_HEAD_DIM
    HP = TOK // G
    HPG = HP // P_
    NC = T // Q

    def body(x_ref, b_ref, c_ref, raw_ref, bias_ref, alog_ref, dsk_ref, y_ref, hp_ref, hT):
        @pl.when(pl.program_id(1) == 0)
        def _():
            hT[...] = jnp.zeros_like(hT)

        xs = x_ref[...]
        Bb = b_ref[...].astype(BF16)
        Cb = c_ref[...].astype(BF16)
        dt, a, A, causal, E = _ssd_common(raw_ref[...], bias_ref[...], alog_ref[...], Q, HP, HPG)
        AT = A.T
        dt_e = _xdot(dt, E)
        A_e = _xdot(A, E)
        dsk_e = _xdot(jnp.broadcast_to(dsk_ref[...], (Q, LANES)), E)
        rows = lax.broadcasted_iota(jnp.int32, (Q, HP), 0)
        cols = lax.broadcasted_iota(jnp.int32, (Q, HP), 1)
        Al_e = jnp.sum(jnp.where(rows == Q - 1, A_e, 0.0), axis=0, keepdims=True)
        xdt = xs * dt_e
        hprev = hT[...]
        hp_ref[...] = hprev
        CB = _dot_nt(Cb, Bb)
        y = _dot(Cb, hprev.astype(BF16)) * jnp.exp(A_e) + dsk_e * xs
        for j in range(HPG):
            a_col, a_row = _head_cols(A, AT, j)
            L = jnp.exp(jnp.where(causal, a_col - a_row, NEG))
            xj = jnp.where((cols >= j * P_) & (cols < (j + 1) * P_), xdt, 0.0)
            y = y + _dot((CB * L).astype(BF16), xj.astype(BF16))
        y_ref[...] = y
        dte = jnp.exp(Al_e - A_e)
        hT[...] = jnp.exp(Al_e) * hprev + _dot(b_ref[...].T.astype(BF16), (xdt * dte).astype(BF16))

    nbx = TOK // N
    par = pl.BlockSpec((None, 1, LANES), lambda g, c: (g, 0, 0))
    return pl.pallas_call(
        body, name=name, grid=(G, NC),
        in_specs=[pl.BlockSpec((Q, HP), lambda g, c: (c, g)),
                  pl.BlockSpec((Q, N), lambda g, c: (c, nbx + g)),
                  pl.BlockSpec((Q, N), lambda g, c: (c, nbx + G + g)),
                  pl.BlockSpec((None, Q, LANES), lambda g, c: (g, c, 0)), par, par, par],
        out_specs=[pl.BlockSpec((Q, HP), lambda g, c: (c, g)),
                   pl.BlockSpec((None, None, N, HP), lambda g, c: (g, c, 0, 0))],
        out_shape=[jax.ShapeDtypeStruct((T, TOK), F32), jax.ShapeDtypeStruct((G, NC, N, HP), F32)],
        scratch_shapes=[pltpu.VMEM((N, HP), F32)],
        compiler_params=_cparams("parallel", "arbitrary"),
    )(act, act, act, raw_g, bias_g, alog_g, dsk_g)


def _ssd_bwd(act, raw_g, bias_g, alog_g, dsk_g, hprev, dy, TOK, name):
    T = act.shape[0]
    G, N, Q, P_ = SSD_GROUPS, SSD_STATE, SSD_CHUNK, SSD_HEAD_DIM
    HP = TOK // G
    HPG = HP // P_
    NC = T // Q

    def body(x_ref, b_ref, c_ref, raw_ref, bias_ref, alog_ref, dsk_ref, hp_ref, dy_ref,
             dx_ref, db_ref, dc_ref, draw_ref, dalog_ref, dbias_ref, ddsk_ref, dHT):
        @pl.when(pl.program_id(1) == 0)
        def _():
            dHT[...] = jnp.zeros_like(dHT)
            dalog_ref[...] = jnp.zeros_like(dalog_ref)
            dbias_ref[...] = jnp.zeros_like(dbias_ref)
            ddsk_ref[...] = jnp.zeros_like(ddsk_ref)

        xs = x_ref[...]
        dyv = dy_ref[...]
        Bm = b_ref[...]
        Cm = c_ref[...]
        Bb = Bm.astype(BF16)
        Cb = Cm.astype(BF16)
        raw_in = raw_ref[...] + bias_ref[...]
        dt, a, A, causal, E = _ssd_common(raw_ref[...], bias_ref[...], alog_ref[...], Q, HP, HPG)
        AT = A.T
        t_r = lax.broadcasted_iota(jnp.int32, (HP, LANES), 0)
        t_c = lax.broadcasted_iota(jnp.int32, (HP, LANES), 1)
        ET = ((t_r >= t_c * P_) & (t_r < (t_c + 1) * P_) & (t_c < HPG)).astype(BF16)
        dt_e = _xdot(dt, E)
        A_e = _xdot(A, E)
        dsk_e = _xdot(jnp.broadcast_to(dsk_ref[...], (Q, LANES)), E)
        rows = lax.broadcasted_iota(jnp.int32, (Q, HP), 0)
        cols = lax.broadcasted_iota(jnp.int32, (Q, HP), 1)
        last = rows == Q - 1
        Al_e = jnp.sum(jnp.where(last, A_e, 0.0), axis=0, keepdims=True)
        eA = jnp.exp(A_e)
        eAl = jnp.exp(Al_e)
        dte = jnp.exp(Al_e - A_e)
        xdt = xs * dt_e
        xdt_b = xdt.astype(BF16)
        CB = _dot_nt(Cb, Bb)
        HT = hp_ref[...]
        HTb = HT.astype(BF16)
        dH = dHT[...]
        dHb = dH.astype(BF16)
        dys = (dyv * eA).astype(BF16)
        CH = _dot(Cb, HTb)
        dC = _dot_nt(dys, HTb)
        dH_prev = _dot(Cm.T.astype(BF16), dys) + eAl * dH
        dAe = dyv * CH * eA
        dAl = eAl * jnp.sum(dH * HT, axis=0, keepdims=True)
        W = _dot(Bb, dHb)
        dxdt = W * dte
        dd = W * xdt * dte
        dB = _dot_nt((xdt * dte).astype(BF16), dHb)
        dAl = dAl + jnp.sum(dd, axis=0, keepdims=True)
        dAe = dAe - dd + jnp.where(last, dAl, 0.0)
        lane = lax.broadcasted_iota(jnp.int32, (Q, LANES), 1)
        sub = lax.broadcasted_iota(jnp.int32, (LANES, Q), 0)
        dCB = jnp.zeros((Q, Q), F32)
        dA_col = jnp.zeros((Q, LANES), F32)
        dA_row = jnp.zeros((LANES, Q), F32)
        for j in range(HPG):
            a_col, a_row = _head_cols(A, AT, j)
            L = jnp.exp(jnp.where(causal, a_col - a_row, NEG))
            hmask = (cols >= j * P_) & (cols < (j + 1) * P_)
            dyj = jnp.where(hmask, dyv, 0.0).astype(BF16)
            GL = _dot_nt(dyj, xdt_b) * L
            dCB = dCB + GL
            dLL = GL * CB
            dA_col = dA_col + jnp.where(lane == j, jnp.sum(dLL, axis=-1, keepdims=True), 0.0)
            dA_row = dA_row + jnp.where(sub == j, jnp.sum(dLL, axis=0, keepdims=True), 0.0)
            dxdt = dxdt + _dot((CB * L).T.astype(BF16), dyj)
        dC = dC + _dot(dCB.astype(BF16), Bb)
        dB = dB + _dot(dCB.T.astype(BF16), Cb)
        dA = dA_col - dA_row.T + _xdot(dAe, ET)
        triu = (lax.broadcasted_iota(jnp.int32, (Q, Q), 1) >= lax.broadcasted_iota(jnp.int32, (Q, Q), 0)).astype(BF16)
        rcs = _xdot_l(triu, dA)
        ddt = a * rcs + _xdot(dxdt * xs, ET)
        draw = ddt * _sigmoid(raw_in)
        draw_ref[...] = draw
        dalog_ref[...] += jnp.sum(dt * rcs, axis=0, keepdims=True) * a
        dbias_ref[...] += jnp.sum(draw, axis=0, keepdims=True)
        ddsk_ref[...] += jnp.sum(_xdot(dyv * xs, ET), axis=0, keepdims=True)
        dx_ref[...] = dxdt * dt_e + dsk_e * dyv
        db_ref[...] = dB
        dc_ref[...] = dC
        dHT[...] = dH_prev

    nbx = TOK // N
    rv = lambda c: NC - 1 - c
    par = pl.BlockSpec((None, 1, LANES), lambda g, c: (g, 0, 0))
    xsp = pl.BlockSpec((Q, HP), lambda g, c: (rv(c), g))
    outs = pl.pallas_call(
        body, name=name, grid=(G, NC),
        in_specs=[xsp,
                  pl.BlockSpec((Q, N), lambda g, c: (rv(c), nbx + g)),
                  pl.BlockSpec((Q, N), lambda g, c: (rv(c), nbx + G + g)),
                  pl.BlockSpec((None, Q, LANES), lambda g, c: (g, rv(c), 0)), par, par, par,
                  pl.BlockSpec((None, None, N, HP), lambda g, c: (g, rv(c), 0, 0)), xsp],
        out_specs=[xsp, pl.BlockSpec((Q, N), lambda g, c: (rv(c), g)), pl.BlockSpec((Q, N), lambda g, c: (rv(c), g)),
                   pl.BlockSpec((None, Q, LANES), lambda g, c: (g, rv(c), 0)), par, par, par],
        out_shape=[jax.ShapeDtypeStruct((T, TOK), F32), jax.ShapeDtypeStruct((T, G * N), F32),
                   jax.ShapeDtypeStruct((T, G * N), F32), jax.ShapeDtypeStruct((G, T, LANES), F32),
                   jax.ShapeDtypeStruct((G, 1, LANES), F32), jax.ShapeDtypeStruct((G, 1, LANES), F32),
                   jax.ShapeDtypeStruct((G, 1, LANES), F32)],
        scratch_shapes=[pltpu.VMEM((N, HP), F32)],
        compiler_params=_cparams("parallel", "arbitrary"),
    )(act, act, act, raw_g, bias_g, alog_g, dsk_g, hprev, dy)
    return outs


def _heads_per_block(H):
    for hb in (8, 6, 4, 3, 2, 1):
        if H % hb == 0:
            return hb
    return 1


def _alibi_slope(head_index, n_alibi):
    c = -ALIBI_MAX_EXP * math.log(2.0) / n_alibi
    return jnp.exp(jnp.full((1, 1), c, F32) * (head_index + 1).astype(F32))


def _attn_masks(b, nb):
    Bq = ATTN_BLOCK
    iq = lax.broadcasted_iota(jnp.int32, (Bq, 2 * Bq), 0)
    jk = lax.broadcasted_iota(jnp.int32, (Bq, 2 * Bq), 1)
    rel = iq + Bq - jk
    mask = (rel >= 0) & (rel <= Bq) & (jk + jnp.where(b > 0, Bq, 0) >= Bq)
    rel_n = lax.broadcasted_iota(jnp.int32, (Bq, Bq), 0) + Bq - lax.broadcasted_iota(jnp.int32, (Bq, Bq), 1)
    mask_n = (rel_n + jnp.where(b < nb - 1, 0, 4 * Bq)) <= Bq
    return rel.astype(F32), mask, rel_n.astype(F32), mask_n


def _rows2(a, b):
    return jnp.concatenate([a, b], axis=0)


def _attn_head_fwd(q, kc, kp, vc, vp, slope, masks, scale):
    rel_f, mask, _, _ = masks
    s = jnp.where(mask, _dot_nt(q, _rows2(kp, kc)) * scale - slope * rel_f, NEG)
    m = jnp.max(s, axis=-1, keepdims=True)
    p = jnp.exp(s - m)
    den = jnp.sum(p, axis=-1, keepdims=True)
    return _dot(p.astype(BF16), _rows2(vp, vc)) / den, m + jnp.log(den)


def _attn_head_bwd(q0, q1, kp, k0, vp, v0, do0, do1, y0, y1, lse0, lse1, slope, masks, scale):
    rel_f, mask, reln_f, mask_n = masks
    Bq = ATTN_BLOCK
    delta0 = jnp.sum(do0 * y0, axis=-1, keepdims=True)
    delta1 = jnp.sum(do1 * y1, axis=-1, keepdims=True)
    do0b = do0.astype(BF16)
    do1b = do1.astype(BF16)
    kcat = _rows2(kp, k0)
    p = jnp.exp(jnp.where(mask, _dot_nt(q0, kcat) * scale - slope * rel_f, NEG) - lse0)
    ds = p * (_dot_nt(do0b, _rows2(vp, v0)) - delta0)
    p_n = jnp.exp(jnp.where(mask_n, _dot_nt(q1, k0) * scale - slope * reln_f, NEG) - lse1)
    ds_n = p_n * (_dot_nt(do1b, v0) - delta1)
    dq = scale * _dot(ds.astype(BF16), kcat)
    dk = scale * _dot(_rows2(ds[:, Bq:], ds_n).T.astype(BF16), _rows2(q0, q1))
    dv = _dot(_rows2(p[:, Bq:], p_n).T.astype(BF16), _rows2(do0b, do1b))
    return dq, dk, dv


def _attn_fwd_strided(q, k, v, gi, d, name):
    T, TOK = q.shape
    E_ = ATTN_HEAD_DIM
    H = TOK // E_
    n_alibi = len(DILATED_GROUPS) * H
    Bq = ATTN_BLOCK
    RB = Bq * d
    nb = T // RB
    scale = E_ ** -0.5

    def body(q_ref, kc_ref, kp_ref, vc_ref, vp_ref, o_ref, l_ref):
        masks = _attn_masks(pl.program_id(1), nb)
        slope = _alibi_slope(gi * H + pl.program_id(0), n_alibi) * float(d)
        for r in range(d):
            rows = pl.ds(r, Bq, stride=d)
            ld = lambda ref: ref[rows, :].astype(BF16)
            o, lse = _attn_head_fwd(ld(q_ref), ld(kc_ref), ld(kp_ref), ld(vc_ref), ld(vp_ref), slope, masks, scale)
            o_ref[rows, :] = o
            l_ref[rows, :] = jnp.broadcast_to(lse, (Bq, E_))

    cur = pl.BlockSpec((RB, E_), lambda h, b: (b, h))
    prev = pl.BlockSpec((RB, E_), lambda h, b: (jnp.maximum(b - 1, 0), h))
    sds = jax.ShapeDtypeStruct((T, TOK), F32)
    return pl.pallas_call(
        body, name=name, grid=(H, nb), in_specs=[cur, cur, prev, cur, prev], out_specs=[cur, cur],
        out_shape=[sds, sds], compiler_params=_cparams("parallel", "parallel"),
    )(q, k, k, v, v)


def _attn_bwd_strided(q, k, v, y, lse, dy, gi, d, name):
    T, TOK = q.shape
    E_ = ATTN_HEAD_DIM
    H = TOK // E_
    n_alibi = len(DILATED_GROUPS) * H
    Bq = ATTN_BLOCK
    RB = Bq * d
    nb = T // RB
    scale = E_ ** -0.5

    def body(q0_ref, q1_ref, kp_ref, k0_ref, vp_ref, v0_ref, do0_ref, do1_ref, y0_ref, y1_ref, l0_ref, l1_ref,
             dq_ref, dk_ref, dv_ref):
        masks = _attn_masks(pl.program_id(1), nb)
        slope = _alibi_slope(gi * H + pl.program_id(0), n_alibi) * float(d)
        for r in range(d):
            rows = pl.ds(r, Bq, stride=d)
            ld = lambda ref: ref[rows, :]
            lb = lambda ref: ref[rows, :].astype(BF16)
            dq, dk, dv = _attn_head_bwd(
                lb(q0_ref), lb(q1_ref), lb(kp_ref), lb(k0_ref), lb(vp_ref), lb(v0_ref), ld(do0_ref), ld(do1_ref),
                ld(y0_ref), ld(y1_ref), jnp.max(ld(l0_ref), axis=-1, keepdims=True),
                jnp.max(ld(l1_ref), axis=-1, keepdims=True), slope, masks, scale)
            dq_ref[rows, :] = dq
            dk_ref[rows, :] = dk
            dv_ref[rows, :] = dv

    cur = pl.BlockSpec((RB, E_), lambda h, b: (b, h))
    prev = pl.BlockSpec((RB, E_), lambda h, b: (jnp.maximum(b - 1, 0), h))
    nxt = pl.BlockSpec((RB, E_), lambda h, b: (jnp.minimum(b + 1, nb - 1), h))
    sds = jax.ShapeDtypeStruct((T, TOK), F32)
    return pl.pallas_call(
        body, name=name, grid=(H, nb),
        in_specs=[cur, nxt, prev, cur, prev, cur, cur, nxt, cur, nxt, cur, nxt],
        out_specs=[cur, cur, cur], out_shape=[sds, sds, sds],
        compiler_params=_cparams("parallel", "parallel"),
    )(q, q, k, k, v, v, dy, dy, y, y, lse, lse)


def _attn_fwd(q, k, v, gi, window, d, name):
    T, TOK = q.shape
    E_ = ATTN_HEAD_DIM
    H = TOK // E_
    n_alibi = len(DILATED_GROUPS) * H
    assert window // d == ATTN_BLOCK and (T // d) % ATTN_BLOCK == 0
    if d > 1:
        return _attn_fwd_strided(q, k, v, gi, d, name)
    n_sub = T // d
    nb = n_sub // ATTN_BLOCK
    HB = _heads_per_block(H)
    NHB = H // HB
    hbw = HB * E_
    scale = E_ ** -0.5
    Bq = ATTN_BLOCK

    def body(q_ref, kc_ref, kp_ref, vc_ref, vp_ref, o_ref, l_ref):
        hb = pl.program_id(1)
        masks = _attn_masks(pl.program_id(2), nb)
        for hh in range(HB):
            sl = slice(hh * E_, (hh + 1) * E_)
            slope = _alibi_slope(gi * H + hb * HB + hh, n_alibi) * float(d)
            ld = lambda ref: ref[:, sl].astype(BF16)
            o, lse = _attn_head_fwd(ld(q_ref), ld(kc_ref), ld(kp_ref), ld(vc_ref), ld(vp_ref), slope, masks, scale)
            o_ref[:, sl] = o
            l_ref[:, sl] = jnp.broadcast_to(lse, (Bq, E_))

    cur = pl.BlockSpec((Bq, hbw), lambda r, h, b: (b, r * NHB + h))
    prev = pl.BlockSpec((Bq, hbw), lambda r, h, b: (jnp.maximum(b - 1, 0), r * NHB + h))
    view = lambda t: t.reshape(n_sub, d * TOK)
    sds = jax.ShapeDtypeStruct((n_sub, d * TOK), F32)
    o, l = pl.pallas_call(
        body, name=name, grid=(d, NHB, nb), in_specs=[cur, cur, prev, cur, prev], out_specs=[cur, cur],
        out_shape=[sds, sds], compiler_params=_cparams("parallel", "parallel", "parallel"),
    )(view(q), view(k), view(k), view(v), view(v))
    return o.reshape(T, TOK), l.reshape(T, TOK)


def _attn_combine(os_, ls_, name):
    def fn(*v):
        n = len(v) // 2
        o, l = v[:n], v[n:]
        m = l[0]
        for t in l[1:]:
            m = jnp.maximum(m, t)
        e = [jnp.exp(t - m) for t in l]
        den = e[0]
        for t in e[1:]:
            den = den + t
        y = e[0] * o[0]
        for t, u in zip(e[1:], o[1:]):
            y = y + t * u
        return y / den, m + jnp.log(den)
    return _elementwise(fn, list(os_) + list(ls_), [], [F32, F32], name)


def _attn_bwd(q, k, v, y, lse, dy, gi, window, d, name):
    T, TOK = q.shape
    E_ = ATTN_HEAD_DIM
    H = TOK // E_
    n_alibi = len(DILATED_GROUPS) * H
    if d > 1:
        return _attn_bwd_strided(q, k, v, y, lse, dy, gi, d, name)
    n_sub = T // d
    nb = n_sub // ATTN_BLOCK
    HB = _heads_per_block(H)
    NHB = H // HB
    hbw = HB * E_
    scale = E_ ** -0.5
    Bq = ATTN_BLOCK

    def body(q0_ref, q1_ref, kp_ref, k0_ref, vp_ref, v0_ref, do0_ref, do1_ref, y0_ref, y1_ref, l0_ref, l1_ref,
             dq_ref, dk_ref, dv_ref):
        hb = pl.program_id(1)
        masks = _attn_masks(pl.program_id(2), nb)
        for hh in range(HB):
            sl = slice(hh * E_, (hh + 1) * E_)
            slope = _alibi_slope(gi * H + hb * HB + hh, n_alibi) * float(d)
            ld = lambda ref: ref[:, sl]
            lb = lambda ref: ref[:, sl].astype(BF16)
            dq, dk, dv = _attn_head_bwd(
                lb(q0_ref), lb(q1_ref), lb(kp_ref), lb(k0_ref), lb(vp_ref), lb(v0_ref), ld(do0_ref), ld(do1_ref),
                ld(y0_ref), ld(y1_ref), jnp.max(ld(l0_ref), axis=-1, keepdims=True),
                jnp.max(ld(l1_ref), axis=-1, keepdims=True), slope, masks, scale)
            dq_ref[:, sl] = dq.astype(BF16)
            dk_ref[:, sl] = dk.astype(BF16)
            dv_ref[:, sl] = dv.astype(BF16)

    cur = pl.BlockSpec((Bq, hbw), lambda r, h, b: (b, r * NHB + h))
    prev = pl.BlockSpec((Bq, hbw), lambda r, h, b: (jnp.maximum(b - 1, 0), r * NHB + h))
    nxt = pl.BlockSpec((Bq, hbw), lambda r, h, b: (jnp.minimum(b + 1, nb - 1), r * NHB + h))
    view = lambda t: t.reshape(n_sub, d * TOK)
    sds = jax.ShapeDtypeStruct((n_sub, d * TOK), BF16)
    dq, dk, dv = pl.pallas_call(
        body, name=name, grid=(d, NHB, nb),
        in_specs=[cur, nxt, prev, cur, prev, cur, cur, nxt, cur, nxt, cur, nxt],
        out_specs=[cur, cur, cur], out_shape=[sds, sds, sds],
        compiler_params=_cparams("parallel", "parallel", "parallel"),
    )(view(q), view(q), view(k), view(k), view(v), view(v), view(dy), view(dy), view(y), view(y), view(lse), view(lse))
    return dq.reshape(T, TOK), dk.reshape(T, TOK), dv.reshape(T, TOK)


_FLIPS = {
    "xy": [(1, 0, 0), (0, 1, 0), (1, 1, 0)],
    "c": [(0, 0, 1)],
    "xyc": [(dx, dy, dc) for dx in (0, 1) for dy in (0, 1) for dc in (0, 1) if (dx, dy, dc) != (0, 0, 0)],
}


def _comm_parts(group, srcs, modes, handshake):
    flips = _FLIPS[group]
    F_ = len(flips)
    P_ = F_ + 1
    n = len(srcs)

    def gidx(px, py, pc):
        if group == "xy":
            return 2 * px + py
        if group == "c":
            return pc
        return 4 * px + 2 * py + pc

    def body(*refs):
        src_refs, out_refs = refs[:n], refs[n:2 * n]
        send_sems, recv_sems, loc_sems = refs[2 * n:]
        x, y, c = lax.axis_index("x"), lax.axis_index("y"), lax.axis_index("c")
        me = gidx(x, y, c)
        peers = [(1 - x if dx else x, 1 - y if dy else y, 1 - c if dc else c) for dx, dy, dc in flips]
        if handshake:
            barrier = pltpu.get_barrier_semaphore()
            for peer in peers:
                pl.semaphore_signal(barrier, inc=1, device_id=peer, device_id_type=MESH)
            pl.semaphore_wait(barrier, F_)
        local, remote = [], []
        for i in range(n):
            mode = modes[i]
            if mode != "swap":
                mine = pltpu.make_async_copy(src_refs[i] if mode == "gather" else src_refs[i].at[me],
                                             out_refs[i].at[me], loc_sems.at[i])
                mine.start()
                local.append(mine)
            for f, peer in enumerate(peers):
                cp = pltpu.make_async_remote_copy(
                    src_ref=src_refs[i].at[gidx(*peer)] if mode == "a2a" else src_refs[i],
                    dst_ref=out_refs[i] if mode == "swap" else out_refs[i].at[me],
                    send_sem=send_sems.at[i * F_ + f], recv_sem=recv_sems.at[i * F_ + f],
                    device_id=peer, device_id_type=MESH)
                cp.start()
                remote.append(cp)
        for cp in local:
            cp.wait()
        for cp in remote:
            cp.wait()

    out_shape = []
    for s, mode in zip(srcs, modes):
        assert mode != "swap" or F_ == 1
        shp = (P_,) + tuple(s.shape) if mode == "gather" else tuple(s.shape)
        out_shape.append(jax.ShapeDtypeStruct(shp, s.dtype))
    sems = [pltpu.SemaphoreType.DMA((n * F_,)), pltpu.SemaphoreType.DMA((n * F_,)), pltpu.SemaphoreType.DMA((n,))]
    return body, out_shape, sems


def _comm(name, group, srcs, modes):
    n = len(srcs)
    body, out_shape, sems = _comm_parts(group, srcs, modes, handshake=False)
    anyspec = pl.BlockSpec(memory_space=pl.ANY)
    return pl.pallas_call(body, name=name, in_specs=[anyspec] * n, out_specs=[anyspec] * n, out_shape=out_shape,
                          scratch_shapes=sems)(*srcs)


def _comm_async(name, collective_id, group, srcs, modes):
    body, out_shape, sems = _comm_parts(group, srcs, modes, handshake=True)
    return pl.kernel(body, name=name, out_type=out_shape,
                     mesh=plsc.ScalarSubcoreMesh(axis_name="sequencer", num_cores=1), scratch_types=sems,
                     compiler_params=pltpu.CompilerParams(collective_id=collective_id))(*srcs)


def _dims(D):
    MIX = 2 * D
    MW = MIX // 4
    TOK = MIX - MW
    H = TOK // SSD_HEAD_DIM
    CONV = TOK + 2 * SSD_GROUPS * SSD_STATE
    return dict(MIX=MIX, MW=MW, TOK=TOK, H=H, CONV=CONV)


def _proj_chain(dsegs, wsegs, name):
    acc = None
    for n, (ds, ws) in enumerate(zip(dsegs, wsegs)):
        acc = _mm(ds, ws, "nt", F32, f"{name}_dh{n}", add=acc)
    return acc


def _pad_lanes(a, width=LANES):
    return jnp.pad(a, [(0, 0)] * (a.ndim - 1) + [(0, width - a.shape[-1])])


def _ssd_layer_fwd(x, kv, p, li):
    T, D = x.shape
    dm = _dims(D)
    TOK, MW, H, CONV = dm["TOK"], dm["MW"], dm["H"], dm["CONV"]
    G = SSD_GROUPS
    HPG = H // G
    w = p["w_in"]
    segs = dict(xbc=w[:, :CONV], dt=_pad_lanes(w[:, CONV:CONV + H]), qm=w[:, CONV + H:CONV + H + MW],
                zt=w[:, CONV + H + MW:CONV + H + MW + TOK], zm=w[:, CONV + H + MW + TOK:])
    h = _rms_fwd(x, p["norm_g"], f"l{li}_rms")
    pr = {k: _mm(h, ws, "nn", BF16 if k == "qm" else F32, f"l{li}_in_{k}") for k, ws in segs.items()}
    act = _conv_fwd(pr["xbc"], p["conv_w"], p["conv_b"], f"l{li}_conv")
    raw_g = _pad_lanes(pr["dt"][:, :H].reshape(T, G, HPG).transpose(1, 0, 2))
    hp = lambda a: _pad_lanes(a.reshape(G, 1, HPG))
    bias_g, alog_g, dsk_g = hp(p["dt_bias"]), hp(p["a_log"]), hp(p["d_skip"])
    y, hprev = _ssd_fwd(act, raw_g, bias_g, alog_g, dsk_g, TOK, f"l{li}_ssd")
    ymem = _memattn_fwd(pr["qm"], kv, f"l{li}_mem")
    gt = _gate_norm_fwd(y, pr["zt"], p["ssd_norm_g"], G, f"l{li}_gate_tok")
    gm = _gate_fwd(ymem, pr["zm"], f"l{li}_gate_mem")
    wo = p["w_out"]
    out = _mm(gt, wo[:TOK], "nn", F32, f"l{li}_out_tok", add=x)
    out = _mm(gm, wo[TOK:], "nn", F32, f"l{li}_out_mem", add=out)
    saved = dict(x=x, h=h, pr=pr, act=act, raw_g=raw_g, par=(bias_g, alog_g, dsk_g), y=y, hprev=hprev, ymem=ymem,
                 gt=gt, gm=gm, segs=segs)
    return out, saved


def _ssd_layer_bwd(dout, doutb, kv, p, s, li):
    x = s["x"]
    T, D = x.shape
    dm = _dims(D)
    TOK, MW, H, CONV = dm["TOK"], dm["MW"], dm["H"], dm["CONV"]
    G = SSD_GROUPS
    HPG = H // G
    wo = p["w_out"]
    pr = s["pr"]
    dgt = _mm(doutb, wo[:TOK], "nt", F32, f"l{li}_dgt")
    dgm = _mm(doutb, wo[TOK:], "nt", F32, f"l{li}_dgm")
    dwo = jnp.concatenate([_mm(s["gt"], doutb, "tn", BF16, f"l{li}_dwo_tok"),
                           _mm(s["gm"], doutb, "tn", BF16, f"l{li}_dwo_mem")], axis=0)
    dy, dzt, dng = _gate_norm_bwd(s["y"], pr["zt"], p["ssd_norm_g"], dgt, G, f"l{li}_gate_tok_b")
    dymem, dzm = _gate_bwd(s["ymem"], pr["zm"], dgm, f"l{li}_gate_mem_b")
    dqm, dkv = _memattn_bwd(pr["qm"], kv, dymem, f"l{li}_mem_b")
    bias_g, alog_g, dsk_g = s["par"]
    dxs, dB, dC, draw_g, dalog, dbias, ddsk = _ssd_bwd(s["act"], s["raw_g"], bias_g, alog_g, dsk_g, s["hprev"], dy, TOK,
                                                      f"l{li}_ssd_b")
    dact = jnp.concatenate([dxs, dB, dC], axis=1)
    dpre, dconv_w, dconv_b = _conv_bwd_pre(pr["xbc"], p["conv_w"], p["conv_b"], dact, f"l{li}_conv_b1")
    dxbc = _conv_bwd_in(dpre, p["conv_w"], f"l{li}_conv_b2")
    draw = _pad_lanes(draw_g[:, :, :HPG].transpose(1, 0, 2).reshape(T, H)).astype(BF16)
    dsegs = dict(xbc=dxbc, dt=draw, qm=dqm, zt=dzt, zm=dzm)
    keys = ["xbc", "dt", "qm", "zt", "zm"]
    dh = _proj_chain([dsegs[k] for k in keys], [s["segs"][k] for k in keys], f"l{li}")
    dws = {k: _mm(s["h"], dsegs[k], "tn", BF16, f"l{li}_dwin_{k}") for k in keys}
    dws["dt"] = dws["dt"][:, :H]
    dwin = jnp.concatenate([dws[k] for k in keys], axis=1)
    dx, dxb, dnorm = _rms_bwd(x, p["norm_g"], dh, dout, f"l{li}_rms_b")
    unhead = lambda a: a[:, 0, :HPG].reshape(H)
    grads = dict(norm_g=dnorm, w_in=dwin, conv_w=dconv_w, conv_b=dconv_b, dt_bias=unhead(dbias), a_log=unhead(dalog),
                 d_skip=unhead(ddsk), ssd_norm_g=dng, w_out=dwo)
    return dx, dxb, dkv, grads


def _attn_layer_fwd(x, kv, p, li):
    T, D = x.shape
    dm = _dims(D)
    TOK, MW = dm["TOK"], dm["MW"]
    w = p["w_in"]
    ng = len(DILATED_GROUPS)
    segs = {}
    for g in range(ng):
        for n, nm in enumerate("qkv"):
            c0 = g * 3 * TOK + n * TOK
            segs[f"{nm}{g}"] = w[:, c0:c0 + TOK]
    c0 = ng * 3 * TOK
    segs["qm"] = w[:, c0:c0 + MW]
    segs["zt"] = w[:, c0 + MW:c0 + MW + TOK]
    segs["zm"] = w[:, c0 + MW + TOK:]
    h = _rms_fwd(x, p["norm_g"], f"l{li}_rms")
    dense = {"qm"} | {f"{nm}{g}" for g, (_, d) in enumerate(DILATED_GROUPS) if d == 1 for nm in "qkv"}
    pr = {k: _mm(h, ws, "nn", BF16 if k in dense else F32, f"l{li}_in_{k}") for k, ws in segs.items()}
    os_, ls_ = [], []
    for g, (window, d) in enumerate(DILATED_GROUPS):
        o, l = _attn_fwd(pr[f"q{g}"], pr[f"k{g}"], pr[f"v{g}"], g, window, d, f"l{li}_attn{g}")
        os_.append(o)
        ls_.append(l)
    ytok, lse = _attn_combine(os_, ls_, f"l{li}_combine")
    ymem = _memattn_fwd(pr["qm"], kv, f"l{li}_mem")
    gt = _gate_fwd(ytok, pr["zt"], f"l{li}_gate_tok")
    gm = _gate_fwd(ymem, pr["zm"], f"l{li}_gate_mem")
    wo = p["w_out"]
    out = _mm(gt, wo[:TOK], "nn", F32, f"l{li}_out_tok", add=x)
    out = _mm(gm, wo[TOK:], "nn", F32, f"l{li}_out_mem", add=out)
    saved = dict(x=x, h=h, pr=pr, ytok=ytok, lse=lse, ymem=ymem, gt=gt, gm=gm, segs=segs)
    return out, saved


def _attn_layer_bwd(dout, doutb, kv, p, s, li):
    x = s["x"]
    T, D = x.shape
    dm = _dims(D)
    TOK, MW = dm["TOK"], dm["MW"]
    wo = p["w_out"]
    pr = s["pr"]
    dgt = _mm(doutb, wo[:TOK], "nt", F32, f"l{li}_dgt")
    dgm = _mm(doutb, wo[TOK:], "nt", F32, f"l{li}_dgm")
    dwo = jnp.concatenate([_mm(s["gt"], doutb, "tn", BF16, f"l{li}_dwo_tok"),
                           _mm(s["gm"], doutb, "tn", BF16, f"l{li}_dwo_mem")], axis=0)
    dytok, dzt = _gate_bwd(s["ytok"], pr["zt"], dgt, f"l{li}_gate_tok_b")
    dymem, dzm = _gate_bwd(s["ymem"], pr["zm"], dgm, f"l{li}_gate_mem_b")
    dqm, dkv = _memattn_bwd(pr["qm"], kv, dymem, f"l{li}_mem_b")
    dsegs = {}
    for g, (window, d) in enumerate(DILATED_GROUPS):
        dq, dk, dv = _attn_bwd(pr[f"q{g}"], pr[f"k{g}"], pr[f"v{g}"], s["ytok"], s["lse"], dytok, g, window, d,
                               f"l{li}_attn{g}_b")
        dsegs[f"q{g}"], dsegs[f"k{g}"], dsegs[f"v{g}"] = dq, dk, dv
    dsegs["qm"], dsegs["zt"], dsegs["zm"] = dqm, dzt, dzm
    keys = list(s["segs"].keys())
    dh = _proj_chain([dsegs[k] for k in keys], [s["segs"][k] for k in keys], f"l{li}")
    dwin = jnp.concatenate([_mm(s["h"], dsegs[k], "tn", BF16, f"l{li}_dwin_{k}") for k in keys], axis=1)
    dx, dxb, dnorm = _rms_bwd(x, p["norm_g"], dh, dout, f"l{li}_rms_b")
    return dx, dxb, dkv, dict(norm_g=dnorm, w_in=dwin, w_out=dwo)


def _local_step(x, mem, tgt, mem_norm_g, final_norm_g, n_layers, layer_params, on_layer_grads):
    mem_n = _rms_fwd(mem, mem_norm_g, "mem_rms")
    layers, kvs, saved = [], [], []
    for li in range(n_layers):
        p, x = layer_params(li, x)
        fwd = _ssd_layer_fwd if li % 2 == 0 else _attn_layer_fwd
        kv = _mm(mem_n, p["w_mem_kv"], "nn", F32, f"l{li}_kv")
        x, s = fwd(x, kv, p, li)
        layers.append(p)
        kvs.append(kv)
        saved.append(s)
    loss, dx, dxb, dfinal = _loss_head(x, final_norm_g, tgt, "loss_head")
    dmem_n = None
    for li in reversed(range(n_layers)):
        p = layers[li]
        bwd = _ssd_layer_bwd if li % 2 == 0 else _attn_layer_bwd
        dx, dxb, dkv, g = bwd(dx, dxb, kvs[li], p, saved[li], li)
        g["w_mem_kv"] = _mm(mem_n, dkv, "tn", BF16, f"l{li}_dwkv")
        dmem_n = _mm(dkv, p["w_mem_kv"], "nt", F32, f"l{li}_dmem", add=dmem_n)
        dx, dxb = on_layer_grads(li, g, dx, dxb)
    _, _, dmem_g = _rms_bwd(mem, mem_norm_g, dmem_n, None, "mem_rms_b")
    return loss, dx, dmem_g, dfinal


_SSD_SMALL = ["norm_g", "conv_w", "conv_b", "dt_bias", "a_log", "d_skip", "ssd_norm_g"]
_ATTN_SMALL = ["norm_g"]
_SSD_ORDER = ["norm_g", "w_in", "conv_w", "conv_b", "dt_bias", "a_log", "d_skip", "ssd_norm_g", "w_mem_kv", "w_out"]
_ATTN_ORDER = ["norm_g", "w_in", "w_mem_kv", "w_out"]


def _pack(arrs):
    flat = jnp.concatenate([a.reshape(-1).astype(F32) for a in arrs])
    n = flat.shape[0]
    pad = (-n) % (8 * LANES)
    return jnp.pad(flat, (0, pad)).reshape(-1, LANES)


def _unpack(mat, shapes):
    flat = mat.reshape(-1)
    out, o = [], 0
    for shp in shapes:
        n = math.prod(shp)
        out.append(flat[o:o + n].reshape(shp))
        o += n
    return out


def kernel(x, mem, mem_norm_g, final_norm_g, norm_g_0, w_in_0, conv_w_0, conv_b_0, dt_bias_0, a_log_0, d_skip_0, ssd_norm_g_0, w_mem_kv_0, w_out_0, norm_g_1, w_in_1, w_mem_kv_1, w_out_1, norm_g_2, w_in_2, conv_w_2, conv_b_2, dt_bias_2, a_log_2, d_skip_2, ssd_norm_g_2, w_mem_kv_2, w_out_2, norm_g_3, w_in_3, w_mem_kv_3, w_out_3, loss_target, m_mem_norm_g, m_final_norm_g, m_norm_g_0, m_w_in_0, m_conv_w_0, m_conv_b_0, m_dt_bias_0, m_a_log_0, m_d_skip_0, m_ssd_norm_g_0, m_w_mem_kv_0, m_w_out_0, m_norm_g_1, m_w_in_1, m_w_mem_kv_1, m_w_out_1, m_norm_g_2, m_w_in_2, m_conv_w_2, m_conv_b_2, m_dt_bias_2, m_a_log_2, m_d_skip_2, m_ssd_norm_g_2, m_w_mem_kv_2, m_w_out_2, m_norm_g_3, m_w_in_3, m_w_mem_kv_3, m_w_out_3, v_mem_norm_g, v_final_norm_g, v_norm_g_0, v_w_in_0, v_conv_w_0, v_conv_b_0, v_dt_bias_0, v_a_log_0, v_d_skip_0, v_ssd_norm_g_0, v_w_mem_kv_0, v_w_out_0, v_norm_g_1, v_w_in_1, v_w_mem_kv_1, v_w_out_1, v_norm_g_2, v_w_in_2, v_conv_w_2, v_conv_b_2, v_dt_bias_2, v_a_log_2, v_d_skip_2, v_ssd_norm_g_2, v_w_mem_kv_2, v_w_out_2, v_norm_g_3, v_w_in_3, v_w_mem_kv_3, v_w_out_3):
    a = dict(locals())
    names = ["mem_norm_g", "final_norm_g"]
    for li in range(DEPTH):
        names += [f"{k}_{li}" for k in (_SSD_ORDER if li % 2 == 0 else _ATTN_ORDER)]
    W = {n: a[n] for n in names}
    Mo = {n: a["m_" + n] for n in names}
    Vo = {n: a["v_" + n] for n in names}
    NX = 4
    chip = 2 * lax.axis_index("x") + lax.axis_index("y")

    gathered = []
    for li in range(DEPTH):
        srcs = [W[f"w_in_{li}"].astype(BF16), W[f"w_mem_kv_{li}"].astype(BF16), W[f"w_out_{li}"].astype(BF16)]
        if li % 2 == 0:
            srcs.append(W[f"conv_w_{li}"])
        gathered.append(_comm_async(f"gather_w{li}", li, "xy", srcs, ["gather"] * len(srcs)))

    def layer_params(li, xin):
        got = gathered[li]
        if li > 0:
            got, xin = lax.optimization_barrier((got, xin))
        rows = lambda g: g.reshape((-1,) + g.shape[2:])
        colcat = lambda g: g.transpose(1, 0, 2).reshape(g.shape[1], -1)
        p = dict(w_in=colcat(got[0]), w_mem_kv=rows(got[1]), w_out=rows(got[2]), norm_g=W[f"norm_g_{li}"])
        if li % 2 == 0:
            p.update(conv_w=colcat(got[3]), conv_b=W[f"conv_b_{li}"], dt_bias=W[f"dt_bias_{li}"], a_log=W[f"a_log_{li}"],
                     d_skip=W[f"d_skip_{li}"], ssd_norm_g=W[f"ssd_norm_g_{li}"])
        return p, xin

    G, Dl, Mn, Vn = {}, {}, {}, {}
    grads = [None] * DEPTH
    in_flight = []

    def finish_exchange(li, got):
        parts = [_sum_lead(t, f"l{li}_gsum{n}") for n, t in enumerate(got)]
        theirs = _comm_async(f"swap_g{li}", 2 * DEPTH + li, "c", parts, ["swap"] * 3)
        for nm, mine, other in zip(["w_in", "w_mem_kv", "w_out"], parts, theirs):
            key = f"{nm}_{li}"
            G[key], Dl[key], Mn[key], Vn[key] = _adamw(W[key], Mo[key], Vo[key], [mine, other], f"adamw_{key}")

    def on_layer_grads(li, g, dx, dxb):
        grads[li] = g
        dwin = g["w_in"]
        Dm, N = dwin.shape
        chunks = [dwin.reshape(Dm, NX, N // NX).transpose(1, 0, 2),
                  g["w_mem_kv"].reshape((NX, -1) + g["w_mem_kv"].shape[1:]),
                  g["w_out"].reshape((NX, -1) + g["w_out"].shape[1:])]
        prev = in_flight.pop() if in_flight else None
        pgot = prev[1] if prev else []
        chunks, pgot, dx, dxb = lax.optimization_barrier((chunks, pgot, dx, dxb))
        in_flight.append((li, _comm_async(f"xchg_g{li}", DEPTH + li, "xy", chunks, ["a2a"] * 3)))
        if prev:
            finish_exchange(prev[0], pgot)
        return dx, dxb

    loss_l, dx, dmem_g, dfinal = _local_step(x[0], mem[0], loss_target[0], W["mem_norm_g"], W["final_norm_g"], DEPTH,
                                             layer_params, on_layer_grads)
    finish_exchange(*in_flight.pop())
    loss = lax.psum(loss_l, ("x", "y", "c"))

    small_names = ["mem_norm_g", "final_norm_g"]
    small_grads = [dmem_g, dfinal]
    for li in range(DEPTH):
        for k in (_SSD_SMALL if li % 2 == 0 else _ATTN_SMALL):
            small_names.append(f"{k}_{li}")
            small_grads.append(grads[li][k])
    shapes = [tuple(t.shape) for t in small_grads]
    allg = _comm("gather_small", "xyc", [_pack(small_grads)], ["gather"])[0]
    gsum = _unpack(_sum_lead(allg, "small_gsum"), shapes)
    small_w, small_m, small_v, small_g = [], [], [], []
    for nme, gv in zip(small_names, gsum):
        if nme.startswith("conv_w"):
            cw = W[nme].shape[1]
            gv = lax.dynamic_slice_in_dim(gv, chip * cw, cw, axis=1)
        small_g.append(gv)
        small_w.append(W[nme])
        small_m.append(Mo[nme])
        small_v.append(Vo[nme])
    sshapes = [tuple(t.shape) for t in small_g]
    res = _adamw(_pack(small_w), _pack(small_m), _pack(small_v), [_pack(small_g)], "adamw_small")
    for dst, mat in zip((G, Dl, Mn, Vn), res):
        for nme, t in zip(small_names, _unpack(mat, sshapes)):
            dst[nme] = t

    return (loss, dx[None], *[G[n] for n in names], *[Dl[n] for n in names], *[Mn[n] for n in names],
            *[Vn[n] for n in names])
```

```python
import functools
import math

import jax
import jax.numpy as jnp
from jax import lax
from jax.experimental import pallas as pl
from jax.experimental.pallas import tpu as pltpu
from jax.experimental.pallas import tpu_sc as plsc

F32 = jnp.float32
BF16 = jnp.bfloat16

EPS = 1e-6
MEM_HEADS = 4
SSD_HEAD_DIM = 64
SSD_GROUPS = 8
SSD_STATE = 128
SSD_CONV = 4
SSD_CHUNK = 128
ATTN_HEAD_DIM = 128
ATTN_BLOCK = 128
DILATED_GROUPS = ((128, 1), (512, 4), (2048, 16))
ALIBI_MAX_EXP = 8.0
DEPTH = 4

ADAM_LR = 0.001
ADAM_B1 = 0.9
ADAM_B2 = 0.999
ADAM_EPS = 1e-08
ADAM_WD = 0.01
ADAM_STEP = 10

LANES = 128
VMEM_LIMIT_BYTES = 48 * 1024 * 1024
NEG = -1e30
MESH = pl.DeviceIdType.MESH


def _cparams(*sem):
    return pltpu.CompilerParams(dimension_semantics=tuple(sem), vmem_limit_bytes=VMEM_LIMIT_BYTES)


def _tile(dim, pref, unit=LANES):
    if dim <= pref:
        return dim
    t = (pref // unit) * unit
    while t >= unit:
        if dim % t == 0:
            return t
        t -= unit
    return dim


def _ew_tiles(rows, cols):
    tc = cols if (cols % LANES != 0 or cols <= 2048) else _tile(cols, 2048)
    tr = rows
    while tr * tc > 256 * 1024 and tr % 2 == 0 and (tr // 2) % 8 == 0:
        tr //= 2
    return tr, tc


def _sigmoid(v):
    return 1.0 / (1.0 + jnp.exp(-v))


def _dot(a, b):
    return jnp.dot(a, b, preferred_element_type=F32)


def _dot_nt(a, b):
    return lax.dot_general(a, b, (((1,), (1,)), ((), ())), preferred_element_type=F32)


def _dot_tn(a, b):
    return lax.dot_general(a, b, (((0,), (0,)), ((), ())), preferred_element_type=F32)


def _split3(v):
    hi = v.astype(BF16)
    r = v - hi.astype(F32)
    mid = r.astype(BF16)
    lo = (r - mid.astype(F32)).astype(BF16)
    return hi, mid, lo


def _xdot(v, onehot):
    hi, mid, lo = _split3(v)
    return _dot(hi, onehot) + _dot(mid, onehot) + _dot(lo, onehot)


def _xdot_l(onehot, v):
    hi, mid, lo = _split3(v)
    return _dot(onehot, hi) + _dot(onehot, mid) + _dot(onehot, lo)


def _mm(a, b, mode, out_dtype, name, add=None):
    if mode == "nn":
        M, K = a.shape
        N = b.shape[1]
    elif mode == "nt":
        M, K = a.shape
        N = b.shape[0]
    else:
        K, M = a.shape
        N = b.shape[1]
    tm, tn, tk = (2048, 1024, 1024) if mode == "tn" else (512, 1024, 3072)
    tm, tn, tk = _tile(M, tm, 8 if M < LANES else LANES), _tile(N, tn), _tile(K, tk)
    nk = K // tk
    has_add = add is not None

    def product(a_ref, b_ref):
        av = a_ref[...].astype(BF16)
        bv = b_ref[...].astype(BF16)
        if mode == "nn":
            return _dot(av, bv)
        if mode == "nt":
            return _dot_nt(av, bv)
        return _dot_tn(av, bv)

    def body(*refs):
        a_ref, b_ref = refs[:2]
        add_ref = refs[2] if has_add else None
        o_ref = refs[3] if has_add else refs[2]

        def finish(r):
            if has_add:
                r = r + add_ref[...]
            o_ref[...] = r.astype(out_dtype)

        if nk == 1:
            finish(product(a_ref, b_ref))
            return
        acc = refs[-1]
        k = pl.program_id(2)

        @pl.when(k == 0)
        def _():
            acc[...] = product(a_ref, b_ref)

        @pl.when((k > 0) & (k < nk - 1))
        def _():
            acc[...] += product(a_ref, b_ref)

        @pl.when(k == nk - 1)
        def _():
            finish(acc[...] + product(a_ref, b_ref))

    if mode == "nn":
        a_spec = pl.BlockSpec((tm, tk), lambda j, i, k: (i, k))
        b_spec = pl.BlockSpec((tk, tn), lambda j, i, k: (k, j))
    elif mode == "nt":
        a_spec = pl.BlockSpec((tm, tk), lambda j, i, k: (i, k))
        b_spec = pl.BlockSpec((tn, tk), lambda j, i, k: (j, k))
    else:
        a_spec = pl.BlockSpec((tk, tm), lambda j, i, k: (k, i))
        b_spec = pl.BlockSpec((tk, tn), lambda j, i, k: (k, j))
    o_spec = pl.BlockSpec((tm, tn), lambda j, i, k: (i, j))
    in_specs = [a_spec, b_spec] + ([o_spec] if has_add else [])
    args = (a, b) + ((add,) if has_add else ())
    return pl.pallas_call(
        body, name=name, grid=(N // tn, M // tm, nk), in_specs=in_specs, out_specs=o_spec,
        out_shape=jax.ShapeDtypeStruct((M, N), out_dtype),
        scratch_shapes=[pltpu.VMEM((tm, tn), F32)] if nk > 1 else [],
        compiler_params=_cparams("parallel", "parallel", "arbitrary"),
    )(*args)


def _rms_fwd(x, g, name):
    R, Dm = x.shape
    tr = _tile(R, 256, 8)

    def body(x_ref, g_ref, o_ref):
        xv = x_ref[...]
        r = lax.rsqrt(jnp.mean(xv * xv, axis=-1, keepdims=True) + EPS)
        o_ref[...] = (xv * r * g_ref[...]).astype(BF16)

    return pl.pallas_call(
        body, name=name, grid=(R // tr,),
        in_specs=[pl.BlockSpec((tr, Dm), lambda i: (i, 0)), pl.BlockSpec((1, Dm), lambda i: (0, 0))],
        out_specs=pl.BlockSpec((tr, Dm), lambda i: (i, 0)),
        out_shape=jax.ShapeDtypeStruct((R, Dm), BF16),
        compiler_params=_cparams("parallel"),
    )(x, g.reshape(1, Dm))


def _rms_bwd(x, g, dh, dres, name):
    R, Dm = x.shape
    tr = _tile(R, 256, 8)
    has_res = dres is not None

    def body(*refs):
        if has_res:
            x_ref, g_ref, dh_ref, dres_ref, dx_ref, dxb_ref, dg_ref = refs
        else:
            x_ref, g_ref, dh_ref, dx_ref, dxb_ref, dg_ref = refs
        xv = x_ref[...]
        r = lax.rsqrt(jnp.mean(xv * xv, axis=-1, keepdims=True) + EPS)
        xhat = xv * r
        dhv = dh_ref[...]
        dxhat = dhv * g_ref[...]
        dx = r * (dxhat - xhat * jnp.mean(dxhat * xhat, axis=-1, keepdims=True))
        if has_res:
            dx = dx + dres_ref[...]
        dx_ref[...] = dx
        dxb_ref[...] = dx.astype(BF16)

        @pl.when(pl.program_id(0) == 0)
        def _():
            dg_ref[...] = jnp.zeros_like(dg_ref)

        dg_ref[...] += jnp.sum(dhv * xhat, axis=0, keepdims=True)

    row = pl.BlockSpec((tr, Dm), lambda i: (i, 0))
    vec = pl.BlockSpec((1, Dm), lambda i: (0, 0))
    in_specs = [row, vec, row] + ([row] if has_res else [])
    args = (x, g.reshape(1, Dm), dh) + ((dres,) if has_res else ())
    dx, dxb, dg = pl.pallas_call(
        body, name=name, grid=(R // tr,), in_specs=in_specs, out_specs=[row, row, vec],
        out_shape=[jax.ShapeDtypeStruct((R, Dm), F32), jax.ShapeDtypeStruct((R, Dm), BF16),
                   jax.ShapeDtypeStruct((1, Dm), F32)],
        compiler_params=_cparams("arbitrary"),
    )(*args)
    return dx, dxb, dg.reshape(Dm)


def _loss_head(x, g, tgt, name):
    R, Dm = x.shape
    tr = _tile(R, 256, 8)

    def body(x_ref, g_ref, t_ref, loss_ref, dx_ref, dxb_ref, dg_ref):
        xv = x_ref[...]
        gv = g_ref[...]
        r = lax.rsqrt(jnp.mean(xv * xv, axis=-1, keepdims=True) + EPS)
        xhat = xv * r
        e = xhat * gv - t_ref[...]
        part = jnp.sum(jnp.mean(e * e, axis=-1, keepdims=True), axis=0, keepdims=True) * 0.5
        dy = e * (1.0 / Dm)
        dxhat = dy * gv
        dx = r * (dxhat - xhat * jnp.mean(dxhat * xhat, axis=-1, keepdims=True))
        dx_ref[...] = dx
        dxb_ref[...] = dx.astype(BF16)

        @pl.when(pl.program_id(0) == 0)
        def _():
            dg_ref[...] = jnp.zeros_like(dg_ref)
            loss_ref[...] = jnp.zeros_like(loss_ref)

        dg_ref[...] += jnp.sum(dy * xhat, axis=0, keepdims=True)
        loss_ref[...] += jnp.broadcast_to(part, loss_ref.shape)

    row = pl.BlockSpec((tr, Dm), lambda i: (i, 0))
    vec = pl.BlockSpec((1, Dm), lambda i: (0, 0))
    lsp = pl.BlockSpec((1, LANES), lambda i: (0, 0))
    loss, dx, dxb, dg = pl.pallas_call(
        body, name=name, grid=(R // tr,), in_specs=[row, vec, row], out_specs=[lsp, row, row, vec],
        out_shape=[jax.ShapeDtypeStruct((1, LANES), F32), jax.ShapeDtypeStruct((R, Dm), F32),
                   jax.ShapeDtypeStruct((R, Dm), BF16), jax.ShapeDtypeStruct((1, Dm), F32)],
        compiler_params=_cparams("arbitrary"),
    )(x, g.reshape(1, Dm), tgt)
    return loss[0, 0], dx, dxb, dg.reshape(Dm)


def _elementwise(fn, mats, vecs, out_dtypes, name):
    R, C = mats[0].shape
    tr, tc = _ew_tiles(R, C)
    nm, nv, no = len(mats), len(vecs), len(out_dtypes)

    def body(*refs):
        ins = [r[...] for r in refs[:nm + nv]]
        outs = fn(*ins)
        for o_ref, o in zip(refs[nm + nv:], outs):
            o_ref[...] = o.astype(o_ref.dtype)

    blk = pl.BlockSpec((tr, tc), lambda i, j: (i, j))
    vblk = pl.BlockSpec((1, tc), lambda i, j: (0, j))
    res = pl.pallas_call(
        body, name=name, grid=(R // tr, C // tc),
        in_specs=[blk] * nm + [vblk] * nv, out_specs=[blk] * no,
        out_shape=[jax.ShapeDtypeStruct((R, C), dt) for dt in out_dtypes],
        compiler_params=_cparams("parallel", "parallel"),
    )(*mats, *[v.reshape(1, C) for v in vecs])
    return res


def _sum_lead(arr, name):
    P_, R, C = arr.shape
    tr, tc = _ew_tiles(R, C)

    def body(a_ref, o_ref):
        s = a_ref[0].astype(F32)
        for p in range(1, P_):
            s = s + a_ref[p].astype(F32)
        o_ref[...] = s

    return pl.pallas_call(
        body, name=name, grid=(R // tr, C // tc),
        in_specs=[pl.BlockSpec((P_, tr, tc), lambda i, j: (0, i, j))],
        out_specs=pl.BlockSpec((tr, tc), lambda i, j: (i, j)),
        out_shape=jax.ShapeDtypeStruct((R, C), F32),
        compiler_params=_cparams("parallel", "parallel"),
    )(arr)


def _adamw(w, m, v, gparts, name):
    P_ = len(gparts)
    R, C = w.shape
    tr, tc = _ew_tiles(R, C)
    c1 = 1.0 / (1.0 - ADAM_B1 ** ADAM_STEP)
    c2 = 1.0 / (1.0 - ADAM_B2 ** ADAM_STEP)

    def body(w_ref, m_ref, v_ref, *rest):
        g_refs, (go_ref, d_ref, mo_ref, vo_ref) = rest[:P_], rest[P_:]
        g = g_refs[0][...]
        for g_ref in g_refs[1:]:
            g = g + g_ref[...]
        mn = ADAM_B1 * m_ref[...] + (1.0 - ADAM_B1) * g
        vn = ADAM_B2 * v_ref[...] + (1.0 - ADAM_B2) * (g * g)
        m_hat = mn * c1
        v_hat = vn * c2
        d_ref[...] = -ADAM_LR * (m_hat / (jnp.sqrt(v_hat) + ADAM_EPS) + ADAM_WD * w_ref[...])
        go_ref[...] = g
        mo_ref[...] = mn
        vo_ref[...] = vn

    blk = pl.BlockSpec((tr, tc), lambda i, j: (i, j))
    sds = jax.ShapeDtypeStruct((R, C), F32)
    return pl.pallas_call(
        body, name=name, grid=(R // tr, C // tc),
        in_specs=[blk] * (3 + P_), out_specs=[blk] * 4, out_shape=[sds] * 4,
        compiler_params=_cparams("parallel", "parallel"),
    )(w, m, v, *gparts)


SUBLANES = 8


def _conv_pre(u, up, w_ref, b, first):
    tr = u.shape[0]
    rows = lax.broadcasted_iota(jnp.int32, u.shape, 0)
    keep = 1.0 - first.astype(F32)
    acc = b + w_ref[SSD_CONV - 1:SSD_CONV, :] * u
    shifted = []
    for j in range(1, SSD_CONV):
        su = pltpu.roll(u, j, 0)
        sp = jnp.tile(pltpu.roll(up, j, 0) * keep, (tr // SUBLANES, 1))
        sh = jnp.where(rows < j, sp, su)
        shifted.append(sh)
        acc = acc + w_ref[SSD_CONV - 1 - j:SSD_CONV - j, :] * sh
    return acc, shifted


def _conv_fwd(u, w, b, name):
    T, C = u.shape
    tr, tc = _tile(T, 256, 8), _tile(C, 1024)

    def body(u_ref, up_ref, w_ref, b_ref, o_ref):
        pre, _ = _conv_pre(u_ref[...], up_ref[...], w_ref, b_ref[...], pl.program_id(0) == 0)
        o_ref[...] = pre * _sigmoid(pre)

    return pl.pallas_call(
        body, name=name, grid=(T // tr, C // tc),
        in_specs=[pl.BlockSpec((tr, tc), lambda i, j: (i, j)),
                  pl.BlockSpec((SUBLANES, tc), lambda i, j: (jnp.maximum(i * (tr // SUBLANES) - 1, 0), j)),
                  pl.BlockSpec((SSD_CONV, tc), lambda i, j: (0, j)),
                  pl.BlockSpec((1, tc), lambda i, j: (0, j))],
        out_specs=pl.BlockSpec((tr, tc), lambda i, j: (i, j)),
        out_shape=jax.ShapeDtypeStruct((T, C), F32),
        compiler_params=_cparams("parallel", "parallel"),
    )(u, u, w, b.reshape(1, C))


def _conv_bwd_pre(u, w, b, dact, name):
    T, C = u.shape
    tr, tc = _tile(T, 256, 8), _tile(C, 1024)

    def body(u_ref, up_ref, w_ref, b_ref, da_ref, dp_ref, dw_ref, db_ref):
        i = pl.program_id(1)
        uv = u_ref[...]
        pre, shifted = _conv_pre(uv, up_ref[...], w_ref, b_ref[...], i == 0)
        sg = _sigmoid(pre)
        dpre = da_ref[...] * (sg * (1.0 + pre * (1.0 - sg)))
        dp_ref[...] = dpre

        @pl.when(i == 0)
        def _():
            dw_ref[...] = jnp.zeros_like(dw_ref)
            db_ref[...] = jnp.zeros_like(db_ref)

        db_ref[...] += jnp.sum(dpre, axis=0, keepdims=True)
        dw_ref[SSD_CONV - 1:SSD_CONV, :] += jnp.sum(dpre * uv, axis=0, keepdims=True)
        for j in range(1, SSD_CONV):
            dw_ref[SSD_CONV - 1 - j:SSD_CONV - j, :] += jnp.sum(dpre * shifted[j - 1], axis=0, keepdims=True)

    blk = pl.BlockSpec((tr, tc), lambda j, i: (i, j))
    dpre, dw, db = pl.pallas_call(
        body, name=name, grid=(C // tc, T // tr),
        in_specs=[blk, pl.BlockSpec((SUBLANES, tc), lambda j, i: (jnp.maximum(i * (tr // SUBLANES) - 1, 0), j)),
                  pl.BlockSpec((SSD_CONV, tc), lambda j, i: (0, j)),
                  pl.BlockSpec((1, tc), lambda j, i: (0, j)), blk],
        out_specs=[blk, pl.BlockSpec((SSD_CONV, tc), lambda j, i: (0, j)), pl.BlockSpec((1, tc), lambda j, i: (0, j))],
        out_shape=[jax.ShapeDtypeStruct((T, C), F32), jax.ShapeDtypeStruct((SSD_CONV, C), F32),
                   jax.ShapeDtypeStruct((1, C), F32)],
        compiler_params=_cparams("parallel", "arbitrary"),
    )(u, u, w, b.reshape(1, C), dact)
    return dpre, dw, db.reshape(C)


def _conv_bwd_in(dpre, w, name):
    T, C = dpre.shape
    tr, tc = _tile(T, 256, 8), _tile(C, 1024)
    nb = T // tr

    def body(d_ref, dn_ref, w_ref, o_ref):
        d = d_ref[...]
        keep = 1.0 - (pl.program_id(0) == nb - 1).astype(F32)
        dn = dn_ref[...] * keep
        rows = lax.broadcasted_iota(jnp.int32, d.shape, 0)
        acc = w_ref[SSD_CONV - 1:SSD_CONV, :] * d
        for j in range(1, SSD_CONV):
            sd = pltpu.roll(d, tr - j, 0)
            sn = jnp.tile(pltpu.roll(dn, SUBLANES - j, 0), (tr // SUBLANES, 1))
            acc = acc + w_ref[SSD_CONV - 1 - j:SSD_CONV - j, :] * jnp.where(rows >= tr - j, sn, sd)
        o_ref[...] = acc.astype(BF16)

    return pl.pallas_call(
        body, name=name, grid=(nb, C // tc),
        in_specs=[pl.BlockSpec((tr, tc), lambda i, j: (i, j)),
                  pl.BlockSpec((SUBLANES, tc), lambda i, j: (jnp.minimum((i + 1) * (tr // SUBLANES), T // SUBLANES - 1), j)),
                  pl.BlockSpec((SSD_CONV, tc), lambda i, j: (0, j))],
        out_specs=pl.BlockSpec((tr, tc), lambda i, j: (i, j)),
        out_shape=jax.ShapeDtypeStruct((T, C), BF16),
        compiler_params=_cparams("parallel", "parallel"),
    )(dpre, dpre, w)


def _mem_probs(qh, kh, scale):
    s = _dot_nt(qh.astype(BF16), kh.astype(BF16)) * scale
    m = jnp.max(s, axis=-1, keepdims=True)
    p = jnp.exp(s - m)
    return p / jnp.sum(p, axis=-1, keepdims=True)


def _memattn_fwd(q, kv, name):
    T, MW = q.shape
    NM = kv.shape[0]
    hd = MW // MEM_HEADS
    scale = hd ** -0.5
    tq = _tile(T, 512, 8)

    def body(q_ref, kv_ref, o_ref):
        for h in range(MEM_HEADS):
            sl = slice(h * hd, (h + 1) * hd)
            p = _mem_probs(q_ref[:, sl], kv_ref[:, sl], scale)
            vh = kv_ref[:, MW + h * hd:MW + (h + 1) * hd]
            o_ref[:, sl] = _dot(p.astype(BF16), vh.astype(BF16))

    return pl.pallas_call(
        body, name=name, grid=(T // tq,),
        in_specs=[pl.BlockSpec((tq, MW), lambda i: (i, 0)), pl.BlockSpec((NM, 2 * MW), lambda i: (0, 0))],
        out_specs=pl.BlockSpec((tq, MW), lambda i: (i, 0)),
        out_shape=jax.ShapeDtypeStruct((T, MW), F32),
        compiler_params=_cparams("parallel"),
    )(q, kv)


def _memattn_bwd(q, kv, dy, name):
    T, MW = q.shape
    NM = kv.shape[0]
    hd = MW // MEM_HEADS
    scale = hd ** -0.5
    tq = _tile(T, 512, 8)

    def body(q_ref, kv_ref, dy_ref, dq_ref, dkv_ref):
        @pl.when(pl.program_id(0) == 0)
        def _():
            dkv_ref[...] = jnp.zeros_like(dkv_ref)

        for h in range(MEM_HEADS):
            sl = slice(h * hd, (h + 1) * hd)
            vsl = slice(MW + h * hd, MW + (h + 1) * hd)
            qh = q_ref[:, sl]
            kh = kv_ref[:, sl]
            vh = kv_ref[:, vsl]
            dyh = dy_ref[:, sl].astype(BF16)
            p = _mem_probs(qh, kh, scale)
            dp = _dot_nt(dyh, vh.astype(BF16))
            ds = p * (dp - jnp.sum(dp * p, axis=-1, keepdims=True)) * scale
            dq_ref[:, sl] = _dot(ds.astype(BF16), kh.astype(BF16)).astype(BF16)
            dkv_ref[:, sl] += _dot(ds.T.astype(BF16), qh.astype(BF16))
            dkv_ref[:, vsl] += _dot(p.T.astype(BF16), dyh)

    return pl.pallas_call(
        body, name=name, grid=(T // tq,),
        in_specs=[pl.BlockSpec((tq, MW), lambda i: (i, 0)), pl.BlockSpec((NM, 2 * MW), lambda i: (0, 0)),
                  pl.BlockSpec((tq, MW), lambda i: (i, 0))],
        out_specs=[pl.BlockSpec((tq, MW), lambda i: (i, 0)), pl.BlockSpec((NM, 2 * MW), lambda i: (0, 0))],
        out_shape=[jax.ShapeDtypeStruct((T, MW), BF16), jax.ShapeDtypeStruct((NM, 2 * MW), F32)],
        compiler_params=_cparams("arbitrary"),
    )(q, kv, dy)


def _silu_parts(z):
    sg = _sigmoid(z)
    return z * sg, sg * (1.0 + z * (1.0 - sg))


def _gate_fwd(a, z, name):
    return _elementwise(lambda av, zv: (av * _silu_parts(zv)[0],), [a, z], [], [BF16], name)[0]


def _gate_bwd(a, z, d, name):
    def fn(av, zv, dv):
        s, ds = _silu_parts(zv)
        return dv * s, dv * av * ds
    return _elementwise(fn, [a, z, d], [], [F32, BF16], name)


def _gate_norm_fwd(y, z, g, groups, name):
    T, C = y.shape
    gw = C // groups
    tr = _tile(T, 128, 8)

    def body(y_ref, z_ref, g_ref, o_ref):
        for k in range(groups):
            sl = slice(k * gw, (k + 1) * gw)
            u = y_ref[:, sl] * _silu_parts(z_ref[:, sl])[0]
            r = lax.rsqrt(jnp.mean(u * u, axis=-1, keepdims=True) + EPS)
            o_ref[:, sl] = (u * r * g_ref[:, sl]).astype(BF16)

    row = pl.BlockSpec((tr, C), lambda i: (i, 0))
    return pl.pallas_call(
        body, name=name, grid=(T // tr,), in_specs=[row, row, pl.BlockSpec((1, C), lambda i: (0, 0))],
        out_specs=row, out_shape=jax.ShapeDtypeStruct((T, C), BF16),
        compiler_params=_cparams("parallel"),
    )(y, z, g.reshape(1, C))


def _gate_norm_bwd(y, z, g, d, groups, name):
    T, C = y.shape
    gw = C // groups
    tr = _tile(T, 128, 8)

    def body(y_ref, z_ref, g_ref, d_ref, dy_ref, dz_ref, dg_ref):
        @pl.when(pl.program_id(0) == 0)
        def _():
            dg_ref[...] = jnp.zeros_like(dg_ref)

        for k in range(groups):
            sl = slice(k * gw, (k + 1) * gw)
            yv = y_ref[:, sl]
            s, ds = _silu_parts(z_ref[:, sl])
            u = yv * s
            r = lax.rsqrt(jnp.mean(u * u, axis=-1, keepdims=True) + EPS)
            uhat = u * r
            dv = d_ref[:, sl]
            dg_ref[:, sl] += jnp.sum(dv * uhat, axis=0, keepdims=True)
            duhat = dv * g_ref[:, sl]
            du = r * (duhat - uhat * jnp.mean(duhat * uhat, axis=-1, keepdims=True))
            dy_ref[:, sl] = du * s
            dz_ref[:, sl] = (du * yv * ds).astype(BF16)

    row = pl.BlockSpec((tr, C), lambda i: (i, 0))
    vec = pl.BlockSpec((1, C), lambda i: (0, 0))
    dy, dz, dg = pl.pallas_call(
        body, name=name, grid=(T // tr,), in_specs=[row, row, vec, row], out_specs=[row, row, vec],
        out_shape=[jax.ShapeDtypeStruct((T, C), F32), jax.ShapeDtypeStruct((T, C), BF16),
                   jax.ShapeDtypeStruct((1, C), F32)],
        compiler_params=_cparams("arbitrary"),
    )(y, z, g.reshape(1, C), d)
    return dy, dz, dg.reshape(C)


def _ssd_common(raw, bias, alog, Q, HP, HPG):
    P_ = SSD_HEAD_DIM
    dt_in = raw + bias
    dt = jnp.maximum(dt_in, 0.0) + jnp.log(1.0 + jnp.exp(-jnp.abs(dt_in)))
    a = -jnp.exp(alog)
    r_q = lax.broadcasted_iota(jnp.int32, (Q, Q), 0)
    c_q = lax.broadcasted_iota(jnp.int32, (Q, Q), 1)
    causal = r_q >= c_q
    tril = causal.astype(BF16)
    A = _xdot_l(tril, dt * a)
    e_r = lax.broadcasted_iota(jnp.int32, (LANES, HP), 0)
    e_c = lax.broadcasted_iota(jnp.int32, (LANES, HP), 1)
    E = ((e_c >= e_r * P_) & (e_c < (e_r + 1) * P_) & (e_r < HPG)).astype(BF16)
    return dt, a, A, causal, E


def _head_cols(v, vt, j):
    lane = lax.broadcasted_iota(jnp.int32, v.shape, 1)
    sub = lax.broadcasted_iota(jnp.int32, vt.shape, 0)
    col = jnp.sum(jnp.where(lane == j, v, 0.0), axis=-1, keepdims=True)
    row = jnp.sum(jnp.where(sub == j, vt, 0.0), axis=0, keepdims=True)
    return col, row


def _ssd_fwd(act, raw_g, bias_g, alog_g, dsk_g, TOK, name):
    T = act.shape[0]
    G, N, Q, P_ = SSD_GROUPS, SSD_STATE, SSD_CHUNK, SSD_HEAD_DIM
    HP = TOK // G
    HPG = HP // P_
    NC = T // Q

    def body(x_ref, b_ref, c_ref, raw_ref, bias_ref, alog_ref, dsk_ref, y_ref, hp_ref, hT):
        @pl.when(pl.program_id(1) == 0)
        def _():
            hT[...] = jnp.zeros_like(hT)

        xs = x_ref[...]
        Bb = b_ref[...].astype(BF16)
        Cb = c_ref[...].astype(BF16)
        dt, a, A, causal, E = _ssd_common(raw_ref[...], bias_ref[...], alog_ref[...], Q, HP, HPG)
        AT = A.T
        dt_e = _xdot(dt, E)
        A_e = _xdot(A, E)
        dsk_e = _xdot(jnp.broadcast_to(dsk_ref[...], (Q, LANES)), E)
        rows = lax.broadcasted_iota(jnp.int32, (Q, HP), 0)
        cols = lax.broadcasted_iota(jnp.int32, (Q, HP), 1)
        Al_e = jnp.sum(jnp.where(rows == Q - 1, A_e, 0.0), axis=0, keepdims=True)
        xdt = xs * dt_e
        hprev = hT[...]
        hp_ref[...] = hprev
        CB = _dot_nt(Cb, Bb)
        y = _dot(Cb, hprev.astype(BF16)) * jnp.exp(A_e) + dsk_e * xs
        for j in range(HPG):
            a_col, a_row = _head_cols(A, AT, j)
            L = jnp.exp(jnp.where(causal, a_col - a_row, NEG))
            xj = jnp.where((cols >= j * P_) & (cols < (j + 1) * P_), xdt, 0.0)
            y = y + _dot((CB * L).astype(BF16), xj.astype(BF16))
        y_ref[...] = y
        dte = jnp.exp(Al_e - A_e)
        hT[...] = jnp.exp(Al_e) * hprev + _dot(b_ref[...].T.astype(BF16), (xdt * dte).astype(BF16))

    nbx = TOK // N
    par = pl.BlockSpec((None, 1, LANES), lambda g, c: (g, 0, 0))
    return pl.pallas_call(
        body, name=name, grid=(G, NC),
        in_specs=[pl.BlockSpec((Q, HP), lambda g, c: (c, g)),
                  pl.BlockSpec((Q, N), lambda g, c: (c, nbx + g)),
                  pl.BlockSpec((Q, N), lambda g, c: (c, nbx + G + g)),
                  pl.BlockSpec((None, Q, LANES), lambda g, c: (g, c, 0)), par, par, par],
        out_specs=[pl.BlockSpec((Q, HP), lambda g, c: (c, g)),
                   pl.BlockSpec((None, None, N, HP), lambda g, c: (g, c, 0, 0))],
        out_shape=[jax.ShapeDtypeStruct((T, TOK), F32), jax.ShapeDtypeStruct((G, NC, N, HP), F32)],
        scratch_shapes=[pltpu.VMEM((N, HP), F32)],
        compiler_params=_cparams("parallel", "arbitrary"),
    )(act, act, act, raw_g, bias_g, alog_g, dsk_g)


def _ssd_bwd(act, raw_g, bias_g, alog_g, dsk_g, hprev, dy, TOK, name):
    T = act.shape[0]
    G, N, Q, P_ = SSD_GROUPS, SSD_STATE, SSD_CHUNK, SSD_HEAD_DIM
    HP = TOK // G
    HPG = HP // P_
    NC = T // Q

    def body(x_ref, b_ref, c_ref, raw_ref, bias_ref, alog_ref, dsk_ref, hp_ref, dy_ref,
             dx_ref, db_ref, dc_ref, draw_ref, dalog_ref, dbias_ref, ddsk_ref, dHT):
        @pl.when(pl.program_id(1) == 0)
        def _():
            dHT[...] = jnp.zeros_like(dHT)
            dalog_ref[...] = jnp.zeros_like(dalog_ref)
            dbias_ref[...] = jnp.zeros_like(dbias_ref)
            ddsk_ref[...] = jnp.zeros_like(ddsk_ref)

        xs = x_ref[...]
        dyv = dy_ref[...]
        Bm = b_ref[...]
        Cm = c_ref[...]
        Bb = Bm.astype(BF16)
        Cb = Cm.astype(BF16)
        raw_in = raw_ref[...] + bias_ref[...]
        dt, a, A, causal, E = _ssd_common(raw_ref[...], bias_ref[...], alog_ref[...], Q, HP, HPG)
        AT = A.T
        t_r = lax.broadcasted_iota(jnp.int32, (HP, LANES), 0)
        t_c = lax.broadcasted_iota(jnp.int32, (HP, LANES), 1)
        ET = ((t_r >= t_c * P_) & (t_r < (t_c + 1) * P_) & (t_c < HPG)).astype(BF16)
        dt_e = _xdot(dt, E)
        A_e = _xdot(A, E)
        dsk_e = _xdot(jnp.broadcast_to(dsk_ref[...], (Q, LANES)), E)
        rows = lax.broadcasted_iota(jnp.int32, (Q, HP), 0)
        cols = lax.broadcasted_iota(jnp.int32, (Q, HP), 1)
        last = rows == Q - 1
        Al_e = jnp.sum(jnp.where(last, A_e, 0.0), axis=0, keepdims=True)
        eA = jnp.exp(A_e)
        eAl = jnp.exp(Al_e)
        dte = jnp.exp(Al_e - A_e)
        xdt = xs * dt_e
        xdt_b = xdt.astype(BF16)
        CB = _dot_nt(Cb, Bb)
        HT = hp_ref[...]
        HTb = HT.astype(BF16)
        dH = dHT[...]
        dHb = dH.astype(BF16)
        dys = (dyv * eA).astype(BF16)
        CH = _dot(Cb, HTb)
        dC = _dot_nt(dys, HTb)
        dH_prev = _dot(Cm.T.astype(BF16), dys) + eAl * dH
        dAe = dyv * CH * eA
        dAl = eAl * jnp.sum(dH * HT, axis=0, keepdims=True)
        W = _dot(Bb, dHb)
        dxdt = W * dte
        dd = W * xdt * dte
        dB = _dot_nt((xdt * dte).astype(BF16), dHb)
        dAl = dAl + jnp.sum(dd, axis=0, keepdims=True)
        dAe = dAe - dd + jnp.where(last, dAl, 0.0)
        lane = lax.broadcasted_iota(jnp.int32, (Q, LANES), 1)
        sub = lax.broadcasted_iota(jnp.int32, (LANES, Q), 0)
        dCB = jnp.zeros((Q, Q), F32)
        dA_col = jnp.zeros((Q, LANES), F32)
        dA_row = jnp.zeros((LANES, Q), F32)
        for j in range(HPG):
            a_col, a_row = _head_cols(A, AT, j)
            L = jnp.exp(jnp.where(causal, a_col - a_row, NEG))
            hmask = (cols >= j * P_) & (cols < (j + 1) * P_)
            dyj = jnp.where(hmask, dyv, 0.0).astype(BF16)
            GL = _dot_nt(dyj, xdt_b) * L
            dCB = dCB + GL
            dLL = GL * CB
            dA_col = dA_col + jnp.where(lane == j, jnp.sum(dLL, axis=-1, keepdims=True), 0.0)
            dA_row = dA_row + jnp.where(sub == j, jnp.sum(dLL, axis=0, keepdims=True), 0.0)
            dxdt = dxdt + _dot((CB * L).T.astype(BF16), dyj)
        dC = dC + _dot(dCB.astype(BF16), Bb)
        dB = dB + _dot(dCB.T.astype(BF16), Cb)
        dA = dA_col - dA_row.T + _xdot(dAe, ET)
        triu = (lax.broadcasted_iota(jnp.int32, (Q, Q), 1) >= lax.broadcasted_iota(jnp.int32, (Q, Q), 0)).astype(BF16)
        rcs = _xdot_l(triu, dA)
        ddt = a * rcs + _xdot(dxdt * xs, ET)
        draw = ddt * _sigmoid(raw_in)
        draw_ref[...] = draw
        dalog_ref[...] += jnp.sum(dt * rcs, axis=0, keepdims=True) * a
        dbias_ref[...] += jnp.sum(draw, axis=0, keepdims=True)
        ddsk_ref[...] += jnp.sum(_xdot(dyv * xs, ET), axis=0, keepdims=True)
        dx_ref[...] = dxdt * dt_e + dsk_e * dyv
        db_ref[...] = dB
        dc_ref[...] = dC
        dHT[...] = dH_prev

    nbx = TOK // N
    rv = lambda c: NC - 1 - c
    par = pl.BlockSpec((None, 1, LANES), lambda g, c: (g, 0, 0))
    xsp = pl.BlockSpec((Q, HP), lambda g, c: (rv(c), g))
    outs = pl.pallas_call(
        body, name=name, grid=(G, NC),
        in_specs=[xsp,
                  pl.BlockSpec((Q, N), lambda g, c: (rv(c), nbx + g)),
                  pl.BlockSpec((Q, N), lambda g, c: (rv(c), nbx + G + g)),
                  pl.BlockSpec((None, Q, LANES), lambda g, c: (g, rv(c), 0)), par, par, par,
                  pl.BlockSpec((None, None, N, HP), lambda g, c: (g, rv(c), 0, 0)), xsp],
        out_specs=[xsp, pl.BlockSpec((Q, N), lambda g, c: (rv(c), g)), pl.BlockSpec((Q, N), lambda g, c: (rv(c), g)),
                   pl.BlockSpec((None, Q, LANES), lambda g, c: (g, rv(c), 0)), par, par, par],
        out_shape=[jax.ShapeDtypeStruct((T, TOK), F32), jax.ShapeDtypeStruct((T, G * N), F32),
                   jax.ShapeDtypeStruct((T, G * N), F32), jax.ShapeDtypeStruct((G, T, LANES), F32),
                   jax.ShapeDtypeStruct((G, 1, LANES), F32), jax.ShapeDtypeStruct((G, 1, LANES), F32),
                   jax.ShapeDtypeStruct((G, 1, LANES), F32)],
        scratch_shapes=[pltpu.VMEM((N, HP), F32)],
        compiler_params=_cparams("parallel", "arbitrary"),
    )(act, act, act, raw_g, bias_g, alog_g, dsk_g, hprev, dy)
    return outs


def _heads_per_block(H):
    for hb in (8, 6, 4, 3, 2, 1):
        if H % hb == 0:
            return hb
    return 1


def _alibi_slope(head_index, n_alibi):
    c = -ALIBI_MAX_EXP * math.log(2.0) / n_alibi
    return jnp.exp(jnp.full((1, 1), c, F32) * (head_index + 1).astype(F32))


def _attn_masks(b, nb):
    Bq = ATTN_BLOCK
    iq = lax.broadcasted_iota(jnp.int32, (Bq, 2 * Bq), 0)
    jk = lax.broadcasted_iota(jnp.int32, (Bq, 2 * Bq), 1)
    rel = iq + Bq - jk
    mask = (rel >= 0) & (rel <= Bq) & (jk + jnp.where(b > 0, Bq, 0) >= Bq)
    rel_n = lax.broadcasted_iota(jnp.int32, (Bq, Bq), 0) + Bq - lax.broadcasted_iota(jnp.int32, (Bq, Bq), 1)
    mask_n = (rel_n + jnp.where(b < nb - 1, 0, 4 * Bq)) <= Bq
    return rel.astype(F32), mask, rel_n.astype(F32), mask_n


def _rows2(a, b):
    return jnp.concatenate([a, b], axis=0)


ATTN_UNITS_IN_FLIGHT = 4


def _attn_units_fwd(load, n_units, slopes, masks, scale, store):
    rel_f, mask, _, _ = masks
    for g0 in range(0, n_units, ATTN_UNITS_IN_FLIGHT):
        ids = range(g0, min(g0 + ATTN_UNITS_IN_FLIGHT, n_units))
        units = [load(i) for i in ids]
        raw = [_dot_nt(q, _rows2(kp, kc)) for q, kc, kp, vc, vp in units]
        soft = []
        for i, s_raw in zip(ids, raw):
            s = jnp.where(mask, s_raw * scale - slopes[i] * rel_f, NEG)
            m = jnp.max(s, axis=-1, keepdims=True)
            p = jnp.exp(s - m)
            den = jnp.sum(p, axis=-1, keepdims=True)
            soft.append((p.astype(BF16), den, m + jnp.log(den)))
        for i, (p, den, lse), (q, kc, kp, vc, vp) in zip(ids, soft, units):
            store(i, _dot(p, _rows2(vp, vc)) / den, lse)


def _attn_units_bwd(load, n_units, slopes, masks, scale, store):
    rel_f, mask, reln_f, mask_n = masks
    Bq = ATTN_BLOCK
    for g0 in range(0, n_units, ATTN_UNITS_IN_FLIGHT):
        ids = range(g0, min(g0 + ATTN_UNITS_IN_FLIGHT, n_units))
        units = [load(i) for i in ids]
        prods = []
        for q0, q1, kp, k0, vp, v0, do0, do1, y0, y1, lse0, lse1 in units:
            kcat = _rows2(kp, k0)
            do0b = do0.astype(BF16)
            do1b = do1.astype(BF16)
            prods.append((kcat, do0b, do1b, _dot_nt(q0, kcat), _dot_nt(do0b, _rows2(vp, v0)), _dot_nt(q1, k0),
                          _dot_nt(do1b, v0)))
        mids = []
        for i, u, (kcat, do0b, do1b, s_raw, dp_raw, sn_raw, dpn_raw) in zip(ids, units, prods):
            q0, q1, kp, k0, vp, v0, do0, do1, y0, y1, lse0, lse1 = u
            delta0 = jnp.sum(do0 * y0, axis=-1, keepdims=True)
            delta1 = jnp.sum(do1 * y1, axis=-1, keepdims=True)
            p = jnp.exp(jnp.where(mask, s_raw * scale - slopes[i] * rel_f, NEG) - lse0)
            ds = p * (dp_raw - delta0)
            p_n = jnp.exp(jnp.where(mask_n, sn_raw * scale - slopes[i] * reln_f, NEG) - lse1)
            ds_n = p_n * (dpn_raw - delta1)
            mids.append((ds.astype(BF16), _rows2(ds[:, Bq:], ds_n).T.astype(BF16),
                         _rows2(p[:, Bq:], p_n).T.astype(BF16)))
        for i, u, (kcat, do0b, do1b, *_), (dsb, dsk_t, pk_t) in zip(ids, units, prods, mids):
            store(i, scale * _dot(dsb, kcat), scale * _dot(dsk_t, _rows2(u[0], u[1])), _dot(pk_t, _rows2(do0b, do1b)))


def _attn_fwd_strided(q, k, v, gi, d, name):
    T, TOK = q.shape
    E_ = ATTN_HEAD_DIM
    H = TOK // E_
    n_alibi = len(DILATED_GROUPS) * H
    Bq = ATTN_BLOCK
    RB = Bq * d
    nb = T // RB
    scale = E_ ** -0.5

    def body(q_ref, kc_ref, kp_ref, vc_ref, vp_ref, o_ref, l_ref):
        masks = _attn_masks(pl.program_id(1), nb)
        slope = _alibi_slope(gi * H + pl.program_id(0), n_alibi) * float(d)
        rows = lambda r: pl.ds(r, Bq, stride=d)

        def load(r):
            return tuple(ref[rows(r), :].astype(BF16) for ref in (q_ref, kc_ref, kp_ref, vc_ref, vp_ref))

        def store(r, o, lse):
            o_ref[rows(r), :] = o
            l_ref[rows(r), :] = jnp.broadcast_to(lse, (Bq, E_))

        _attn_units_fwd(load, d, [slope] * d, masks, scale, store)

    cur = pl.BlockSpec((RB, E_), lambda h, b: (b, h))
    prev = pl.BlockSpec((RB, E_), lambda h, b: (jnp.maximum(b - 1, 0), h))
    sds = jax.ShapeDtypeStruct((T, TOK), F32)
    return pl.pallas_call(
        body, name=name, grid=(H, nb), in_specs=[cur, cur, prev, cur, prev], out_specs=[cur, cur],
        out_shape=[sds, sds], compiler_params=_cparams("parallel", "parallel"),
    )(q, k, k, v, v)


def _attn_bwd_strided(q, k, v, y, lse, dy, gi, d, name):
    T, TOK = q.shape
    E_ = ATTN_HEAD_DIM
    H = TOK // E_
    n_alibi = len(DILATED_GROUPS) * H
    Bq = ATTN_BLOCK
    RB = Bq * d
    nb = T // RB
    scale = E_ ** -0.5

    def body(q0_ref, q1_ref, kp_ref, k0_ref, vp_ref, v0_ref, do0_ref, do1_ref, y0_ref, y1_ref, l0_ref, l1_ref,
             dq_ref, dk_ref, dv_ref):
        masks = _attn_masks(pl.program_id(1), nb)
        slope = _alibi_slope(gi * H + pl.program_id(0), n_alibi) * float(d)
        rows = lambda r: pl.ds(r, Bq, stride=d)

        def load(r):
            return (tuple(ref[rows(r), :].astype(BF16) for ref in (q0_ref, q1_ref, kp_ref, k0_ref, vp_ref, v0_ref))
                    + tuple(ref[rows(r), :] for ref in (do0_ref, do1_ref, y0_ref, y1_ref))
                    + tuple(jnp.max(ref[rows(r), :], axis=-1, keepdims=True) for ref in (l0_ref, l1_ref)))

        def store(r, dq, dk, dv):
            dq_ref[rows(r), :] = dq
            dk_ref[rows(r), :] = dk
            dv_ref[rows(r), :] = dv

        _attn_units_bwd(load, d, [slope] * d, masks, scale, store)

    cur = pl.BlockSpec((RB, E_), lambda h, b: (b, h))
    prev = pl.BlockSpec((RB, E_), lambda h, b: (jnp.maximum(b - 1, 0), h))
    nxt = pl.BlockSpec((RB, E_), lambda h, b: (jnp.minimum(b + 1, nb - 1), h))
    sds = jax.ShapeDtypeStruct((T, TOK), F32)
    return pl.pallas_call(
        body, name=name, grid=(H, nb),
        in_specs=[cur, nxt, prev, cur, prev, cur, cur, nxt, cur, nxt, cur, nxt],
        out_specs=[cur, cur, cur], out_shape=[sds, sds, sds],
        compiler_params=_cparams("parallel", "parallel"),
    )(q, q, k, k, v, v, dy, dy, y, y, lse, lse)


def _attn_fwd(q, k, v, gi, window, d, name):
    T, TOK = q.shape
    E_ = ATTN_HEAD_DIM
    H = TOK // E_
    n_alibi = len(DILATED_GROUPS) * H
    assert window // d == ATTN_BLOCK and (T // d) % ATTN_BLOCK == 0
    if d > 1:
        return _attn_fwd_strided(q, k, v, gi, d, name)
    n_sub = T // d
    nb = n_sub // ATTN_BLOCK
    HB = _heads_per_block(H)
    NHB = H // HB
    hbw = HB * E_
    scale = E_ ** -0.5
    Bq = ATTN_BLOCK

    def body(q_ref, kc_ref, kp_ref, vc_ref, vp_ref, o_ref, l_ref):
        hb = pl.program_id(1)
        masks = _attn_masks(pl.program_id(2), nb)
        slopes = [_alibi_slope(gi * H + hb * HB + hh, n_alibi) * float(d) for hh in range(HB)]
        cols = lambda hh: slice(hh * E_, (hh + 1) * E_)

        def load(hh):
            return tuple(ref[:, cols(hh)].astype(BF16) for ref in (q_ref, kc_ref, kp_ref, vc_ref, vp_ref))

        def store(hh, o, lse):
            o_ref[:, cols(hh)] = o
            l_ref[:, cols(hh)] = jnp.broadcast_to(lse, (Bq, E_))

        _attn_units_fwd(load, HB, slopes, masks, scale, store)

    cur = pl.BlockSpec((Bq, hbw), lambda r, h, b: (b, r * NHB + h))
    prev = pl.BlockSpec((Bq, hbw), lambda r, h, b: (jnp.maximum(b - 1, 0), r * NHB + h))
    view = lambda t: t.reshape(n_sub, d * TOK)
    sds = jax.ShapeDtypeStruct((n_sub, d * TOK), F32)
    o, l = pl.pallas_call(
        body, name=name, grid=(d, NHB, nb), in_specs=[cur, cur, prev, cur, prev], out_specs=[cur, cur],
        out_shape=[sds, sds], compiler_params=_cparams("parallel", "parallel", "parallel"),
    )(view(q), view(k), view(k), view(v), view(v))
    return o.reshape(T, TOK), l.reshape(T, TOK)


def _attn_combine(os_, ls_, name):
    def fn(*v):
        n = len(v) // 2
        o, l = v[:n], v[n:]
        m = l[0]
        for t in l[1:]:
            m = jnp.maximum(m, t)
        e = [jnp.exp(t - m) for t in l]
        den = e[0]
        for t in e[1:]:
            den = den + t
        y = e[0] * o[0]
        for t, u in zip(e[1:], o[1:]):
            y = y + t * u
        return y / den, m + jnp.log(den)
    return _elementwise(fn, list(os_) + list(ls_), [], [F32, F32], name)


def _attn_bwd(q, k, v, y, lse, dy, gi, window, d, name):
    T, TOK = q.shape
    E_ = ATTN_HEAD_DIM
    H = TOK // E_
    n_alibi = len(DILATED_GROUPS) * H
    if d > 1:
        return _attn_bwd_strided(q, k, v, y, lse, dy, gi, d, name)
    n_sub = T // d
    nb = n_sub // ATTN_BLOCK
    HB = _heads_per_block(H)
    NHB = H // HB
    hbw = HB * E_
    scale = E_ ** -0.5
    Bq = ATTN_BLOCK

    def body(q0_ref, q1_ref, kp_ref, k0_ref, vp_ref, v0_ref, do0_ref, do1_ref, y0_ref, y1_ref, l0_ref, l1_ref,
             dq_ref, dk_ref, dv_ref):
        hb = pl.program_id(1)
        masks = _attn_masks(pl.program_id(2), nb)
        slopes = [_alibi_slope(gi * H + hb * HB + hh, n_alibi) * float(d) for hh in range(HB)]
        cols = lambda hh: slice(hh * E_, (hh + 1) * E_)

        def load(hh):
            return (tuple(ref[:, cols(hh)].astype(BF16) for ref in (q0_ref, q1_ref, kp_ref, k0_ref, vp_ref, v0_ref))
                    + tuple(ref[:, cols(hh)] for ref in (do0_ref, do1_ref, y0_ref, y1_ref))
                    + tuple(jnp.max(ref[:, cols(hh)], axis=-1, keepdims=True) for ref in (l0_ref, l1_ref)))

        def store(hh, dq, dk, dv):
            dq_ref[:, cols(hh)] = dq.astype(BF16)
            dk_ref[:, cols(hh)] = dk.astype(BF16)
            dv_ref[:, cols(hh)] = dv.astype(BF16)

        _attn_units_bwd(load, HB, slopes, masks, scale, store)

    cur = pl.BlockSpec((Bq, hbw), lambda r, h, b: (b, r * NHB + h))
    prev = pl.BlockSpec((Bq, hbw), lambda r, h, b: (jnp.maximum(b - 1, 0), r * NHB + h))
    nxt = pl.BlockSpec((Bq, hbw), lambda r, h, b: (jnp.minimum(b + 1, nb - 1), r * NHB + h))
    view = lambda t: t.reshape(n_sub, d * TOK)
    sds = jax.ShapeDtypeStruct((n_sub, d * TOK), BF16)
    dq, dk, dv = pl.pallas_call(
        body, name=name, grid=(d, NHB, nb),
        in_specs=[cur, nxt, prev, cur, prev, cur, cur, nxt, cur, nxt, cur, nxt],
        out_specs=[cur, cur, cur], out_shape=[sds, sds, sds],
        compiler_params=_cparams("parallel", "parallel", "parallel"),
    )(view(q), view(q), view(k), view(k), view(v), view(v), view(dy), view(dy), view(y), view(y), view(lse), view(lse))
    return dq.reshape(T, TOK), dk.reshape(T, TOK), dv.reshape(T, TOK)


_FLIPS = {
    "xy": [(1, 0, 0), (0, 1, 0), (1, 1, 0)],
    "c": [(0, 0, 1)],
    "xyc": [(dx, dy, dc) for dx in (0, 1) for dy in (0, 1) for dc in (0, 1) if (dx, dy, dc) != (0, 0, 0)],
}


def _comm_parts(group, srcs, modes, handshake):
    flips = _FLIPS[group]
    F_ = len(flips)
    P_ = F_ + 1
    n = len(srcs)

    def gidx(px, py, pc):
        if group == "xy":
            return 2 * px + py
        if group == "c":
            return pc
        return 4 * px + 2 * py + pc

    def body(*refs):
        src_refs, out_refs = refs[:n], refs[n:2 * n]
        send_sems, recv_sems, loc_sems = refs[2 * n:]
        x, y, c = lax.axis_index("x"), lax.axis_index("y"), lax.axis_index("c")
        me = gidx(x, y, c)
        peers = [(1 - x if dx else x, 1 - y if dy else y, 1 - c if dc else c) for dx, dy, dc in flips]
        if handshake:
            barrier = pltpu.get_barrier_semaphore()
            for peer in peers:
                pl.semaphore_signal(barrier, inc=1, device_id=peer, device_id_type=MESH)
            pl.semaphore_wait(barrier, F_)
        local, remote = [], []
        for i in range(n):
            mode = modes[i]
            if mode != "swap":
                mine = pltpu.make_async_copy(src_refs[i] if mode == "gather" else src_refs[i].at[me],
                                             out_refs[i].at[me], loc_sems.at[i])
                mine.start()
                local.append(mine)
            for f, peer in enumerate(peers):
                cp = pltpu.make_async_remote_copy(
                    src_ref=src_refs[i].at[gidx(*peer)] if mode == "a2a" else src_refs[i],
                    dst_ref=out_refs[i] if mode == "swap" else out_refs[i].at[me],
                    send_sem=send_sems.at[i * F_ + f], recv_sem=recv_sems.at[i * F_ + f],
                    device_id=peer, device_id_type=MESH)
                cp.start()
                remote.append(cp)
        for cp in local:
            cp.wait()
        for cp in remote:
            cp.wait()

    out_shape = []
    for s, mode in zip(srcs, modes):
        assert mode != "swap" or F_ == 1
        shp = (P_,) + tuple(s.shape) if mode == "gather" else tuple(s.shape)
        out_shape.append(jax.ShapeDtypeStruct(shp, s.dtype))
    sems = [pltpu.SemaphoreType.DMA((n * F_,)), pltpu.SemaphoreType.DMA((n * F_,)), pltpu.SemaphoreType.DMA((n,))]
    return body, out_shape, sems


def _comm(name, group, srcs, modes):
    n = len(srcs)
    body, out_shape, sems = _comm_parts(group, srcs, modes, handshake=False)
    anyspec = pl.BlockSpec(memory_space=pl.ANY)
    return pl.pallas_call(body, name=name, in_specs=[anyspec] * n, out_specs=[anyspec] * n, out_shape=out_shape,
                          scratch_shapes=sems)(*srcs)


def _comm_async(name, collective_id, group, srcs, modes):
    body, out_shape, sems = _comm_parts(group, srcs, modes, handshake=True)
    return pl.kernel(body, name=name, out_type=out_shape,
                     mesh=plsc.ScalarSubcoreMesh(axis_name="sequencer", num_cores=1), scratch_types=sems,
                     compiler_params=pltpu.CompilerParams(collective_id=collective_id))(*srcs)


def _dims(D):
    MIX = 2 * D
    MW = MIX // 4
    TOK = MIX - MW
    H = TOK // SSD_HEAD_DIM
    CONV = TOK + 2 * SSD_GROUPS * SSD_STATE
    return dict(MIX=MIX, MW=MW, TOK=TOK, H=H, CONV=CONV)


def _proj_chain(dsegs, wsegs, name):
    acc = None
    for n, (ds, ws) in enumerate(zip(dsegs, wsegs)):
        acc = _mm(ds, ws, "nt", F32, f"{name}_dh{n}", add=acc)
    return acc


def _pad_lanes(a, width=LANES):
    return jnp.pad(a, [(0, 0)] * (a.ndim - 1) + [(0, width - a.shape[-1])])


def _ssd_layer_fwd(x, kv, p, li):
    T, D = x.shape
    dm = _dims(D)
    TOK, MW, H, CONV = dm["TOK"], dm["MW"], dm["H"], dm["CONV"]
    G = SSD_GROUPS
    HPG = H // G
    w = p["w_in"]
    segs = dict(xbc=w[:, :CONV], dt=_pad_lanes(w[:, CONV:CONV + H]), qm=w[:, CONV + H:CONV + H + MW],
                zt=w[:, CONV + H + MW:CONV + H + MW + TOK], zm=w[:, CONV + H + MW + TOK:])
    h = _rms_fwd(x, p["norm_g"], f"l{li}_rms")
    pr = {k: _mm(h, ws, "nn", BF16 if k == "qm" else F32, f"l{li}_in_{k}") for k, ws in segs.items()}
    act = _conv_fwd(pr["xbc"], p["conv_w"], p["conv_b"], f"l{li}_conv")
    raw_g = _pad_lanes(pr["dt"][:, :H].reshape(T, G, HPG).transpose(1, 0, 2))
    hp = lambda a: _pad_lanes(a.reshape(G, 1, HPG))
    bias_g, alog_g, dsk_g = hp(p["dt_bias"]), hp(p["a_log"]), hp(p["d_skip"])
    y, hprev = _ssd_fwd(act, raw_g, bias_g, alog_g, dsk_g, TOK, f"l{li}_ssd")
    ymem = _memattn_fwd(pr["qm"], kv, f"l{li}_mem")
    gt = _gate_norm_fwd(y, pr["zt"], p["ssd_norm_g"], G, f"l{li}_gate_tok")
    gm = _gate_fwd(ymem, pr["zm"], f"l{li}_gate_mem")
    wo = p["w_out"]
    out = _mm(gt, wo[:TOK], "nn", F32, f"l{li}_out_tok", add=x)
    out = _mm(gm, wo[TOK:], "nn", F32, f"l{li}_out_mem", add=out)
    saved = dict(x=x, h=h, pr=pr, act=act, raw_g=raw_g, par=(bias_g, alog_g, dsk_g), y=y, hprev=hprev, ymem=ymem,
                 gt=gt, gm=gm, segs=segs)
    return out, saved


def _ssd_layer_bwd(dout, doutb, kv, p, s, li):
    x = s["x"]
    T, D = x.shape
    dm = _dims(D)
    TOK, MW, H, CONV = dm["TOK"], dm["MW"], dm["H"], dm["CONV"]
    G = SSD_GROUPS
    HPG = H // G
    wo = p["w_out"]
    pr = s["pr"]
    dgt = _mm(doutb, wo[:TOK], "nt", F32, f"l{li}_dgt")
    dgm = _mm(doutb, wo[TOK:], "nt", F32, f"l{li}_dgm")
    dwo = jnp.concatenate([_mm(s["gt"], doutb, "tn", BF16, f"l{li}_dwo_tok"),
                           _mm(s["gm"], doutb, "tn", BF16, f"l{li}_dwo_mem")], axis=0)
    dy, dzt, dng = _gate_norm_bwd(s["y"], pr["zt"], p["ssd_norm_g"], dgt, G, f"l{li}_gate_tok_b")
    dymem, dzm = _gate_bwd(s["ymem"], pr["zm"], dgm, f"l{li}_gate_mem_b")
    dqm, dkv = _memattn_bwd(pr["qm"], kv, dymem, f"l{li}_mem_b")
    bias_g, alog_g, dsk_g = s["par"]
    dxs, dB, dC, draw_g, dalog, dbias, ddsk = _ssd_bwd(s["act"], s["raw_g"], bias_g, alog_g, dsk_g, s["hprev"], dy, TOK,
                                                      f"l{li}_ssd_b")
    dact = jnp.concatenate([dxs, dB, dC], axis=1)
    dpre, dconv_w, dconv_b = _conv_bwd_pre(pr["xbc"], p["conv_w"], p["conv_b"], dact, f"l{li}_conv_b1")
    dxbc = _conv_bwd_in(dpre, p["conv_w"], f"l{li}_conv_b2")
    draw = _pad_lanes(draw_g[:, :, :HPG].transpose(1, 0, 2).reshape(T, H)).astype(BF16)
    dsegs = dict(xbc=dxbc, dt=draw, qm=dqm, zt=dzt, zm=dzm)
    keys = ["xbc", "dt", "qm", "zt", "zm"]
    dh = _proj_chain([dsegs[k] for k in keys], [s["segs"][k] for k in keys], f"l{li}")
    dws = {k: _mm(s["h"], dsegs[k], "tn", BF16, f"l{li}_dwin_{k}") for k in keys}
    dws["dt"] = dws["dt"][:, :H]
    dwin = jnp.concatenate([dws[k] for k in keys], axis=1)
    dh, dwin, dwo = lax.optimization_barrier((dh, dwin, dwo))
    dx, dxb, dnorm = _rms_bwd(x, p["norm_g"], dh, dout, f"l{li}_rms_b")
    unhead = lambda a: a[:, 0, :HPG].reshape(H)
    grads = dict(norm_g=dnorm, w_in=dwin, conv_w=dconv_w, conv_b=dconv_b, dt_bias=unhead(dbias), a_log=unhead(dalog),
                 d_skip=unhead(ddsk), ssd_norm_g=dng, w_out=dwo)
    return dx, dxb, dkv, grads


def _attn_layer_fwd(x, kv, p, li):
    T, D = x.shape
    dm = _dims(D)
    TOK, MW = dm["TOK"], dm["MW"]
    w = p["w_in"]
    ng = len(DILATED_GROUPS)
    segs = {}
    for g in range(ng):
        for n, nm in enumerate("qkv"):
            c0 = g * 3 * TOK + n * TOK
            segs[f"{nm}{g}"] = w[:, c0:c0 + TOK]
    c0 = ng * 3 * TOK
    segs["qm"] = w[:, c0:c0 + MW]
    segs["zt"] = w[:, c0 + MW:c0 + MW + TOK]
    segs["zm"] = w[:, c0 + MW + TOK:]
    h = _rms_fwd(x, p["norm_g"], f"l{li}_rms")
    dense = {"qm"} | {f"{nm}{g}" for g, (_, d) in enumerate(DILATED_GROUPS) if d == 1 for nm in "qkv"}
    pr = {k: _mm(h, ws, "nn", BF16 if k in dense else F32, f"l{li}_in_{k}") for k, ws in segs.items()}
    os_, ls_ = [], []
    for g, (window, d) in enumerate(DILATED_GROUPS):
        o, l = _attn_fwd(pr[f"q{g}"], pr[f"k{g}"], pr[f"v{g}"], g, window, d, f"l{li}_attn{g}")
        os_.append(o)
        ls_.append(l)
    ytok, lse = _attn_combine(os_, ls_, f"l{li}_combine")
    ymem = _memattn_fwd(pr["qm"], kv, f"l{li}_mem")
    gt = _gate_fwd(ytok, pr["zt"], f"l{li}_gate_tok")
    gm = _gate_fwd(ymem, pr["zm"], f"l{li}_gate_mem")
    wo = p["w_out"]
    out = _mm(gt, wo[:TOK], "nn", F32, f"l{li}_out_tok", add=x)
    out = _mm(gm, wo[TOK:], "nn", F32, f"l{li}_out_mem", add=out)
    saved = dict(x=x, h=h, pr=pr, ytok=ytok, lse=lse, ymem=ymem, gt=gt, gm=gm, segs=segs)
    return out, saved


def _attn_layer_bwd(dout, doutb, kv, p, s, li):
    x = s["x"]
    T, D = x.shape
    dm = _dims(D)
    TOK, MW = dm["TOK"], dm["MW"]
    wo = p["w_out"]
    pr = s["pr"]
    dgt = _mm(doutb, wo[:TOK], "nt", F32, f"l{li}_dgt")
    dgm = _mm(doutb, wo[TOK:], "nt", F32, f"l{li}_dgm")
    dwo = jnp.concatenate([_mm(s["gt"], doutb, "tn", BF16, f"l{li}_dwo_tok"),
                           _mm(s["gm"], doutb, "tn", BF16, f"l{li}_dwo_mem")], axis=0)
    dytok, dzt = _gate_bwd(s["ytok"], pr["zt"], dgt, f"l{li}_gate_tok_b")
    dymem, dzm = _gate_bwd(s["ymem"], pr["zm"], dgm, f"l{li}_gate_mem_b")
    dqm, dkv = _memattn_bwd(pr["qm"], kv, dymem, f"l{li}_mem_b")
    dsegs = {}
    for g, (window, d) in enumerate(DILATED_GROUPS):
        dq, dk, dv = _attn_bwd(pr[f"q{g}"], pr[f"k{g}"], pr[f"v{g}"], s["ytok"], s["lse"], dytok, g, window, d,
                               f"l{li}_attn{g}_b")
        dsegs[f"q{g}"], dsegs[f"k{g}"], dsegs[f"v{g}"] = dq, dk, dv
    dsegs["qm"], dsegs["zt"], dsegs["zm"] = dqm, dzt, dzm
    keys = list(s["segs"].keys())
    dh = _proj_chain([dsegs[k] for k in keys], [s["segs"][k] for k in keys], f"l{li}")
    dwin = jnp.concatenate([_mm(s["h"], dsegs[k], "tn", BF16, f"l{li}_dwin_{k}") for k in keys], axis=1)
    dh, dwin, dwo = lax.optimization_barrier((dh, dwin, dwo))
    dx, dxb, dnorm = _rms_bwd(x, p["norm_g"], dh, dout, f"l{li}_rms_b")
    return dx, dxb, dkv, dict(norm_g=dnorm, w_in=dwin, w_out=dwo)


def _local_step(x, mem, tgt, mem_norm_g, final_norm_g, n_layers, layer_params, on_layer_grads):
    mem_n = _rms_fwd(mem, mem_norm_g, "mem_rms")
    layers, kvs, saved = [], [], []
    for li in range(n_layers):
        p, x = layer_params(li, x)
        fwd = _ssd_layer_fwd if li % 2 == 0 else _attn_layer_fwd
        kv = _mm(mem_n, p["w_mem_kv"], "nn", F32, f"l{li}_kv")
        x, s = fwd(x, kv, p, li)
        layers.append(p)
        kvs.append(kv)
        saved.append(s)
    loss, dx, dxb, dfinal = _loss_head(x, final_norm_g, tgt, "loss_head")
    dmem_n = None
    for li in reversed(range(n_layers)):
        p = layers[li]
        bwd = _ssd_layer_bwd if li % 2 == 0 else _attn_layer_bwd
        dx, dxb, dkv, g = bwd(dx, dxb, kvs[li], p, saved[li], li)
        g["w_mem_kv"] = _mm(mem_n, dkv, "tn", BF16, f"l{li}_dwkv")
        dmem_n = _mm(dkv, p["w_mem_kv"], "nt", F32, f"l{li}_dmem", add=dmem_n)
        dx, dxb = on_layer_grads(li, g, dx, dxb)
    _, _, dmem_g = _rms_bwd(mem, mem_norm_g, dmem_n, None, "mem_rms_b")
    return loss, dx, dmem_g, dfinal


_SSD_SMALL = ["norm_g", "conv_w", "conv_b", "dt_bias", "a_log", "d_skip", "ssd_norm_g"]
_ATTN_SMALL = ["norm_g"]
_SSD_ORDER = ["norm_g", "w_in", "conv_w", "conv_b", "dt_bias", "a_log", "d_skip", "ssd_norm_g", "w_mem_kv", "w_out"]
_ATTN_ORDER = ["norm_g", "w_in", "w_mem_kv", "w_out"]


def _pack(arrs):
    flat = jnp.concatenate([a.reshape(-1).astype(F32) for a in arrs])
    n = flat.shape[0]
    pad = (-n) % (8 * LANES)
    return jnp.pad(flat, (0, pad)).reshape(-1, LANES)


def _unpack(mat, shapes):
    flat = mat.reshape(-1)
    out, o = [], 0
    for shp in shapes:
        n = math.prod(shp)
        out.append(flat[o:o + n].reshape(shp))
        o += n
    return out


def kernel(x, mem, mem_norm_g, final_norm_g, norm_g_0, w_in_0, conv_w_0, conv_b_0, dt_bias_0, a_log_0, d_skip_0, ssd_norm_g_0, w_mem_kv_0, w_out_0, norm_g_1, w_in_1, w_mem_kv_1, w_out_1, norm_g_2, w_in_2, conv_w_2, conv_b_2, dt_bias_2, a_log_2, d_skip_2, ssd_norm_g_2, w_mem_kv_2, w_out_2, norm_g_3, w_in_3, w_mem_kv_3, w_out_3, loss_target, m_mem_norm_g, m_final_norm_g, m_norm_g_0, m_w_in_0, m_conv_w_0, m_conv_b_0, m_dt_bias_0, m_a_log_0, m_d_skip_0, m_ssd_norm_g_0, m_w_mem_kv_0, m_w_out_0, m_norm_g_1, m_w_in_1, m_w_mem_kv_1, m_w_out_1, m_norm_g_2, m_w_in_2, m_conv_w_2, m_conv_b_2, m_dt_bias_2, m_a_log_2, m_d_skip_2, m_ssd_norm_g_2, m_w_mem_kv_2, m_w_out_2, m_norm_g_3, m_w_in_3, m_w_mem_kv_3, m_w_out_3, v_mem_norm_g, v_final_norm_g, v_norm_g_0, v_w_in_0, v_conv_w_0, v_conv_b_0, v_dt_bias_0, v_a_log_0, v_d_skip_0, v_ssd_norm_g_0, v_w_mem_kv_0, v_w_out_0, v_norm_g_1, v_w_in_1, v_w_mem_kv_1, v_w_out_1, v_norm_g_2, v_w_in_2, v_conv_w_2, v_conv_b_2, v_dt_bias_2, v_a_log_2, v_d_skip_2, v_ssd_norm_g_2, v_w_mem_kv_2, v_w_out_2, v_norm_g_3, v_w_in_3, v_w_mem_kv_3, v_w_out_3):
    a = dict(locals())
    names = ["mem_norm_g", "final_norm_g"]
    for li in range(DEPTH):
        names += [f"{k}_{li}" for k in (_SSD_ORDER if li % 2 == 0 else _ATTN_ORDER)]
    W = {n: a[n] for n in names}
    Mo = {n: a["m_" + n] for n in names}
    Vo = {n: a["v_" + n] for n in names}
    NX = 4
    chip = 2 * lax.axis_index("x") + lax.axis_index("y")

    gathered = []
    for li in range(DEPTH):
        srcs = [W[f"w_in_{li}"].astype(BF16), W[f"w_mem_kv_{li}"].astype(BF16), W[f"w_out_{li}"].astype(BF16)]
        if li % 2 == 0:
            srcs.append(W[f"conv_w_{li}"])
        gathered.append(_comm_async(f"gather_w{li}", li, "xy", srcs, ["gather"] * len(srcs)))

    def layer_params(li, xin):
        got = gathered[li]
        if li > 0:
            got, xin = lax.optimization_barrier((got, xin))
        rows = lambda g: g.reshape((-1,) + g.shape[2:])
        colcat = lambda g: g.transpose(1, 0, 2).reshape(g.shape[1], -1)
        p = dict(w_in=colcat(got[0]), w_mem_kv=rows(got[1]), w_out=rows(got[2]), norm_g=W[f"norm_g_{li}"])
        if li % 2 == 0:
            p.update(conv_w=colcat(got[3]), conv_b=W[f"conv_b_{li}"], dt_bias=W[f"dt_bias_{li}"], a_log=W[f"a_log_{li}"],
                     d_skip=W[f"d_skip_{li}"], ssd_norm_g=W[f"ssd_norm_g_{li}"])
        return p, xin

    G, Dl, Mn, Vn = {}, {}, {}, {}
    grads = [None] * DEPTH
    in_flight = []

    def finish_exchange(li, got):
        parts = [_sum_lead(t, f"l{li}_gsum{n}") for n, t in enumerate(got)]
        theirs = _comm_async(f"swap_g{li}", 2 * DEPTH + li, "c", parts, ["swap"] * 3)
        for nm, mine, other in zip(["w_in", "w_mem_kv", "w_out"], parts, theirs):
            key = f"{nm}_{li}"
            G[key], Dl[key], Mn[key], Vn[key] = _adamw(W[key], Mo[key], Vo[key], [mine, other], f"adamw_{key}")

    def on_layer_grads(li, g, dx, dxb):
        grads[li] = g
        dwin = g["w_in"]
        Dm, N = dwin.shape
        chunks = [dwin.reshape(Dm, NX, N // NX).transpose(1, 0, 2),
                  g["w_mem_kv"].reshape((NX, -1) + g["w_mem_kv"].shape[1:]),
                  g["w_out"].reshape((NX, -1) + g["w_out"].shape[1:])]
        prev = in_flight.pop() if in_flight else None
        pgot = prev[1] if prev else []
        chunks, pgot, dx, dxb = lax.optimization_barrier((chunks, pgot, dx, dxb))
        in_flight.append((li, _comm_async(f"xchg_g{li}", DEPTH + li, "xy", chunks, ["a2a"] * 3)))
        if prev:
            finish_exchange(prev[0], pgot)
        return dx, dxb

    loss_l, dx, dmem_g, dfinal = _local_step(x[0], mem[0], loss_target[0], W["mem_norm_g"], W["final_norm_g"], DEPTH,
                                             layer_params, on_layer_grads)
    finish_exchange(*in_flight.pop())
    loss = lax.psum(loss_l, ("x", "y", "c"))

    small_names = ["mem_norm_g", "final_norm_g"]
    small_grads = [dmem_g, dfinal]
    for li in range(DEPTH):
        for k in (_SSD_SMALL if li % 2 == 0 else _ATTN_SMALL):
            small_names.append(f"{k}_{li}")
            small_grads.append(grads[li][k])
    shapes = [tuple(t.shape) for t in small_grads]
    allg = _comm("gather_small", "xyc", [_pack(small_grads)], ["gather"])[0]
    gsum = _unpack(_sum_lead(allg, "small_gsum"), shapes)
    small_w, small_m, small_v, small_g = [], [], [], []
    for nme, gv in zip(small_names, gsum):
        if nme.startswith("conv_w"):
            cw = W[nme].shape[1]
            gv = lax.dynamic_slice_in_dim(gv, chip * cw, cw, axis=1)
        small_g.append(gv)
        small_w.append(W[nme])
        small_m.append(Mo[nme])
        small_v.append(Vo[nme])
    sshapes = [tuple(t.shape) for t in small_g]
    res = _adamw(_pack(small_w), _pack(small_m), _pack(small_v), [_pack(small_g)], "adamw_small")
    for dst, mat in zip((G, Dl, Mn, Vn), res):
        for nme, t in zip(small_names, _unpack(mat, sshapes)):
            dst[nme] = t

    return (loss, dx[None], *[G[n] for n in names], *[Dl[n] for n in names], *[Mn[n] for n in names],
            *[Vn[n] for n in names])
```

```python
import functools
import math

import jax
import jax.numpy as jnp
from jax import lax
from jax.experimental import pallas as pl
from jax.experimental.pallas import tpu as pltpu
from jax.experimental.pallas import tpu_sc as plsc

F32 = jnp.float32
BF16 = jnp.bfloat16

EPS = 1e-6
MEM_HEADS = 4
SSD_HEAD_DIM = 64
SSD_GROUPS = 8
SSD_STATE = 128
SSD_CONV = 4
SSD_CHUNK = 128
ATTN_HEAD_DIM = 128
ATTN_BLOCK = 128
DILATED_GROUPS = ((128, 1), (512, 4), (2048, 16))
ALIBI_MAX_EXP = 8.0
DEPTH = 4

ADAM_LR = 0.001
ADAM_B1 = 0.9
ADAM_B2 = 0.999
ADAM_EPS = 1e-08
ADAM_WD = 0.01
ADAM_STEP = 10

LANES = 128
VMEM_LIMIT_BYTES = 48 * 1024 * 1024
NEG = -1e30
MESH = pl.DeviceIdType.MESH


def _cparams(*sem):
    return pltpu.CompilerParams(dimension_semantics=tuple(sem), vmem_limit_bytes=VMEM_LIMIT_BYTES)


def _tile(dim, pref, unit=LANES):
    if dim <= pref:
        return dim
    t = (pref // unit) * unit
    while t >= unit:
        if dim % t == 0:
            return t
        t -= unit
    return dim


def _ew_tiles(rows, cols):
    tc = cols if (cols % LANES != 0 or cols <= 2048) else _tile(cols, 2048)
    tr = rows
    while tr * tc > 256 * 1024 and tr % 2 == 0 and (tr // 2) % 8 == 0:
        tr //= 2
    return tr, tc


def _sigmoid(v):
    return 1.0 / (1.0 + jnp.exp(-v))


def _dot(a, b):
    return jnp.dot(a, b, preferred_element_type=F32)


def _dot_nt(a, b):
    return lax.dot_general(a, b, (((1,), (1,)), ((), ())), preferred_element_type=F32)


def _dot_tn(a, b):
    return lax.dot_general(a, b, (((0,), (0,)), ((), ())), preferred_element_type=F32)


def _split3(v):
    hi = v.astype(BF16)
    r = v - hi.astype(F32)
    mid = r.astype(BF16)
    lo = (r - mid.astype(F32)).astype(BF16)
    return hi, mid, lo


def _xdot(v, onehot):
    hi, mid, lo = _split3(v)
    return _dot(hi, onehot) + _dot(mid, onehot) + _dot(lo, onehot)


def _xdot_l(onehot, v):
    hi, mid, lo = _split3(v)
    return _dot(onehot, hi) + _dot(onehot, mid) + _dot(onehot, lo)


def _mm(a, b, mode, out_dtype, name, add=None):
    if mode == "nn":
        M, K = a.shape
        N = b.shape[1]
    elif mode == "nt":
        M, K = a.shape
        N = b.shape[0]
    else:
        K, M = a.shape
        N = b.shape[1]
    tm, tn, tk = (2048, 1024, 1024) if mode == "tn" else (512, 1024, 3072)
    tm, tn, tk = _tile(M, tm, 8 if M < LANES else LANES), _tile(N, tn), _tile(K, tk)
    nk = K // tk
    has_add = add is not None

    def product(a_ref, b_ref):
        av = a_ref[...].astype(BF16)
        bv = b_ref[...].astype(BF16)
        if mode == "nn":
            return _dot(av, bv)
        if mode == "nt":
            return _dot_nt(av, bv)
        return _dot_tn(av, bv)

    def body(*refs):
        a_ref, b_ref = refs[:2]
        add_ref = refs[2] if has_add else None
        o_ref = refs[3] if has_add else refs[2]

        def finish(r):
            if has_add:
                r = r + add_ref[...]
            o_ref[...] = r.astype(out_dtype)

        if nk == 1:
            finish(product(a_ref, b_ref))
            return
        acc = refs[-1]
        k = pl.program_id(2)

        @pl.when(k == 0)
        def _():
            acc[...] = product(a_ref, b_ref)

        @pl.when((k > 0) & (k < nk - 1))
        def _():
            acc[...] += product(a_ref, b_ref)

        @pl.when(k == nk - 1)
        def _():
            finish(acc[...] + product(a_ref, b_ref))

    if mode == "nn":
        a_spec = pl.BlockSpec((tm, tk), lambda j, i, k: (i, k))
        b_spec = pl.BlockSpec((tk, tn), lambda j, i, k: (k, j))
    elif mode == "nt":
        a_spec = pl.BlockSpec((tm, tk), lambda j, i, k: (i, k))
        b_spec = pl.BlockSpec((tn, tk), lambda j, i, k: (j, k))
    else:
        a_spec = pl.BlockSpec((tk, tm), lambda j, i, k: (k, i))
        b_spec = pl.BlockSpec((tk, tn), lambda j, i, k: (k, j))
    o_spec = pl.BlockSpec((tm, tn), lambda j, i, k: (i, j))
    in_specs = [a_spec, b_spec] + ([o_spec] if has_add else [])
    args = (a, b) + ((add,) if has_add else ())
    return pl.pallas_call(
        body, name=name, grid=(N // tn, M // tm, nk), in_specs=in_specs, out_specs=o_spec,
        out_shape=jax.ShapeDtypeStruct((M, N), out_dtype),
        scratch_shapes=[pltpu.VMEM((tm, tn), F32)] if nk > 1 else [],
        compiler_params=_cparams("parallel", "parallel", "arbitrary"),
    )(*args)


def _rms_fwd(x, g, name):
    R, Dm = x.shape
    tr = _tile(R, 256, 8)

    def body(x_ref, g_ref, o_ref):
        xv = x_ref[...]
        r = lax.rsqrt(jnp.mean(xv * xv, axis=-1, keepdims=True) + EPS)
        o_ref[...] = (xv * r * g_ref[...]).astype(BF16)

    return pl.pallas_call(
        body, name=name, grid=(R // tr,),
        in_specs=[pl.BlockSpec((tr, Dm), lambda i: (i, 0)), pl.BlockSpec((1, Dm), lambda i: (0, 0))],
        out_specs=pl.BlockSpec((tr, Dm), lambda i: (i, 0)),
        out_shape=jax.ShapeDtypeStruct((R, Dm), BF16),
        compiler_params=_cparams("parallel"),
    )(x, g.reshape(1, Dm))


def _rms_bwd(x, g, dh, dres, name):
    R, Dm = x.shape
    tr = _tile(R, 256, 8)
    has_res = dres is not None

    def body(*refs):
        if has_res:
            x_ref, g_ref, dh_ref, dres_ref, dx_ref, dxb_ref, dg_ref = refs
        else:
            x_ref, g_ref, dh_ref, dx_ref, dxb_ref, dg_ref = refs
        xv = x_ref[...]
        r = lax.rsqrt(jnp.mean(xv * xv, axis=-1, keepdims=True) + EPS)
        xhat = xv * r
        dhv = dh_ref[...]
        dxhat = dhv * g_ref[...]
        dx = r * (dxhat - xhat * jnp.mean(dxhat * xhat, axis=-1, keepdims=True))
        if has_res:
            dx = dx + dres_ref[...]
        dx_ref[...] = dx
        dxb_ref[...] = dx.astype(BF16)

        @pl.when(pl.program_id(0) == 0)
        def _():
            dg_ref[...] = jnp.zeros_like(dg_ref)

        dg_ref[...] += jnp.sum(dhv * xhat, axis=0, keepdims=True)

    row = pl.BlockSpec((tr, Dm), lambda i: (i, 0))
    vec = pl.BlockSpec((1, Dm), lambda i: (0, 0))
    in_specs = [row, vec, row] + ([row] if has_res else [])
    args = (x, g.reshape(1, Dm), dh) + ((dres,) if has_res else ())
    dx, dxb, dg = pl.pallas_call(
        body, name=name, grid=(R // tr,), in_specs=in_specs, out_specs=[row, row, vec],
        out_shape=[jax.ShapeDtypeStruct((R, Dm), F32), jax.ShapeDtypeStruct((R, Dm), BF16),
                   jax.ShapeDtypeStruct((1, Dm), F32)],
        compiler_params=_cparams("arbitrary"),
    )(*args)
    return dx, dxb, dg.reshape(Dm)


def _loss_head(x, g, tgt, name):
    R, Dm = x.shape
    tr = _tile(R, 256, 8)

    def body(x_ref, g_ref, t_ref, loss_ref, dx_ref, dxb_ref, dg_ref):
        xv = x_ref[...]
        gv = g_ref[...]
        r = lax.rsqrt(jnp.mean(xv * xv, axis=-1, keepdims=True) + EPS)
        xhat = xv * r
        e = xhat * gv - t_ref[...]
        part = jnp.sum(jnp.mean(e * e, axis=-1, keepdims=True), axis=0, keepdims=True) * 0.5
        dy = e * (1.0 / Dm)
        dxhat = dy * gv
        dx = r * (dxhat - xhat * jnp.mean(dxhat * xhat, axis=-1, keepdims=True))
        dx_ref[...] = dx
        dxb_ref[...] = dx.astype(BF16)

        @pl.when(pl.program_id(0) == 0)
        def _():
            dg_ref[...] = jnp.zeros_like(dg_ref)
            loss_ref[...] = jnp.zeros_like(loss_ref)

        dg_ref[...] += jnp.sum(dy * xhat, axis=0, keepdims=True)
        loss_ref[...] += jnp.broadcast_to(part, loss_ref.shape)

    row = pl.BlockSpec((tr, Dm), lambda i: (i, 0))
    vec = pl.BlockSpec((1, Dm), lambda i: (0, 0))
    lsp = pl.BlockSpec((1, LANES), lambda i: (0, 0))
    loss, dx, dxb, dg = pl.pallas_call(
        body, name=name, grid=(R // tr,), in_specs=[row, vec, row], out_specs=[lsp, row, row, vec],
        out_shape=[jax.ShapeDtypeStruct((1, LANES), F32), jax.ShapeDtypeStruct((R, Dm), F32),
                   jax.ShapeDtypeStruct((R, Dm), BF16), jax.ShapeDtypeStruct((1, Dm), F32)],
        compiler_params=_cparams("arbitrary"),
    )(x, g.reshape(1, Dm), tgt)
    return loss[0, 0], dx, dxb, dg.reshape(Dm)


def _elementwise(fn, mats, vecs, out_dtypes, name):
    R, C = mats[0].shape
    tr, tc = _ew_tiles(R, C)
    nm, nv, no = len(mats), len(vecs), len(out_dtypes)

    def body(*refs):
        ins = [r[...] for r in refs[:nm + nv]]
        outs = fn(*ins)
        for o_ref, o in zip(refs[nm + nv:], outs):
            o_ref[...] = o.astype(o_ref.dtype)

    blk = pl.BlockSpec((tr, tc), lambda i, j: (i, j))
    vblk = pl.BlockSpec((1, tc), lambda i, j: (0, j))
    res = pl.pallas_call(
        body, name=name, grid=(R // tr, C // tc),
        in_specs=[blk] * nm + [vblk] * nv, out_specs=[blk] * no,
        out_shape=[jax.ShapeDtypeStruct((R, C), dt) for dt in out_dtypes],
        compiler_params=_cparams("parallel", "parallel"),
    )(*mats, *[v.reshape(1, C) for v in vecs])
    return res


def _sum_lead(arr, name):
    P_, R, C = arr.shape
    tr, tc = _ew_tiles(R, C)

    def body(a_ref, o_ref):
        s = a_ref[0].astype(F32)
        for p in range(1, P_):
            s = s + a_ref[p].astype(F32)
        o_ref[...] = s

    return pl.pallas_call(
        body, name=name, grid=(R // tr, C // tc),
        in_specs=[pl.BlockSpec((P_, tr, tc), lambda i, j: (0, i, j))],
        out_specs=pl.BlockSpec((tr, tc), lambda i, j: (i, j)),
        out_shape=jax.ShapeDtypeStruct((R, C), F32),
        compiler_params=_cparams("parallel", "parallel"),
    )(arr)


def _adamw(w, m, v, gparts, name):
    P_ = len(gparts)
    R, C = w.shape
    tr, tc = _ew_tiles(R, C)
    c1 = 1.0 / (1.0 - ADAM_B1 ** ADAM_STEP)
    c2 = 1.0 / (1.0 - ADAM_B2 ** ADAM_STEP)

    def body(w_ref, m_ref, v_ref, *rest):
        g_refs, (go_ref, d_ref, mo_ref, vo_ref) = rest[:P_], rest[P_:]
        g = g_refs[0][...]
        for g_ref in g_refs[1:]:
            g = g + g_ref[...]
        mn = ADAM_B1 * m_ref[...] + (1.0 - ADAM_B1) * g
        vn = ADAM_B2 * v_ref[...] + (1.0 - ADAM_B2) * (g * g)
        m_hat = mn * c1
        v_hat = vn * c2
        d_ref[...] = -ADAM_LR * (m_hat / (jnp.sqrt(v_hat) + ADAM_EPS) + ADAM_WD * w_ref[...])
        go_ref[...] = g
        mo_ref[...] = mn
        vo_ref[...] = vn

    blk = pl.BlockSpec((tr, tc), lambda i, j: (i, j))
    sds = jax.ShapeDtypeStruct((R, C), F32)
    return pl.pallas_call(
        body, name=name, grid=(R // tr, C // tc),
        in_specs=[blk] * (3 + P_), out_specs=[blk] * 4, out_shape=[sds] * 4,
        compiler_params=_cparams("parallel", "parallel"),
    )(w, m, v, *gparts)


SUBLANES = 8


def _conv_pre(u, up, w_ref, b, first):
    tr = u.shape[0]
    rows = lax.broadcasted_iota(jnp.int32, u.shape, 0)
    keep = 1.0 - first.astype(F32)
    acc = b + w_ref[SSD_CONV - 1:SSD_CONV, :] * u
    shifted = []
    for j in range(1, SSD_CONV):
        su = pltpu.roll(u, j, 0)
        sp = jnp.tile(pltpu.roll(up, j, 0) * keep, (tr // SUBLANES, 1))
        sh = jnp.where(rows < j, sp, su)
        shifted.append(sh)
        acc = acc + w_ref[SSD_CONV - 1 - j:SSD_CONV - j, :] * sh
    return acc, shifted


def _conv_fwd(u, w, b, name):
    T, C = u.shape
    tr, tc = _tile(T, 256, 8), _tile(C, 1024)

    def body(u_ref, up_ref, w_ref, b_ref, o_ref):
        pre, _ = _conv_pre(u_ref[...], up_ref[...], w_ref, b_ref[...], pl.program_id(0) == 0)
        o_ref[...] = pre * _sigmoid(pre)

    return pl.pallas_call(
        body, name=name, grid=(T // tr, C // tc),
        in_specs=[pl.BlockSpec((tr, tc), lambda i, j: (i, j)),
                  pl.BlockSpec((SUBLANES, tc), lambda i, j: (jnp.maximum(i * (tr // SUBLANES) - 1, 0), j)),
                  pl.BlockSpec((SSD_CONV, tc), lambda i, j: (0, j)),
                  pl.BlockSpec((1, tc), lambda i, j: (0, j))],
        out_specs=pl.BlockSpec((tr, tc), lambda i, j: (i, j)),
        out_shape=jax.ShapeDtypeStruct((T, C), F32),
        compiler_params=_cparams("parallel", "parallel"),
    )(u, u, w, b.reshape(1, C))


def _conv_bwd_pre(u, w, b, dact, name):
    T, C = u.shape
    tr, tc = _tile(T, 256, 8), _tile(C, 1024)

    def body(u_ref, up_ref, w_ref, b_ref, da_ref, dp_ref, dw_ref, db_ref):
        i = pl.program_id(1)
        uv = u_ref[...]
        pre, shifted = _conv_pre(uv, up_ref[...], w_ref, b_ref[...], i == 0)
        sg = _sigmoid(pre)
        dpre = da_ref[...] * (sg * (1.0 + pre * (1.0 - sg)))
        dp_ref[...] = dpre

        @pl.when(i == 0)
        def _():
            dw_ref[...] = jnp.zeros_like(dw_ref)
            db_ref[...] = jnp.zeros_like(db_ref)

        db_ref[...] += jnp.sum(dpre, axis=0, keepdims=True)
        dw_ref[SSD_CONV - 1:SSD_CONV, :] += jnp.sum(dpre * uv, axis=0, keepdims=True)
        for j in range(1, SSD_CONV):
            dw_ref[SSD_CONV - 1 - j:SSD_CONV - j, :] += jnp.sum(dpre * shifted[j - 1], axis=0, keepdims=True)

    blk = pl.BlockSpec((tr, tc), lambda j, i: (i, j))
    dpre, dw, db = pl.pallas_call(
        body, name=name, grid=(C // tc, T // tr),
        in_specs=[blk, pl.BlockSpec((SUBLANES, tc), lambda j, i: (jnp.maximum(i * (tr // SUBLANES) - 1, 0), j)),
                  pl.BlockSpec((SSD_CONV, tc), lambda j, i: (0, j)),
                  pl.BlockSpec((1, tc), lambda j, i: (0, j)), blk],
        out_specs=[blk, pl.BlockSpec((SSD_CONV, tc), lambda j, i: (0, j)), pl.BlockSpec((1, tc), lambda j, i: (0, j))],
        out_shape=[jax.ShapeDtypeStruct((T, C), F32), jax.ShapeDtypeStruct((SSD_CONV, C), F32),
                   jax.ShapeDtypeStruct((1, C), F32)],
        compiler_params=_cparams("parallel", "arbitrary"),
    )(u, u, w, b.reshape(1, C), dact)
    return dpre, dw, db.reshape(C)


def _conv_bwd_in(dpre, w, name):
    T, C = dpre.shape
    tr, tc = _tile(T, 256, 8), _tile(C, 1024)
    nb = T // tr

    def body(d_ref, dn_ref, w_ref, o_ref):
        d = d_ref[...]
        keep = 1.0 - (pl.program_id(0) == nb - 1).astype(F32)
        dn = dn_ref[...] * keep
        rows = lax.broadcasted_iota(jnp.int32, d.shape, 0)
        acc = w_ref[SSD_CONV - 1:SSD_CONV, :] * d
        for j in range(1, SSD_CONV):
            sd = pltpu.roll(d, tr - j, 0)
            sn = jnp.tile(pltpu.roll(dn, SUBLANES - j, 0), (tr // SUBLANES, 1))
            acc = acc + w_ref[SSD_CONV - 1 - j:SSD_CONV - j, :] * jnp.where(rows >= tr - j, sn, sd)
        o_ref[...] = acc.astype(BF16)

    return pl.pallas_call(
        body, name=name, grid=(nb, C // tc),
        in_specs=[pl.BlockSpec((tr, tc), lambda i, j: (i, j)),
                  pl.BlockSpec((SUBLANES, tc), lambda i, j: (jnp.minimum((i + 1) * (tr // SUBLANES), T // SUBLANES - 1), j)),
                  pl.BlockSpec((SSD_CONV, tc), lambda i, j: (0, j))],
        out_specs=pl.BlockSpec((tr, tc), lambda i, j: (i, j)),
        out_shape=jax.ShapeDtypeStruct((T, C), BF16),
        compiler_params=_cparams("parallel", "parallel"),
    )(dpre, dpre, w)


def _mem_probs(qh, kh, scale):
    s = _dot_nt(qh.astype(BF16), kh.astype(BF16)) * scale
    m = jnp.max(s, axis=-1, keepdims=True)
    p = jnp.exp(s - m)
    return p / jnp.sum(p, axis=-1, keepdims=True)


def _memattn_fwd(q, kv, name):
    T, MW = q.shape
    NM = kv.shape[0]
    hd = MW // MEM_HEADS
    scale = hd ** -0.5
    tq = _tile(T, 512, 8)

    def body(q_ref, kv_ref, o_ref):
        for h in range(MEM_HEADS):
            sl = slice(h * hd, (h + 1) * hd)
            p = _mem_probs(q_ref[:, sl], kv_ref[:, sl], scale)
            vh = kv_ref[:, MW + h * hd:MW + (h + 1) * hd]
            o_ref[:, sl] = _dot(p.astype(BF16), vh.astype(BF16))

    return pl.pallas_call(
        body, name=name, grid=(T // tq,),
        in_specs=[pl.BlockSpec((tq, MW), lambda i: (i, 0)), pl.BlockSpec((NM, 2 * MW), lambda i: (0, 0))],
        out_specs=pl.BlockSpec((tq, MW), lambda i: (i, 0)),
        out_shape=jax.ShapeDtypeStruct((T, MW), F32),
        compiler_params=_cparams("parallel"),
    )(q, kv)


def _memattn_bwd(q, kv, dy, name):
    T, MW = q.shape
    NM = kv.shape[0]
    hd = MW // MEM_HEADS
    scale = hd ** -0.5
    tq = _tile(T, 512, 8)

    def body(q_ref, kv_ref, dy_ref, dq_ref, dkv_ref):
        @pl.when(pl.program_id(0) == 0)
        def _():
            dkv_ref[...] = jnp.zeros_like(dkv_ref)

        for h in range(MEM_HEADS):
            sl = slice(h * hd, (h + 1) * hd)
            vsl = slice(MW + h * hd, MW + (h + 1) * hd)
            qh = q_ref[:, sl]
            kh = kv_ref[:, sl]
            vh = kv_ref[:, vsl]
            dyh = dy_ref[:, sl].astype(BF16)
            p = _mem_probs(qh, kh, scale)
            dp = _dot_nt(dyh, vh.astype(BF16))
            ds = p * (dp - jnp.sum(dp * p, axis=-1, keepdims=True)) * scale
            dq_ref[:, sl] = _dot(ds.astype(BF16), kh.astype(BF16)).astype(BF16)
            dkv_ref[:, sl] += _dot(ds.T.astype(BF16), qh.astype(BF16))
            dkv_ref[:, vsl] += _dot(p.T.astype(BF16), dyh)

    return pl.pallas_call(
        body, name=name, grid=(T // tq,),
        in_specs=[pl.BlockSpec((tq, MW), lambda i: (i, 0)), pl.BlockSpec((NM, 2 * MW), lambda i: (0, 0)),
                  pl.BlockSpec((tq, MW), lambda i: (i, 0))],
        out_specs=[pl.BlockSpec((tq, MW), lambda i: (i, 0)), pl.BlockSpec((NM, 2 * MW), lambda i: (0, 0))],
        out_shape=[jax.ShapeDtypeStruct((T, MW), BF16), jax.ShapeDtypeStruct((NM, 2 * MW), F32)],
        compiler_params=_cparams("arbitrary"),
    )(q, kv, dy)


def _silu_parts(z):
    sg = _sigmoid(z)
    return z * sg, sg * (1.0 + z * (1.0 - sg))


def _gate_fwd(a, z, name):
    return _elementwise(lambda av, zv: (av * _silu_parts(zv)[0],), [a, z], [], [BF16], name)[0]


def _gate_bwd(a, z, d, name):
    def fn(av, zv, dv):
        s, ds = _silu_parts(zv)
        return dv * s, dv * av * ds
    return _elementwise(fn, [a, z, d], [], [F32, BF16], name)


def _gate_norm_fwd(y, z, g, groups, name):
    T, C = y.shape
    gw = C // groups
    tr = _tile(T, 128, 8)

    def body(y_ref, z_ref, g_ref, o_ref):
        for k in range(groups):
            sl = slice(k * gw, (k + 1) * gw)
            u = y_ref[:, sl] * _silu_parts(z_ref[:, sl])[0]
            r = lax.rsqrt(jnp.mean(u * u, axis=-1, keepdims=True) + EPS)
            o_ref[:, sl] = (u * r * g_ref[:, sl]).astype(BF16)

    row = pl.BlockSpec((tr, C), lambda i: (i, 0))
    return pl.pallas_call(
        body, name=name, grid=(T // tr,), in_specs=[row, row, pl.BlockSpec((1, C), lambda i: (0, 0))],
        out_specs=row, out_shape=jax.ShapeDtypeStruct((T, C), BF16),
        compiler_params=_cparams("parallel"),
    )(y, z, g.reshape(1, C))


def _gate_norm_bwd(y, z, g, d, groups, name):
    T, C = y.shape
    gw = C // groups
    tr = _tile(T, 128, 8)

    def body(y_ref, z_ref, g_ref, d_ref, dy_ref, dz_ref, dg_ref):
        @pl.when(pl.program_id(0) == 0)
        def _():
            dg_ref[...] = jnp.zeros_like(dg_ref)

        for k in range(groups):
            sl = slice(k * gw, (k + 1) * gw)
            yv = y_ref[:, sl]
            s, ds = _silu_parts(z_ref[:, sl])
            u = yv * s
            r = lax.rsqrt(jnp.mean(u * u, axis=-1, keepdims=True) + EPS)
            uhat = u * r
            dv = d_ref[:, sl]
            dg_ref[:, sl] += jnp.sum(dv * uhat, axis=0, keepdims=True)
            duhat = dv * g_ref[:, sl]
            du = r * (duhat - uhat * jnp.mean(duhat * uhat, axis=-1, keepdims=True))
            dy_ref[:, sl] = du * s
            dz_ref[:, sl] = (du * yv * ds).astype(BF16)

    row = pl.BlockSpec((tr, C), lambda i: (i, 0))
    vec = pl.BlockSpec((1, C), lambda i: (0, 0))
    dy, dz, dg = pl.pallas_call(
        body, name=name, grid=(T // tr,), in_specs=[row, row, vec, row], out_specs=[row, row, vec],
        out_shape=[jax.ShapeDtypeStruct((T, C), F32), jax.ShapeDtypeStruct((T, C), BF16),
                   jax.ShapeDtypeStruct((1, C), F32)],
        compiler_params=_cparams("arbitrary"),
    )(y, z, g.reshape(1, C), d)
    return dy, dz, dg.reshape(C)


def _ssd_common(raw, bias, alog, Q, HP, HPG):
    P_ = SSD_HEAD_DIM
    dt_in = raw + bias
    dt = jnp.maximum(dt_in, 0.0) + jnp.log(1.0 + jnp.exp(-jnp.abs(dt_in)))
    a = -jnp.exp(alog)
    r_q = lax.broadcasted_iota(jnp.int32, (Q, Q), 0)
    c_q = lax.broadcasted_iota(jnp.int32, (Q, Q), 1)
    causal = r_q >= c_q
    tril = causal.astype(BF16)
    A = _xdot_l(tril, dt * a)
    e_r = lax.broadcasted_iota(jnp.int32, (LANES, HP), 0)
    e_c = lax.broadcasted_iota(jnp.int32, (LANES, HP), 1)
    E = ((e_c >= e_r * P_) & (e_c < (e_r + 1) * P_) & (e_r < HPG)).astype(BF16)
    return dt, a, A, causal, E


def _head_cols(v, vt, j):
    lane = lax.broadcasted_iota(jnp.int32, v.shape, 1)
    sub = lax.broadcasted_iota(jnp.int32, vt.shape, 0)
    col = jnp.sum(jnp.where(lane == j, v, 0.0), axis=-1, keepdims=True)
    row = jnp.sum(jnp.where(sub == j, vt, 0.0), axis=0, keepdims=True)
    return col, row


def _ssd_fwd(act, raw_g, bias_g, alog_g, dsk_g, TOK, name):
    T = act.shape[0]
    G, N, Q, P_ = SSD_GROUPS, SSD_STATE, SSD_CHUNK, SSD_HEAD_DIM
    HP = TOK // G
    HPG = HP // P_
    NC = T // Q

    def body(x_ref, b_ref, c_ref, raw_ref, bias_ref, alog_ref, dsk_ref, y_ref, hp_ref, hT):
        @pl.when(pl.program_id(1) == 0)
        def _():
            hT[...] = jnp.zeros_like(hT)

        xs = x_ref[...]
        Bb = b_ref[...].astype(BF16)
        Cb = c_ref[...].astype(BF16)
        dt, a, A, causal, E = _ssd_common(raw_ref[...], bias_ref[...], alog_ref[...], Q, HP, HPG)
        AT = A.T
        dt_e = _xdot(dt, E)
        A_e = _xdot(A, E)
        dsk_e = _xdot(jnp.broadcast_to(dsk_ref[...], (Q, LANES)), E)
        rows = lax.broadcasted_iota(jnp.int32, (Q, HP), 0)
        cols = lax.broadcasted_iota(jnp.int32, (Q, HP), 1)
        Al_e = jnp.sum(jnp.where(rows == Q - 1, A_e, 0.0), axis=0, keepdims=True)
        xdt = xs * dt_e
        hprev = hT[...]
        hp_ref[...] = hprev
        CB = _dot_nt(Cb, Bb)
        y = _dot(Cb, hprev.astype(BF16)) * jnp.exp(A_e) + dsk_e * xs
        for j in range(HPG):
            a_col, a_row = _head_cols(A, AT, j)
            L = jnp.exp(jnp.where(causal, a_col - a_row, NEG))
            xj = jnp.where((cols >= j * P_) & (cols < (j + 1) * P_), xdt, 0.0)
            y = y + _dot((CB * L).astype(BF16), xj.astype(BF16))
        y_ref[...] = y
        dte = jnp.exp(Al_e - A_e)
        hT[...] = jnp.exp(Al_e) * hprev + _dot(b_ref[...].T.astype(BF16), (xdt * dte).astype(BF16))

    nbx = TOK // N
    par = pl.BlockSpec((None, 1, LANES), lambda g, c: (g, 0, 0))
    return pl.pallas_call(
        body, name=name, grid=(G, NC),
        in_specs=[pl.BlockSpec((Q, HP), lambda g, c: (c, g)),
                  pl.BlockSpec((Q, N), lambda g, c: (c, nbx + g)),
                  pl.BlockSpec((Q, N), lambda g, c: (c, nbx + G + g)),
                  pl.BlockSpec((None, Q, LANES), lambda g, c: (g, c, 0)), par, par, par],
        out_specs=[pl.BlockSpec((Q, HP), lambda g, c: (c, g)),
                   pl.BlockSpec((None, None, N, HP), lambda g, c: (g, c, 0, 0))],
        out_shape=[jax.ShapeDtypeStruct((T, TOK), F32), jax.ShapeDtypeStruct((G, NC, N, HP), F32)],
        scratch_shapes=[pltpu.VMEM((N, HP), F32)],
        compiler_params=_cparams("parallel", "arbitrary"),
    )(act, act, act, raw_g, bias_g, alog_g, dsk_g)


def _ssd_bwd(act, raw_g, bias_g, alog_g, dsk_g, hprev, dy, TOK, name):
    T = act.shape[0]
    G, N, Q, P_ = SSD_GROUPS, SSD_STATE, SSD_CHUNK, SSD_HEAD_DIM
    HP = TOK // G
    HPG = HP // P_
    NC = T // Q

    def body(x_ref, b_ref, c_ref, raw_ref, bias_ref, alog_ref, dsk_ref, hp_ref, dy_ref,
             dx_ref, db_ref, dc_ref, draw_ref, dalog_ref, dbias_ref, ddsk_ref, dHT):
        @pl.when(pl.program_id(1) == 0)
        def _():
            dHT[...] = jnp.zeros_like(dHT)
            dalog_ref[...] = jnp.zeros_like(dalog_ref)
            dbias_ref[...] = jnp.zeros_like(dbias_ref)
            ddsk_ref[...] = jnp.zeros_like(ddsk_ref)

        xs = x_ref[...]
        dyv = dy_ref[...]
        Bm = b_ref[...]
        Cm = c_ref[...]
        Bb = Bm.astype(BF16)
        Cb = Cm.astype(BF16)
        raw_in = raw_ref[...] + bias_ref[...]
        dt, a, A, causal, E = _ssd_common(raw_ref[...], bias_ref[...], alog_ref[...], Q, HP, HPG)
        AT = A.T
        t_r = lax.broadcasted_iota(jnp.int32, (HP, LANES), 0)
        t_c = lax.broadcasted_iota(jnp.int32, (HP, LANES), 1)
        ET = ((t_r >= t_c * P_) & (t_r < (t_c + 1) * P_) & (t_c < HPG)).astype(BF16)
        dt_e = _xdot(dt, E)
        A_e = _xdot(A, E)
        dsk_e = _xdot(jnp.broadcast_to(dsk_ref[...], (Q, LANES)), E)
        rows = lax.broadcasted_iota(jnp.int32, (Q, HP), 0)
        cols = lax.broadcasted_iota(jnp.int32, (Q, HP), 1)
        last = rows == Q - 1
        Al_e = jnp.sum(jnp.where(last, A_e, 0.0), axis=0, keepdims=True)
        eA = jnp.exp(A_e)
        eAl = jnp.exp(Al_e)
        dte = jnp.exp(Al_e - A_e)
        xdt = xs * dt_e
        xdt_b = xdt.astype(BF16)
        CB = _dot_nt(Cb, Bb)
        HT = hp_ref[...]
        HTb = HT.astype(BF16)
        dH = dHT[...]
        dHb = dH.astype(BF16)
        dys = (dyv * eA).astype(BF16)
        CH = _dot(Cb, HTb)
        dC = _dot_nt(dys, HTb)
        dH_prev = _dot(Cm.T.astype(BF16), dys) + eAl * dH
        dAe = dyv * CH * eA
        dAl = eAl * jnp.sum(dH * HT, axis=0, keepdims=True)
        W = _dot(Bb, dHb)
        dxdt = W * dte
        dd = W * xdt * dte
        dB = _dot_nt((xdt * dte).astype(BF16), dHb)
        dAl = dAl + jnp.sum(dd, axis=0, keepdims=True)
        dAe = dAe - dd + jnp.where(last, dAl, 0.0)
        lane = lax.broadcasted_iota(jnp.int32, (Q, LANES), 1)
        sub = lax.broadcasted_iota(jnp.int32, (LANES, Q), 0)
        dCB = jnp.zeros((Q, Q), F32)
        dA_col = jnp.zeros((Q, LANES), F32)
        dA_row = jnp.zeros((LANES, Q), F32)
        for j in range(HPG):
            a_col, a_row = _head_cols(A, AT, j)
            L = jnp.exp(jnp.where(causal, a_col - a_row, NEG))
            hmask = (cols >= j * P_) & (cols < (j + 1) * P_)
            dyj = jnp.where(hmask, dyv, 0.0).astype(BF16)
            GL = _dot_nt(dyj, xdt_b) * L
            dCB = dCB + GL
            dLL = GL * CB
            dA_col = dA_col + jnp.where(lane == j, jnp.sum(dLL, axis=-1, keepdims=True), 0.0)
            dA_row = dA_row + jnp.where(sub == j, jnp.sum(dLL, axis=0, keepdims=True), 0.0)
            dxdt = dxdt + _dot((CB * L).T.astype(BF16), dyj)
        dC = dC + _dot(dCB.astype(BF16), Bb)
        dB = dB + _dot(dCB.T.astype(BF16), Cb)
        dA = dA_col - dA_row.T + _xdot(dAe, ET)
        triu = (lax.broadcasted_iota(jnp.int32, (Q, Q), 1) >= lax.broadcasted_iota(jnp.int32, (Q, Q), 0)).astype(BF16)
        rcs = _xdot_l(triu, dA)
        ddt = a * rcs + _xdot(dxdt * xs, ET)
        draw = ddt * _sigmoid(raw_in)
        draw_ref[...] = draw
        dalog_ref[...] += jnp.sum(dt * rcs, axis=0, keepdims=True) * a
        dbias_ref[...] += jnp.sum(draw, axis=0, keepdims=True)
        ddsk_ref[...] += jnp.sum(_xdot(dyv * xs, ET), axis=0, keepdims=True)
        dx_ref[...] = dxdt * dt_e + dsk_e * dyv
        db_ref[...] = dB
        dc_ref[...] = dC
        dHT[...] = dH_prev

    nbx = TOK // N
    rv = lambda c: NC - 1 - c
    par = pl.BlockSpec((None, 1, LANES), lambda g, c: (g, 0, 0))
    xsp = pl.BlockSpec((Q, HP), lambda g, c: (rv(c), g))
    outs = pl.pallas_call(
        body, name=name, grid=(G, NC),
        in_specs=[xsp,
                  pl.BlockSpec((Q, N), lambda g, c: (rv(c), nbx + g)),
                  pl.BlockSpec((Q, N), lambda g, c: (rv(c), nbx + G + g)),
                  pl.BlockSpec((None, Q, LANES), lambda g, c: (g, rv(c), 0)), par, par, par,
                  pl.BlockSpec((None, None, N, HP), lambda g, c: (g, rv(c), 0, 0)), xsp],
        out_specs=[xsp, pl.BlockSpec((Q, N), lambda g, c: (rv(c), g)), pl.BlockSpec((Q, N), lambda g, c: (rv(c), g)),
                   pl.BlockSpec((None, Q, LANES), lambda g, c: (g, rv(c), 0)), par, par, par],
        out_shape=[jax.ShapeDtypeStruct((T, TOK), F32), jax.ShapeDtypeStruct((T, G * N), F32),
                   jax.ShapeDtypeStruct((T, G * N), F32), jax.ShapeDtypeStruct((G, T, LANES), F32),
                   jax.ShapeDtypeStruct((G, 1, LANES), F32), jax.ShapeDtypeStruct((G, 1, LANES), F32),
                   jax.ShapeDtypeStruct((G, 1, LANES), F32)],
        scratch_shapes=[pltpu.VMEM((N, HP), F32)],
        compiler_params=_cparams("parallel", "arbitrary"),
    )(act, act, act, raw_g, bias_g, alog_g, dsk_g, hprev, dy)
    return outs


def _heads_per_block(H):
    for hb in (8, 6, 4, 3, 2, 1):
        if H % hb == 0:
            return hb
    return 1


def _alibi_slope(head_index, n_alibi):
    c = -ALIBI_MAX_EXP * math.log(2.0) / n_alibi
    return jnp.exp(jnp.full((1, 1), c, F32) * (head_index + 1).astype(F32))


def _attn_masks(b, nb):
    Bq = ATTN_BLOCK
    iq = lax.broadcasted_iota(jnp.int32, (Bq, 2 * Bq), 0)
    jk = lax.broadcasted_iota(jnp.int32, (Bq, 2 * Bq), 1)
    rel = iq + Bq - jk
    mask = (rel >= 0) & (rel <= Bq) & (jk + jnp.where(b > 0, Bq, 0) >= Bq)
    rel_n = lax.broadcasted_iota(jnp.int32, (Bq, Bq), 0) + Bq - lax.broadcasted_iota(jnp.int32, (Bq, Bq), 1)
    mask_n = (rel_n + jnp.where(b < nb - 1, 0, 4 * Bq)) <= Bq
    return rel.astype(F32), mask, rel_n.astype(F32), mask_n


def _rows2(a, b):
    return jnp.concatenate([a, b], axis=0)


ATTN_UNITS_IN_FLIGHT = 4


def _attn_units_fwd(load, n_units, slopes, masks, scale, store):
    rel_f, mask, _, _ = masks
    for g0 in range(0, n_units, ATTN_UNITS_IN_FLIGHT):
        ids = range(g0, min(g0 + ATTN_UNITS_IN_FLIGHT, n_units))
        units = [load(i) for i in ids]
        raw = [_dot_nt(q, _rows2(kp, kc)) for q, kc, kp, vc, vp in units]
        soft = []
        for i, s_raw in zip(ids, raw):
            s = jnp.where(mask, s_raw * scale - slopes[i] * rel_f, NEG)
            m = jnp.max(s, axis=-1, keepdims=True)
            p = jnp.exp(s - m)
            den = jnp.sum(p, axis=-1, keepdims=True)
            soft.append((p.astype(BF16), den, m + jnp.log(den)))
        for i, (p, den, lse), (q, kc, kp, vc, vp) in zip(ids, soft, units):
            store(i, _dot(p, _rows2(vp, vc)) / den, lse)


def _attn_units_bwd(load, n_units, slopes, masks, scale, store):
    rel_f, mask, reln_f, mask_n = masks
    Bq = ATTN_BLOCK
    for g0 in range(0, n_units, ATTN_UNITS_IN_FLIGHT):
        ids = range(g0, min(g0 + ATTN_UNITS_IN_FLIGHT, n_units))
        units = [load(i) for i in ids]
        prods = []
        for q0, q1, kp, k0, vp, v0, do0, do1, y0, y1, lse0, lse1 in units:
            kcat = _rows2(kp, k0)
            do0b = do0.astype(BF16)
            do1b = do1.astype(BF16)
            prods.append((kcat, do0b, do1b, _dot_nt(q0, kcat), _dot_nt(do0b, _rows2(vp, v0)), _dot_nt(q1, k0),
                          _dot_nt(do1b, v0)))
        mids = []
        for i, u, (kcat, do0b, do1b, s_raw, dp_raw, sn_raw, dpn_raw) in zip(ids, units, prods):
            q0, q1, kp, k0, vp, v0, do0, do1, y0, y1, lse0, lse1 = u
            delta0 = jnp.sum(do0 * y0, axis=-1, keepdims=True)
            delta1 = jnp.sum(do1 * y1, axis=-1, keepdims=True)
            p = jnp.exp(jnp.where(mask, s_raw * scale - slopes[i] * rel_f, NEG) - lse0)
            ds = p * (dp_raw - delta0)
            p_n = jnp.exp(jnp.where(mask_n, sn_raw * scale - slopes[i] * reln_f, NEG) - lse1)
            ds_n = p_n * (dpn_raw - delta1)
            mids.append((ds.astype(BF16), _rows2(ds[:, Bq:], ds_n).T.astype(BF16),
                         _rows2(p[:, Bq:], p_n).T.astype(BF16)))
        for i, u, (kcat, do0b, do1b, *_), (dsb, dsk_t, pk_t) in zip(ids, units, prods, mids):
            store(i, scale * _dot(dsb, kcat), scale * _dot(dsk_t, _rows2(u[0], u[1])), _dot(pk_t, _rows2(do0b, do1b)))


def _attn_fwd_strided(q, k, v, gi, d, name):
    T, TOK = q.shape
    E_ = ATTN_HEAD_DIM
    H = TOK // E_
    n_alibi = len(DILATED_GROUPS) * H
    Bq = ATTN_BLOCK
    RB = Bq * d
    nb = T // RB
    scale = E_ ** -0.5

    def body(q_ref, kc_ref, kp_ref, vc_ref, vp_ref, o_ref, l_ref):
        masks = _attn_masks(pl.program_id(1), nb)
        slope = _alibi_slope(gi * H + pl.program_id(0), n_alibi) * float(d)
        rows = lambda r: pl.ds(r, Bq, stride=d)

        def load(r):
            return tuple(ref[rows(r), :].astype(BF16) for ref in (q_ref, kc_ref, kp_ref, vc_ref, vp_ref))

        def store(r, o, lse):
            o_ref[rows(r), :] = o
            l_ref[rows(r), :] = jnp.broadcast_to(lse, (Bq, E_))

        _attn_units_fwd(load, d, [slope] * d, masks, scale, store)

    cur = pl.BlockSpec((RB, E_), lambda h, b: (b, h))
    prev = pl.BlockSpec((RB, E_), lambda h, b: (jnp.maximum(b - 1, 0), h))
    sds = jax.ShapeDtypeStruct((T, TOK), F32)
    return pl.pallas_call(
        body, name=name, grid=(H, nb), in_specs=[cur, cur, prev, cur, prev], out_specs=[cur, cur],
        out_shape=[sds, sds], compiler_params=_cparams("parallel", "parallel"),
    )(q, k, k, v, v)


def _attn_bwd_strided(q, k, v, y, lse, dy, gi, d, name):
    T, TOK = q.shape
    E_ = ATTN_HEAD_DIM
    H = TOK // E_
    n_alibi = len(DILATED_GROUPS) * H
    Bq = ATTN_BLOCK
    RB = Bq * d
    nb = T // RB
    scale = E_ ** -0.5

    def body(q0_ref, q1_ref, kp_ref, k0_ref, vp_ref, v0_ref, do0_ref, do1_ref, y0_ref, y1_ref, l0_ref, l1_ref,
             dq_ref, dk_ref, dv_ref):
        masks = _attn_masks(pl.program_id(1), nb)
        slope = _alibi_slope(gi * H + pl.program_id(0), n_alibi) * float(d)
        rows = lambda r: pl.ds(r, Bq, stride=d)

        def load(r):
            return (tuple(ref[rows(r), :].astype(BF16) for ref in (q0_ref, q1_ref, kp_ref, k0_ref, vp_ref, v0_ref))
                    + tuple(ref[rows(r), :] for ref in (do0_ref, do1_ref, y0_ref, y1_ref))
                    + tuple(jnp.max(ref[rows(r), :], axis=-1, keepdims=True) for ref in (l0_ref, l1_ref)))

        def store(r, dq, dk, dv):
            dq_ref[rows(r), :] = dq
            dk_ref[rows(r), :] = dk
            dv_ref[rows(r), :] = dv

        _attn_units_bwd(load, d, [slope] * d, masks, scale, store)

    cur = pl.BlockSpec((RB, E_), lambda h, b: (b, h))
    prev = pl.BlockSpec((RB, E_), lambda h, b: (jnp.maximum(b - 1, 0), h))
    nxt = pl.BlockSpec((RB, E_), lambda h, b: (jnp.minimum(b + 1, nb - 1), h))
    sds = jax.ShapeDtypeStruct((T, TOK), F32)
    return pl.pallas_call(
        body, name=name, grid=(H, nb),
        in_specs=[cur, nxt, prev, cur, prev, cur, cur, nxt, cur, nxt, cur, nxt],
        out_specs=[cur, cur, cur], out_shape=[sds, sds, sds],
        compiler_params=_cparams("parallel", "parallel"),
    )(q, q, k, k, v, v, dy, dy, y, y, lse, lse)


def _attn_fwd(q, k, v, gi, window, d, name):
    T, TOK = q.shape
    E_ = ATTN_HEAD_DIM
    H = TOK // E_
    n_alibi = len(DILATED_GROUPS) * H
    assert window // d == ATTN_BLOCK and (T // d) % ATTN_BLOCK == 0
    if d > 1:
        return _attn_fwd_strided(q, k, v, gi, d, name)
    n_sub = T // d
    nb = n_sub // ATTN_BLOCK
    HB = _heads_per_block(H)
    NHB = H // HB
    hbw = HB * E_
    scale = E_ ** -0.5
    Bq = ATTN_BLOCK

    def body(q_ref, kc_ref, kp_ref, vc_ref, vp_ref, o_ref, l_ref):
        hb = pl.program_id(1)
        masks = _attn_masks(pl.program_id(2), nb)
        slopes = [_alibi_slope(gi * H + hb * HB + hh, n_alibi) * float(d) for hh in range(HB)]
        cols = lambda hh: slice(hh * E_, (hh + 1) * E_)

        def load(hh):
            return tuple(ref[:, cols(hh)].astype(BF16) for ref in (q_ref, kc_ref, kp_ref, vc_ref, vp_ref))

        def store(hh, o, lse):
            o_ref[:, cols(hh)] = o
            l_ref[:, cols(hh)] = jnp.broadcast_to(lse, (Bq, E_))

        _attn_units_fwd(load, HB, slopes, masks, scale, store)

    cur = pl.BlockSpec((Bq, hbw), lambda r, h, b: (b, r * NHB + h))
    prev = pl.BlockSpec((Bq, hbw), lambda r, h, b: (jnp.maximum(b - 1, 0), r * NHB + h))
    view = lambda t: t.reshape(n_sub, d * TOK)
    sds = jax.ShapeDtypeStruct((n_sub, d * TOK), F32)
    o, l = pl.pallas_call(
        body, name=name, grid=(d, NHB, nb), in_specs=[cur, cur, prev, cur, prev], out_specs=[cur, cur],
        out_shape=[sds, sds], compiler_params=_cparams("parallel", "parallel", "parallel"),
    )(view(q), view(k), view(k), view(v), view(v))
    return o.reshape(T, TOK), l.reshape(T, TOK)


def _attn_combine(os_, ls_, name):
    def fn(*v):
        n = len(v) // 2
        o, l = v[:n], v[n:]
        m = l[0]
        for t in l[1:]:
            m = jnp.maximum(m, t)
        e = [jnp.exp(t - m) for t in l]
        den = e[0]
        for t in e[1:]:
            den = den + t
        y = e[0] * o[0]
        for t, u in zip(e[1:], o[1:]):
            y = y + t * u
        return y / den, m + jnp.log(den)
    return _elementwise(fn, list(os_) + list(ls_), [], [F32, F32], name)


def _attn_bwd(q, k, v, y, lse, dy, gi, window, d, name):
    T, TOK = q.shape
    E_ = ATTN_HEAD_DIM
    H = TOK // E_
    n_alibi = len(DILATED_GROUPS) * H
    if d > 1:
        return _attn_bwd_strided(q, k, v, y, lse, dy, gi, d, name)
    n_sub = T // d
    nb = n_sub // ATTN_BLOCK
    HB = _heads_per_block(H)
    NHB = H // HB
    hbw = HB * E_
    scale = E_ ** -0.5
    Bq = ATTN_BLOCK

    def body(q0_ref, q1_ref, kp_ref, k0_ref, vp_ref, v0_ref, do0_ref, do1_ref, y0_ref, y1_ref, l0_ref, l1_ref,
             dq_ref, dk_ref, dv_ref):
        hb = pl.program_id(1)
        masks = _attn_masks(pl.program_id(2), nb)
        slopes = [_alibi_slope(gi * H + hb * HB + hh, n_alibi) * float(d) for hh in range(HB)]
        cols = lambda hh: slice(hh * E_, (hh + 1) * E_)

        def load(hh):
            return (tuple(ref[:, cols(hh)].astype(BF16) for ref in (q0_ref, q1_ref, kp_ref, k0_ref, vp_ref, v0_ref))
                    + tuple(ref[:, cols(hh)] for ref in (do0_ref, do1_ref, y0_ref, y1_ref))
                    + tuple(jnp.max(ref[:, cols(hh)], axis=-1, keepdims=True) for ref in (l0_ref, l1_ref)))

        def store(hh, dq, dk, dv):
            dq_ref[:, cols(hh)] = dq.astype(BF16)
            dk_ref[:, cols(hh)] = dk.astype(BF16)
            dv_ref[:, cols(hh)] = dv.astype(BF16)

        _attn_units_bwd(load, HB, slopes, masks, scale, store)

    cur = pl.BlockSpec((Bq, hbw), lambda r, h, b: (b, r * NHB + h))
    prev = pl.BlockSpec((Bq, hbw), lambda r, h, b: (jnp.maximum(b - 1, 0), r * NHB + h))
    nxt = pl.BlockSpec((Bq, hbw), lambda r, h, b: (jnp.minimum(b + 1, nb - 1), r * NHB + h))
    view = lambda t: t.reshape(n_sub, d * TOK)
    sds = jax.ShapeDtypeStruct((n_sub, d * TOK), BF16)
    dq, dk, dv = pl.pallas_call(
        body, name=name, grid=(d, NHB, nb),
        in_specs=[cur, nxt, prev, cur, prev, cur, cur, nxt, cur, nxt, cur, nxt],
        out_specs=[cur, cur, cur], out_shape=[sds, sds, sds],
        compiler_params=_cparams("parallel", "parallel", "parallel"),
    )(view(q), view(q), view(k), view(k), view(v), view(v), view(dy), view(dy), view(y), view(y), view(lse), view(lse))
    return dq.reshape(T, TOK), dk.reshape(T, TOK), dv.reshape(T, TOK)


_FLIPS = {
    "xy": [(1, 0, 0), (0, 1, 0), (1, 1, 0)],
    "c": [(0, 0, 1)],
    "xyc": [(dx, dy, dc) for dx in (0, 1) for dy in (0, 1) for dc in (0, 1) if (dx, dy, dc) != (0, 0, 0)],
}


def _comm_parts(group, srcs, modes, handshake):
    flips = _FLIPS[group]
    F_ = len(flips)
    P_ = F_ + 1
    n = len(srcs)

    def gidx(px, py, pc):
        if group == "xy":
            return 2 * px + py
        if group == "c":
            return pc
        return 4 * px + 2 * py + pc

    def body(*refs):
        src_refs, out_refs = refs[:n], refs[n:2 * n]
        send_sems, recv_sems, loc_sems = refs[2 * n:]
        x, y, c = lax.axis_index("x"), lax.axis_index("y"), lax.axis_index("c")
        me = gidx(x, y, c)
        peers = [(1 - x if dx else x, 1 - y if dy else y, 1 - c if dc else c) for dx, dy, dc in flips]
        if handshake:
            barrier = pltpu.get_barrier_semaphore()
            for peer in peers:
                pl.semaphore_signal(barrier, inc=1, device_id=peer, device_id_type=MESH)
            pl.semaphore_wait(barrier, F_)
        local, remote = [], []
        for i in range(n):
            mode = modes[i]
            if mode != "swap":
                mine = pltpu.make_async_copy(src_refs[i] if mode == "gather" else src_refs[i].at[me],
                                             out_refs[i].at[me], loc_sems.at[i])
                mine.start()
                local.append(mine)
            for f, peer in enumerate(peers):
                cp = pltpu.make_async_remote_copy(
                    src_ref=src_refs[i].at[gidx(*peer)] if mode == "a2a" else src_refs[i],
                    dst_ref=out_refs[i] if mode == "swap" else out_refs[i].at[me],
                    send_sem=send_sems.at[i * F_ + f], recv_sem=recv_sems.at[i * F_ + f],
                    device_id=peer, device_id_type=MESH)
                cp.start()
                remote.append(cp)
        for cp in local:
            cp.wait()
        for cp in remote:
            cp.wait()

    out_shape = []
    for s, mode in zip(srcs, modes):
        assert mode != "swap" or F_ == 1
        shp = (P_,) + tuple(s.shape) if mode == "gather" else tuple(s.shape)
        out_shape.append(jax.ShapeDtypeStruct(shp, s.dtype))
    sems = [pltpu.SemaphoreType.DMA((n * F_,)), pltpu.SemaphoreType.DMA((n * F_,)), pltpu.SemaphoreType.DMA((n,))]
    return body, out_shape, sems


def _comm(name, group, srcs, modes):
    n = len(srcs)
    body, out_shape, sems = _comm_parts(group, srcs, modes, handshake=False)
    anyspec = pl.BlockSpec(memory_space=pl.ANY)
    return pl.pallas_call(body, name=name, in_specs=[anyspec] * n, out_specs=[anyspec] * n, out_shape=out_shape,
                          scratch_shapes=sems)(*srcs)


def _comm_async(name, collective_id, group, srcs, modes):
    body, out_shape, sems = _comm_parts(group, srcs, modes, handshake=True)
    return pl.kernel(body, name=name, out_type=out_shape,
                     mesh=plsc.ScalarSubcoreMesh(axis_name="sequencer", num_cores=1), scratch_types=sems,
                     compiler_params=pltpu.CompilerParams(collective_id=collective_id))(*srcs)


def _dims(D):
    MIX = 2 * D
    MW = MIX // 4
    TOK = MIX - MW
    H = TOK // SSD_HEAD_DIM
    CONV = TOK + 2 * SSD_GROUPS * SSD_STATE
    return dict(MIX=MIX, MW=MW, TOK=TOK, H=H, CONV=CONV)


def _proj_chain(dsegs, wsegs, name):
    acc = None
    for n, (ds, ws) in enumerate(zip(dsegs, wsegs)):
        acc = _mm(ds, ws, "nt", F32, f"{name}_dh{n}", add=acc)
    return acc


def _pad_lanes(a, width=LANES):
    return jnp.pad(a, [(0, 0)] * (a.ndim - 1) + [(0, width - a.shape[-1])])


def _take_cols(parts, a, b):
    out, o = [], 0
    for part in parts:
        w = part.shape[1]
        lo, hi = max(a, o), min(b, o + w)
        if lo < hi:
            out.append(part[:, lo - o:hi - o])
        o += w
    return out[0] if len(out) == 1 else jnp.concatenate(out, axis=1)


def _heads_to_groups(a, G, HPG):
    return jnp.stack([_pad_lanes(a[:, g * HPG:(g + 1) * HPG]) for g in range(G)])


def _groups_to_heads(a, HPG):
    return jnp.concatenate([a[g, :, :HPG] for g in range(a.shape[0])], axis=1)


def _ssd_layer_fwd(x, kv, p, li):
    T, D = x.shape
    dm = _dims(D)
    TOK, MW, H, CONV = dm["TOK"], dm["MW"], dm["H"], dm["CONV"]
    G = SSD_GROUPS
    HPG = H // G
    w = p["w_in"]
    cuts = [0, CONV, CONV + H, CONV + H + MW, CONV + H + MW + TOK, CONV + H + MW + TOK + MW]
    segs = {k: _take_cols(w, cuts[n], cuts[n + 1]) for n, k in enumerate(["xbc", "dt", "qm", "zt", "zm"])}
    segs["dt"] = _pad_lanes(segs["dt"])
    h = _rms_fwd(x, p["norm_g"], f"l{li}_rms")
    pr = {k: _mm(h, ws, "nn", BF16 if k == "qm" else F32, f"l{li}_in_{k}") for k, ws in segs.items()}
    act = _conv_fwd(pr["xbc"], p["conv_w"], p["conv_b"], f"l{li}_conv")
    raw_g = _heads_to_groups(pr["dt"][:, :H], G, HPG)
    hp = lambda a: _pad_lanes(a.reshape(G, 1, HPG))
    bias_g, alog_g, dsk_g = hp(p["dt_bias"]), hp(p["a_log"]), hp(p["d_skip"])
    y, hprev = _ssd_fwd(act, raw_g, bias_g, alog_g, dsk_g, TOK, f"l{li}_ssd")
    ymem = _memattn_fwd(pr["qm"], kv, f"l{li}_mem")
    gt = _gate_norm_fwd(y, pr["zt"], p["ssd_norm_g"], G, f"l{li}_gate_tok")
    gm = _gate_fwd(ymem, pr["zm"], f"l{li}_gate_mem")
    wo = p["w_out"]
    out = _mm(gt, wo[:TOK], "nn", F32, f"l{li}_out_tok", add=x)
    out = _mm(gm, wo[TOK:], "nn", F32, f"l{li}_out_mem", add=out)
    saved = dict(x=x, h=h, pr=pr, act=act, raw_g=raw_g, par=(bias_g, alog_g, dsk_g), y=y, hprev=hprev, ymem=ymem,
                 gt=gt, gm=gm, segs=segs)
    return out, saved


def _ssd_layer_bwd(dout, doutb, kv, p, s, li):
    x = s["x"]
    T, D = x.shape
    dm = _dims(D)
    TOK, MW, H, CONV = dm["TOK"], dm["MW"], dm["H"], dm["CONV"]
    G = SSD_GROUPS
    HPG = H // G
    wo = p["w_out"]
    pr = s["pr"]
    dgt = _mm(doutb, wo[:TOK], "nt", F32, f"l{li}_dgt")
    dgm = _mm(doutb, wo[TOK:], "nt", F32, f"l{li}_dgm")
    dwo = jnp.concatenate([_mm(s["gt"], doutb, "tn", BF16, f"l{li}_dwo_tok"),
                           _mm(s["gm"], doutb, "tn", BF16, f"l{li}_dwo_mem")], axis=0)
    dy, dzt, dng = _gate_norm_bwd(s["y"], pr["zt"], p["ssd_norm_g"], dgt, G, f"l{li}_gate_tok_b")
    dymem, dzm = _gate_bwd(s["ymem"], pr["zm"], dgm, f"l{li}_gate_mem_b")
    dqm, dkv = _memattn_bwd(pr["qm"], kv, dymem, f"l{li}_mem_b")
    bias_g, alog_g, dsk_g = s["par"]
    dxs, dB, dC, draw_g, dalog, dbias, ddsk = _ssd_bwd(s["act"], s["raw_g"], bias_g, alog_g, dsk_g, s["hprev"], dy, TOK,
                                                      f"l{li}_ssd_b")
    dact = jnp.concatenate([dxs, dB, dC], axis=1)
    dpre, dconv_w, dconv_b = _conv_bwd_pre(pr["xbc"], p["conv_w"], p["conv_b"], dact, f"l{li}_conv_b1")
    dxbc = _conv_bwd_in(dpre, p["conv_w"], f"l{li}_conv_b2")
    draw = _pad_lanes(_groups_to_heads(draw_g, HPG)).astype(BF16)
    dsegs = dict(xbc=dxbc, dt=draw, qm=dqm, zt=dzt, zm=dzm)
    keys = ["xbc", "dt", "qm", "zt", "zm"]
    dh = _proj_chain([dsegs[k] for k in keys], [s["segs"][k] for k in keys], f"l{li}")
    dws = {k: _mm(s["h"], dsegs[k], "tn", BF16, f"l{li}_dwin_{k}") for k in keys}
    dws["dt"] = dws["dt"][:, :H]
    dwin = [dws[k] for k in keys]
    dh, dwin, dwo = lax.optimization_barrier((dh, dwin, dwo))
    dx, dxb, dnorm = _rms_bwd(x, p["norm_g"], dh, dout, f"l{li}_rms_b")
    unhead = lambda a: a[:, 0, :HPG].reshape(H)
    grads = dict(norm_g=dnorm, w_in=dwin, conv_w=dconv_w, conv_b=dconv_b, dt_bias=unhead(dbias), a_log=unhead(dalog),
                 d_skip=unhead(ddsk), ssd_norm_g=dng, w_out=dwo)
    return dx, dxb, dkv, grads


def _attn_layer_fwd(x, kv, p, li):
    T, D = x.shape
    dm = _dims(D)
    TOK, MW = dm["TOK"], dm["MW"]
    w = p["w_in"]
    ng = len(DILATED_GROUPS)
    segs = {}
    for g in range(ng):
        for n, nm in enumerate("qkv"):
            c0 = g * 3 * TOK + n * TOK
            segs[f"{nm}{g}"] = _take_cols(w, c0, c0 + TOK)
    c0 = ng * 3 * TOK
    segs["qm"] = _take_cols(w, c0, c0 + MW)
    segs["zt"] = _take_cols(w, c0 + MW, c0 + MW + TOK)
    segs["zm"] = _take_cols(w, c0 + MW + TOK, c0 + MW + TOK + MW)
    h = _rms_fwd(x, p["norm_g"], f"l{li}_rms")
    dense = {"qm"} | {f"{nm}{g}" for g, (_, d) in enumerate(DILATED_GROUPS) if d == 1 for nm in "qkv"}
    pr = {k: _mm(h, ws, "nn", BF16 if k in dense else F32, f"l{li}_in_{k}") for k, ws in segs.items()}
    os_, ls_ = [], []
    for g, (window, d) in enumerate(DILATED_GROUPS):
        o, l = _attn_fwd(pr[f"q{g}"], pr[f"k{g}"], pr[f"v{g}"], g, window, d, f"l{li}_attn{g}")
        os_.append(o)
        ls_.append(l)
    ytok, lse = _attn_combine(os_, ls_, f"l{li}_combine")
    ymem = _memattn_fwd(pr["qm"], kv, f"l{li}_mem")
    gt = _gate_fwd(ytok, pr["zt"], f"l{li}_gate_tok")
    gm = _gate_fwd(ymem, pr["zm"], f"l{li}_gate_mem")
    wo = p["w_out"]
    out = _mm(gt, wo[:TOK], "nn", F32, f"l{li}_out_tok", add=x)
    out = _mm(gm, wo[TOK:], "nn", F32, f"l{li}_out_mem", add=out)
    saved = dict(x=x, h=h, pr=pr, ytok=ytok, lse=lse, ymem=ymem, gt=gt, gm=gm, segs=segs)
    return out, saved


def _attn_layer_bwd(dout, doutb, kv, p, s, li):
    x = s["x"]
    T, D = x.shape
    dm = _dims(D)
    TOK, MW = dm["TOK"], dm["MW"]
    wo = p["w_out"]
    pr = s["pr"]
    dgt = _mm(doutb, wo[:TOK], "nt", F32, f"l{li}_dgt")
    dgm = _mm(doutb, wo[TOK:], "nt", F32, f"l{li}_dgm")
    dwo = jnp.concatenate([_mm(s["gt"], doutb, "tn", BF16, f"l{li}_dwo_tok"),
                           _mm(s["gm"], doutb, "tn", BF16, f"l{li}_dwo_mem")], axis=0)
    dytok, dzt = _gate_bwd(s["ytok"], pr["zt"], dgt, f"l{li}_gate_tok_b")
    dymem, dzm = _gate_bwd(s["ymem"], pr["zm"], dgm, f"l{li}_gate_mem_b")
    dqm, dkv = _memattn_bwd(pr["qm"], kv, dymem, f"l{li}_mem_b")
    dsegs = {}
    for g, (window, d) in enumerate(DILATED_GROUPS):
        dq, dk, dv = _attn_bwd(pr[f"q{g}"], pr[f"k{g}"], pr[f"v{g}"], s["ytok"], s["lse"], dytok, g, window, d,
                               f"l{li}_attn{g}_b")
        dsegs[f"q{g}"], dsegs[f"k{g}"], dsegs[f"v{g}"] = dq, dk, dv
    dsegs["qm"], dsegs["zt"], dsegs["zm"] = dqm, dzt, dzm
    keys = list(s["segs"].keys())
    dh = _proj_chain([dsegs[k] for k in keys], [s["segs"][k] for k in keys], f"l{li}")
    dwin = [_mm(s["h"], dsegs[k], "tn", BF16, f"l{li}_dwin_{k}") for k in keys]
    dh, dwin, dwo = lax.optimization_barrier((dh, dwin, dwo))
    dx, dxb, dnorm = _rms_bwd(x, p["norm_g"], dh, dout, f"l{li}_rms_b")
    return dx, dxb, dkv, dict(norm_g=dnorm, w_in=dwin, w_out=dwo)


def _local_step(x, mem, tgt, mem_norm_g, final_norm_g, n_layers, layer_params, on_layer_grads):
    mem_n = _rms_fwd(mem, mem_norm_g, "mem_rms")
    layers, kvs, saved = [], [], []
    for li in range(n_layers):
        p, x = layer_params(li, x)
        fwd = _ssd_layer_fwd if li % 2 == 0 else _attn_layer_fwd
        kv = _mm(mem_n, p["w_mem_kv"], "nn", F32, f"l{li}_kv")
        x, s = fwd(x, kv, p, li)
        layers.append(p)
        kvs.append(kv)
        saved.append(s)
    loss, dx, dxb, dfinal = _loss_head(x, final_norm_g, tgt, "loss_head")
    dmem_n = None
    for li in reversed(range(n_layers)):
        p = layers[li]
        bwd = _ssd_layer_bwd if li % 2 == 0 else _attn_layer_bwd
        dx, dxb, dkv, g = bwd(dx, dxb, kvs[li], p, saved[li], li)
        g["w_mem_kv"] = _mm(mem_n, dkv, "tn", BF16, f"l{li}_dwkv")
        dmem_n = _mm(dkv, p["w_mem_kv"], "nt", F32, f"l{li}_dmem", add=dmem_n)
        dx, dxb = on_layer_grads(li, g, dx, dxb)
    _, _, dmem_g = _rms_bwd(mem, mem_norm_g, dmem_n, None, "mem_rms_b")
    return loss, dx, dmem_g, dfinal


_SSD_SMALL = ["norm_g", "conv_w", "conv_b", "dt_bias", "a_log", "d_skip", "ssd_norm_g"]
_ATTN_SMALL = ["norm_g"]
_SSD_ORDER = ["norm_g", "w_in", "conv_w", "conv_b", "dt_bias", "a_log", "d_skip", "ssd_norm_g", "w_mem_kv", "w_out"]
_ATTN_ORDER = ["norm_g", "w_in", "w_mem_kv", "w_out"]


def _pack(arrs):
    flat = jnp.concatenate([a.reshape(-1).astype(F32) for a in arrs])
    n = flat.shape[0]
    pad = (-n) % (8 * LANES)
    return jnp.pad(flat, (0, pad)).reshape(-1, LANES)


def _unpack(mat, shapes):
    flat = mat.reshape(-1)
    out, o = [], 0
    for shp in shapes:
        n = math.prod(shp)
        out.append(flat[o:o + n].reshape(shp))
        o += n
    return out


def kernel(x, mem, mem_norm_g, final_norm_g, norm_g_0, w_in_0, conv_w_0, conv_b_0, dt_bias_0, a_log_0, d_skip_0, ssd_norm_g_0, w_mem_kv_0, w_out_0, norm_g_1, w_in_1, w_mem_kv_1, w_out_1, norm_g_2, w_in_2, conv_w_2, conv_b_2, dt_bias_2, a_log_2, d_skip_2, ssd_norm_g_2, w_mem_kv_2, w_out_2, norm_g_3, w_in_3, w_mem_kv_3, w_out_3, loss_target, m_mem_norm_g, m_final_norm_g, m_norm_g_0, m_w_in_0, m_conv_w_0, m_conv_b_0, m_dt_bias_0, m_a_log_0, m_d_skip_0, m_ssd_norm_g_0, m_w_mem_kv_0, m_w_out_0, m_norm_g_1, m_w_in_1, m_w_mem_kv_1, m_w_out_1, m_norm_g_2, m_w_in_2, m_conv_w_2, m_conv_b_2, m_dt_bias_2, m_a_log_2, m_d_skip_2, m_ssd_norm_g_2, m_w_mem_kv_2, m_w_out_2, m_norm_g_3, m_w_in_3, m_w_mem_kv_3, m_w_out_3, v_mem_norm_g, v_final_norm_g, v_norm_g_0, v_w_in_0, v_conv_w_0, v_conv_b_0, v_dt_bias_0, v_a_log_0, v_d_skip_0, v_ssd_norm_g_0, v_w_mem_kv_0, v_w_out_0, v_norm_g_1, v_w_in_1, v_w_mem_kv_1, v_w_out_1, v_norm_g_2, v_w_in_2, v_conv_w_2, v_conv_b_2, v_dt_bias_2, v_a_log_2, v_d_skip_2, v_ssd_norm_g_2, v_w_mem_kv_2, v_w_out_2, v_norm_g_3, v_w_in_3, v_w_mem_kv_3, v_w_out_3):
    a = dict(locals())
    names = ["mem_norm_g", "final_norm_g"]
    for li in range(DEPTH):
        names += [f"{k}_{li}" for k in (_SSD_ORDER if li % 2 == 0 else _ATTN_ORDER)]
    W = {n: a[n] for n in names}
    Mo = {n: a["m_" + n] for n in names}
    Vo = {n: a["v_" + n] for n in names}
    NX = 4
    chip = 2 * lax.axis_index("x") + lax.axis_index("y")

    gathered = []
    for li in range(DEPTH):
        srcs = [W[f"w_in_{li}"].astype(BF16), W[f"w_mem_kv_{li}"].astype(BF16), W[f"w_out_{li}"].astype(BF16)]
        if li % 2 == 0:
            srcs.append(W[f"conv_w_{li}"])
        gathered.append(_comm_async(f"gather_w{li}", li, "xy", srcs, ["gather"] * len(srcs)))

    def layer_params(li, xin):
        got = gathered[li]
        if li > 0:
            got, xin = lax.optimization_barrier((got, xin))
        rows = lambda g: g.reshape((-1,) + g.shape[2:])
        parts = lambda g: [g[k] for k in range(NX)]
        p = dict(w_in=parts(got[0]), w_mem_kv=rows(got[1]), w_out=rows(got[2]), norm_g=W[f"norm_g_{li}"])
        if li % 2 == 0:
            p.update(conv_w=jnp.concatenate(parts(got[3]), axis=1), conv_b=W[f"conv_b_{li}"],
                     dt_bias=W[f"dt_bias_{li}"], a_log=W[f"a_log_{li}"], d_skip=W[f"d_skip_{li}"],
                     ssd_norm_g=W[f"ssd_norm_g_{li}"])
        return p, xin

    G, Dl, Mn, Vn = {}, {}, {}, {}
    grads = [None] * DEPTH
    in_flight = []

    def finish_exchange(li, got):
        parts = [_sum_lead(t, f"l{li}_gsum{n}") for n, t in enumerate(got)]
        theirs = _comm_async(f"swap_g{li}", 2 * DEPTH + li, "c", parts, ["swap"] * 3)
        for nm, mine, other in zip(["w_in", "w_mem_kv", "w_out"], parts, theirs):
            key = f"{nm}_{li}"
            G[key], Dl[key], Mn[key], Vn[key] = _adamw(W[key], Mo[key], Vo[key], [mine, other], f"adamw_{key}")

    def on_layer_grads(li, g, dx, dxb):
        grads[li] = g
        dwin = g["w_in"]
        cw = sum(t.shape[1] for t in dwin) // NX
        chunks = [jnp.stack([_take_cols(dwin, k * cw, (k + 1) * cw) for k in range(NX)]),
                  g["w_mem_kv"].reshape((NX, -1) + g["w_mem_kv"].shape[1:]),
                  g["w_out"].reshape((NX, -1) + g["w_out"].shape[1:])]
        prev = in_flight.pop() if in_flight else None
        pgot = prev[1] if prev else []
        chunks, pgot, dx, dxb = lax.optimization_barrier((chunks, pgot, dx, dxb))
        in_flight.append((li, _comm_async(f"xchg_g{li}", DEPTH + li, "xy", chunks, ["a2a"] * 3)))
        if prev:
            finish_exchange(prev[0], pgot)
        return dx, dxb

    loss_l, dx, dmem_g, dfinal = _local_step(x[0], mem[0], loss_target[0], W["mem_norm_g"], W["final_norm_g"], DEPTH,
                                             layer_params, on_layer_grads)
    finish_exchange(*in_flight.pop())
    loss = lax.psum(loss_l, ("x", "y", "c"))

    small_names = ["mem_norm_g", "final_norm_g"]
    small_grads = [dmem_g, dfinal]
    for li in range(DEPTH):
        for k in (_SSD_SMALL if li % 2 == 0 else _ATTN_SMALL):
            small_names.append(f"{k}_{li}")
            small_grads.append(grads[li][k])
    shapes = [tuple(t.shape) for t in small_grads]
    allg = _comm("gather_small", "xyc", [_pack(small_grads)], ["gather"])[0]
    gsum = _unpack(_sum_lead(allg, "small_gsum"), shapes)
    small_w, small_m, small_v, small_g = [], [], [], []
    for nme, gv in zip(small_names, gsum):
        if nme.startswith("conv_w"):
            cw = W[nme].shape[1]
            gv = lax.dynamic_slice_in_dim(gv, chip * cw, cw, axis=1)
        small_g.append(gv)
        small_w.append(W[nme])
        small_m.append(Mo[nme])
        small_v.append(Vo[nme])
    sshapes = [tuple(t.shape) for t in small_g]
    res = _adamw(_pack(small_w), _pack(small_m), _pack(small_v), [_pack(small_g)], "adamw_small")
    for dst, mat in zip((G, Dl, Mn, Vn), res):
        for nme, t in zip(small_names, _unpack(mat, sshapes)):
            dst[nme] = t

    return (loss, dx[None], *[G[n] for n in names], *[Dl[n] for n in names], *[Mn[n] for n in names],
            *[Vn[n] for n in names])
```

```python
import functools
import math

import jax
import jax.numpy as jnp
from jax import lax
from jax.experimental import pallas as pl
from jax.experimental.pallas import tpu as pltpu
from jax.experimental.pallas import tpu_sc as plsc

F32 = jnp.float32
BF16 = jnp.bfloat16

EPS = 1e-6
MEM_HEADS = 4
SSD_HEAD_DIM = 64
SSD_GROUPS = 8
SSD_STATE = 128
SSD_CONV = 4
SSD_CHUNK = 128
ATTN_HEAD_DIM = 128
ATTN_BLOCK = 128
DILATED_GROUPS = ((128, 1), (512, 4), (2048, 16))
ALIBI_MAX_EXP = 8.0
DEPTH = 4

ADAM_LR = 0.001
ADAM_B1 = 0.9
ADAM_B2 = 0.999
ADAM_EPS = 1e-08
ADAM_WD = 0.01
ADAM_STEP = 10

LANES = 128
VMEM_LIMIT_BYTES = 48 * 1024 * 1024
NEG = -1e30
MESH = pl.DeviceIdType.MESH


def _cparams(*sem):
    return pltpu.CompilerParams(dimension_semantics=tuple(sem), vmem_limit_bytes=VMEM_LIMIT_BYTES)


def _tile(dim, pref, unit=LANES):
    if dim <= pref:
        return dim
    t = (pref // unit) * unit
    while t >= unit:
        if dim % t == 0:
            return t
        t -= unit
    return dim


def _ew_tiles(rows, cols):
    tc = cols if (cols % LANES != 0 or cols <= 2048) else _tile(cols, 2048)
    tr = rows
    while tr * tc > 256 * 1024 and tr % 2 == 0 and (tr // 2) % 8 == 0:
        tr //= 2
    return tr, tc


def _sigmoid(v):
    return 1.0 / (1.0 + jnp.exp(-v))


def _dot(a, b):
    return jnp.dot(a, b, preferred_element_type=F32)


def _dot_nt(a, b):
    return lax.dot_general(a, b, (((1,), (1,)), ((), ())), preferred_element_type=F32)


def _dot_tn(a, b):
    return lax.dot_general(a, b, (((0,), (0,)), ((), ())), preferred_element_type=F32)


def _split3(v):
    hi = v.astype(BF16)
    r = v - hi.astype(F32)
    mid = r.astype(BF16)
    lo = (r - mid.astype(F32)).astype(BF16)
    return hi, mid, lo


def _xdot(v, onehot):
    hi, mid, lo = _split3(v)
    return _dot(hi, onehot) + _dot(mid, onehot) + _dot(lo, onehot)


def _xdot_l(onehot, v):
    hi, mid, lo = _split3(v)
    return _dot(onehot, hi) + _dot(onehot, mid) + _dot(onehot, lo)


def _mm(a, b, mode, out_dtype, name, add=None):
    if mode == "nn":
        M, K = a.shape
        N = b.shape[1]
    elif mode == "nt":
        M, K = a.shape
        N = b.shape[0]
    else:
        K, M = a.shape
        N = b.shape[1]
    tm, tn, tk = (2048, 1024, 1024) if mode == "tn" else (512, 1024, 3072)
    tm, tn, tk = _tile(M, tm, 8 if M < LANES else LANES), _tile(N, tn), _tile(K, tk)
    nk = K // tk
    has_add = add is not None

    def product(a_ref, b_ref):
        av = a_ref[...].astype(BF16)
        bv = b_ref[...].astype(BF16)
        if mode == "nn":
            return _dot(av, bv)
        if mode == "nt":
            return _dot_nt(av, bv)
        return _dot_tn(av, bv)

    def body(*refs):
        a_ref, b_ref = refs[:2]
        add_ref = refs[2] if has_add else None
        o_ref = refs[3] if has_add else refs[2]

        def finish(r):
            if has_add:
                r = r + add_ref[...]
            o_ref[...] = r.astype(out_dtype)

        if nk == 1:
            finish(product(a_ref, b_ref))
            return
        acc = refs[-1]
        k = pl.program_id(2)

        @pl.when(k == 0)
        def _():
            acc[...] = product(a_ref, b_ref)

        @pl.when((k > 0) & (k < nk - 1))
        def _():
            acc[...] += product(a_ref, b_ref)

        @pl.when(k == nk - 1)
        def _():
            finish(acc[...] + product(a_ref, b_ref))

    if mode == "nn":
        a_spec = pl.BlockSpec((tm, tk), lambda j, i, k: (i, k))
        b_spec = pl.BlockSpec((tk, tn), lambda j, i, k: (k, j))
    elif mode == "nt":
        a_spec = pl.BlockSpec((tm, tk), lambda j, i, k: (i, k))
        b_spec = pl.BlockSpec((tn, tk), lambda j, i, k: (j, k))
    else:
        a_spec = pl.BlockSpec((tk, tm), lambda j, i, k: (k, i))
        b_spec = pl.BlockSpec((tk, tn), lambda j, i, k: (k, j))
    o_spec = pl.BlockSpec((tm, tn), lambda j, i, k: (i, j))
    in_specs = [a_spec, b_spec] + ([o_spec] if has_add else [])
    args = (a, b) + ((add,) if has_add else ())
    return pl.pallas_call(
        body, name=name, grid=(N // tn, M // tm, nk), in_specs=in_specs, out_specs=o_spec,
        out_shape=jax.ShapeDtypeStruct((M, N), out_dtype),
        scratch_shapes=[pltpu.VMEM((tm, tn), F32)] if nk > 1 else [],
        compiler_params=_cparams("parallel", "parallel", "arbitrary"),
    )(*args)


def _rms_fwd(x, g, name):
    R, Dm = x.shape
    tr = _tile(R, 256, 8)

    def body(x_ref, g_ref, o_ref):
        xv = x_ref[...]
        r = lax.rsqrt(jnp.mean(xv * xv, axis=-1, keepdims=True) + EPS)
        o_ref[...] = (xv * r * g_ref[...]).astype(BF16)

    return pl.pallas_call(
        body, name=name, grid=(R // tr,),
        in_specs=[pl.BlockSpec((tr, Dm), lambda i: (i, 0)), pl.BlockSpec((1, Dm), lambda i: (0, 0))],
        out_specs=pl.BlockSpec((tr, Dm), lambda i: (i, 0)),
        out_shape=jax.ShapeDtypeStruct((R, Dm), BF16),
        compiler_params=_cparams("parallel"),
    )(x, g.reshape(1, Dm))


def _rms_bwd(x, g, dh, dres, name):
    R, Dm = x.shape
    tr = _tile(R, 256, 8)
    has_res = dres is not None

    def body(*refs):
        if has_res:
            x_ref, g_ref, dh_ref, dres_ref, dx_ref, dxb_ref, dg_ref = refs
        else:
            x_ref, g_ref, dh_ref, dx_ref, dxb_ref, dg_ref = refs
        xv = x_ref[...]
        r = lax.rsqrt(jnp.mean(xv * xv, axis=-1, keepdims=True) + EPS)
        xhat = xv * r
        dhv = dh_ref[...]
        dxhat = dhv * g_ref[...]
        dx = r * (dxhat - xhat * jnp.mean(dxhat * xhat, axis=-1, keepdims=True))
        if has_res:
            dx = dx + dres_ref[...]
        dx_ref[...] = dx
        dxb_ref[...] = dx.astype(BF16)

        @pl.when(pl.program_id(0) == 0)
        def _():
            dg_ref[...] = jnp.zeros_like(dg_ref)

        dg_ref[...] += jnp.sum(dhv * xhat, axis=0, keepdims=True)

    row = pl.BlockSpec((tr, Dm), lambda i: (i, 0))
    vec = pl.BlockSpec((1, Dm), lambda i: (0, 0))
    in_specs = [row, vec, row] + ([row] if has_res else [])
    args = (x, g.reshape(1, Dm), dh) + ((dres,) if has_res else ())
    dx, dxb, dg = pl.pallas_call(
        body, name=name, grid=(R // tr,), in_specs=in_specs, out_specs=[row, row, vec],
        out_shape=[jax.ShapeDtypeStruct((R, Dm), F32), jax.ShapeDtypeStruct((R, Dm), BF16),
                   jax.ShapeDtypeStruct((1, Dm), F32)],
        compiler_params=_cparams("arbitrary"),
    )(*args)
    return dx, dxb, dg.reshape(Dm)


def _loss_head(x, g, tgt, name):
    R, Dm = x.shape
    tr = _tile(R, 256, 8)

    def body(x_ref, g_ref, t_ref, loss_ref, dx_ref, dxb_ref, dg_ref):
        xv = x_ref[...]
        gv = g_ref[...]
        r = lax.rsqrt(jnp.mean(xv * xv, axis=-1, keepdims=True) + EPS)
        xhat = xv * r
        e = xhat * gv - t_ref[...]
        part = jnp.sum(jnp.mean(e * e, axis=-1, keepdims=True), axis=0, keepdims=True) * 0.5
        dy = e * (1.0 / Dm)
        dxhat = dy * gv
        dx = r * (dxhat - xhat * jnp.mean(dxhat * xhat, axis=-1, keepdims=True))
        dx_ref[...] = dx
        dxb_ref[...] = dx.astype(BF16)

        @pl.when(pl.program_id(0) == 0)
        def _():
            dg_ref[...] = jnp.zeros_like(dg_ref)
            loss_ref[...] = jnp.zeros_like(loss_ref)

        dg_ref[...] += jnp.sum(dy * xhat, axis=0, keepdims=True)
        loss_ref[...] += jnp.broadcast_to(part, loss_ref.shape)

    row = pl.BlockSpec((tr, Dm), lambda i: (i, 0))
    vec = pl.BlockSpec((1, Dm), lambda i: (0, 0))
    lsp = pl.BlockSpec((1, LANES), lambda i: (0, 0))
    loss, dx, dxb, dg = pl.pallas_call(
        body, name=name, grid=(R // tr,), in_specs=[row, vec, row], out_specs=[lsp, row, row, vec],
        out_shape=[jax.ShapeDtypeStruct((1, LANES), F32), jax.ShapeDtypeStruct((R, Dm), F32),
                   jax.ShapeDtypeStruct((R, Dm), BF16), jax.ShapeDtypeStruct((1, Dm), F32)],
        compiler_params=_cparams("arbitrary"),
    )(x, g.reshape(1, Dm), tgt)
    return loss[0, 0], dx, dxb, dg.reshape(Dm)


def _elementwise(fn, mats, vecs, out_dtypes, name):
    R, C = mats[0].shape
    tr, tc = _ew_tiles(R, C)
    nm, nv, no = len(mats), len(vecs), len(out_dtypes)

    def body(*refs):
        ins = [r[...] for r in refs[:nm + nv]]
        outs = fn(*ins)
        for o_ref, o in zip(refs[nm + nv:], outs):
            o_ref[...] = o.astype(o_ref.dtype)

    blk = pl.BlockSpec((tr, tc), lambda i, j: (i, j))
    vblk = pl.BlockSpec((1, tc), lambda i, j: (0, j))
    res = pl.pallas_call(
        body, name=name, grid=(R // tr, C // tc),
        in_specs=[blk] * nm + [vblk] * nv, out_specs=[blk] * no,
        out_shape=[jax.ShapeDtypeStruct((R, C), dt) for dt in out_dtypes],
        compiler_params=_cparams("parallel", "parallel"),
    )(*mats, *[v.reshape(1, C) for v in vecs])
    return res


def _sum_lead(arr, name):
    P_, R, C = arr.shape
    tr, tc = _ew_tiles(R, C)

    def body(a_ref, o_ref):
        s = a_ref[0].astype(F32)
        for p in range(1, P_):
            s = s + a_ref[p].astype(F32)
        o_ref[...] = s

    return pl.pallas_call(
        body, name=name, grid=(R // tr, C // tc),
        in_specs=[pl.BlockSpec((P_, tr, tc), lambda i, j: (0, i, j))],
        out_specs=pl.BlockSpec((tr, tc), lambda i, j: (i, j)),
        out_shape=jax.ShapeDtypeStruct((R, C), F32),
        compiler_params=_cparams("parallel", "parallel"),
    )(arr)


def _adamw(w, m, v, gparts, name):
    P_ = len(gparts)
    R, C = w.shape
    tr, tc = _ew_tiles(R, C)
    c1 = 1.0 / (1.0 - ADAM_B1 ** ADAM_STEP)
    c2 = 1.0 / (1.0 - ADAM_B2 ** ADAM_STEP)

    def body(w_ref, m_ref, v_ref, *rest):
        g_refs, (go_ref, d_ref, mo_ref, vo_ref) = rest[:P_], rest[P_:]
        g = g_refs[0][...]
        for g_ref in g_refs[1:]:
            g = g + g_ref[...]
        mn = ADAM_B1 * m_ref[...] + (1.0 - ADAM_B1) * g
        vn = ADAM_B2 * v_ref[...] + (1.0 - ADAM_B2) * (g * g)
        m_hat = mn * c1
        v_hat = vn * c2
        d_ref[...] = -ADAM_LR * (m_hat / (jnp.sqrt(v_hat) + ADAM_EPS) + ADAM_WD * w_ref[...])
        go_ref[...] = g
        mo_ref[...] = mn
        vo_ref[...] = vn

    blk = pl.BlockSpec((tr, tc), lambda i, j: (i, j))
    sds = jax.ShapeDtypeStruct((R, C), F32)
    return pl.pallas_call(
        body, name=name, grid=(R // tr, C // tc),
        in_specs=[blk] * (3 + P_), out_specs=[blk] * 4, out_shape=[sds] * 4,
        compiler_params=_cparams("parallel", "parallel"),
    )(w, m, v, *gparts)


SUBLANES = 8


def _conv_pre(u, up, w_ref, b, first):
    tr = u.shape[0]
    rows = lax.broadcasted_iota(jnp.int32, u.shape, 0)
    keep = 1.0 - first.astype(F32)
    acc = b + w_ref[SSD_CONV - 1:SSD_CONV, :] * u
    shifted = []
    for j in range(1, SSD_CONV):
        su = pltpu.roll(u, j, 0)
        sp = jnp.tile(pltpu.roll(up, j, 0) * keep, (tr // SUBLANES, 1))
        sh = jnp.where(rows < j, sp, su)
        shifted.append(sh)
        acc = acc + w_ref[SSD_CONV - 1 - j:SSD_CONV - j, :] * sh
    return acc, shifted


def _conv_fwd(u, w, b, name):
    T, C = u.shape
    tr, tc = _tile(T, 256, 8), _tile(C, 1024)

    def body(u_ref, up_ref, w_ref, b_ref, o_ref):
        pre, _ = _conv_pre(u_ref[...], up_ref[...], w_ref, b_ref[...], pl.program_id(0) == 0)
        o_ref[...] = pre * _sigmoid(pre)

    return pl.pallas_call(
        body, name=name, grid=(T // tr, C // tc),
        in_specs=[pl.BlockSpec((tr, tc), lambda i, j: (i, j)),
                  pl.BlockSpec((SUBLANES, tc), lambda i, j: (jnp.maximum(i * (tr // SUBLANES) - 1, 0), j)),
                  pl.BlockSpec((SSD_CONV, tc), lambda i, j: (0, j)),
                  pl.BlockSpec((1, tc), lambda i, j: (0, j))],
        out_specs=pl.BlockSpec((tr, tc), lambda i, j: (i, j)),
        out_shape=jax.ShapeDtypeStruct((T, C), F32),
        compiler_params=_cparams("parallel", "parallel"),
    )(u, u, w, b.reshape(1, C))


def _conv_bwd_pre(u, w, b, dact, name):
    T, C = u.shape
    tr, tc = _tile(T, 256, 8), _tile(C, 1024)

    def body(u_ref, up_ref, w_ref, b_ref, da_ref, dp_ref, dw_ref, db_ref):
        i = pl.program_id(1)
        uv = u_ref[...]
        pre, shifted = _conv_pre(uv, up_ref[...], w_ref, b_ref[...], i == 0)
        sg = _sigmoid(pre)
        dpre = da_ref[...] * (sg * (1.0 + pre * (1.0 - sg)))
        dp_ref[...] = dpre

        @pl.when(i == 0)
        def _():
            dw_ref[...] = jnp.zeros_like(dw_ref)
            db_ref[...] = jnp.zeros_like(db_ref)

        db_ref[...] += jnp.sum(dpre, axis=0, keepdims=True)
        dw_ref[SSD_CONV - 1:SSD_CONV, :] += jnp.sum(dpre * uv, axis=0, keepdims=True)
        for j in range(1, SSD_CONV):
            dw_ref[SSD_CONV - 1 - j:SSD_CONV - j, :] += jnp.sum(dpre * shifted[j - 1], axis=0, keepdims=True)

    blk = pl.BlockSpec((tr, tc), lambda j, i: (i, j))
    dpre, dw, db = pl.pallas_call(
        body, name=name, grid=(C // tc, T // tr),
        in_specs=[blk, pl.BlockSpec((SUBLANES, tc), lambda j, i: (jnp.maximum(i * (tr // SUBLANES) - 1, 0), j)),
                  pl.BlockSpec((SSD_CONV, tc), lambda j, i: (0, j)),
                  pl.BlockSpec((1, tc), lambda j, i: (0, j)), blk],
        out_specs=[blk, pl.BlockSpec((SSD_CONV, tc), lambda j, i: (0, j)), pl.BlockSpec((1, tc), lambda j, i: (0, j))],
        out_shape=[jax.ShapeDtypeStruct((T, C), F32), jax.ShapeDtypeStruct((SSD_CONV, C), F32),
                   jax.ShapeDtypeStruct((1, C), F32)],
        compiler_params=_cparams("parallel", "arbitrary"),
    )(u, u, w, b.reshape(1, C), dact)
    return dpre, dw, db.reshape(C)


def _conv_bwd_in(dpre, w, name):
    T, C = dpre.shape
    tr, tc = _tile(T, 256, 8), _tile(C, 1024)
    nb = T // tr

    def body(d_ref, dn_ref, w_ref, o_ref):
        d = d_ref[...]
        keep = 1.0 - (pl.program_id(0) == nb - 1).astype(F32)
        dn = dn_ref[...] * keep
        rows = lax.broadcasted_iota(jnp.int32, d.shape, 0)
        acc = w_ref[SSD_CONV - 1:SSD_CONV, :] * d
        for j in range(1, SSD_CONV):
            sd = pltpu.roll(d, tr - j, 0)
            sn = jnp.tile(pltpu.roll(dn, SUBLANES - j, 0), (tr // SUBLANES, 1))
            acc = acc + w_ref[SSD_CONV - 1 - j:SSD_CONV - j, :] * jnp.where(rows >= tr - j, sn, sd)
        o_ref[...] = acc.astype(BF16)

    return pl.pallas_call(
        body, name=name, grid=(nb, C // tc),
        in_specs=[pl.BlockSpec((tr, tc), lambda i, j: (i, j)),
                  pl.BlockSpec((SUBLANES, tc), lambda i, j: (jnp.minimum((i + 1) * (tr // SUBLANES), T // SUBLANES - 1), j)),
                  pl.BlockSpec((SSD_CONV, tc), lambda i, j: (0, j))],
        out_specs=pl.BlockSpec((tr, tc), lambda i, j: (i, j)),
        out_shape=jax.ShapeDtypeStruct((T, C), BF16),
        compiler_params=_cparams("parallel", "parallel"),
    )(dpre, dpre, w)


def _mem_probs(qh, kh, scale):
    s = _dot_nt(qh.astype(BF16), kh.astype(BF16)) * scale
    m = jnp.max(s, axis=-1, keepdims=True)
    p = jnp.exp(s - m)
    return p / jnp.sum(p, axis=-1, keepdims=True)


def _memattn_fwd(q, kv, name):
    T, MW = q.shape
    NM = kv.shape[0]
    hd = MW // MEM_HEADS
    scale = hd ** -0.5
    tq = _tile(T, 512, 8)

    def body(q_ref, kv_ref, o_ref):
        for h in range(MEM_HEADS):
            sl = slice(h * hd, (h + 1) * hd)
            p = _mem_probs(q_ref[:, sl], kv_ref[:, sl], scale)
            vh = kv_ref[:, MW + h * hd:MW + (h + 1) * hd]
            o_ref[:, sl] = _dot(p.astype(BF16), vh.astype(BF16))

    return pl.pallas_call(
        body, name=name, grid=(T // tq,),
        in_specs=[pl.BlockSpec((tq, MW), lambda i: (i, 0)), pl.BlockSpec((NM, 2 * MW), lambda i: (0, 0))],
        out_specs=pl.BlockSpec((tq, MW), lambda i: (i, 0)),
        out_shape=jax.ShapeDtypeStruct((T, MW), F32),
        compiler_params=_cparams("parallel"),
    )(q, kv)


def _memattn_bwd(q, kv, dy, name):
    T, MW = q.shape
    NM = kv.shape[0]
    hd = MW // MEM_HEADS
    scale = hd ** -0.5
    tq = _tile(T, 512, 8)

    def body(q_ref, kv_ref, dy_ref, dq_ref, dkv_ref):
        @pl.when(pl.program_id(0) == 0)
        def _():
            dkv_ref[...] = jnp.zeros_like(dkv_ref)

        for h in range(MEM_HEADS):
            sl = slice(h * hd, (h + 1) * hd)
            vsl = slice(MW + h * hd, MW + (h + 1) * hd)
            qh = q_ref[:, sl]
            kh = kv_ref[:, sl]
            vh = kv_ref[:, vsl]
            dyh = dy_ref[:, sl].astype(BF16)
            p = _mem_probs(qh, kh, scale)
            dp = _dot_nt(dyh, vh.astype(BF16))
            ds = p * (dp - jnp.sum(dp * p, axis=-1, keepdims=True)) * scale
            dq_ref[:, sl] = _dot(ds.astype(BF16), kh.astype(BF16)).astype(BF16)
            dkv_ref[:, sl] += _dot(ds.T.astype(BF16), qh.astype(BF16))
            dkv_ref[:, vsl] += _dot(p.T.astype(BF16), dyh)

    return pl.pallas_call(
        body, name=name, grid=(T // tq,),
        in_specs=[pl.BlockSpec((tq, MW), lambda i: (i, 0)), pl.BlockSpec((NM, 2 * MW), lambda i: (0, 0)),
                  pl.BlockSpec((tq, MW), lambda i: (i, 0))],
        out_specs=[pl.BlockSpec((tq, MW), lambda i: (i, 0)), pl.BlockSpec((NM, 2 * MW), lambda i: (0, 0))],
        out_shape=[jax.ShapeDtypeStruct((T, MW), BF16), jax.ShapeDtypeStruct((NM, 2 * MW), F32)],
        compiler_params=_cparams("arbitrary"),
    )(q, kv, dy)


def _silu_parts(z):
    sg = _sigmoid(z)
    return z * sg, sg * (1.0 + z * (1.0 - sg))


def _gate_fwd(a, z, name):
    return _elementwise(lambda av, zv: (av * _silu_parts(zv)[0],), [a, z], [], [BF16], name)[0]


def _gate_bwd(a, z, d, name):
    def fn(av, zv, dv):
        s, ds = _silu_parts(zv)
        return dv * s, dv * av * ds
    return _elementwise(fn, [a, z, d], [], [F32, BF16], name)


def _gate_norm_fwd(y, z, g, groups, name):
    T, C = y.shape
    gw = C // groups
    tr = _tile(T, 128, 8)

    def body(y_ref, z_ref, g_ref, o_ref):
        for k in range(groups):
            sl = slice(k * gw, (k + 1) * gw)
            u = y_ref[:, sl] * _silu_parts(z_ref[:, sl])[0]
            r = lax.rsqrt(jnp.mean(u * u, axis=-1, keepdims=True) + EPS)
            o_ref[:, sl] = (u * r * g_ref[:, sl]).astype(BF16)

    row = pl.BlockSpec((tr, C), lambda i: (i, 0))
    return pl.pallas_call(
        body, name=name, grid=(T // tr,), in_specs=[row, row, pl.BlockSpec((1, C), lambda i: (0, 0))],
        out_specs=row, out_shape=jax.ShapeDtypeStruct((T, C), BF16),
        compiler_params=_cparams("parallel"),
    )(y, z, g.reshape(1, C))


def _gate_norm_bwd(y, z, g, d, groups, name):
    T, C = y.shape
    gw = C // groups
    tr = _tile(T, 128, 8)

    def body(y_ref, z_ref, g_ref, d_ref, dy_ref, dz_ref, dg_ref):
        @pl.when(pl.program_id(0) == 0)
        def _():
            dg_ref[...] = jnp.zeros_like(dg_ref)

        for k in range(groups):
            sl = slice(k * gw, (k + 1) * gw)
            yv = y_ref[:, sl]
            s, ds = _silu_parts(z_ref[:, sl])
            u = yv * s
            r = lax.rsqrt(jnp.mean(u * u, axis=-1, keepdims=True) + EPS)
            uhat = u * r
            dv = d_ref[:, sl]
            dg_ref[:, sl] += jnp.sum(dv * uhat, axis=0, keepdims=True)
            duhat = dv * g_ref[:, sl]
            du = r * (duhat - uhat * jnp.mean(duhat * uhat, axis=-1, keepdims=True))
            dy_ref[:, sl] = du * s
            dz_ref[:, sl] = (du * yv * ds).astype(BF16)

    row = pl.BlockSpec((tr, C), lambda i: (i, 0))
    vec = pl.BlockSpec((1, C), lambda i: (0, 0))
    dy, dz, dg = pl.pallas_call(
        body, name=name, grid=(T // tr,), in_specs=[row, row, vec, row], out_specs=[row, row, vec],
        out_shape=[jax.ShapeDtypeStruct((T, C), F32), jax.ShapeDtypeStruct((T, C), BF16),
                   jax.ShapeDtypeStruct((1, C), F32)],
        compiler_params=_cparams("arbitrary"),
    )(y, z, g.reshape(1, C), d)
    return dy, dz, dg.reshape(C)


def _ssd_common(raw, bias, alog, Q, HP, HPG):
    P_ = SSD_HEAD_DIM
    dt_in = raw + bias
    dt = jnp.maximum(dt_in, 0.0) + jnp.log(1.0 + jnp.exp(-jnp.abs(dt_in)))
    a = -jnp.exp(alog)
    r_q = lax.broadcasted_iota(jnp.int32, (Q, Q), 0)
    c_q = lax.broadcasted_iota(jnp.int32, (Q, Q), 1)
    causal = r_q >= c_q
    tril = causal.astype(BF16)
    A = _xdot_l(tril, dt * a)
    e_r = lax.broadcasted_iota(jnp.int32, (LANES, HP), 0)
    e_c = lax.broadcasted_iota(jnp.int32, (LANES, HP), 1)
    E = ((e_c >= e_r * P_) & (e_c < (e_r + 1) * P_) & (e_r < HPG)).astype(BF16)
    return dt, a, A, causal, E


def _head_cols(v, vt, j):
    lane = lax.broadcasted_iota(jnp.int32, v.shape, 1)
    sub = lax.broadcasted_iota(jnp.int32, vt.shape, 0)
    col = jnp.sum(jnp.where(lane == j, v, 0.0), axis=-1, keepdims=True)
    row = jnp.sum(jnp.where(sub == j, vt, 0.0), axis=0, keepdims=True)
    return col, row


def _ssd_fwd(act, raw_g, bias_g, alog_g, dsk_g, TOK, name):
    T = act.shape[0]
    G, N, Q, P_ = SSD_GROUPS, SSD_STATE, SSD_CHUNK, SSD_HEAD_DIM
    HP = TOK // G
    HPG = HP // P_
    NC = T // Q

    def body(x_ref, b_ref, c_ref, raw_ref, bias_ref, alog_ref, dsk_ref, y_ref, hp_ref, hT):
        @pl.when(pl.program_id(1) == 0)
        def _():
            hT[...] = jnp.zeros_like(hT)

        xs = x_ref[...]
        Bb = b_ref[...].astype(BF16)
        Cb = c_ref[...].astype(BF16)
        dt, a, A, causal, E = _ssd_common(raw_ref[...], bias_ref[...], alog_ref[...], Q, HP, HPG)
        AT = A.T
        dt_e = _xdot(dt, E)
        A_e = _xdot(A, E)
        dsk_e = _xdot(jnp.broadcast_to(dsk_ref[...], (Q, LANES)), E)
        rows = lax.broadcasted_iota(jnp.int32, (Q, HP), 0)
        cols = lax.broadcasted_iota(jnp.int32, (Q, HP), 1)
        Al_e = jnp.sum(jnp.where(rows == Q - 1, A_e, 0.0), axis=0, keepdims=True)
        xdt = xs * dt_e
        hprev = hT[...]
        hp_ref[...] = hprev
        CB = _dot_nt(Cb, Bb)
        y = _dot(Cb, hprev.astype(BF16)) * jnp.exp(A_e) + dsk_e * xs
        for j in range(HPG):
            a_col, a_row = _head_cols(A, AT, j)
            L = jnp.exp(jnp.where(causal, a_col - a_row, NEG))
            xj = jnp.where((cols >= j * P_) & (cols < (j + 1) * P_), xdt, 0.0)
            y = y + _dot((CB * L).astype(BF16), xj.astype(BF16))
        y_ref[...] = y
        dte = jnp.exp(Al_e - A_e)
        hT[...] = jnp.exp(Al_e) * hprev + _dot(b_ref[...].T.astype(BF16), (xdt * dte).astype(BF16))

    nbx = TOK // N
    par = pl.BlockSpec((None, 1, LANES), lambda g, c: (g, 0, 0))
    return pl.pallas_call(
        body, name=name, grid=(G, NC),
        in_specs=[pl.BlockSpec((Q, HP), lambda g, c: (c, g)),
                  pl.BlockSpec((Q, N), lambda g, c: (c, nbx + g)),
                  pl.BlockSpec((Q, N), lambda g, c: (c, nbx + G + g)),
                  pl.BlockSpec((None, Q, LANES), lambda g, c: (g, c, 0)), par, par, par],
        out_specs=[pl.BlockSpec((Q, HP), lambda g, c: (c, g)),
                   pl.BlockSpec((None, None, N, HP), lambda g, c: (g, c, 0, 0))],
        out_shape=[jax.ShapeDtypeStruct((T, TOK), F32), jax.ShapeDtypeStruct((G, NC, N, HP), F32)],
        scratch_shapes=[pltpu.VMEM((N, HP), F32)],
        compiler_params=_cparams("parallel", "arbitrary"),
    )(act, act, act, raw_g, bias_g, alog_g, dsk_g)


def _ssd_bwd(act, raw_g, bias_g, alog_g, dsk_g, hprev, dy, TOK, name):
    T = act.shape[0]
    G, N, Q, P_ = SSD_GROUPS, SSD_STATE, SSD_CHUNK, SSD_HEAD_DIM
    HP = TOK // G
    HPG = HP // P_
    NC = T // Q

    def body(x_ref, b_ref, c_ref, raw_ref, bias_ref, alog_ref, dsk_ref, hp_ref, dy_ref,
             dx_ref, db_ref, dc_ref, draw_ref, dalog_ref, dbias_ref, ddsk_ref, dHT):
        @pl.when(pl.program_id(1) == 0)
        def _():
            dHT[...] = jnp.zeros_like(dHT)
            dalog_ref[...] = jnp.zeros_like(dalog_ref)
            dbias_ref[...] = jnp.zeros_like(dbias_ref)
            ddsk_ref[...] = jnp.zeros_like(ddsk_ref)

        xs = x_ref[...]
        dyv = dy_ref[...]
        Bm = b_ref[...]
        Cm = c_ref[...]
        Bb = Bm.astype(BF16)
        Cb = Cm.astype(BF16)
        raw_in = raw_ref[...] + bias_ref[...]
        dt, a, A, causal, E = _ssd_common(raw_ref[...], bias_ref[...], alog_ref[...], Q, HP, HPG)
        AT = A.T
        t_r = lax.broadcasted_iota(jnp.int32, (HP, LANES), 0)
        t_c = lax.broadcasted_iota(jnp.int32, (HP, LANES), 1)
        ET = ((t_r >= t_c * P_) & (t_r < (t_c + 1) * P_) & (t_c < HPG)).astype(BF16)
        dt_e = _xdot(dt, E)
        A_e = _xdot(A, E)
        dsk_e = _xdot(jnp.broadcast_to(dsk_ref[...], (Q, LANES)), E)
        rows = lax.broadcasted_iota(jnp.int32, (Q, HP), 0)
        cols = lax.broadcasted_iota(jnp.int32, (Q, HP), 1)
        last = rows == Q - 1
        Al_e = jnp.sum(jnp.where(last, A_e, 0.0), axis=0, keepdims=True)
        eA = jnp.exp(A_e)
        eAl = jnp.exp(Al_e)
        dte = jnp.exp(Al_e - A_e)
        xdt = xs * dt_e
        xdt_b = xdt.astype(BF16)
        CB = _dot_nt(Cb, Bb)
        HT = hp_ref[...]
        HTb = HT.astype(BF16)
        dH = dHT[...]
        dHb = dH.astype(BF16)
        dys = (dyv * eA).astype(BF16)
        CH = _dot(Cb, HTb)
        dC = _dot_nt(dys, HTb)
        dH_prev = _dot(Cm.T.astype(BF16), dys) + eAl * dH
        dAe = dyv * CH * eA
        dAl = eAl * jnp.sum(dH * HT, axis=0, keepdims=True)
        W = _dot(Bb, dHb)
        dxdt = W * dte
        dd = W * xdt * dte
        dB = _dot_nt((xdt * dte).astype(BF16), dHb)
        dAl = dAl + jnp.sum(dd, axis=0, keepdims=True)
        dAe = dAe - dd + jnp.where(last, dAl, 0.0)
        lane = lax.broadcasted_iota(jnp.int32, (Q, LANES), 1)
        sub = lax.broadcasted_iota(jnp.int32, (LANES, Q), 0)
        dCB = jnp.zeros((Q, Q), F32)
        dA_col = jnp.zeros((Q, LANES), F32)
        dA_row = jnp.zeros((LANES, Q), F32)
        for j in range(HPG):
            a_col, a_row = _head_cols(A, AT, j)
            L = jnp.exp(jnp.where(causal, a_col - a_row, NEG))
            hmask = (cols >= j * P_) & (cols < (j + 1) * P_)
            dyj = jnp.where(hmask, dyv, 0.0).astype(BF16)
            GL = _dot_nt(dyj, xdt_b) * L
            dCB = dCB + GL
            dLL = GL * CB
            dA_col = dA_col + jnp.where(lane == j, jnp.sum(dLL, axis=-1, keepdims=True), 0.0)
            dA_row = dA_row + jnp.where(sub == j, jnp.sum(dLL, axis=0, keepdims=True), 0.0)
            dxdt = dxdt + _dot((CB * L).T.astype(BF16), dyj)
        dC = dC + _dot(dCB.astype(BF16), Bb)
        dB = dB + _dot(dCB.T.astype(BF16), Cb)
        dA = dA_col - dA_row.T + _xdot(dAe, ET)
        triu = (lax.broadcasted_iota(jnp.int32, (Q, Q), 1) >= lax.broadcasted_iota(jnp.int32, (Q, Q), 0)).astype(BF16)
        rcs = _xdot_l(triu, dA)
        ddt = a * rcs + _xdot(dxdt * xs, ET)
        draw = ddt * _sigmoid(raw_in)
        draw_ref[...] = draw
        dalog_ref[...] += jnp.sum(dt * rcs, axis=0, keepdims=True) * a
        dbias_ref[...] += jnp.sum(draw, axis=0, keepdims=True)
        ddsk_ref[...] += jnp.sum(_xdot(dyv * xs, ET), axis=0, keepdims=True)
        dx_ref[...] = dxdt * dt_e + dsk_e * dyv
        db_ref[...] = dB
        dc_ref[...] = dC
        dHT[...] = dH_prev

    nbx = TOK // N
    rv = lambda c: NC - 1 - c
    par = pl.BlockSpec((None, 1, LANES), lambda g, c: (g, 0, 0))
    xsp = pl.BlockSpec((Q, HP), lambda g, c: (rv(c), g))
    outs = pl.pallas_call(
        body, name=name, grid=(G, NC),
        in_specs=[xsp,
                  pl.BlockSpec((Q, N), lambda g, c: (rv(c), nbx + g)),
                  pl.BlockSpec((Q, N), lambda g, c: (rv(c), nbx + G + g)),
                  pl.BlockSpec((None, Q, LANES), lambda g, c: (g, rv(c), 0)), par, par, par,
                  pl.BlockSpec((None, None, N, HP), lambda g, c: (g, rv(c), 0, 0)), xsp],
        out_specs=[xsp, pl.BlockSpec((Q, N), lambda g, c: (rv(c), g)), pl.BlockSpec((Q, N), lambda g, c: (rv(c), g)),
                   pl.BlockSpec((None, Q, LANES), lambda g, c: (g, rv(c), 0)), par, par, par],
        out_shape=[jax.ShapeDtypeStruct((T, TOK), F32), jax.ShapeDtypeStruct((T, G * N), F32),
                   jax.ShapeDtypeStruct((T, G * N), F32), jax.ShapeDtypeStruct((G, T, LANES), F32),
                   jax.ShapeDtypeStruct((G, 1, LANES), F32), jax.ShapeDtypeStruct((G, 1, LANES), F32),
                   jax.ShapeDtypeStruct((G, 1, LANES), F32)],
        scratch_shapes=[pltpu.VMEM((N, HP), F32)],
        compiler_params=_cparams("parallel", "arbitrary"),
    )(act, act, act, raw_g, bias_g, alog_g, dsk_g, hprev, dy)
    return outs


def _heads_per_block(H):
    for hb in (8, 6, 4, 3, 2, 1):
        if H % hb == 0:
            return hb
    return 1


def _alibi_slope(head_index, n_alibi):
    c = -ALIBI_MAX_EXP * math.log(2.0) / n_alibi
    return jnp.exp(jnp.full((1, 1), c, F32) * (head_index + 1).astype(F32))


def _attn_masks(b, nb):
    Bq = ATTN_BLOCK
    iq = lax.broadcasted_iota(jnp.int32, (Bq, 2 * Bq), 0)
    jk = lax.broadcasted_iota(jnp.int32, (Bq, 2 * Bq), 1)
    rel = iq + Bq - jk
    mask = (rel >= 0) & (rel <= Bq) & (jk + jnp.where(b > 0, Bq, 0) >= Bq)
    rel_n = lax.broadcasted_iota(jnp.int32, (Bq, Bq), 0) + Bq - lax.broadcasted_iota(jnp.int32, (Bq, Bq), 1)
    mask_n = (rel_n + jnp.where(b < nb - 1, 0, 4 * Bq)) <= Bq
    return rel.astype(F32), mask, rel_n.astype(F32), mask_n


def _rows2(a, b):
    return jnp.concatenate([a, b], axis=0)


ATTN_UNITS_IN_FLIGHT = 4


def _attn_units_fwd(load, n_units, slopes, masks, scale, store):
    rel_f, mask, _, _ = masks
    for g0 in range(0, n_units, ATTN_UNITS_IN_FLIGHT):
        ids = range(g0, min(g0 + ATTN_UNITS_IN_FLIGHT, n_units))
        units = [load(i) for i in ids]
        raw = [_dot_nt(q, _rows2(kp, kc)) for q, kc, kp, vc, vp in units]
        soft = []
        for i, s_raw in zip(ids, raw):
            s = jnp.where(mask, s_raw * scale - slopes[i] * rel_f, NEG)
            m = jnp.max(s, axis=-1, keepdims=True)
            p = jnp.exp(s - m)
            den = jnp.sum(p, axis=-1, keepdims=True)
            soft.append((p.astype(BF16), den, m + jnp.log(den)))
        for i, (p, den, lse), (q, kc, kp, vc, vp) in zip(ids, soft, units):
            store(i, _dot(p, _rows2(vp, vc)) / den, lse)


def _attn_units_bwd(load, n_units, slopes, masks, scale, store):
    rel_f, mask, reln_f, mask_n = masks
    Bq = ATTN_BLOCK
    for g0 in range(0, n_units, ATTN_UNITS_IN_FLIGHT):
        ids = range(g0, min(g0 + ATTN_UNITS_IN_FLIGHT, n_units))
        units = [load(i) for i in ids]
        prods = []
        for q0, q1, kp, k0, vp, v0, do0, do1, y0, y1, lse0, lse1 in units:
            kcat = _rows2(kp, k0)
            do0b = do0.astype(BF16)
            do1b = do1.astype(BF16)
            prods.append((kcat, do0b, do1b, _dot_nt(q0, kcat), _dot_nt(do0b, _rows2(vp, v0)), _dot_nt(q1, k0),
                          _dot_nt(do1b, v0)))
        mids = []
        for i, u, (kcat, do0b, do1b, s_raw, dp_raw, sn_raw, dpn_raw) in zip(ids, units, prods):
            q0, q1, kp, k0, vp, v0, do0, do1, y0, y1, lse0, lse1 = u
            delta0 = jnp.sum(do0 * y0, axis=-1, keepdims=True)
            delta1 = jnp.sum(do1 * y1, axis=-1, keepdims=True)
            p = jnp.exp(jnp.where(mask, s_raw * scale - slopes[i] * rel_f, NEG) - lse0)
            ds = p * (dp_raw - delta0)
            p_n = jnp.exp(jnp.where(mask_n, sn_raw * scale - slopes[i] * reln_f, NEG) - lse1)
            ds_n = p_n * (dpn_raw - delta1)
            mids.append((ds.astype(BF16), _rows2(ds[:, Bq:], ds_n).T.astype(BF16),
                         _rows2(p[:, Bq:], p_n).T.astype(BF16)))
        for i, u, (kcat, do0b, do1b, *_), (dsb, dsk_t, pk_t) in zip(ids, units, prods, mids):
            store(i, scale * _dot(dsb, kcat), scale * _dot(dsk_t, _rows2(u[0], u[1])), _dot(pk_t, _rows2(do0b, do1b)))


def _attn_fwd_strided(q, k, v, gi, d, name):
    T, TOK = q.shape
    E_ = ATTN_HEAD_DIM
    H = TOK // E_
    n_alibi = len(DILATED_GROUPS) * H
    Bq = ATTN_BLOCK
    RB = Bq * d
    nb = T // RB
    scale = E_ ** -0.5

    def body(q_ref, kc_ref, kp_ref, vc_ref, vp_ref, o_ref, l_ref):
        masks = _attn_masks(pl.program_id(1), nb)
        slope = _alibi_slope(gi * H + pl.program_id(0), n_alibi) * float(d)
        rows = lambda r: pl.ds(r, Bq, stride=d)

        def load(r):
            return tuple(ref[rows(r), :].astype(BF16) for ref in (q_ref, kc_ref, kp_ref, vc_ref, vp_ref))

        def store(r, o, lse):
            o_ref[rows(r), :] = o
            l_ref[rows(r), :] = jnp.broadcast_to(lse, (Bq, E_))

        _attn_units_fwd(load, d, [slope] * d, masks, scale, store)

    cur = pl.BlockSpec((RB, E_), lambda h, b: (b, h))
    prev = pl.BlockSpec((RB, E_), lambda h, b: (jnp.maximum(b - 1, 0), h))
    sds = jax.ShapeDtypeStruct((T, TOK), F32)
    return pl.pallas_call(
        body, name=name, grid=(H, nb), in_specs=[cur, cur, prev, cur, prev], out_specs=[cur, cur],
        out_shape=[sds, sds], compiler_params=_cparams("parallel", "parallel"),
    )(q, k, k, v, v)


def _attn_bwd_strided(q, k, v, y, lse, dy, gi, d, name):
    T, TOK = q.shape
    E_ = ATTN_HEAD_DIM
    H = TOK // E_
    n_alibi = len(DILATED_GROUPS) * H
    Bq = ATTN_BLOCK
    RB = Bq * d
    nb = T // RB
    scale = E_ ** -0.5

    def body(q0_ref, q1_ref, kp_ref, k0_ref, vp_ref, v0_ref, do0_ref, do1_ref, y0_ref, y1_ref, l0_ref, l1_ref,
             dq_ref, dk_ref, dv_ref):
        masks = _attn_masks(pl.program_id(1), nb)
        slope = _alibi_slope(gi * H + pl.program_id(0), n_alibi) * float(d)
        rows = lambda r: pl.ds(r, Bq, stride=d)

        def load(r):
            return (tuple(ref[rows(r), :].astype(BF16) for ref in (q0_ref, q1_ref, kp_ref, k0_ref, vp_ref, v0_ref))
                    + tuple(ref[rows(r), :] for ref in (do0_ref, do1_ref, y0_ref, y1_ref))
                    + tuple(jnp.max(ref[rows(r), :], axis=-1, keepdims=True) for ref in (l0_ref, l1_ref)))

        def store(r, dq, dk, dv):
            dq_ref[rows(r), :] = dq
            dk_ref[rows(r), :] = dk
            dv_ref[rows(r), :] = dv

        _attn_units_bwd(load, d, [slope] * d, masks, scale, store)

    cur = pl.BlockSpec((RB, E_), lambda h, b: (b, h))
    prev = pl.BlockSpec((RB, E_), lambda h, b: (jnp.maximum(b - 1, 0), h))
    nxt = pl.BlockSpec((RB, E_), lambda h, b: (jnp.minimum(b + 1, nb - 1), h))
    sds = jax.ShapeDtypeStruct((T, TOK), F32)
    return pl.pallas_call(
        body, name=name, grid=(H, nb),
        in_specs=[cur, nxt, prev, cur, prev, cur, cur, nxt, cur, nxt, cur, nxt],
        out_specs=[cur, cur, cur], out_shape=[sds, sds, sds],
        compiler_params=_cparams("parallel", "parallel"),
    )(q, q, k, k, v, v, dy, dy, y, y, lse, lse)


def _attn_fwd(q, k, v, gi, window, d, name):
    T, TOK = q.shape
    E_ = ATTN_HEAD_DIM
    H = TOK // E_
    n_alibi = len(DILATED_GROUPS) * H
    assert window // d == ATTN_BLOCK and (T // d) % ATTN_BLOCK == 0
    if d > 1:
        return _attn_fwd_strided(q, k, v, gi, d, name)
    n_sub = T // d
    nb = n_sub // ATTN_BLOCK
    HB = _heads_per_block(H)
    NHB = H // HB
    hbw = HB * E_
    scale = E_ ** -0.5
    Bq = ATTN_BLOCK

    def body(q_ref, kc_ref, kp_ref, vc_ref, vp_ref, o_ref, l_ref):
        hb = pl.program_id(1)
        masks = _attn_masks(pl.program_id(2), nb)
        slopes = [_alibi_slope(gi * H + hb * HB + hh, n_alibi) * float(d) for hh in range(HB)]
        cols = lambda hh: slice(hh * E_, (hh + 1) * E_)

        def load(hh):
            return tuple(ref[:, cols(hh)].astype(BF16) for ref in (q_ref, kc_ref, kp_ref, vc_ref, vp_ref))

        def store(hh, o, lse):
            o_ref[:, cols(hh)] = o
            l_ref[:, cols(hh)] = jnp.broadcast_to(lse, (Bq, E_))

        _attn_units_fwd(load, HB, slopes, masks, scale, store)

    cur = pl.BlockSpec((Bq, hbw), lambda r, h, b: (b, r * NHB + h))
    prev = pl.BlockSpec((Bq, hbw), lambda r, h, b: (jnp.maximum(b - 1, 0), r * NHB + h))
    view = lambda t: t.reshape(n_sub, d * TOK)
    sds = jax.ShapeDtypeStruct((n_sub, d * TOK), F32)
    o, l = pl.pallas_call(
        body, name=name, grid=(d, NHB, nb), in_specs=[cur, cur, prev, cur, prev], out_specs=[cur, cur],
        out_shape=[sds, sds], compiler_params=_cparams("parallel", "parallel", "parallel"),
    )(view(q), view(k), view(k), view(v), view(v))
    return o.reshape(T, TOK), l.reshape(T, TOK)


def _attn_combine(os_, ls_, name):
    def fn(*v):
        n = len(v) // 2
        o, l = v[:n], v[n:]
        m = l[0]
        for t in l[1:]:
            m = jnp.maximum(m, t)
        e = [jnp.exp(t - m) for t in l]
        den = e[0]
        for t in e[1:]:
            den = den + t
        y = e[0] * o[0]
        for t, u in zip(e[1:], o[1:]):
            y = y + t * u
        return y / den, m + jnp.log(den)
    return _elementwise(fn, list(os_) + list(ls_), [], [F32, F32], name)


def _attn_bwd(q, k, v, y, lse, dy, gi, window, d, name):
    T, TOK = q.shape
    E_ = ATTN_HEAD_DIM
    H = TOK // E_
    n_alibi = len(DILATED_GROUPS) * H
    if d > 1:
        return _attn_bwd_strided(q, k, v, y, lse, dy, gi, d, name)
    n_sub = T // d
    nb = n_sub // ATTN_BLOCK
    HB = _heads_per_block(H)
    NHB = H // HB
    hbw = HB * E_
    scale = E_ ** -0.5
    Bq = ATTN_BLOCK

    def body(q0_ref, q1_ref, kp_ref, k0_ref, vp_ref, v0_ref, do0_ref, do1_ref, y0_ref, y1_ref, l0_ref, l1_ref,
             dq_ref, dk_ref, dv_ref):
        hb = pl.program_id(1)
        masks = _attn_masks(pl.program_id(2), nb)
        slopes = [_alibi_slope(gi * H + hb * HB + hh, n_alibi) * float(d) for hh in range(HB)]
        cols = lambda hh: slice(hh * E_, (hh + 1) * E_)

        def load(hh):
            return (tuple(ref[:, cols(hh)].astype(BF16) for ref in (q0_ref, q1_ref, kp_ref, k0_ref, vp_ref, v0_ref))
                    + tuple(ref[:, cols(hh)] for ref in (do0_ref, do1_ref, y0_ref, y1_ref))
                    + tuple(jnp.max(ref[:, cols(hh)], axis=-1, keepdims=True) for ref in (l0_ref, l1_ref)))

        def store(hh, dq, dk, dv):
            dq_ref[:, cols(hh)] = dq.astype(BF16)
            dk_ref[:, cols(hh)] = dk.astype(BF16)
            dv_ref[:, cols(hh)] = dv.astype(BF16)

        _attn_units_bwd(load, HB, slopes, masks, scale, store)

    cur = pl.BlockSpec((Bq, hbw), lambda r, h, b: (b, r * NHB + h))
    prev = pl.BlockSpec((Bq, hbw), lambda r, h, b: (jnp.maximum(b - 1, 0), r * NHB + h))
    nxt = pl.BlockSpec((Bq, hbw), lambda r, h, b: (jnp.minimum(b + 1, nb - 1), r * NHB + h))
    view = lambda t: t.reshape(n_sub, d * TOK)
    sds = jax.ShapeDtypeStruct((n_sub, d * TOK), BF16)
    dq, dk, dv = pl.pallas_call(
        body, name=name, grid=(d, NHB, nb),
        in_specs=[cur, nxt, prev, cur, prev, cur, cur, nxt, cur, nxt, cur, nxt],
        out_specs=[cur, cur, cur], out_shape=[sds, sds, sds],
        compiler_params=_cparams("parallel", "parallel", "parallel"),
    )(view(q), view(q), view(k), view(k), view(v), view(v), view(dy), view(dy), view(y), view(y), view(lse), view(lse))
    return dq.reshape(T, TOK), dk.reshape(T, TOK), dv.reshape(T, TOK)


_FLIPS = {
    "xy": [(1, 0, 0), (0, 1, 0), (1, 1, 0)],
    "c": [(0, 0, 1)],
    "xyc": [(dx, dy, dc) for dx in (0, 1) for dy in (0, 1) for dc in (0, 1) if (dx, dy, dc) != (0, 0, 0)],
}


def _comm_parts(group, srcs, modes, handshake):
    flips = _FLIPS[group]
    F_ = len(flips)
    P_ = F_ + 1
    n = len(srcs)

    def gidx(px, py, pc):
        if group == "xy":
            return 2 * px + py
        if group == "c":
            return pc
        return 4 * px + 2 * py + pc

    def body(*refs):
        src_refs, out_refs = refs[:n], refs[n:2 * n]
        send_sems, recv_sems, loc_sems = refs[2 * n:]
        x, y, c = lax.axis_index("x"), lax.axis_index("y"), lax.axis_index("c")
        me = gidx(x, y, c)
        peers = [(1 - x if dx else x, 1 - y if dy else y, 1 - c if dc else c) for dx, dy, dc in flips]
        if handshake:
            barrier = pltpu.get_barrier_semaphore()
            for peer in peers:
                pl.semaphore_signal(barrier, inc=1, device_id=peer, device_id_type=MESH)
            pl.semaphore_wait(barrier, F_)
        local, remote = [], []
        for i in range(n):
            mode = modes[i]
            if mode != "swap":
                mine = pltpu.make_async_copy(src_refs[i] if mode == "gather" else src_refs[i].at[me],
                                             out_refs[i].at[me], loc_sems.at[i])
                mine.start()
                local.append(mine)
            for f, peer in enumerate(peers):
                cp = pltpu.make_async_remote_copy(
                    src_ref=src_refs[i].at[gidx(*peer)] if mode == "a2a" else src_refs[i],
                    dst_ref=out_refs[i] if mode == "swap" else out_refs[i].at[me],
                    send_sem=send_sems.at[i * F_ + f], recv_sem=recv_sems.at[i * F_ + f],
                    device_id=peer, device_id_type=MESH)
                cp.start()
                remote.append(cp)
        for cp in local:
            cp.wait()
        for cp in remote:
            cp.wait()

    out_shape = []
    for s, mode in zip(srcs, modes):
        assert mode != "swap" or F_ == 1
        shp = (P_,) + tuple(s.shape) if mode == "gather" else tuple(s.shape)
        out_shape.append(jax.ShapeDtypeStruct(shp, s.dtype))
    sems = [pltpu.SemaphoreType.DMA((n * F_,)), pltpu.SemaphoreType.DMA((n * F_,)), pltpu.SemaphoreType.DMA((n,))]
    return body, out_shape, sems


def _comm(name, group, srcs, modes):
    n = len(srcs)
    body, out_shape, sems = _comm_parts(group, srcs, modes, handshake=False)
    anyspec = pl.BlockSpec(memory_space=pl.ANY)
    return pl.pallas_call(body, name=name, in_specs=[anyspec] * n, out_specs=[anyspec] * n, out_shape=out_shape,
                          scratch_shapes=sems)(*srcs)


def _comm_async(name, collective_id, group, srcs, modes):
    body, out_shape, sems = _comm_parts(group, srcs, modes, handshake=True)
    return pl.kernel(body, name=name, out_type=out_shape,
                     mesh=plsc.ScalarSubcoreMesh(axis_name="sequencer", num_cores=1), scratch_types=sems,
                     compiler_params=pltpu.CompilerParams(collective_id=collective_id))(*srcs)


def _dims(D):
    MIX = 2 * D
    MW = MIX // 4
    TOK = MIX - MW
    H = TOK // SSD_HEAD_DIM
    CONV = TOK + 2 * SSD_GROUPS * SSD_STATE
    return dict(MIX=MIX, MW=MW, TOK=TOK, H=H, CONV=CONV)


def _proj_chain(dsegs, wsegs, name):
    acc = None
    for n, (ds, ws) in enumerate(zip(dsegs, wsegs)):
        acc = _mm(ds, ws, "nt", F32, f"{name}_dh{n}", add=acc)
    return acc


def _pad_lanes(a, width=LANES):
    return jnp.pad(a, [(0, 0)] * (a.ndim - 1) + [(0, width - a.shape[-1])])


def _take_cols(parts, a, b):
    out, o = [], 0
    for part in parts:
        w = part.shape[1]
        lo, hi = max(a, o), min(b, o + w)
        if lo < hi:
            out.append(part[:, lo - o:hi - o])
        o += w
    return out[0] if len(out) == 1 else jnp.concatenate(out, axis=1)


def _heads_to_groups(a, G, HPG):
    return jnp.stack([_pad_lanes(a[:, g * HPG:(g + 1) * HPG]) for g in range(G)])


def _groups_to_heads(a, HPG):
    return jnp.concatenate([a[g, :, :HPG] for g in range(a.shape[0])], axis=1)


def _ssd_layer_fwd(x, kv, p, li):
    T, D = x.shape
    dm = _dims(D)
    TOK, MW, H, CONV = dm["TOK"], dm["MW"], dm["H"], dm["CONV"]
    G = SSD_GROUPS
    HPG = H // G
    w = p["w_in"]
    cuts = [0, CONV, CONV + H, CONV + H + MW, CONV + H + MW + TOK, CONV + H + MW + TOK + MW]
    segs = {k: _take_cols(w, cuts[n], cuts[n + 1]) for n, k in enumerate(["xbc", "dt", "qm", "zt", "zm"])}
    segs["dt"] = _pad_lanes(segs["dt"])
    h = _rms_fwd(x, p["norm_g"], f"l{li}_rms")
    pr = {k: _mm(h, ws, "nn", BF16 if k == "qm" else F32, f"l{li}_in_{k}") for k, ws in segs.items()}
    act = _conv_fwd(pr["xbc"], p["conv_w"], p["conv_b"], f"l{li}_conv")
    raw_g = _heads_to_groups(pr["dt"][:, :H], G, HPG)
    hp = lambda a: _pad_lanes(a.reshape(G, 1, HPG))
    bias_g, alog_g, dsk_g = hp(p["dt_bias"]), hp(p["a_log"]), hp(p["d_skip"])
    y, hprev = _ssd_fwd(act, raw_g, bias_g, alog_g, dsk_g, TOK, f"l{li}_ssd")
    ymem = _memattn_fwd(pr["qm"], kv, f"l{li}_mem")
    gt = _gate_norm_fwd(y, pr["zt"], p["ssd_norm_g"], G, f"l{li}_gate_tok")
    gm = _gate_fwd(ymem, pr["zm"], f"l{li}_gate_mem")
    wo = p["w_out"]
    out = _mm(gt, wo[:TOK], "nn", F32, f"l{li}_out_tok", add=x)
    out = _mm(gm, wo[TOK:], "nn", F32, f"l{li}_out_mem", add=out)
    saved = dict(x=x, h=h, pr=pr, act=act, raw_g=raw_g, par=(bias_g, alog_g, dsk_g), y=y, hprev=hprev, ymem=ymem,
                 gt=gt, gm=gm, segs=segs)
    return out, saved


def _ssd_layer_bwd(dout, doutb, kv, p, s, li):
    x = s["x"]
    T, D = x.shape
    dm = _dims(D)
    TOK, MW, H, CONV = dm["TOK"], dm["MW"], dm["H"], dm["CONV"]
    G = SSD_GROUPS
    HPG = H // G
    wo = p["w_out"]
    pr = s["pr"]
    dgt = _mm(doutb, wo[:TOK], "nt", F32, f"l{li}_dgt")
    dgm = _mm(doutb, wo[TOK:], "nt", F32, f"l{li}_dgm")
    dwo = jnp.concatenate([_mm(s["gt"], doutb, "tn", BF16, f"l{li}_dwo_tok"),
                           _mm(s["gm"], doutb, "tn", BF16, f"l{li}_dwo_mem")], axis=0)
    dy, dzt, dng = _gate_norm_bwd(s["y"], pr["zt"], p["ssd_norm_g"], dgt, G, f"l{li}_gate_tok_b")
    dymem, dzm = _gate_bwd(s["ymem"], pr["zm"], dgm, f"l{li}_gate_mem_b")
    dqm, dkv = _memattn_bwd(pr["qm"], kv, dymem, f"l{li}_mem_b")
    bias_g, alog_g, dsk_g = s["par"]
    dxs, dB, dC, draw_g, dalog, dbias, ddsk = _ssd_bwd(s["act"], s["raw_g"], bias_g, alog_g, dsk_g, s["hprev"], dy, TOK,
                                                      f"l{li}_ssd_b")
    dact = jnp.concatenate([dxs, dB, dC], axis=1)
    dpre, dconv_w, dconv_b = _conv_bwd_pre(pr["xbc"], p["conv_w"], p["conv_b"], dact, f"l{li}_conv_b1")
    dxbc = _conv_bwd_in(dpre, p["conv_w"], f"l{li}_conv_b2")
    draw = _pad_lanes(_groups_to_heads(draw_g, HPG)).astype(BF16)
    dsegs = dict(xbc=dxbc, dt=draw, qm=dqm, zt=dzt, zm=dzm)
    keys = ["xbc", "dt", "qm", "zt", "zm"]
    dh = _proj_chain([dsegs[k] for k in keys], [s["segs"][k] for k in keys], f"l{li}")
    dws = {k: _mm(s["h"], dsegs[k], "tn", BF16, f"l{li}_dwin_{k}") for k in keys}
    dws["dt"] = dws["dt"][:, :H]
    dwin = [dws[k] for k in keys]
    dh, dwin, dwo = lax.optimization_barrier((dh, dwin, dwo))
    dx, dxb, dnorm = _rms_bwd(x, p["norm_g"], dh, dout, f"l{li}_rms_b")
    unhead = lambda a: a[:, 0, :HPG].reshape(H)
    grads = dict(norm_g=dnorm, w_in=dwin, conv_w=dconv_w, conv_b=dconv_b, dt_bias=unhead(dbias), a_log=unhead(dalog),
                 d_skip=unhead(ddsk), ssd_norm_g=dng, w_out=dwo)
    return dx, dxb, dkv, grads


def _attn_layer_fwd(x, kv, p, li):
    T, D = x.shape
    dm = _dims(D)
    TOK, MW = dm["TOK"], dm["MW"]
    w = p["w_in"]
    ng = len(DILATED_GROUPS)
    segs = {}
    for g in range(ng):
        for n, nm in enumerate("qkv"):
            c0 = g * 3 * TOK + n * TOK
            segs[f"{nm}{g}"] = _take_cols(w, c0, c0 + TOK)
    c0 = ng * 3 * TOK
    segs["qm"] = _take_cols(w, c0, c0 + MW)
    segs["zt"] = _take_cols(w, c0 + MW, c0 + MW + TOK)
    segs["zm"] = _take_cols(w, c0 + MW + TOK, c0 + MW + TOK + MW)
    h = _rms_fwd(x, p["norm_g"], f"l{li}_rms")
    dense = {"qm"} | {f"{nm}{g}" for g, (_, d) in enumerate(DILATED_GROUPS) if d == 1 for nm in "qkv"}
    pr = {k: _mm(h, ws, "nn", BF16 if k in dense else F32, f"l{li}_in_{k}") for k, ws in segs.items()}
    os_, ls_ = [], []
    for g, (window, d) in enumerate(DILATED_GROUPS):
        o, l = _attn_fwd(pr[f"q{g}"], pr[f"k{g}"], pr[f"v{g}"], g, window, d, f"l{li}_attn{g}")
        os_.append(o)
        ls_.append(l)
    ytok, lse = _attn_combine(os_, ls_, f"l{li}_combine")
    ymem = _memattn_fwd(pr["qm"], kv, f"l{li}_mem")
    gt = _gate_fwd(ytok, pr["zt"], f"l{li}_gate_tok")
    gm = _gate_fwd(ymem, pr["zm"], f"l{li}_gate_mem")
    wo = p["w_out"]
    out = _mm(gt, wo[:TOK], "nn", F32, f"l{li}_out_tok", add=x)
    out = _mm(gm, wo[TOK:], "nn", F32, f"l{li}_out_mem", add=out)
    saved = dict(x=x, h=h, pr=pr, ytok=ytok, lse=lse, ymem=ymem, gt=gt, gm=gm, segs=segs)
    return out, saved


def _attn_layer_bwd(dout, doutb, kv, p, s, li):
    x = s["x"]
    T, D = x.shape
    dm = _dims(D)
    TOK, MW = dm["TOK"], dm["MW"]
    wo = p["w_out"]
    pr = s["pr"]
    dgt = _mm(doutb, wo[:TOK], "nt", F32, f"l{li}_dgt")
    dgm = _mm(doutb, wo[TOK:], "nt", F32, f"l{li}_dgm")
    dwo = jnp.concatenate([_mm(s["gt"], doutb, "tn", BF16, f"l{li}_dwo_tok"),
                           _mm(s["gm"], doutb, "tn", BF16, f"l{li}_dwo_mem")], axis=0)
    dytok, dzt = _gate_bwd(s["ytok"], pr["zt"], dgt, f"l{li}_gate_tok_b")
    dymem, dzm = _gate_bwd(s["ymem"], pr["zm"], dgm, f"l{li}_gate_mem_b")
    dqm, dkv = _memattn_bwd(pr["qm"], kv, dymem, f"l{li}_mem_b")
    dsegs = {}
    for g, (window, d) in enumerate(DILATED_GROUPS):
        dq, dk, dv = _attn_bwd(pr[f"q{g}"], pr[f"k{g}"], pr[f"v{g}"], s["ytok"], s["lse"], dytok, g, window, d,
                               f"l{li}_attn{g}_b")
        dsegs[f"q{g}"], dsegs[f"k{g}"], dsegs[f"v{g}"] = dq, dk, dv
    dsegs["qm"], dsegs["zt"], dsegs["zm"] = dqm, dzt, dzm
    keys = list(s["segs"].keys())
    dh = _proj_chain([dsegs[k] for k in keys], [s["segs"][k] for k in keys], f"l{li}")
    dwin = [_mm(s["h"], dsegs[k], "tn", BF16, f"l{li}_dwin_{k}") for k in keys]
    dh, dwin, dwo = lax.optimization_barrier((dh, dwin, dwo))
    dx, dxb, dnorm = _rms_bwd(x, p["norm_g"], dh, dout, f"l{li}_rms_b")
    return dx, dxb, dkv, dict(norm_g=dnorm, w_in=dwin, w_out=dwo)


def _local_step(x, mem, mem_n, tgt, mem_norm_g, final_norm_g, n_layers, layer_params, on_layer_grads):
    layers, kvs, saved = [], [], []
    for li in range(n_layers):
        p, x = layer_params(li, x)
        fwd = _ssd_layer_fwd if li % 2 == 0 else _attn_layer_fwd
        kv = _mm(mem_n, p["w_mem_kv"], "nn", F32, f"l{li}_kv")
        x, s = fwd(x, kv, p, li)
        layers.append(p)
        kvs.append(kv)
        saved.append(s)
    loss, dx, dxb, dfinal = _loss_head(x, final_norm_g, tgt, "loss_head")
    dmem_n = None
    for li in reversed(range(n_layers)):
        p = layers[li]
        bwd = _ssd_layer_bwd if li % 2 == 0 else _attn_layer_bwd
        dx, dxb, dkv, g = bwd(dx, dxb, kvs[li], p, saved[li], li)
        g["w_mem_kv"] = _mm(mem_n, dkv, "tn", BF16, f"l{li}_dwkv")
        dmem_n = _mm(dkv, p["w_mem_kv"], "nt", F32, f"l{li}_dmem", add=dmem_n)
        dx, dxb = on_layer_grads(li, g, dx, dxb)
    _, _, dmem_g = _rms_bwd(mem, mem_norm_g, dmem_n, None, "mem_rms_b")
    return loss, dx, dmem_g, dfinal


_SSD_SMALL = ["norm_g", "conv_w", "conv_b", "dt_bias", "a_log", "d_skip", "ssd_norm_g"]
_ATTN_SMALL = ["norm_g"]
_SSD_ORDER = ["norm_g", "w_in", "conv_w", "conv_b", "dt_bias", "a_log", "d_skip", "ssd_norm_g", "w_mem_kv", "w_out"]
_ATTN_ORDER = ["norm_g", "w_in", "w_mem_kv", "w_out"]


def _pack(arrs):
    flat = jnp.concatenate([a.reshape(-1).astype(F32) for a in arrs])
    n = flat.shape[0]
    pad = (-n) % (8 * LANES)
    return jnp.pad(flat, (0, pad)).reshape(-1, LANES)


def _unpack(mat, shapes):
    flat = mat.reshape(-1)
    out, o = [], 0
    for shp in shapes:
        n = math.prod(shp)
        out.append(flat[o:o + n].reshape(shp))
        o += n
    return out


def kernel(x, mem, mem_norm_g, final_norm_g, norm_g_0, w_in_0, conv_w_0, conv_b_0, dt_bias_0, a_log_0, d_skip_0, ssd_norm_g_0, w_mem_kv_0, w_out_0, norm_g_1, w_in_1, w_mem_kv_1, w_out_1, norm_g_2, w_in_2, conv_w_2, conv_b_2, dt_bias_2, a_log_2, d_skip_2, ssd_norm_g_2, w_mem_kv_2, w_out_2, norm_g_3, w_in_3, w_mem_kv_3, w_out_3, loss_target, m_mem_norm_g, m_final_norm_g, m_norm_g_0, m_w_in_0, m_conv_w_0, m_conv_b_0, m_dt_bias_0, m_a_log_0, m_d_skip_0, m_ssd_norm_g_0, m_w_mem_kv_0, m_w_out_0, m_norm_g_1, m_w_in_1, m_w_mem_kv_1, m_w_out_1, m_norm_g_2, m_w_in_2, m_conv_w_2, m_conv_b_2, m_dt_bias_2, m_a_log_2, m_d_skip_2, m_ssd_norm_g_2, m_w_mem_kv_2, m_w_out_2, m_norm_g_3, m_w_in_3, m_w_mem_kv_3, m_w_out_3, v_mem_norm_g, v_final_norm_g, v_norm_g_0, v_w_in_0, v_conv_w_0, v_conv_b_0, v_dt_bias_0, v_a_log_0, v_d_skip_0, v_ssd_norm_g_0, v_w_mem_kv_0, v_w_out_0, v_norm_g_1, v_w_in_1, v_w_mem_kv_1, v_w_out_1, v_norm_g_2, v_w_in_2, v_conv_w_2, v_conv_b_2, v_dt_bias_2, v_a_log_2, v_d_skip_2, v_ssd_norm_g_2, v_w_mem_kv_2, v_w_out_2, v_norm_g_3, v_w_in_3, v_w_mem_kv_3, v_w_out_3):
    a = dict(locals())
    names = ["mem_norm_g", "final_norm_g"]
    for li in range(DEPTH):
        names += [f"{k}_{li}" for k in (_SSD_ORDER if li % 2 == 0 else _ATTN_ORDER)]
    W = {n: a[n] for n in names}
    Mo = {n: a["m_" + n] for n in names}
    Vo = {n: a["v_" + n] for n in names}
    NX = 4
    chip = 2 * lax.axis_index("x") + lax.axis_index("y")

    gathered = []
    mem_n = None
    for li in range(DEPTH):
        mats = [W[f"w_in_{li}"], W[f"w_mem_kv_{li}"], W[f"w_out_{li}"]]
        if li == 1:
            mem_n = _rms_fwd(mem[0], W["mem_norm_g"], "mem_rms")
            later = [[W[f"{k}_{lj}"] for k in ("w_in", "w_mem_kv", "w_out")] for lj in range(1, DEPTH)]
            later, mem_n = lax.optimization_barrier((later, mem_n))
        if li >= 1:
            mats = later[li - 1]
        srcs = [m.astype(BF16) for m in mats]
        if li % 2 == 0:
            srcs.append(W[f"conv_w_{li}"])
        gathered.append(_comm_async(f"gather_w{li}", li, "xy", srcs, ["gather"] * len(srcs)))

    def layer_params(li, xin):
        got = gathered[li]
        if li > 0:
            got, xin = lax.optimization_barrier((got, xin))
        rows = lambda g: g.reshape((-1,) + g.shape[2:])
        parts = lambda g: [g[k] for k in range(NX)]
        p = dict(w_in=parts(got[0]), w_mem_kv=rows(got[1]), w_out=rows(got[2]), norm_g=W[f"norm_g_{li}"])
        if li % 2 == 0:
            p.update(conv_w=jnp.concatenate(parts(got[3]), axis=1), conv_b=W[f"conv_b_{li}"],
                     dt_bias=W[f"dt_bias_{li}"], a_log=W[f"a_log_{li}"], d_skip=W[f"d_skip_{li}"],
                     ssd_norm_g=W[f"ssd_norm_g_{li}"])
        return p, xin

    G, Dl, Mn, Vn = {}, {}, {}, {}
    grads = [None] * DEPTH
    in_flight = []

    def finish_exchange(li, got):
        parts = [_sum_lead(t, f"l{li}_gsum{n}") for n, t in enumerate(got)]
        theirs = _comm_async(f"swap_g{li}", 2 * DEPTH + li, "c", parts, ["swap"] * 3)
        for nm, mine, other in zip(["w_in", "w_mem_kv", "w_out"], parts, theirs):
            key = f"{nm}_{li}"
            G[key], Dl[key], Mn[key], Vn[key] = _adamw(W[key], Mo[key], Vo[key], [mine, other], f"adamw_{key}")

    def on_layer_grads(li, g, dx, dxb):
        grads[li] = g
        dwin = g["w_in"]
        cw = sum(t.shape[1] for t in dwin) // NX
        chunks = [jnp.stack([_take_cols(dwin, k * cw, (k + 1) * cw) for k in range(NX)]),
                  g["w_mem_kv"].reshape((NX, -1) + g["w_mem_kv"].shape[1:]),
                  g["w_out"].reshape((NX, -1) + g["w_out"].shape[1:])]
        prev = in_flight.pop() if in_flight else None
        pgot = prev[1] if prev else []
        chunks, pgot, dx, dxb = lax.optimization_barrier((chunks, pgot, dx, dxb))
        in_flight.append((li, _comm_async(f"xchg_g{li}", DEPTH + li, "xy", chunks, ["a2a"] * 3)))
        if prev:
            finish_exchange(prev[0], pgot)
        return dx, dxb

    loss_l, dx, dmem_g, dfinal = _local_step(x[0], mem[0], mem_n, loss_target[0], W["mem_norm_g"], W["final_norm_g"],
                                             DEPTH, layer_params, on_layer_grads)
    last_li, last_got = in_flight.pop()
    updates = (G, Dl, Mn, Vn)
    last_got, tied = lax.optimization_barrier((last_got, [dict(u) for u in updates]))
    for u, t in zip(updates, tied):
        u.update(t)
    finish_exchange(last_li, last_got)
    loss = lax.psum(loss_l, ("x", "y", "c"))

    small_names = ["mem_norm_g", "final_norm_g"]
    small_grads = [dmem_g, dfinal]
    for li in range(DEPTH):
        for k in (_SSD_SMALL if li % 2 == 0 else _ATTN_SMALL):
            small_names.append(f"{k}_{li}")
            small_grads.append(grads[li][k])
    shapes = [tuple(t.shape) for t in small_grads]
    allg = _comm("gather_small", "xyc", [_pack(small_grads)], ["gather"])[0]
    gsum = _unpack(_sum_lead(allg, "small_gsum"), shapes)
    small_w, small_m, small_v, small_g = [], [], [], []
    for nme, gv in zip(small_names, gsum):
        if nme.startswith("conv_w"):
            cw = W[nme].shape[1]
            gv = lax.dynamic_slice_in_dim(gv, chip * cw, cw, axis=1)
        small_g.append(gv)
        small_w.append(W[nme])
        small_m.append(Mo[nme])
        small_v.append(Vo[nme])
    sshapes = [tuple(t.shape) for t in small_g]
    res = _adamw(_pack(small_w), _pack(small_m), _pack(small_v), [_pack(small_g)], "adamw_small")
    for dst, mat in zip((G, Dl, Mn, Vn), res):
        for nme, t in zip(small_names, _unpack(mat, sshapes)):
            dst[nme] = t

    return (loss, dx[None], *[G[n] for n in names], *[Dl[n] for n in names], *[Mn[n] for n in names],
            *[Vn[n] for n in names])
```

```python
import functools
import math

import jax
import jax.numpy as jnp
from jax import lax
from jax.experimental import pallas as pl
from jax.experimental.pallas import tpu as pltpu
from jax.experimental.pallas import tpu_sc as plsc

F32 = jnp.float32
BF16 = jnp.bfloat16

EPS = 1e-6
MEM_HEADS = 4
SSD_HEAD_DIM = 64
SSD_GROUPS = 8
SSD_STATE = 128
SSD_CONV = 4
SSD_CHUNK = 128
ATTN_HEAD_DIM = 128
ATTN_BLOCK = 128
DILATED_GROUPS = ((128, 1), (512, 4), (2048, 16))
ALIBI_MAX_EXP = 8.0
DEPTH = 4

ADAM_LR = 0.001
ADAM_B1 = 0.9
ADAM_B2 = 0.999
ADAM_EPS = 1e-08
ADAM_WD = 0.01
ADAM_STEP = 10

LANES = 128
VMEM_LIMIT_BYTES = 48 * 1024 * 1024
NEG = -1e30
MESH = pl.DeviceIdType.MESH


def _cparams(*sem):
    return pltpu.CompilerParams(dimension_semantics=tuple(sem), vmem_limit_bytes=VMEM_LIMIT_BYTES)


def _tile(dim, pref, unit=LANES):
    if dim <= pref:
        return dim
    t = (pref // unit) * unit
    while t >= unit:
        if dim % t == 0:
            return t
        t -= unit
    return dim


def _ew_tiles(rows, cols):
    tc = cols if (cols % LANES != 0 or cols <= 2048) else _tile(cols, 2048)
    tr = rows
    while tr * tc > 256 * 1024 and tr % 2 == 0 and (tr // 2) % 8 == 0:
        tr //= 2
    return tr, tc


def _sigmoid(v):
    return 1.0 / (1.0 + jnp.exp(-v))


def _dot(a, b):
    return jnp.dot(a, b, preferred_element_type=F32)


def _dot_nt(a, b):
    return lax.dot_general(a, b, (((1,), (1,)), ((), ())), preferred_element_type=F32)


def _dot_tn(a, b):
    return lax.dot_general(a, b, (((0,), (0,)), ((), ())), preferred_element_type=F32)


def _split3(v):
    hi = v.astype(BF16)
    r = v - hi.astype(F32)
    mid = r.astype(BF16)
    lo = (r - mid.astype(F32)).astype(BF16)
    return hi, mid, lo


def _xdot(v, onehot):
    hi, mid, lo = _split3(v)
    return _dot(hi, onehot) + _dot(mid, onehot) + _dot(lo, onehot)


def _xdot_l(onehot, v):
    hi, mid, lo = _split3(v)
    return _dot(onehot, hi) + _dot(onehot, mid) + _dot(onehot, lo)


def _mm(a, b, mode, out_dtype, name, add=None):
    if mode == "nn":
        M, K = a.shape
        N = b.shape[1]
    elif mode == "nt":
        M, K = a.shape
        N = b.shape[0]
    else:
        K, M = a.shape
        N = b.shape[1]
    tm, tn, tk = (2048, 1024, 1024) if mode == "tn" else (512, 1024, 3072)
    tm, tn, tk = _tile(M, tm, 8 if M < LANES else LANES), _tile(N, tn), _tile(K, tk)
    nk = K // tk
    has_add = add is not None

    def product(a_ref, b_ref):
        av = a_ref[...].astype(BF16)
        bv = b_ref[...].astype(BF16)
        if mode == "nn":
            return _dot(av, bv)
        if mode == "nt":
            return _dot_nt(av, bv)
        return _dot_tn(av, bv)

    def body(*refs):
        a_ref, b_ref = refs[:2]
        add_ref = refs[2] if has_add else None
        o_ref = refs[3] if has_add else refs[2]

        def finish(r):
            if has_add:
                r = r + add_ref[...]
            o_ref[...] = r.astype(out_dtype)

        if nk == 1:
            finish(product(a_ref, b_ref))
            return
        acc = refs[-1]
        k = pl.program_id(2)

        @pl.when(k == 0)
        def _():
            acc[...] = product(a_ref, b_ref)

        @pl.when((k > 0) & (k < nk - 1))
        def _():
            acc[...] += product(a_ref, b_ref)

        @pl.when(k == nk - 1)
        def _():
            finish(acc[...] + product(a_ref, b_ref))

    if mode == "nn":
        a_spec = pl.BlockSpec((tm, tk), lambda j, i, k: (i, k))
        b_spec = pl.BlockSpec((tk, tn), lambda j, i, k: (k, j))
    elif mode == "nt":
        a_spec = pl.BlockSpec((tm, tk), lambda j, i, k: (i, k))
        b_spec = pl.BlockSpec((tn, tk), lambda j, i, k: (j, k))
    else:
        a_spec = pl.BlockSpec((tk, tm), lambda j, i, k: (k, i))
        b_spec = pl.BlockSpec((tk, tn), lambda j, i, k: (k, j))
    o_spec = pl.BlockSpec((tm, tn), lambda j, i, k: (i, j))
    in_specs = [a_spec, b_spec] + ([o_spec] if has_add else [])
    args = (a, b) + ((add,) if has_add else ())
    return pl.pallas_call(
        body, name=name, grid=(N // tn, M // tm, nk), in_specs=in_specs, out_specs=o_spec,
        out_shape=jax.ShapeDtypeStruct((M, N), out_dtype),
        scratch_shapes=[pltpu.VMEM((tm, tn), F32)] if nk > 1 else [],
        compiler_params=_cparams("parallel", "parallel", "arbitrary"),
    )(*args)


def _rms_fwd(x, g, name):
    R, Dm = x.shape
    tr = _tile(R, 256, 8)

    def body(x_ref, g_ref, o_ref):
        xv = x_ref[...]
        r = lax.rsqrt(jnp.mean(xv * xv, axis=-1, keepdims=True) + EPS)
        o_ref[...] = (xv * r * g_ref[...]).astype(BF16)

    return pl.pallas_call(
        body, name=name, grid=(R // tr,),
        in_specs=[pl.BlockSpec((tr, Dm), lambda i: (i, 0)), pl.BlockSpec((1, Dm), lambda i: (0, 0))],
        out_specs=pl.BlockSpec((tr, Dm), lambda i: (i, 0)),
        out_shape=jax.ShapeDtypeStruct((R, Dm), BF16),
        compiler_params=_cparams("parallel"),
    )(x, g.reshape(1, Dm))


def _rms_bwd(x, g, dh, dres, name):
    R, Dm = x.shape
    tr = _tile(R, 256, 8)
    has_res = dres is not None

    def body(*refs):
        if has_res:
            x_ref, g_ref, dh_ref, dres_ref, dx_ref, dxb_ref, dg_ref = refs
        else:
            x_ref, g_ref, dh_ref, dx_ref, dxb_ref, dg_ref = refs
        xv = x_ref[...]
        r = lax.rsqrt(jnp.mean(xv * xv, axis=-1, keepdims=True) + EPS)
        xhat = xv * r
        dhv = dh_ref[...]
        dxhat = dhv * g_ref[...]
        dx = r * (dxhat - xhat * jnp.mean(dxhat * xhat, axis=-1, keepdims=True))
        if has_res:
            dx = dx + dres_ref[...]
        dx_ref[...] = dx
        dxb_ref[...] = dx.astype(BF16)

        @pl.when(pl.program_id(0) == 0)
        def _():
            dg_ref[...] = jnp.zeros_like(dg_ref)

        dg_ref[...] += jnp.sum(dhv * xhat, axis=0, keepdims=True)

    row = pl.BlockSpec((tr, Dm), lambda i: (i, 0))
    vec = pl.BlockSpec((1, Dm), lambda i: (0, 0))
    in_specs = [row, vec, row] + ([row] if has_res else [])
    args = (x, g.reshape(1, Dm), dh) + ((dres,) if has_res else ())
    dx, dxb, dg = pl.pallas_call(
        body, name=name, grid=(R // tr,), in_specs=in_specs, out_specs=[row, row, vec],
        out_shape=[jax.ShapeDtypeStruct((R, Dm), F32), jax.ShapeDtypeStruct((R, Dm), BF16),
                   jax.ShapeDtypeStruct((1, Dm), F32)],
        compiler_params=_cparams("arbitrary"),
    )(*args)
    return dx, dxb, dg.reshape(Dm)


def _loss_head(x, g, tgt, name):
    R, Dm = x.shape
    tr = _tile(R, 256, 8)

    def body(x_ref, g_ref, t_ref, loss_ref, dx_ref, dxb_ref, dg_ref):
        xv = x_ref[...]
        gv = g_ref[...]
        r = lax.rsqrt(jnp.mean(xv * xv, axis=-1, keepdims=True) + EPS)
        xhat = xv * r
        e = xhat * gv - t_ref[...]
        part = jnp.sum(jnp.mean(e * e, axis=-1, keepdims=True), axis=0, keepdims=True) * 0.5
        dy = e * (1.0 / Dm)
        dxhat = dy * gv
        dx = r * (dxhat - xhat * jnp.mean(dxhat * xhat, axis=-1, keepdims=True))
        dx_ref[...] = dx
        dxb_ref[...] = dx.astype(BF16)

        @pl.when(pl.program_id(0) == 0)
        def _():
            dg_ref[...] = jnp.zeros_like(dg_ref)
            loss_ref[...] = jnp.zeros_like(loss_ref)

        dg_ref[...] += jnp.sum(dy * xhat, axis=0, keepdims=True)
        loss_ref[...] += jnp.broadcast_to(part, loss_ref.shape)

    row = pl.BlockSpec((tr, Dm), lambda i: (i, 0))
    vec = pl.BlockSpec((1, Dm), lambda i: (0, 0))
    lsp = pl.BlockSpec((1, LANES), lambda i: (0, 0))
    loss, dx, dxb, dg = pl.pallas_call(
        body, name=name, grid=(R // tr,), in_specs=[row, vec, row], out_specs=[lsp, row, row, vec],
        out_shape=[jax.ShapeDtypeStruct((1, LANES), F32), jax.ShapeDtypeStruct((R, Dm), F32),
                   jax.ShapeDtypeStruct((R, Dm), BF16), jax.ShapeDtypeStruct((1, Dm), F32)],
        compiler_params=_cparams("arbitrary"),
    )(x, g.reshape(1, Dm), tgt)
    return loss[0, 0], dx, dxb, dg.reshape(Dm)


def _elementwise(fn, mats, vecs, out_dtypes, name):
    R, C = mats[0].shape
    tr, tc = _ew_tiles(R, C)
    nm, nv, no = len(mats), len(vecs), len(out_dtypes)

    def body(*refs):
        ins = [r[...] for r in refs[:nm + nv]]
        outs = fn(*ins)
        for o_ref, o in zip(refs[nm + nv:], outs):
            o_ref[...] = o.astype(o_ref.dtype)

    blk = pl.BlockSpec((tr, tc), lambda i, j: (i, j))
    vblk = pl.BlockSpec((1, tc), lambda i, j: (0, j))
    res = pl.pallas_call(
        body, name=name, grid=(R // tr, C // tc),
        in_specs=[blk] * nm + [vblk] * nv, out_specs=[blk] * no,
        out_shape=[jax.ShapeDtypeStruct((R, C), dt) for dt in out_dtypes],
        compiler_params=_cparams("parallel", "parallel"),
    )(*mats, *[v.reshape(1, C) for v in vecs])
    return res


def _sum_lead(arr, name):
    P_, R, C = arr.shape
    tr, tc = _ew_tiles(R, C)

    def body(a_ref, o_ref):
        s = a_ref[0].astype(F32)
        for p in range(1, P_):
            s = s + a_ref[p].astype(F32)
        o_ref[...] = s

    return pl.pallas_call(
        body, name=name, grid=(R // tr, C // tc),
        in_specs=[pl.BlockSpec((P_, tr, tc), lambda i, j: (0, i, j))],
        out_specs=pl.BlockSpec((tr, tc), lambda i, j: (i, j)),
        out_shape=jax.ShapeDtypeStruct((R, C), F32),
        compiler_params=_cparams("parallel", "parallel"),
    )(arr)


def _adamw(w, m, v, gparts, name):
    P_ = len(gparts)
    R, C = w.shape
    tr, tc = _ew_tiles(R, C)
    c1 = 1.0 / (1.0 - ADAM_B1 ** ADAM_STEP)
    c2 = 1.0 / (1.0 - ADAM_B2 ** ADAM_STEP)

    def body(w_ref, m_ref, v_ref, *rest):
        g_refs, (go_ref, d_ref, mo_ref, vo_ref) = rest[:P_], rest[P_:]
        g = g_refs[0][...]
        for g_ref in g_refs[1:]:
            g = g + g_ref[...]
        mn = ADAM_B1 * m_ref[...] + (1.0 - ADAM_B1) * g
        vn = ADAM_B2 * v_ref[...] + (1.0 - ADAM_B2) * (g * g)
        m_hat = mn * c1
        v_hat = vn * c2
        d_ref[...] = -ADAM_LR * (m_hat / (jnp.sqrt(v_hat) + ADAM_EPS) + ADAM_WD * w_ref[...])
        go_ref[...] = g
        mo_ref[...] = mn
        vo_ref[...] = vn

    blk = pl.BlockSpec((tr, tc), lambda i, j: (i, j))
    sds = jax.ShapeDtypeStruct((R, C), F32)
    return pl.pallas_call(
        body, name=name, grid=(R // tr, C // tc),
        in_specs=[blk] * (3 + P_), out_specs=[blk] * 4, out_shape=[sds] * 4,
        compiler_params=_cparams("parallel", "parallel"),
    )(w, m, v, *gparts)


SUBLANES = 8


def _conv_pre(u, up, w_ref, b, first):
    tr = u.shape[0]
    rows = lax.broadcasted_iota(jnp.int32, u.shape, 0)
    keep = 1.0 - first.astype(F32)
    acc = b + w_ref[SSD_CONV - 1:SSD_CONV, :] * u
    shifted = []
    for j in range(1, SSD_CONV):
        su = pltpu.roll(u, j, 0)
        sp = jnp.tile(pltpu.roll(up, j, 0) * keep, (tr // SUBLANES, 1))
        sh = jnp.where(rows < j, sp, su)
        shifted.append(sh)
        acc = acc + w_ref[SSD_CONV - 1 - j:SSD_CONV - j, :] * sh
    return acc, shifted


def _conv_fwd(u, w, b, name):
    T, C = u.shape
    tr, tc = _tile(T, 256, 8), _tile(C, 1024)

    def body(u_ref, up_ref, w_ref, b_ref, o_ref):
        pre, _ = _conv_pre(u_ref[...], up_ref[...], w_ref, b_ref[...], pl.program_id(0) == 0)
        o_ref[...] = pre * _sigmoid(pre)

    return pl.pallas_call(
        body, name=name, grid=(T // tr, C // tc),
        in_specs=[pl.BlockSpec((tr, tc), lambda i, j: (i, j)),
                  pl.BlockSpec((SUBLANES, tc), lambda i, j: (jnp.maximum(i * (tr // SUBLANES) - 1, 0), j)),
                  pl.BlockSpec((SSD_CONV, tc), lambda i, j: (0, j)),
                  pl.BlockSpec((1, tc), lambda i, j: (0, j))],
        out_specs=pl.BlockSpec((tr, tc), lambda i, j: (i, j)),
        out_shape=jax.ShapeDtypeStruct((T, C), F32),
        compiler_params=_cparams("parallel", "parallel"),
    )(u, u, w, b.reshape(1, C))


def _conv_bwd_pre(u, w, b, dact, name):
    T, C = u.shape
    tr, tc = _tile(T, 256, 8), _tile(C, 1024)

    def body(u_ref, up_ref, w_ref, b_ref, da_ref, dp_ref, dw_ref, db_ref):
        i = pl.program_id(1)
        uv = u_ref[...]
        pre, shifted = _conv_pre(uv, up_ref[...], w_ref, b_ref[...], i == 0)
        sg = _sigmoid(pre)
        dpre = da_ref[...] * (sg * (1.0 + pre * (1.0 - sg)))
        dp_ref[...] = dpre

        @pl.when(i == 0)
        def _():
            dw_ref[...] = jnp.zeros_like(dw_ref)
            db_ref[...] = jnp.zeros_like(db_ref)

        db_ref[...] += jnp.sum(dpre, axis=0, keepdims=True)
        dw_ref[SSD_CONV - 1:SSD_CONV, :] += jnp.sum(dpre * uv, axis=0, keepdims=True)
        for j in range(1, SSD_CONV):
            dw_ref[SSD_CONV - 1 - j:SSD_CONV - j, :] += jnp.sum(dpre * shifted[j - 1], axis=0, keepdims=True)

    blk = pl.BlockSpec((tr, tc), lambda j, i: (i, j))
    dpre, dw, db = pl.pallas_call(
        body, name=name, grid=(C // tc, T // tr),
        in_specs=[blk, pl.BlockSpec((SUBLANES, tc), lambda j, i: (jnp.maximum(i * (tr // SUBLANES) - 1, 0), j)),
                  pl.BlockSpec((SSD_CONV, tc), lambda j, i: (0, j)),
                  pl.BlockSpec((1, tc), lambda j, i: (0, j)), blk],
        out_specs=[blk, pl.BlockSpec((SSD_CONV, tc), lambda j, i: (0, j)), pl.BlockSpec((1, tc), lambda j, i: (0, j))],
        out_shape=[jax.ShapeDtypeStruct((T, C), F32), jax.ShapeDtypeStruct((SSD_CONV, C), F32),
                   jax.ShapeDtypeStruct((1, C), F32)],
        compiler_params=_cparams("parallel", "arbitrary"),
    )(u, u, w, b.reshape(1, C), dact)
    return dpre, dw, db.reshape(C)


def _conv_bwd_in(dpre, w, name):
    T, C = dpre.shape
    tr, tc = _tile(T, 256, 8), _tile(C, 1024)
    nb = T // tr

    def body(d_ref, dn_ref, w_ref, o_ref):
        d = d_ref[...]
        keep = 1.0 - (pl.program_id(0) == nb - 1).astype(F32)
        dn = dn_ref[...] * keep
        rows = lax.broadcasted_iota(jnp.int32, d.shape, 0)
        acc = w_ref[SSD_CONV - 1:SSD_CONV, :] * d
        for j in range(1, SSD_CONV):
            sd = pltpu.roll(d, tr - j, 0)
            sn = jnp.tile(pltpu.roll(dn, SUBLANES - j, 0), (tr // SUBLANES, 1))
            acc = acc + w_ref[SSD_CONV - 1 - j:SSD_CONV - j, :] * jnp.where(rows >= tr - j, sn, sd)
        o_ref[...] = acc.astype(BF16)

    return pl.pallas_call(
        body, name=name, grid=(nb, C // tc),
        in_specs=[pl.BlockSpec((tr, tc), lambda i, j: (i, j)),
                  pl.BlockSpec((SUBLANES, tc), lambda i, j: (jnp.minimum((i + 1) * (tr // SUBLANES), T // SUBLANES - 1), j)),
                  pl.BlockSpec((SSD_CONV, tc), lambda i, j: (0, j))],
        out_specs=pl.BlockSpec((tr, tc), lambda i, j: (i, j)),
        out_shape=jax.ShapeDtypeStruct((T, C), BF16),
        compiler_params=_cparams("parallel", "parallel"),
    )(dpre, dpre, w)


def _mem_probs(qh, kh, scale):
    s = _dot_nt(qh.astype(BF16), kh.astype(BF16)) * scale
    m = jnp.max(s, axis=-1, keepdims=True)
    p = jnp.exp(s - m)
    return p / jnp.sum(p, axis=-1, keepdims=True)


def _memattn_fwd(q, kv, name):
    T, MW = q.shape
    NM = kv.shape[0]
    hd = MW // MEM_HEADS
    scale = hd ** -0.5
    tq = _tile(T, 512, 8)

    def body(q_ref, kv_ref, o_ref):
        for h in range(MEM_HEADS):
            sl = slice(h * hd, (h + 1) * hd)
            p = _mem_probs(q_ref[:, sl], kv_ref[:, sl], scale)
            vh = kv_ref[:, MW + h * hd:MW + (h + 1) * hd]
            o_ref[:, sl] = _dot(p.astype(BF16), vh.astype(BF16))

    return pl.pallas_call(
        body, name=name, grid=(T // tq,),
        in_specs=[pl.BlockSpec((tq, MW), lambda i: (i, 0)), pl.BlockSpec((NM, 2 * MW), lambda i: (0, 0))],
        out_specs=pl.BlockSpec((tq, MW), lambda i: (i, 0)),
        out_shape=jax.ShapeDtypeStruct((T, MW), F32),
        compiler_params=_cparams("parallel"),
    )(q, kv)


def _memattn_bwd(q, kv, dy, name):
    T, MW = q.shape
    NM = kv.shape[0]
    hd = MW // MEM_HEADS
    scale = hd ** -0.5
    tq = _tile(T, 512, 8)

    def body(q_ref, kv_ref, dy_ref, dq_ref, dkv_ref):
        @pl.when(pl.program_id(0) == 0)
        def _():
            dkv_ref[...] = jnp.zeros_like(dkv_ref)

        for h in range(MEM_HEADS):
            sl = slice(h * hd, (h + 1) * hd)
            vsl = slice(MW + h * hd, MW + (h + 1) * hd)
            qh = q_ref[:, sl]
            kh = kv_ref[:, sl]
            vh = kv_ref[:, vsl]
            dyh = dy_ref[:, sl].astype(BF16)
            p = _mem_probs(qh, kh, scale)
            dp = _dot_nt(dyh, vh.astype(BF16))
            ds = p * (dp - jnp.sum(dp * p, axis=-1, keepdims=True)) * scale
            dq_ref[:, sl] = _dot(ds.astype(BF16), kh.astype(BF16)).astype(BF16)
            dkv_ref[:, sl] += _dot(ds.T.astype(BF16), qh.astype(BF16))
            dkv_ref[:, vsl] += _dot(p.T.astype(BF16), dyh)

    return pl.pallas_call(
        body, name=name, grid=(T // tq,),
        in_specs=[pl.BlockSpec((tq, MW), lambda i: (i, 0)), pl.BlockSpec((NM, 2 * MW), lambda i: (0, 0)),
                  pl.BlockSpec((tq, MW), lambda i: (i, 0))],
        out_specs=[pl.BlockSpec((tq, MW), lambda i: (i, 0)), pl.BlockSpec((NM, 2 * MW), lambda i: (0, 0))],
        out_shape=[jax.ShapeDtypeStruct((T, MW), BF16), jax.ShapeDtypeStruct((NM, 2 * MW), F32)],
        compiler_params=_cparams("arbitrary"),
    )(q, kv, dy)


def _silu_parts(z):
    sg = _sigmoid(z)
    return z * sg, sg * (1.0 + z * (1.0 - sg))


def _gate_fwd(a, z, name):
    return _elementwise(lambda av, zv: (av * _silu_parts(zv)[0],), [a, z], [], [BF16], name)[0]


def _gate_bwd(a, z, d, name):
    def fn(av, zv, dv):
        s, ds = _silu_parts(zv)
        return dv * s, dv * av * ds
    return _elementwise(fn, [a, z, d], [], [F32, BF16], name)


def _gate_norm_fwd(y, z, g, groups, name):
    T, C = y.shape
    gw = C // groups
    tr = _tile(T, 128, 8)

    def body(y_ref, z_ref, g_ref, o_ref):
        for k in range(groups):
            sl = slice(k * gw, (k + 1) * gw)
            u = y_ref[:, sl] * _silu_parts(z_ref[:, sl])[0]
            r = lax.rsqrt(jnp.mean(u * u, axis=-1, keepdims=True) + EPS)
            o_ref[:, sl] = (u * r * g_ref[:, sl]).astype(BF16)

    row = pl.BlockSpec((tr, C), lambda i: (i, 0))
    return pl.pallas_call(
        body, name=name, grid=(T // tr,), in_specs=[row, row, pl.BlockSpec((1, C), lambda i: (0, 0))],
        out_specs=row, out_shape=jax.ShapeDtypeStruct((T, C), BF16),
        compiler_params=_cparams("parallel"),
    )(y, z, g.reshape(1, C))


def _gate_norm_bwd(y, z, g, d, groups, name):
    T, C = y.shape
    gw = C // groups
    tr = _tile(T, 128, 8)

    def body(y_ref, z_ref, g_ref, d_ref, dy_ref, dz_ref, dg_ref):
        @pl.when(pl.program_id(0) == 0)
        def _():
            dg_ref[...] = jnp.zeros_like(dg_ref)

        for k in range(groups):
            sl = slice(k * gw, (k + 1) * gw)
            yv = y_ref[:, sl]
            s, ds = _silu_parts(z_ref[:, sl])
            u = yv * s
            r = lax.rsqrt(jnp.mean(u * u, axis=-1, keepdims=True) + EPS)
            uhat = u * r
            dv = d_ref[:, sl]
            dg_ref[:, sl] += jnp.sum(dv * uhat, axis=0, keepdims=True)
            duhat = dv * g_ref[:, sl]
            du = r * (duhat - uhat * jnp.mean(duhat * uhat, axis=-1, keepdims=True))
            dy_ref[:, sl] = du * s
            dz_ref[:, sl] = (du * yv * ds).astype(BF16)

    row = pl.BlockSpec((tr, C), lambda i: (i, 0))
    vec = pl.BlockSpec((1, C), lambda i: (0, 0))
    dy, dz, dg = pl.pallas_call(
        body, name=name, grid=(T // tr,), in_specs=[row, row, vec, row], out_specs=[row, row, vec],
        out_shape=[jax.ShapeDtypeStruct((T, C), F32), jax.ShapeDtypeStruct((T, C), BF16),
                   jax.ShapeDtypeStruct((1, C), F32)],
        compiler_params=_cparams("arbitrary"),
    )(y, z, g.reshape(1, C), d)
    return dy, dz, dg.reshape(C)


def _ssd_common(raw, bias, alog, Q, HP, HPG):
    P_ = SSD_HEAD_DIM
    dt_in = raw + bias
    dt = jnp.maximum(dt_in, 0.0) + jnp.log(1.0 + jnp.exp(-jnp.abs(dt_in)))
    a = -jnp.exp(alog)
    r_q = lax.broadcasted_iota(jnp.int32, (Q, Q), 0)
    c_q = lax.broadcasted_iota(jnp.int32, (Q, Q), 1)
    causal = r_q >= c_q
    tril = causal.astype(BF16)
    A = _xdot_l(tril, dt * a)
    e_r = lax.broadcasted_iota(jnp.int32, (LANES, HP), 0)
    e_c = lax.broadcasted_iota(jnp.int32, (LANES, HP), 1)
    E = ((e_c >= e_r * P_) & (e_c < (e_r + 1) * P_) & (e_r < HPG)).astype(BF16)
    return dt, a, A, causal, E


def _head_cols(v, vt, j):
    lane = lax.broadcasted_iota(jnp.int32, v.shape, 1)
    sub = lax.broadcasted_iota(jnp.int32, vt.shape, 0)
    col = jnp.sum(jnp.where(lane == j, v, 0.0), axis=-1, keepdims=True)
    row = jnp.sum(jnp.where(sub == j, vt, 0.0), axis=0, keepdims=True)
    return col, row


def _ssd_fwd(act, raw_g, bias_g, alog_g, dsk_g, TOK, name):
    T = act.shape[0]
    G, N, Q, P_ = SSD_GROUPS, SSD_STATE, SSD_CHUNK, SSD_HEAD_DIM
    HP = TOK // G
    HPG = HP // P_
    NC = T // Q

    def body(x_ref, b_ref, c_ref, raw_ref, bias_ref, alog_ref, dsk_ref, y_ref, hp_ref, hT):
        @pl.when(pl.program_id(1) == 0)
        def _():
            hT[...] = jnp.zeros_like(hT)

        xs = x_ref[...]
        Bb = b_ref[...].astype(BF16)
        Cb = c_ref[...].astype(BF16)
        dt, a, A, causal, E = _ssd_common(raw_ref[...], bias_ref[...], alog_ref[...], Q, HP, HPG)
        AT = A.T
        dt_e = _xdot(dt, E)
        A_e = _xdot(A, E)
        dsk_e = _xdot(jnp.broadcast_to(dsk_ref[...], (Q, LANES)), E)
        rows = lax.broadcasted_iota(jnp.int32, (Q, HP), 0)
        cols = lax.broadcasted_iota(jnp.int32, (Q, HP), 1)
        Al_e = jnp.sum(jnp.where(rows == Q - 1, A_e, 0.0), axis=0, keepdims=True)
        xdt = xs * dt_e
        hprev = hT[...]
        hp_ref[...] = hprev
        CB = _dot_nt(Cb, Bb)
        y = _dot(Cb, hprev.astype(BF16)) * jnp.exp(A_e) + dsk_e * xs
        M, xh = [], []
        for j in range(HPG):
            a_col, a_row = _head_cols(A, AT, j)
            M.append((CB * jnp.exp(jnp.where(causal, a_col - a_row, NEG))).astype(BF16))
            xh.append(jnp.where((cols >= j * P_) & (cols < (j + 1) * P_), xdt, 0.0).astype(BF16))
        y_ref[...] = y + _dot(jnp.concatenate(M, axis=1), jnp.concatenate(xh, axis=0))
        dte = jnp.exp(Al_e - A_e)
        hT[...] = jnp.exp(Al_e) * hprev + _dot(b_ref[...].T.astype(BF16), (xdt * dte).astype(BF16))

    nbx = TOK // N
    par = pl.BlockSpec((None, 1, LANES), lambda g, c: (g, 0, 0))
    return pl.pallas_call(
        body, name=name, grid=(G, NC),
        in_specs=[pl.BlockSpec((Q, HP), lambda g, c: (c, g)),
                  pl.BlockSpec((Q, N), lambda g, c: (c, nbx + g)),
                  pl.BlockSpec((Q, N), lambda g, c: (c, nbx + G + g)),
                  pl.BlockSpec((None, Q, LANES), lambda g, c: (g, c, 0)), par, par, par],
        out_specs=[pl.BlockSpec((Q, HP), lambda g, c: (c, g)),
                   pl.BlockSpec((None, None, N, HP), lambda g, c: (g, c, 0, 0))],
        out_shape=[jax.ShapeDtypeStruct((T, TOK), F32), jax.ShapeDtypeStruct((G, NC, N, HP), F32)],
        scratch_shapes=[pltpu.VMEM((N, HP), F32)],
        compiler_params=_cparams("parallel", "arbitrary"),
    )(act, act, act, raw_g, bias_g, alog_g, dsk_g)


def _ssd_bwd(act, raw_g, bias_g, alog_g, dsk_g, hprev, dy, TOK, name):
    T = act.shape[0]
    G, N, Q, P_ = SSD_GROUPS, SSD_STATE, SSD_CHUNK, SSD_HEAD_DIM
    HP = TOK // G
    HPG = HP // P_
    NC = T // Q

    def body(x_ref, b_ref, c_ref, raw_ref, bias_ref, alog_ref, dsk_ref, hp_ref, dy_ref,
             dx_ref, db_ref, dc_ref, draw_ref, dalog_ref, dbias_ref, ddsk_ref, dHT):
        @pl.when(pl.program_id(1) == 0)
        def _():
            dHT[...] = jnp.zeros_like(dHT)
            dalog_ref[...] = jnp.zeros_like(dalog_ref)
            dbias_ref[...] = jnp.zeros_like(dbias_ref)
            ddsk_ref[...] = jnp.zeros_like(ddsk_ref)

        xs = x_ref[...]
        dyv = dy_ref[...]
        Bm = b_ref[...]
        Cm = c_ref[...]
        Bb = Bm.astype(BF16)
        Cb = Cm.astype(BF16)
        raw_in = raw_ref[...] + bias_ref[...]
        dt, a, A, causal, E = _ssd_common(raw_ref[...], bias_ref[...], alog_ref[...], Q, HP, HPG)
        AT = A.T
        t_r = lax.broadcasted_iota(jnp.int32, (HP, LANES), 0)
        t_c = lax.broadcasted_iota(jnp.int32, (HP, LANES), 1)
        ET = ((t_r >= t_c * P_) & (t_r < (t_c + 1) * P_) & (t_c < HPG)).astype(BF16)
        dt_e = _xdot(dt, E)
        A_e = _xdot(A, E)
        dsk_e = _xdot(jnp.broadcast_to(dsk_ref[...], (Q, LANES)), E)
        rows = lax.broadcasted_iota(jnp.int32, (Q, HP), 0)
        cols = lax.broadcasted_iota(jnp.int32, (Q, HP), 1)
        last = rows == Q - 1
        Al_e = jnp.sum(jnp.where(last, A_e, 0.0), axis=0, keepdims=True)
        eA = jnp.exp(A_e)
        eAl = jnp.exp(Al_e)
        dte = jnp.exp(Al_e - A_e)
        xdt = xs * dt_e
        xdt_b = xdt.astype(BF16)
        CB = _dot_nt(Cb, Bb)
        HT = hp_ref[...]
        HTb = HT.astype(BF16)
        dH = dHT[...]
        dHb = dH.astype(BF16)
        dys = (dyv * eA).astype(BF16)
        CH = _dot(Cb, HTb)
        dC = _dot_nt(dys, HTb)
        dH_prev = _dot(Cm.T.astype(BF16), dys) + eAl * dH
        dAe = dyv * CH * eA
        dAl = eAl * jnp.sum(dH * HT, axis=0, keepdims=True)
        W = _dot(Bb, dHb)
        dxdt = W * dte
        dd = W * xdt * dte
        dB = _dot_nt((xdt * dte).astype(BF16), dHb)
        dAl = dAl + jnp.sum(dd, axis=0, keepdims=True)
        dAe = dAe - dd + jnp.where(last, dAl, 0.0)
        lane = lax.broadcasted_iota(jnp.int32, (Q, LANES), 1)
        sub = lax.broadcasted_iota(jnp.int32, (LANES, Q), 0)
        dCB = jnp.zeros((Q, Q), F32)
        dA_col = jnp.zeros((Q, LANES), F32)
        dA_row = jnp.zeros((LANES, Q), F32)
        dy_heads = jnp.concatenate(
            [jnp.where((cols >= j * P_) & (cols < (j + 1) * P_), dyv, 0.0).astype(BF16) for j in range(HPG)], axis=0)
        G_heads = _dot_nt(dy_heads, xdt_b)
        MT = []
        for j in range(HPG):
            a_col, a_row = _head_cols(A, AT, j)
            L = jnp.exp(jnp.where(causal, a_col - a_row, NEG))
            GL = G_heads[j * Q:(j + 1) * Q] * L
            dCB = dCB + GL
            dLL = GL * CB
            dA_col = dA_col + jnp.where(lane == j, jnp.sum(dLL, axis=-1, keepdims=True), 0.0)
            dA_row = dA_row + jnp.where(sub == j, jnp.sum(dLL, axis=0, keepdims=True), 0.0)
            MT.append((CB * L).T.astype(BF16))
        dxdt = dxdt + _dot(jnp.concatenate(MT, axis=1), dy_heads)
        dC = dC + _dot(dCB.astype(BF16), Bb)
        dB = dB + _dot(dCB.T.astype(BF16), Cb)
        dA = dA_col - dA_row.T + _xdot(dAe, ET)
        triu = (lax.broadcasted_iota(jnp.int32, (Q, Q), 1) >= lax.broadcasted_iota(jnp.int32, (Q, Q), 0)).astype(BF16)
        rcs = _xdot_l(triu, dA)
        ddt = a * rcs + _xdot(dxdt * xs, ET)
        draw = ddt * _sigmoid(raw_in)
        draw_ref[...] = draw
        dalog_ref[...] += jnp.sum(dt * rcs, axis=0, keepdims=True) * a
        dbias_ref[...] += jnp.sum(draw, axis=0, keepdims=True)
        ddsk_ref[...] += jnp.sum(_xdot(dyv * xs, ET), axis=0, keepdims=True)
        dx_ref[...] = dxdt * dt_e + dsk_e * dyv
        db_ref[...] = dB
        dc_ref[...] = dC
        dHT[...] = dH_prev

    nbx = TOK // N
    rv = lambda c: NC - 1 - c
    par = pl.BlockSpec((None, 1, LANES), lambda g, c: (g, 0, 0))
    xsp = pl.BlockSpec((Q, HP), lambda g, c: (rv(c), g))
    outs = pl.pallas_call(
        body, name=name, grid=(G, NC),
        in_specs=[xsp,
                  pl.BlockSpec((Q, N), lambda g, c: (rv(c), nbx + g)),
                  pl.BlockSpec((Q, N), lambda g, c: (rv(c), nbx + G + g)),
                  pl.BlockSpec((None, Q, LANES), lambda g, c: (g, rv(c), 0)), par, par, par,
                  pl.BlockSpec((None, None, N, HP), lambda g, c: (g, rv(c), 0, 0)), xsp],
        out_specs=[xsp, pl.BlockSpec((Q, N), lambda g, c: (rv(c), g)), pl.BlockSpec((Q, N), lambda g, c: (rv(c), g)),
                   pl.BlockSpec((None, Q, LANES), lambda g, c: (g, rv(c), 0)), par, par, par],
        out_shape=[jax.ShapeDtypeStruct((T, TOK), F32), jax.ShapeDtypeStruct((T, G * N), F32),
                   jax.ShapeDtypeStruct((T, G * N), F32), jax.ShapeDtypeStruct((G, T, LANES), F32),
                   jax.ShapeDtypeStruct((G, 1, LANES), F32), jax.ShapeDtypeStruct((G, 1, LANES), F32),
                   jax.ShapeDtypeStruct((G, 1, LANES), F32)],
        scratch_shapes=[pltpu.VMEM((N, HP), F32)],
        compiler_params=_cparams("parallel", "arbitrary"),
    )(act, act, act, raw_g, bias_g, alog_g, dsk_g, hprev, dy)
    return outs


def _heads_per_block(H):
    for hb in (12, 8, 6, 4, 3, 2, 1):
        if H % hb == 0:
            return hb
    return 1


def _alibi_slope(head_index, n_alibi):
    c = -ALIBI_MAX_EXP * math.log(2.0) / n_alibi
    return jnp.exp(jnp.full((1, 1), c, F32) * (head_index + 1).astype(F32))


def _attn_masks(b, nb):
    Bq = ATTN_BLOCK
    iq = lax.broadcasted_iota(jnp.int32, (Bq, 2 * Bq), 0)
    jk = lax.broadcasted_iota(jnp.int32, (Bq, 2 * Bq), 1)
    rel = iq + Bq - jk
    mask = (rel >= 0) & (rel <= Bq) & (jk + jnp.where(b > 0, Bq, 0) >= Bq)
    rel_n = lax.broadcasted_iota(jnp.int32, (Bq, Bq), 0) + Bq - lax.broadcasted_iota(jnp.int32, (Bq, Bq), 1)
    mask_n = (rel_n + jnp.where(b < nb - 1, 0, 4 * Bq)) <= Bq
    return rel.astype(F32), mask, rel_n.astype(F32), mask_n


def _rows2(a, b):
    return jnp.concatenate([a, b], axis=0)


ATTN_UNITS_IN_FLIGHT = 6


def _attn_units_fwd(load, n_units, slopes, masks, scale, store):
    rel_f, mask, _, _ = masks
    for g0 in range(0, n_units, ATTN_UNITS_IN_FLIGHT):
        ids = range(g0, min(g0 + ATTN_UNITS_IN_FLIGHT, n_units))
        units = [load(i) for i in ids]
        raw = [_dot_nt(q, _rows2(kp, kc)) for q, kc, kp, vc, vp in units]
        soft = []
        for i, s_raw in zip(ids, raw):
            s = jnp.where(mask, s_raw * scale - slopes[i] * rel_f, NEG)
            m = jnp.max(s, axis=-1, keepdims=True)
            p = jnp.exp(s - m)
            den = jnp.sum(p, axis=-1, keepdims=True)
            soft.append((p.astype(BF16), den, m + jnp.log(den)))
        for i, (p, den, lse), (q, kc, kp, vc, vp) in zip(ids, soft, units):
            store(i, _dot(p, _rows2(vp, vc)) / den, lse)


def _attn_units_bwd(load, n_units, slopes, masks, scale, store):
    rel_f, mask, reln_f, mask_n = masks
    Bq = ATTN_BLOCK
    for g0 in range(0, n_units, ATTN_UNITS_IN_FLIGHT):
        ids = range(g0, min(g0 + ATTN_UNITS_IN_FLIGHT, n_units))
        units = [load(i) for i in ids]
        prods = []
        for q0, q1, kp, k0, vp, v0, do0, do1, y0, y1, lse0, lse1 in units:
            kcat = _rows2(kp, k0)
            do0b = do0.astype(BF16)
            do1b = do1.astype(BF16)
            prods.append((kcat, do0b, do1b, _dot_nt(q0, kcat), _dot_nt(do0b, _rows2(vp, v0)), _dot_nt(q1, k0),
                          _dot_nt(do1b, v0)))
        mids = []
        for i, u, (kcat, do0b, do1b, s_raw, dp_raw, sn_raw, dpn_raw) in zip(ids, units, prods):
            q0, q1, kp, k0, vp, v0, do0, do1, y0, y1, lse0, lse1 = u
            delta0 = jnp.sum(do0 * y0, axis=-1, keepdims=True)
            delta1 = jnp.sum(do1 * y1, axis=-1, keepdims=True)
            p = jnp.exp(jnp.where(mask, s_raw * scale - slopes[i] * rel_f, NEG) - lse0)
            ds = p * (dp_raw - delta0)
            p_n = jnp.exp(jnp.where(mask_n, sn_raw * scale - slopes[i] * reln_f, NEG) - lse1)
            ds_n = p_n * (dpn_raw - delta1)
            mids.append((ds.astype(BF16), _rows2(ds[:, Bq:], ds_n).T.astype(BF16),
                         _rows2(p[:, Bq:], p_n).T.astype(BF16)))
        for i, u, (kcat, do0b, do1b, *_), (dsb, dsk_t, pk_t) in zip(ids, units, prods, mids):
            store(i, scale * _dot(dsb, kcat), scale * _dot(dsk_t, _rows2(u[0], u[1])), _dot(pk_t, _rows2(do0b, do1b)))


def _attn_fwd_strided(q, k, v, gi, d, name):
    T, TOK = q.shape
    E_ = ATTN_HEAD_DIM
    H = TOK // E_
    n_alibi = len(DILATED_GROUPS) * H
    Bq = ATTN_BLOCK
    RB = Bq * d
    nb = T // RB
    scale = E_ ** -0.5

    def body(q_ref, kc_ref, kp_ref, vc_ref, vp_ref, o_ref, l_ref):
        masks = _attn_masks(pl.program_id(1), nb)
        slope = _alibi_slope(gi * H + pl.program_id(0), n_alibi) * float(d)
        rows = lambda r: pl.ds(r, Bq, stride=d)

        def load(r):
            return tuple(ref[rows(r), :].astype(BF16) for ref in (q_ref, kc_ref, kp_ref, vc_ref, vp_ref))

        def store(r, o, lse):
            o_ref[rows(r), :] = o
            l_ref[rows(r), :] = jnp.broadcast_to(lse, (Bq, E_))

        _attn_units_fwd(load, d, [slope] * d, masks, scale, store)

    cur = pl.BlockSpec((RB, E_), lambda h, b: (b, h))
    prev = pl.BlockSpec((RB, E_), lambda h, b: (jnp.maximum(b - 1, 0), h))
    sds = jax.ShapeDtypeStruct((T, TOK), F32)
    return pl.pallas_call(
        body, name=name, grid=(H, nb), in_specs=[cur, cur, prev, cur, prev], out_specs=[cur, cur],
        out_shape=[sds, sds], compiler_params=_cparams("parallel", "parallel"),
    )(q, k, k, v, v)


def _attn_bwd_strided(q, k, v, y, lse, dy, gi, d, name):
    T, TOK = q.shape
    E_ = ATTN_HEAD_DIM
    H = TOK // E_
    n_alibi = len(DILATED_GROUPS) * H
    Bq = ATTN_BLOCK
    RB = Bq * d
    nb = T // RB
    scale = E_ ** -0.5

    def body(q0_ref, q1_ref, kp_ref, k0_ref, vp_ref, v0_ref, do0_ref, do1_ref, y0_ref, y1_ref, l0_ref, l1_ref,
             dq_ref, dk_ref, dv_ref):
        masks = _attn_masks(pl.program_id(1), nb)
        slope = _alibi_slope(gi * H + pl.program_id(0), n_alibi) * float(d)
        rows = lambda r: pl.ds(r, Bq, stride=d)

        def load(r):
            return (tuple(ref[rows(r), :].astype(BF16) for ref in (q0_ref, q1_ref, kp_ref, k0_ref, vp_ref, v0_ref))
                    + tuple(ref[rows(r), :] for ref in (do0_ref, do1_ref, y0_ref, y1_ref))
                    + tuple(jnp.max(ref[rows(r), :], axis=-1, keepdims=True) for ref in (l0_ref, l1_ref)))

        def store(r, dq, dk, dv):
            dq_ref[rows(r), :] = dq
            dk_ref[rows(r), :] = dk
            dv_ref[rows(r), :] = dv

        _attn_units_bwd(load, d, [slope] * d, masks, scale, store)

    cur = pl.BlockSpec((RB, E_), lambda h, b: (b, h))
    prev = pl.BlockSpec((RB, E_), lambda h, b: (jnp.maximum(b - 1, 0), h))
    nxt = pl.BlockSpec((RB, E_), lambda h, b: (jnp.minimum(b + 1, nb - 1), h))
    sds = jax.ShapeDtypeStruct((T, TOK), F32)
    return pl.pallas_call(
        body, name=name, grid=(H, nb),
        in_specs=[cur, nxt, prev, cur, prev, cur, cur, nxt, cur, nxt, cur, nxt],
        out_specs=[cur, cur, cur], out_shape=[sds, sds, sds],
        compiler_params=_cparams("parallel", "parallel"),
    )(q, q, k, k, v, v, dy, dy, y, y, lse, lse)


def _attn_fwd(q, k, v, gi, window, d, name):
    T, TOK = q.shape
    E_ = ATTN_HEAD_DIM
    H = TOK // E_
    n_alibi = len(DILATED_GROUPS) * H
    assert window // d == ATTN_BLOCK and (T // d) % ATTN_BLOCK == 0
    if d > 1:
        return _attn_fwd_strided(q, k, v, gi, d, name)
    n_sub = T // d
    nb = n_sub // ATTN_BLOCK
    HB = _heads_per_block(H)
    NHB = H // HB
    hbw = HB * E_
    scale = E_ ** -0.5
    Bq = ATTN_BLOCK

    def body(q_ref, kc_ref, kp_ref, vc_ref, vp_ref, o_ref, l_ref):
        hb = pl.program_id(1)
        masks = _attn_masks(pl.program_id(2), nb)
        slopes = [_alibi_slope(gi * H + hb * HB + hh, n_alibi) * float(d) for hh in range(HB)]
        cols = lambda hh: slice(hh * E_, (hh + 1) * E_)

        def load(hh):
            return tuple(ref[:, cols(hh)].astype(BF16) for ref in (q_ref, kc_ref, kp_ref, vc_ref, vp_ref))

        def store(hh, o, lse):
            o_ref[:, cols(hh)] = o
            l_ref[:, cols(hh)] = jnp.broadcast_to(lse, (Bq, E_))

        _attn_units_fwd(load, HB, slopes, masks, scale, store)

    cur = pl.BlockSpec((Bq, hbw), lambda r, h, b: (b, r * NHB + h))
    prev = pl.BlockSpec((Bq, hbw), lambda r, h, b: (jnp.maximum(b - 1, 0), r * NHB + h))
    view = lambda t: t.reshape(n_sub, d * TOK)
    sds = jax.ShapeDtypeStruct((n_sub, d * TOK), F32)
    o, l = pl.pallas_call(
        body, name=name, grid=(d, NHB, nb), in_specs=[cur, cur, prev, cur, prev], out_specs=[cur, cur],
        out_shape=[sds, sds], compiler_params=_cparams("parallel", "parallel", "parallel"),
    )(view(q), view(k), view(k), view(v), view(v))
    return o.reshape(T, TOK), l.reshape(T, TOK)


def _attn_combine(os_, ls_, name):
    def fn(*v):
        n = len(v) // 2
        o, l = v[:n], v[n:]
        m = l[0]
        for t in l[1:]:
            m = jnp.maximum(m, t)
        e = [jnp.exp(t - m) for t in l]
        den = e[0]
        for t in e[1:]:
            den = den + t
        y = e[0] * o[0]
        for t, u in zip(e[1:], o[1:]):
            y = y + t * u
        return y / den, m + jnp.log(den)
    return _elementwise(fn, list(os_) + list(ls_), [], [F32, F32], name)


def _attn_bwd(q, k, v, y, lse, dy, gi, window, d, name):
    T, TOK = q.shape
    E_ = ATTN_HEAD_DIM
    H = TOK // E_
    n_alibi = len(DILATED_GROUPS) * H
    if d > 1:
        return _attn_bwd_strided(q, k, v, y, lse, dy, gi, d, name)
    n_sub = T // d
    nb = n_sub // ATTN_BLOCK
    HB = _heads_per_block(H)
    NHB = H // HB
    hbw = HB * E_
    scale = E_ ** -0.5
    Bq = ATTN_BLOCK

    def body(q0_ref, q1_ref, kp_ref, k0_ref, vp_ref, v0_ref, do0_ref, do1_ref, y0_ref, y1_ref, l0_ref, l1_ref,
             dq_ref, dk_ref, dv_ref):
        hb = pl.program_id(1)
        masks = _attn_masks(pl.program_id(2), nb)
        slopes = [_alibi_slope(gi * H + hb * HB + hh, n_alibi) * float(d) for hh in range(HB)]
        cols = lambda hh: slice(hh * E_, (hh + 1) * E_)

        def load(hh):
            return (tuple(ref[:, cols(hh)].astype(BF16) for ref in (q0_ref, q1_ref, kp_ref, k0_ref, vp_ref, v0_ref))
                    + tuple(ref[:, cols(hh)] for ref in (do0_ref, do1_ref, y0_ref, y1_ref))
                    + tuple(jnp.max(ref[:, cols(hh)], axis=-1, keepdims=True) for ref in (l0_ref, l1_ref)))

        def store(hh, dq, dk, dv):
            dq_ref[:, cols(hh)] = dq.astype(BF16)
            dk_ref[:, cols(hh)] = dk.astype(BF16)
            dv_ref[:, cols(hh)] = dv.astype(BF16)

        _attn_units_bwd(load, HB, slopes, masks, scale, store)

    cur = pl.BlockSpec((Bq, hbw), lambda r, h, b: (b, r * NHB + h))
    prev = pl.BlockSpec((Bq, hbw), lambda r, h, b: (jnp.maximum(b - 1, 0), r * NHB + h))
    nxt = pl.BlockSpec((Bq, hbw), lambda r, h, b: (jnp.minimum(b + 1, nb - 1), r * NHB + h))
    view = lambda t: t.reshape(n_sub, d * TOK)
    sds = jax.ShapeDtypeStruct((n_sub, d * TOK), BF16)
    dq, dk, dv = pl.pallas_call(
        body, name=name, grid=(d, NHB, nb),
        in_specs=[cur, nxt, prev, cur, prev, cur, cur, nxt, cur, nxt, cur, nxt],
        out_specs=[cur, cur, cur], out_shape=[sds, sds, sds],
        compiler_params=_cparams("parallel", "parallel", "parallel"),
    )(view(q), view(q), view(k), view(k), view(v), view(v), view(dy), view(dy), view(y), view(y), view(lse), view(lse))
    return dq.reshape(T, TOK), dk.reshape(T, TOK), dv.reshape(T, TOK)


_FLIPS = {
    "xy": [(1, 0, 0), (0, 1, 0), (1, 1, 0)],
    "c": [(0, 0, 1)],
    "xyc": [(dx, dy, dc) for dx in (0, 1) for dy in (0, 1) for dc in (0, 1) if (dx, dy, dc) != (0, 0, 0)],
}


def _comm_parts(group, srcs, modes, handshake):
    flips = _FLIPS[group]
    F_ = len(flips)
    P_ = F_ + 1
    n = len(srcs)

    def gidx(px, py, pc):
        if group == "xy":
            return 2 * px + py
        if group == "c":
            return pc
        return 4 * px + 2 * py + pc

    def body(*refs):
        src_refs, out_refs = refs[:n], refs[n:2 * n]
        send_sems, recv_sems, loc_sems = refs[2 * n:]
        x, y, c = lax.axis_index("x"), lax.axis_index("y"), lax.axis_index("c")
        me = gidx(x, y, c)
        peers = [(1 - x if dx else x, 1 - y if dy else y, 1 - c if dc else c) for dx, dy, dc in flips]
        if handshake:
            barrier = pltpu.get_barrier_semaphore()
            for peer in peers:
                pl.semaphore_signal(barrier, inc=1, device_id=peer, device_id_type=MESH)
            pl.semaphore_wait(barrier, F_)
        local, remote = [], []
        for i in range(n):
            mode = modes[i]
            if mode != "swap":
                mine = pltpu.make_async_copy(src_refs[i] if mode == "gather" else src_refs[i].at[me],
                                             out_refs[i].at[me], loc_sems.at[i])
                mine.start()
                local.append(mine)
            for f, peer in enumerate(peers):
                cp = pltpu.make_async_remote_copy(
                    src_ref=src_refs[i].at[gidx(*peer)] if mode == "a2a" else src_refs[i],
                    dst_ref=out_refs[i] if mode == "swap" else out_refs[i].at[me],
                    send_sem=send_sems.at[i * F_ + f], recv_sem=recv_sems.at[i * F_ + f],
                    device_id=peer, device_id_type=MESH)
                cp.start()
                remote.append(cp)
        for cp in local:
            cp.wait()
        for cp in remote:
            cp.wait()

    out_shape = []
    for s, mode in zip(srcs, modes):
        assert mode != "swap" or F_ == 1
        shp = (P_,) + tuple(s.shape) if mode == "gather" else tuple(s.shape)
        out_shape.append(jax.ShapeDtypeStruct(shp, s.dtype))
    sems = [pltpu.SemaphoreType.DMA((n * F_,)), pltpu.SemaphoreType.DMA((n * F_,)), pltpu.SemaphoreType.DMA((n,))]
    return body, out_shape, sems


def _comm(name, group, srcs, modes):
    n = len(srcs)
    body, out_shape, sems = _comm_parts(group, srcs, modes, handshake=False)
    anyspec = pl.BlockSpec(memory_space=pl.ANY)
    return pl.pallas_call(body, name=name, in_specs=[anyspec] * n, out_specs=[anyspec] * n, out_shape=out_shape,
                          scratch_shapes=sems)(*srcs)


def _comm_async(name, collective_id, group, srcs, modes):
    body, out_shape, sems = _comm_parts(group, srcs, modes, handshake=True)
    return pl.kernel(body, name=name, out_type=out_shape,
                     mesh=plsc.ScalarSubcoreMesh(axis_name="sequencer", num_cores=1), scratch_types=sems,
                     compiler_params=pltpu.CompilerParams(collective_id=collective_id))(*srcs)


def _dims(D):
    MIX = 2 * D
    MW = MIX // 4
    TOK = MIX - MW
    H = TOK // SSD_HEAD_DIM
    CONV = TOK + 2 * SSD_GROUPS * SSD_STATE
    return dict(MIX=MIX, MW=MW, TOK=TOK, H=H, CONV=CONV)


def _proj_chain(dsegs, wsegs, name):
    acc = None
    for n, (ds, ws) in enumerate(zip(dsegs, wsegs)):
        acc = _mm(ds, ws, "nt", F32, f"{name}_dh{n}", add=acc)
    return acc


def _pad_lanes(a, width=LANES):
    return jnp.pad(a, [(0, 0)] * (a.ndim - 1) + [(0, width - a.shape[-1])])


def _take_cols(parts, a, b):
    out, o = [], 0
    for part in parts:
        w = part.shape[1]
        lo, hi = max(a, o), min(b, o + w)
        if lo < hi:
            out.append(part[:, lo - o:hi - o])
        o += w
    return out[0] if len(out) == 1 else jnp.concatenate(out, axis=1)


def _heads_to_groups(a, G, HPG):
    return jnp.stack([_pad_lanes(a[:, g * HPG:(g + 1) * HPG]) for g in range(G)])


def _groups_to_heads(a, HPG):
    return jnp.concatenate([a[g, :, :HPG] for g in range(a.shape[0])], axis=1)


def _ssd_layer_fwd(x, kv, p, li):
    T, D = x.shape
    dm = _dims(D)
    TOK, MW, H, CONV = dm["TOK"], dm["MW"], dm["H"], dm["CONV"]
    G = SSD_GROUPS
    HPG = H // G
    w = p["w_in"]
    cuts = [0, CONV, CONV + H, CONV + H + MW, CONV + H + MW + TOK, CONV + H + MW + TOK + MW]
    segs = {k: _take_cols(w, cuts[n], cuts[n + 1]) for n, k in enumerate(["xbc", "dt", "qm", "zt", "zm"])}
    segs["dt"] = _pad_lanes(segs["dt"])
    h = _rms_fwd(x, p["norm_g"], f"l{li}_rms")
    pr = {k: _mm(h, ws, "nn", BF16 if k == "qm" else F32, f"l{li}_in_{k}") for k, ws in segs.items()}
    act = _conv_fwd(pr["xbc"], p["conv_w"], p["conv_b"], f"l{li}_conv")
    raw_g = _heads_to_groups(pr["dt"][:, :H], G, HPG)
    hp = lambda a: _pad_lanes(a.reshape(G, 1, HPG))
    bias_g, alog_g, dsk_g = hp(p["dt_bias"]), hp(p["a_log"]), hp(p["d_skip"])
    y, hprev = _ssd_fwd(act, raw_g, bias_g, alog_g, dsk_g, TOK, f"l{li}_ssd")
    ymem = _memattn_fwd(pr["qm"], kv, f"l{li}_mem")
    gt = _gate_norm_fwd(y, pr["zt"], p["ssd_norm_g"], G, f"l{li}_gate_tok")
    gm = _gate_fwd(ymem, pr["zm"], f"l{li}_gate_mem")
    wo = p["w_out"]
    out = _mm(gt, wo[:TOK], "nn", F32, f"l{li}_out_tok", add=x)
    out = _mm(gm, wo[TOK:], "nn", F32, f"l{li}_out_mem", add=out)
    saved = dict(x=x, h=h, pr=pr, act=act, raw_g=raw_g, par=(bias_g, alog_g, dsk_g), y=y, hprev=hprev, ymem=ymem,
                 gt=gt, gm=gm, segs=segs)
    return out, saved


def _ssd_layer_bwd(dout, doutb, kv, p, s, li):
    x = s["x"]
    T, D = x.shape
    dm = _dims(D)
    TOK, MW, H, CONV = dm["TOK"], dm["MW"], dm["H"], dm["CONV"]
    G = SSD_GROUPS
    HPG = H // G
    wo = p["w_out"]
    pr = s["pr"]
    dgt = _mm(doutb, wo[:TOK], "nt", F32, f"l{li}_dgt")
    dgm = _mm(doutb, wo[TOK:], "nt", F32, f"l{li}_dgm")
    dwo = jnp.concatenate([_mm(s["gt"], doutb, "tn", BF16, f"l{li}_dwo_tok"),
                           _mm(s["gm"], doutb, "tn", BF16, f"l{li}_dwo_mem")], axis=0)
    dy, dzt, dng = _gate_norm_bwd(s["y"], pr["zt"], p["ssd_norm_g"], dgt, G, f"l{li}_gate_tok_b")
    dymem, dzm = _gate_bwd(s["ymem"], pr["zm"], dgm, f"l{li}_gate_mem_b")
    dqm, dkv = _memattn_bwd(pr["qm"], kv, dymem, f"l{li}_mem_b")
    bias_g, alog_g, dsk_g = s["par"]
    dxs, dB, dC, draw_g, dalog, dbias, ddsk = _ssd_bwd(s["act"], s["raw_g"], bias_g, alog_g, dsk_g, s["hprev"], dy, TOK,
                                                      f"l{li}_ssd_b")
    dact = jnp.concatenate([dxs, dB, dC], axis=1)
    dpre, dconv_w, dconv_b = _conv_bwd_pre(pr["xbc"], p["conv_w"], p["conv_b"], dact, f"l{li}_conv_b1")
    dxbc = _conv_bwd_in(dpre, p["conv_w"], f"l{li}_conv_b2")
    draw = _pad_lanes(_groups_to_heads(draw_g, HPG)).astype(BF16)
    dsegs = dict(xbc=dxbc, dt=draw, qm=dqm, zt=dzt, zm=dzm)
    keys = ["xbc", "dt", "qm", "zt", "zm"]
    dh = _proj_chain([dsegs[k] for k in keys], [s["segs"][k] for k in keys], f"l{li}")
    dws = {k: _mm(s["h"], dsegs[k], "tn", BF16, f"l{li}_dwin_{k}") for k in keys}
    dws["dt"] = dws["dt"][:, :H]
    dwin = [dws[k] for k in keys]
    dh, dwin, dwo = lax.optimization_barrier((dh, dwin, dwo))
    dx, dxb, dnorm = _rms_bwd(x, p["norm_g"], dh, dout, f"l{li}_rms_b")
    unhead = lambda a: a[:, 0, :HPG].reshape(H)
    grads = dict(norm_g=dnorm, w_in=dwin, conv_w=dconv_w, conv_b=dconv_b, dt_bias=unhead(dbias), a_log=unhead(dalog),
                 d_skip=unhead(ddsk), ssd_norm_g=dng, w_out=dwo)
    return dx, dxb, dkv, grads


def _attn_layer_fwd(x, kv, p, li):
    T, D = x.shape
    dm = _dims(D)
    TOK, MW = dm["TOK"], dm["MW"]
    w = p["w_in"]
    ng = len(DILATED_GROUPS)
    segs = {}
    for g in range(ng):
        for n, nm in enumerate("qkv"):
            c0 = g * 3 * TOK + n * TOK
            segs[f"{nm}{g}"] = _take_cols(w, c0, c0 + TOK)
    c0 = ng * 3 * TOK
    segs["qm"] = _take_cols(w, c0, c0 + MW)
    segs["zt"] = _take_cols(w, c0 + MW, c0 + MW + TOK)
    segs["zm"] = _take_cols(w, c0 + MW + TOK, c0 + MW + TOK + MW)
    h = _rms_fwd(x, p["norm_g"], f"l{li}_rms")
    dense = {"qm"} | {f"{nm}{g}" for g, (_, d) in enumerate(DILATED_GROUPS) if d == 1 for nm in "qkv"}
    pr = {k: _mm(h, ws, "nn", BF16 if k in dense else F32, f"l{li}_in_{k}") for k, ws in segs.items()}
    os_, ls_ = [], []
    for g, (window, d) in enumerate(DILATED_GROUPS):
        o, l = _attn_fwd(pr[f"q{g}"], pr[f"k{g}"], pr[f"v{g}"], g, window, d, f"l{li}_attn{g}")
        os_.append(o)
        ls_.append(l)
    ytok, lse = _attn_combine(os_, ls_, f"l{li}_combine")
    ymem = _memattn_fwd(pr["qm"], kv, f"l{li}_mem")
    gt = _gate_fwd(ytok, pr["zt"], f"l{li}_gate_tok")
    gm = _gate_fwd(ymem, pr["zm"], f"l{li}_gate_mem")
    wo = p["w_out"]
    out = _mm(gt, wo[:TOK], "nn", F32, f"l{li}_out_tok", add=x)
    out = _mm(gm, wo[TOK:], "nn", F32, f"l{li}_out_mem", add=out)
    saved = dict(x=x, h=h, pr=pr, ytok=ytok, lse=lse, ymem=ymem, gt=gt, gm=gm, segs=segs)
    return out, saved


def _attn_layer_bwd(dout, doutb, kv, p, s, li):
    x = s["x"]
    T, D = x.shape
    dm = _dims(D)
    TOK, MW = dm["TOK"], dm["MW"]
    wo = p["w_out"]
    pr = s["pr"]
    dgt = _mm(doutb, wo[:TOK], "nt", F32, f"l{li}_dgt")
    dgm = _mm(doutb, wo[TOK:], "nt", F32, f"l{li}_dgm")
    dwo = jnp.concatenate([_mm(s["gt"], doutb, "tn", BF16, f"l{li}_dwo_tok"),
                           _mm(s["gm"], doutb, "tn", BF16, f"l{li}_dwo_mem")], axis=0)
    dytok, dzt = _gate_bwd(s["ytok"], pr["zt"], dgt, f"l{li}_gate_tok_b")
    dymem, dzm = _gate_bwd(s["ymem"], pr["zm"], dgm, f"l{li}_gate_mem_b")
    dqm, dkv = _memattn_bwd(pr["qm"], kv, dymem, f"l{li}_mem_b")
    dsegs = {}
    for g, (window, d) in enumerate(DILATED_GROUPS):
        dq, dk, dv = _attn_bwd(pr[f"q{g}"], pr[f"k{g}"], pr[f"v{g}"], s["ytok"], s["lse"], dytok, g, window, d,
                               f"l{li}_attn{g}_b")
        dsegs[f"q{g}"], dsegs[f"k{g}"], dsegs[f"v{g}"] = dq, dk, dv
    dsegs["qm"], dsegs["zt"], dsegs["zm"] = dqm, dzt, dzm
    keys = list(s["segs"].keys())
    dh = _proj_chain([dsegs[k] for k in keys], [s["segs"][k] for k in keys], f"l{li}")
    dwin = [_mm(s["h"], dsegs[k], "tn", BF16, f"l{li}_dwin_{k}") for k in keys]
    dh, dwin, dwo = lax.optimization_barrier((dh, dwin, dwo))
    dx, dxb, dnorm = _rms_bwd(x, p["norm_g"], dh, dout, f"l{li}_rms_b")
    return dx, dxb, dkv, dict(norm_g=dnorm, w_in=dwin, w_out=dwo)


def _local_step(x, mem, mem_n, tgt, mem_norm_g, final_norm_g, n_layers, layer_params, on_layer_grads):
    layers, kvs, saved = [], [], []
    for li in range(n_layers):
        p, x = layer_params(li, x)
        fwd = _ssd_layer_fwd if li % 2 == 0 else _attn_layer_fwd
        kv = _mm(mem_n, p["w_mem_kv"], "nn", F32, f"l{li}_kv")
        x, s = fwd(x, kv, p, li)
        layers.append(p)
        kvs.append(kv)
        saved.append(s)
    loss, dx, dxb, dfinal = _loss_head(x, final_norm_g, tgt, "loss_head")
    dmem_n = None
    for li in reversed(range(n_layers)):
        p = layers[li]
        bwd = _ssd_layer_bwd if li % 2 == 0 else _attn_layer_bwd
        dx, dxb, dkv, g = bwd(dx, dxb, kvs[li], p, saved[li], li)
        g["w_mem_kv"] = _mm(mem_n, dkv, "tn", BF16, f"l{li}_dwkv")
        dmem_n = _mm(dkv, p["w_mem_kv"], "nt", F32, f"l{li}_dmem", add=dmem_n)
        dx, dxb = on_layer_grads(li, g, dx, dxb)
    _, _, dmem_g = _rms_bwd(mem, mem_norm_g, dmem_n, None, "mem_rms_b")
    return loss, dx, dmem_g, dfinal


_SSD_SMALL = ["norm_g", "conv_w", "conv_b", "dt_bias", "a_log", "d_skip", "ssd_norm_g"]
_ATTN_SMALL = ["norm_g"]
_SSD_ORDER = ["norm_g", "w_in", "conv_w", "conv_b", "dt_bias", "a_log", "d_skip", "ssd_norm_g", "w_mem_kv", "w_out"]
_ATTN_ORDER = ["norm_g", "w_in", "w_mem_kv", "w_out"]


def _pack(arrs):
    flat = jnp.concatenate([a.reshape(-1).astype(F32) for a in arrs])
    n = flat.shape[0]
    pad = (-n) % (8 * LANES)
    return jnp.pad(flat, (0, pad)).reshape(-1, LANES)


def _unpack(mat, shapes):
    flat = mat.reshape(-1)
    out, o = [], 0
    for shp in shapes:
        n = math.prod(shp)
        out.append(flat[o:o + n].reshape(shp))
        o += n
    return out


def kernel(x, mem, mem_norm_g, final_norm_g, norm_g_0, w_in_0, conv_w_0, conv_b_0, dt_bias_0, a_log_0, d_skip_0, ssd_norm_g_0, w_mem_kv_0, w_out_0, norm_g_1, w_in_1, w_mem_kv_1, w_out_1, norm_g_2, w_in_2, conv_w_2, conv_b_2, dt_bias_2, a_log_2, d_skip_2, ssd_norm_g_2, w_mem_kv_2, w_out_2, norm_g_3, w_in_3, w_mem_kv_3, w_out_3, loss_target, m_mem_norm_g, m_final_norm_g, m_norm_g_0, m_w_in_0, m_conv_w_0, m_conv_b_0, m_dt_bias_0, m_a_log_0, m_d_skip_0, m_ssd_norm_g_0, m_w_mem_kv_0, m_w_out_0, m_norm_g_1, m_w_in_1, m_w_mem_kv_1, m_w_out_1, m_norm_g_2, m_w_in_2, m_conv_w_2, m_conv_b_2, m_dt_bias_2, m_a_log_2, m_d_skip_2, m_ssd_norm_g_2, m_w_mem_kv_2, m_w_out_2, m_norm_g_3, m_w_in_3, m_w_mem_kv_3, m_w_out_3, v_mem_norm_g, v_final_norm_g, v_norm_g_0, v_w_in_0, v_conv_w_0, v_conv_b_0, v_dt_bias_0, v_a_log_0, v_d_skip_0, v_ssd_norm_g_0, v_w_mem_kv_0, v_w_out_0, v_norm_g_1, v_w_in_1, v_w_mem_kv_1, v_w_out_1, v_norm_g_2, v_w_in_2, v_conv_w_2, v_conv_b_2, v_dt_bias_2, v_a_log_2, v_d_skip_2, v_ssd_norm_g_2, v_w_mem_kv_2, v_w_out_2, v_norm_g_3, v_w_in_3, v_w_mem_kv_3, v_w_out_3):
    a = dict(locals())
    names = ["mem_norm_g", "final_norm_g"]
    for li in range(DEPTH):
        names += [f"{k}_{li}" for k in (_SSD_ORDER if li % 2 == 0 else _ATTN_ORDER)]
    W = {n: a[n] for n in names}
    Mo = {n: a["m_" + n] for n in names}
    Vo = {n: a["v_" + n] for n in names}
    NX = 4
    chip = 2 * lax.axis_index("x") + lax.axis_index("y")

    gathered = []
    mem_n = None
    for li in range(DEPTH):
        mats = [W[f"w_in_{li}"], W[f"w_mem_kv_{li}"], W[f"w_out_{li}"]]
        if li == 1:
            mem_n = _rms_fwd(mem[0], W["mem_norm_g"], "mem_rms")
            later = [[W[f"{k}_{lj}"] for k in ("w_in", "w_mem_kv", "w_out")] for lj in range(1, DEPTH)]
            later, mem_n = lax.optimization_barrier((later, mem_n))
        if li >= 1:
            mats = later[li - 1]
        srcs = [m.astype(BF16) for m in mats]
        if li % 2 == 0:
            srcs.append(W[f"conv_w_{li}"])
        gathered.append(_comm_async(f"gather_w{li}", li, "xy", srcs, ["gather"] * len(srcs)))

    def layer_params(li, xin):
        got = gathered[li]
        if li > 0:
            got, xin = lax.optimization_barrier((got, xin))
        rows = lambda g: g.reshape((-1,) + g.shape[2:])
        parts = lambda g: [g[k] for k in range(NX)]
        p = dict(w_in=parts(got[0]), w_mem_kv=rows(got[1]), w_out=rows(got[2]), norm_g=W[f"norm_g_{li}"])
        if li % 2 == 0:
            p.update(conv_w=jnp.concatenate(parts(got[3]), axis=1), conv_b=W[f"conv_b_{li}"],
                     dt_bias=W[f"dt_bias_{li}"], a_log=W[f"a_log_{li}"], d_skip=W[f"d_skip_{li}"],
                     ssd_norm_g=W[f"ssd_norm_g_{li}"])
        return p, xin

    G, Dl, Mn, Vn = {}, {}, {}, {}
    grads = [None] * DEPTH
    in_flight = []

    def finish_exchange(li, got):
        parts = [_sum_lead(t, f"l{li}_gsum{n}") for n, t in enumerate(got)]
        theirs = _comm_async(f"swap_g{li}", 2 * DEPTH + li, "c", parts, ["swap"] * 3)
        for nm, mine, other in zip(["w_in", "w_mem_kv", "w_out"], parts, theirs):
            key = f"{nm}_{li}"
            G[key], Dl[key], Mn[key], Vn[key] = _adamw(W[key], Mo[key], Vo[key], [mine, other], f"adamw_{key}")

    def on_layer_grads(li, g, dx, dxb):
        grads[li] = g
        dwin = g["w_in"]
        cw = sum(t.shape[1] for t in dwin) // NX
        chunks = [jnp.stack([_take_cols(dwin, k * cw, (k + 1) * cw) for k in range(NX)]),
                  g["w_mem_kv"].reshape((NX, -1) + g["w_mem_kv"].shape[1:]),
                  g["w_out"].reshape((NX, -1) + g["w_out"].shape[1:])]
        prev = in_flight.pop() if in_flight else None
        pgot = prev[1] if prev else []
        chunks, pgot, dx, dxb = lax.optimization_barrier((chunks, pgot, dx, dxb))
        in_flight.append((li, _comm_async(f"xchg_g{li}", DEPTH + li, "xy", chunks, ["a2a"] * 3)))
        if prev:
            finish_exchange(prev[0], pgot)
        return dx, dxb

    loss_l, dx, dmem_g, dfinal = _local_step(x[0], mem[0], mem_n, loss_target[0], W["mem_norm_g"], W["final_norm_g"],
                                             DEPTH, layer_params, on_layer_grads)
    last_li, last_got = in_flight.pop()
    updates = (G, Dl, Mn, Vn)
    last_got, tied = lax.optimization_barrier((last_got, [dict(u) for u in updates]))
    for u, t in zip(updates, tied):
        u.update(t)
    finish_exchange(last_li, last_got)
    loss = lax.psum(loss_l, ("x", "y", "c"))

    small_names = ["mem_norm_g", "final_norm_g"]
    small_grads = [dmem_g, dfinal]
    for li in range(DEPTH):
        for k in (_SSD_SMALL if li % 2 == 0 else _ATTN_SMALL):
            small_names.append(f"{k}_{li}")
            small_grads.append(grads[li][k])
    shapes = [tuple(t.shape) for t in small_grads]
    allg = _comm("gather_small", "xyc", [_pack(small_grads)], ["gather"])[0]
    gsum = _unpack(_sum_lead(allg, "small_gsum"), shapes)
    small_w, small_m, small_v, small_g = [], [], [], []
    for nme, gv in zip(small_names, gsum):
        if nme.startswith("conv_w"):
            cw = W[nme].shape[1]
            gv = lax.dynamic_slice_in_dim(gv, chip * cw, cw, axis=1)
        small_g.append(gv)
        small_w.append(W[nme])
        small_m.append(Mo[nme])
        small_v.append(Vo[nme])
    sshapes = [tuple(t.shape) for t in small_g]
    res = _adamw(_pack(small_w), _pack(small_m), _pack(small_v), [_pack(small_g)], "adamw_small")
    for dst, mat in zip((G, Dl, Mn, Vn), res):
        for nme, t in zip(small_names, _unpack(mat, sshapes)):
            dst[nme] = t

    return (loss, dx[None], *[G[n] for n in names], *[Dl[n] for n in names], *[Mn[n] for n in names],
            *[Vn[n] for n in names])
```

```python
import functools
import math

import jax
import jax.numpy as jnp
from jax import lax
from jax.experimental import pallas as pl
from jax.experimental.pallas import tpu as pltpu
from jax.experimental.pallas import tpu_sc as plsc

F32 = jnp.float32
BF16 = jnp.bfloat16

EPS = 1e-6
MEM_HEADS = 4
SSD_HEAD_DIM = 64
SSD_GROUPS = 8
SSD_STATE = 128
SSD_CONV = 4
SSD_CHUNK = 128
ATTN_HEAD_DIM = 128
ATTN_BLOCK = 128
DILATED_GROUPS = ((128, 1), (512, 4), (2048, 16))
ALIBI_MAX_EXP = 8.0
DEPTH = 4

ADAM_LR = 0.001
ADAM_B1 = 0.9
ADAM_B2 = 0.999
ADAM_EPS = 1e-08
ADAM_WD = 0.01
ADAM_STEP = 10

LANES = 128
VMEM_LIMIT_BYTES = 48 * 1024 * 1024
NEG = -1e30
MESH = pl.DeviceIdType.MESH


def _cparams(*sem):
    return pltpu.CompilerParams(dimension_semantics=tuple(sem), vmem_limit_bytes=VMEM_LIMIT_BYTES)


def _tile(dim, pref, unit=LANES):
    if dim <= pref:
        return dim
    t = (pref // unit) * unit
    while t >= unit:
        if dim % t == 0:
            return t
        t -= unit
    return dim


def _ew_tiles(rows, cols):
    tc = cols if (cols % LANES != 0 or cols <= 2048) else _tile(cols, 2048)
    tr = rows
    while tr * tc > 256 * 1024 and tr % 2 == 0 and (tr // 2) % 8 == 0:
        tr //= 2
    return tr, tc


def _sigmoid(v):
    return 1.0 / (1.0 + jnp.exp(-v))


def _dot(a, b):
    return jnp.dot(a, b, preferred_element_type=F32)


def _dot_nt(a, b):
    return lax.dot_general(a, b, (((1,), (1,)), ((), ())), preferred_element_type=F32)


def _dot_tn(a, b):
    return lax.dot_general(a, b, (((0,), (0,)), ((), ())), preferred_element_type=F32)


def _split3(v):
    hi = v.astype(BF16)
    r = v - hi.astype(F32)
    mid = r.astype(BF16)
    lo = (r - mid.astype(F32)).astype(BF16)
    return hi, mid, lo


def _xdot(v, onehot):
    hi, mid, lo = _split3(v)
    return _dot(hi, onehot) + _dot(mid, onehot) + _dot(lo, onehot)


def _xdot_l(onehot, v):
    hi, mid, lo = _split3(v)
    return _dot(onehot, hi) + _dot(onehot, mid) + _dot(onehot, lo)


def _mm(a, b, mode, out_dtype, name, add=None):
    if mode == "nn":
        M, K = a.shape
        N = b.shape[1]
    elif mode == "nt":
        M, K = a.shape
        N = b.shape[0]
    else:
        K, M = a.shape
        N = b.shape[1]
    tm, tn, tk = (2048, 1024, 1024) if mode == "tn" else (512, 1024, 3072)
    tm, tn, tk = _tile(M, tm, 8 if M < LANES else LANES), _tile(N, tn), _tile(K, tk)
    nk = K // tk
    has_add = add is not None

    def product(a_ref, b_ref):
        av = a_ref[...].astype(BF16)
        bv = b_ref[...].astype(BF16)
        if mode == "nn":
            return _dot(av, bv)
        if mode == "nt":
            return _dot_nt(av, bv)
        return _dot_tn(av, bv)

    def body(*refs):
        a_ref, b_ref = refs[:2]
        add_ref = refs[2] if has_add else None
        o_ref = refs[3] if has_add else refs[2]

        def finish(r):
            if has_add:
                r = r + add_ref[...]
            o_ref[...] = r.astype(out_dtype)

        if nk == 1:
            finish(product(a_ref, b_ref))
            return
        acc = refs[-1]
        k = pl.program_id(2)

        @pl.when(k == 0)
        def _():
            acc[...] = product(a_ref, b_ref)

        @pl.when((k > 0) & (k < nk - 1))
        def _():
            acc[...] += product(a_ref, b_ref)

        @pl.when(k == nk - 1)
        def _():
            finish(acc[...] + product(a_ref, b_ref))

    if mode == "nn":
        a_spec = pl.BlockSpec((tm, tk), lambda j, i, k: (i, k))
        b_spec = pl.BlockSpec((tk, tn), lambda j, i, k: (k, j))
    elif mode == "nt":
        a_spec = pl.BlockSpec((tm, tk), lambda j, i, k: (i, k))
        b_spec = pl.BlockSpec((tn, tk), lambda j, i, k: (j, k))
    else:
        a_spec = pl.BlockSpec((tk, tm), lambda j, i, k: (k, i))
        b_spec = pl.BlockSpec((tk, tn), lambda j, i, k: (k, j))
    o_spec = pl.BlockSpec((tm, tn), lambda j, i, k: (i, j))
    in_specs = [a_spec, b_spec] + ([o_spec] if has_add else [])
    args = (a, b) + ((add,) if has_add else ())
    return pl.pallas_call(
        body, name=name, grid=(N // tn, M // tm, nk), in_specs=in_specs, out_specs=o_spec,
        out_shape=jax.ShapeDtypeStruct((M, N), out_dtype),
        scratch_shapes=[pltpu.VMEM((tm, tn), F32)] if nk > 1 else [],
        compiler_params=_cparams("parallel", "parallel", "arbitrary"),
    )(*args)


def _rms_fwd(x, g, name):
    R, Dm = x.shape
    tr = _tile(R, 256, 8)

    def body(x_ref, g_ref, o_ref):
        xv = x_ref[...]
        r = lax.rsqrt(jnp.mean(xv * xv, axis=-1, keepdims=True) + EPS)
        o_ref[...] = (xv * r * g_ref[...]).astype(BF16)

    return pl.pallas_call(
        body, name=name, grid=(R // tr,),
        in_specs=[pl.BlockSpec((tr, Dm), lambda i: (i, 0)), pl.BlockSpec((1, Dm), lambda i: (0, 0))],
        out_specs=pl.BlockSpec((tr, Dm), lambda i: (i, 0)),
        out_shape=jax.ShapeDtypeStruct((R, Dm), BF16),
        compiler_params=_cparams("parallel"),
    )(x, g.reshape(1, Dm))


def _rms_bwd(x, g, dh, dres, name):
    R, Dm = x.shape
    tr = _tile(R, 256, 8)
    has_res = dres is not None

    def body(*refs):
        if has_res:
            x_ref, g_ref, dh_ref, dres_ref, dx_ref, dxb_ref, dg_ref = refs
        else:
            x_ref, g_ref, dh_ref, dx_ref, dxb_ref, dg_ref = refs
        xv = x_ref[...]
        r = lax.rsqrt(jnp.mean(xv * xv, axis=-1, keepdims=True) + EPS)
        xhat = xv * r
        dhv = dh_ref[...]
        dxhat = dhv * g_ref[...]
        dx = r * (dxhat - xhat * jnp.mean(dxhat * xhat, axis=-1, keepdims=True))
        if has_res:
            dx = dx + dres_ref[...]
        dx_ref[...] = dx
        dxb_ref[...] = dx.astype(BF16)

        @pl.when(pl.program_id(0) == 0)
        def _():
            dg_ref[...] = jnp.zeros_like(dg_ref)

        dg_ref[...] += jnp.sum(dhv * xhat, axis=0, keepdims=True)

    row = pl.BlockSpec((tr, Dm), lambda i: (i, 0))
    vec = pl.BlockSpec((1, Dm), lambda i: (0, 0))
    in_specs = [row, vec, row] + ([row] if has_res else [])
    args = (x, g.reshape(1, Dm), dh) + ((dres,) if has_res else ())
    dx, dxb, dg = pl.pallas_call(
        body, name=name, grid=(R // tr,), in_specs=in_specs, out_specs=[row, row, vec],
        out_shape=[jax.ShapeDtypeStruct((R, Dm), F32), jax.ShapeDtypeStruct((R, Dm), BF16),
                   jax.ShapeDtypeStruct((1, Dm), F32)],
        compiler_params=_cparams("arbitrary"),
    )(*args)
    return dx, dxb, dg.reshape(Dm)


def _loss_head(x, g, tgt, name):
    R, Dm = x.shape
    tr = _tile(R, 256, 8)

    def body(x_ref, g_ref, t_ref, loss_ref, dx_ref, dxb_ref, dg_ref):
        xv = x_ref[...]
        gv = g_ref[...]
        r = lax.rsqrt(jnp.mean(xv * xv, axis=-1, keepdims=True) + EPS)
        xhat = xv * r
        e = xhat * gv - t_ref[...]
        part = jnp.sum(jnp.mean(e * e, axis=-1, keepdims=True), axis=0, keepdims=True) * 0.5
        dy = e * (1.0 / Dm)
        dxhat = dy * gv
        dx = r * (dxhat - xhat * jnp.mean(dxhat * xhat, axis=-1, keepdims=True))
        dx_ref[...] = dx
        dxb_ref[...] = dx.astype(BF16)

        @pl.when(pl.program_id(0) == 0)
        def _():
            dg_ref[...] = jnp.zeros_like(dg_ref)
            loss_ref[...] = jnp.zeros_like(loss_ref)

        dg_ref[...] += jnp.sum(dy * xhat, axis=0, keepdims=True)
        loss_ref[...] += jnp.broadcast_to(part, loss_ref.shape)

    row = pl.BlockSpec((tr, Dm), lambda i: (i, 0))
    vec = pl.BlockSpec((1, Dm), lambda i: (0, 0))
    lsp = pl.BlockSpec((1, LANES), lambda i: (0, 0))
    loss, dx, dxb, dg = pl.pallas_call(
        body, name=name, grid=(R // tr,), in_specs=[row, vec, row], out_specs=[lsp, row, row, vec],
        out_shape=[jax.ShapeDtypeStruct((1, LANES), F32), jax.ShapeDtypeStruct((R, Dm), F32),
                   jax.ShapeDtypeStruct((R, Dm), BF16), jax.ShapeDtypeStruct((1, Dm), F32)],
        compiler_params=_cparams("arbitrary"),
    )(x, g.reshape(1, Dm), tgt)
    return loss[0, 0], dx, dxb, dg.reshape(Dm)


def _elementwise(fn, mats, vecs, out_dtypes, name):
    R, C = mats[0].shape
    tr, tc = _ew_tiles(R, C)
    nm, nv, no = len(mats), len(vecs), len(out_dtypes)

    def body(*refs):
        ins = [r[...] for r in refs[:nm + nv]]
        outs = fn(*ins)
        for o_ref, o in zip(refs[nm + nv:], outs):
            o_ref[...] = o.astype(o_ref.dtype)

    blk = pl.BlockSpec((tr, tc), lambda i, j: (i, j))
    vblk = pl.BlockSpec((1, tc), lambda i, j: (0, j))
    res = pl.pallas_call(
        body, name=name, grid=(R // tr, C // tc),
        in_specs=[blk] * nm + [vblk] * nv, out_specs=[blk] * no,
        out_shape=[jax.ShapeDtypeStruct((R, C), dt) for dt in out_dtypes],
        compiler_params=_cparams("parallel", "parallel"),
    )(*mats, *[v.reshape(1, C) for v in vecs])
    return res


def _sum_lead(arr, name):
    P_, R, C = arr.shape
    tr, tc = _ew_tiles(R, C)

    def body(a_ref, o_ref):
        s = a_ref[0].astype(F32)
        for p in range(1, P_):
            s = s + a_ref[p].astype(F32)
        o_ref[...] = s

    return pl.pallas_call(
        body, name=name, grid=(R // tr, C // tc),
        in_specs=[pl.BlockSpec((P_, tr, tc), lambda i, j: (0, i, j))],
        out_specs=pl.BlockSpec((tr, tc), lambda i, j: (i, j)),
        out_shape=jax.ShapeDtypeStruct((R, C), F32),
        compiler_params=_cparams("parallel", "parallel"),
    )(arr)


def _adamw(w, m, v, gparts, name):
    P_ = len(gparts)
    R, C = w.shape
    tr, tc = _ew_tiles(R, C)
    c1 = 1.0 / (1.0 - ADAM_B1 ** ADAM_STEP)
    c2 = 1.0 / (1.0 - ADAM_B2 ** ADAM_STEP)

    def body(w_ref, m_ref, v_ref, *rest):
        g_refs, (go_ref, d_ref, mo_ref, vo_ref) = rest[:P_], rest[P_:]
        g = g_refs[0][...]
        for g_ref in g_refs[1:]:
            g = g + g_ref[...]
        mn = ADAM_B1 * m_ref[...] + (1.0 - ADAM_B1) * g
        vn = ADAM_B2 * v_ref[...] + (1.0 - ADAM_B2) * (g * g)
        m_hat = mn * c1
        v_hat = vn * c2
        d_ref[...] = -ADAM_LR * (m_hat / (jnp.sqrt(v_hat) + ADAM_EPS) + ADAM_WD * w_ref[...])
        go_ref[...] = g
        mo_ref[...] = mn
        vo_ref[...] = vn

    blk = pl.BlockSpec((tr, tc), lambda i, j: (i, j))
    sds = jax.ShapeDtypeStruct((R, C), F32)
    return pl.pallas_call(
        body, name=name, grid=(R // tr, C // tc),
        in_specs=[blk] * (3 + P_), out_specs=[blk] * 4, out_shape=[sds] * 4,
        compiler_params=_cparams("parallel", "parallel"),
    )(w, m, v, *gparts)


SUBLANES = 8


def _conv_pre(u, up, w_ref, b, first):
    tr = u.shape[0]
    rows = lax.broadcasted_iota(jnp.int32, u.shape, 0)
    keep = 1.0 - first.astype(F32)
    acc = b + w_ref[SSD_CONV - 1:SSD_CONV, :] * u
    shifted = []
    for j in range(1, SSD_CONV):
        su = pltpu.roll(u, j, 0)
        sp = jnp.tile(pltpu.roll(up, j, 0) * keep, (tr // SUBLANES, 1))
        sh = jnp.where(rows < j, sp, su)
        shifted.append(sh)
        acc = acc + w_ref[SSD_CONV - 1 - j:SSD_CONV - j, :] * sh
    return acc, shifted


def _conv_fwd(u, w, b, name):
    T, C = u.shape
    tr, tc = _tile(T, 256, 8), _tile(C, 1024)

    def body(u_ref, up_ref, w_ref, b_ref, o_ref):
        pre, _ = _conv_pre(u_ref[...], up_ref[...], w_ref, b_ref[...], pl.program_id(0) == 0)
        o_ref[...] = pre * _sigmoid(pre)

    return pl.pallas_call(
        body, name=name, grid=(T // tr, C // tc),
        in_specs=[pl.BlockSpec((tr, tc), lambda i, j: (i, j)),
                  pl.BlockSpec((SUBLANES, tc), lambda i, j: (jnp.maximum(i * (tr // SUBLANES) - 1, 0), j)),
                  pl.BlockSpec((SSD_CONV, tc), lambda i, j: (0, j)),
                  pl.BlockSpec((1, tc), lambda i, j: (0, j))],
        out_specs=pl.BlockSpec((tr, tc), lambda i, j: (i, j)),
        out_shape=jax.ShapeDtypeStruct((T, C), F32),
        compiler_params=_cparams("parallel", "parallel"),
    )(u, u, w, b.reshape(1, C))


def _conv_bwd_pre(u, w, b, dact, name):
    T, C = u.shape
    tr, tc = _tile(T, 256, 8), _tile(C, 1024)

    def body(u_ref, up_ref, w_ref, b_ref, da_ref, dp_ref, dw_ref, db_ref):
        i = pl.program_id(1)
        uv = u_ref[...]
        pre, shifted = _conv_pre(uv, up_ref[...], w_ref, b_ref[...], i == 0)
        sg = _sigmoid(pre)
        dpre = da_ref[...] * (sg * (1.0 + pre * (1.0 - sg)))
        dp_ref[...] = dpre

        @pl.when(i == 0)
        def _():
            dw_ref[...] = jnp.zeros_like(dw_ref)
            db_ref[...] = jnp.zeros_like(db_ref)

        db_ref[...] += jnp.sum(dpre, axis=0, keepdims=True)
        dw_ref[SSD_CONV - 1:SSD_CONV, :] += jnp.sum(dpre * uv, axis=0, keepdims=True)
        for j in range(1, SSD_CONV):
            dw_ref[SSD_CONV - 1 - j:SSD_CONV - j, :] += jnp.sum(dpre * shifted[j - 1], axis=0, keepdims=True)

    blk = pl.BlockSpec((tr, tc), lambda j, i: (i, j))
    dpre, dw, db = pl.pallas_call(
        body, name=name, grid=(C // tc, T // tr),
        in_specs=[blk, pl.BlockSpec((SUBLANES, tc), lambda j, i: (jnp.maximum(i * (tr // SUBLANES) - 1, 0), j)),
                  pl.BlockSpec((SSD_CONV, tc), lambda j, i: (0, j)),
                  pl.BlockSpec((1, tc), lambda j, i: (0, j)), blk],
        out_specs=[blk, pl.BlockSpec((SSD_CONV, tc), lambda j, i: (0, j)), pl.BlockSpec((1, tc), lambda j, i: (0, j))],
        out_shape=[jax.ShapeDtypeStruct((T, C), F32), jax.ShapeDtypeStruct((SSD_CONV, C), F32),
                   jax.ShapeDtypeStruct((1, C), F32)],
        compiler_params=_cparams("parallel", "arbitrary"),
    )(u, u, w, b.reshape(1, C), dact)
    return dpre, dw, db.reshape(C)


def _conv_bwd_in(dpre, w, name):
    T, C = dpre.shape
    tr, tc = _tile(T, 256, 8), _tile(C, 1024)
    nb = T // tr

    def body(d_ref, dn_ref, w_ref, o_ref):
        d = d_ref[...]
        keep = 1.0 - (pl.program_id(0) == nb - 1).astype(F32)
        dn = dn_ref[...] * keep
        rows = lax.broadcasted_iota(jnp.int32, d.shape, 0)
        acc = w_ref[SSD_CONV - 1:SSD_CONV, :] * d
        for j in range(1, SSD_CONV):
            sd = pltpu.roll(d, tr - j, 0)
            sn = jnp.tile(pltpu.roll(dn, SUBLANES - j, 0), (tr // SUBLANES, 1))
            acc = acc + w_ref[SSD_CONV - 1 - j:SSD_CONV - j, :] * jnp.where(rows >= tr - j, sn, sd)
        o_ref[...] = acc.astype(BF16)

    return pl.pallas_call(
        body, name=name, grid=(nb, C // tc),
        in_specs=[pl.BlockSpec((tr, tc), lambda i, j: (i, j)),
                  pl.BlockSpec((SUBLANES, tc), lambda i, j: (jnp.minimum((i + 1) * (tr // SUBLANES), T // SUBLANES - 1), j)),
                  pl.BlockSpec((SSD_CONV, tc), lambda i, j: (0, j))],
        out_specs=pl.BlockSpec((tr, tc), lambda i, j: (i, j)),
        out_shape=jax.ShapeDtypeStruct((T, C), BF16),
        compiler_params=_cparams("parallel", "parallel"),
    )(dpre, dpre, w)


def _mem_probs(qh, kh, scale):
    s = _dot_nt(qh.astype(BF16), kh.astype(BF16)) * scale
    m = jnp.max(s, axis=-1, keepdims=True)
    p = jnp.exp(s - m)
    return p / jnp.sum(p, axis=-1, keepdims=True)


def _memattn_fwd(q, kv, name):
    T, MW = q.shape
    NM = kv.shape[0]
    hd = MW // MEM_HEADS
    scale = hd ** -0.5
    tq = _tile(T, 512, 8)

    def body(q_ref, kv_ref, o_ref):
        for h in range(MEM_HEADS):
            sl = slice(h * hd, (h + 1) * hd)
            p = _mem_probs(q_ref[:, sl], kv_ref[:, sl], scale)
            vh = kv_ref[:, MW + h * hd:MW + (h + 1) * hd]
            o_ref[:, sl] = _dot(p.astype(BF16), vh.astype(BF16))

    return pl.pallas_call(
        body, name=name, grid=(T // tq,),
        in_specs=[pl.BlockSpec((tq, MW), lambda i: (i, 0)), pl.BlockSpec((NM, 2 * MW), lambda i: (0, 0))],
        out_specs=pl.BlockSpec((tq, MW), lambda i: (i, 0)),
        out_shape=jax.ShapeDtypeStruct((T, MW), F32),
        compiler_params=_cparams("parallel"),
    )(q, kv)


def _memattn_bwd(q, kv, dy, name):
    T, MW = q.shape
    NM = kv.shape[0]
    hd = MW // MEM_HEADS
    scale = hd ** -0.5
    tq = _tile(T, 512, 8)

    def body(q_ref, kv_ref, dy_ref, dq_ref, dkv_ref):
        @pl.when(pl.program_id(0) == 0)
        def _():
            dkv_ref[...] = jnp.zeros_like(dkv_ref)

        for h in range(MEM_HEADS):
            sl = slice(h * hd, (h + 1) * hd)
            vsl = slice(MW + h * hd, MW + (h + 1) * hd)
            qh = q_ref[:, sl]
            kh = kv_ref[:, sl]
            vh = kv_ref[:, vsl]
            dyh = dy_ref[:, sl].astype(BF16)
            p = _mem_probs(qh, kh, scale)
            dp = _dot_nt(dyh, vh.astype(BF16))
            ds = p * (dp - jnp.sum(dp * p, axis=-1, keepdims=True)) * scale
            dq_ref[:, sl] = _dot(ds.astype(BF16), kh.astype(BF16)).astype(BF16)
            dkv_ref[:, sl] += _dot(ds.T.astype(BF16), qh.astype(BF16))
            dkv_ref[:, vsl] += _dot(p.T.astype(BF16), dyh)

    return pl.pallas_call(
        body, name=name, grid=(T // tq,),
        in_specs=[pl.BlockSpec((tq, MW), lambda i: (i, 0)), pl.BlockSpec((NM, 2 * MW), lambda i: (0, 0)),
                  pl.BlockSpec((tq, MW), lambda i: (i, 0))],
        out_specs=[pl.BlockSpec((tq, MW), lambda i: (i, 0)), pl.BlockSpec((NM, 2 * MW), lambda i: (0, 0))],
        out_shape=[jax.ShapeDtypeStruct((T, MW), BF16), jax.ShapeDtypeStruct((NM, 2 * MW), F32)],
        compiler_params=_cparams("arbitrary"),
    )(q, kv, dy)


def _silu_parts(z):
    sg = _sigmoid(z)
    return z * sg, sg * (1.0 + z * (1.0 - sg))


def _gate_fwd(a, z, name):
    return _elementwise(lambda av, zv: (av * _silu_parts(zv)[0],), [a, z], [], [BF16], name)[0]


def _gate_bwd(a, z, d, name):
    def fn(av, zv, dv):
        s, ds = _silu_parts(zv)
        return dv * s, dv * av * ds
    return _elementwise(fn, [a, z, d], [], [F32, BF16], name)


def _gate_norm_fwd(y, z, g, groups, name):
    T, C = y.shape
    gw = C // groups
    tr = _tile(T, 128, 8)

    def body(y_ref, z_ref, g_ref, o_ref):
        for k in range(groups):
            sl = slice(k * gw, (k + 1) * gw)
            u = y_ref[:, sl] * _silu_parts(z_ref[:, sl])[0]
            r = lax.rsqrt(jnp.mean(u * u, axis=-1, keepdims=True) + EPS)
            o_ref[:, sl] = (u * r * g_ref[:, sl]).astype(BF16)

    row = pl.BlockSpec((tr, C), lambda i: (i, 0))
    return pl.pallas_call(
        body, name=name, grid=(T // tr,), in_specs=[row, row, pl.BlockSpec((1, C), lambda i: (0, 0))],
        out_specs=row, out_shape=jax.ShapeDtypeStruct((T, C), BF16),
        compiler_params=_cparams("parallel"),
    )(y, z, g.reshape(1, C))


def _gate_norm_bwd(y, z, g, d, groups, name):
    T, C = y.shape
    gw = C // groups
    tr = _tile(T, 128, 8)

    def body(y_ref, z_ref, g_ref, d_ref, dy_ref, dz_ref, dg_ref):
        @pl.when(pl.program_id(0) == 0)
        def _():
            dg_ref[...] = jnp.zeros_like(dg_ref)

        for k in range(groups):
            sl = slice(k * gw, (k + 1) * gw)
            yv = y_ref[:, sl]
            s, ds = _silu_parts(z_ref[:, sl])
            u = yv * s
            r = lax.rsqrt(jnp.mean(u * u, axis=-1, keepdims=True) + EPS)
            uhat = u * r
            dv = d_ref[:, sl]
            dg_ref[:, sl] += jnp.sum(dv * uhat, axis=0, keepdims=True)
            duhat = dv * g_ref[:, sl]
            du = r * (duhat - uhat * jnp.mean(duhat * uhat, axis=-1, keepdims=True))
            dy_ref[:, sl] = du * s
            dz_ref[:, sl] = (du * yv * ds).astype(BF16)

    row = pl.BlockSpec((tr, C), lambda i: (i, 0))
    vec = pl.BlockSpec((1, C), lambda i: (0, 0))
    dy, dz, dg = pl.pallas_call(
        body, name=name, grid=(T // tr,), in_specs=[row, row, vec, row], out_specs=[row, row, vec],
        out_shape=[jax.ShapeDtypeStruct((T, C), F32), jax.ShapeDtypeStruct((T, C), BF16),
                   jax.ShapeDtypeStruct((1, C), F32)],
        compiler_params=_cparams("arbitrary"),
    )(y, z, g.reshape(1, C), d)
    return dy, dz, dg.reshape(C)


def _ssd_common(raw, bias, alog, Q, HP, HPG):
    P_ = SSD_HEAD_DIM
    dt_in = raw + bias
    dt = jnp.maximum(dt_in, 0.0) + jnp.log(1.0 + jnp.exp(-jnp.abs(dt_in)))
    a = -jnp.exp(alog)
    r_q = lax.broadcasted_iota(jnp.int32, (Q, Q), 0)
    c_q = lax.broadcasted_iota(jnp.int32, (Q, Q), 1)
    causal = r_q >= c_q
    tril = causal.astype(BF16)
    A = _xdot_l(tril, dt * a)
    e_r = lax.broadcasted_iota(jnp.int32, (LANES, HP), 0)
    e_c = lax.broadcasted_iota(jnp.int32, (LANES, HP), 1)
    E = ((e_c >= e_r * P_) & (e_c < (e_r + 1) * P_) & (e_r < HPG)).astype(BF16)
    return dt, a, A, causal, E


def _head_cols(v, vt, j):
    lane = lax.broadcasted_iota(jnp.int32, v.shape, 1)
    sub = lax.broadcasted_iota(jnp.int32, vt.shape, 0)
    col = jnp.sum(jnp.where(lane == j, v, 0.0), axis=-1, keepdims=True)
    row = jnp.sum(jnp.where(sub == j, vt, 0.0), axis=0, keepdims=True)
    return col, row


def _ssd_fwd(act, raw_g, bias_g, alog_g, dsk_g, TOK, name):
    T = act.shape[0]
    G, N, Q, P_ = SSD_GROUPS, SSD_STATE, SSD_CHUNK, SSD_HEAD_DIM
    HP = TOK // G
    HPG = HP // P_
    NC = T // Q

    def body(x_ref, b_ref, c_ref, raw_ref, bias_ref, alog_ref, dsk_ref, y_ref, hp_ref, hT):
        @pl.when(pl.program_id(1) == 0)
        def _():
            hT[...] = jnp.zeros_like(hT)

        xs = x_ref[...]
        Bb = b_ref[...].astype(BF16)
        Cb = c_ref[...].astype(BF16)
        dt, a, A, causal, E = _ssd_common(raw_ref[...], bias_ref[...], alog_ref[...], Q, HP, HPG)
        AT = A.T
        dt_e = _xdot(dt, E)
        A_e = _xdot(A, E)
        dsk_e = _xdot(jnp.broadcast_to(dsk_ref[...], (Q, LANES)), E)
        rows = lax.broadcasted_iota(jnp.int32, (Q, HP), 0)
        cols = lax.broadcasted_iota(jnp.int32, (Q, HP), 1)
        Al_e = jnp.sum(jnp.where(rows == Q - 1, A_e, 0.0), axis=0, keepdims=True)
        xdt = xs * dt_e
        hprev = hT[...]
        hp_ref[...] = hprev
        CB = _dot_nt(Cb, Bb)
        y = _dot(Cb, hprev.astype(BF16)) * jnp.exp(A_e) + dsk_e * xs
        M, xh = [], []
        for j in range(HPG):
            a_col, a_row = _head_cols(A, AT, j)
            M.append((CB * jnp.exp(jnp.where(causal, a_col - a_row, NEG))).astype(BF16))
            xh.append(jnp.where((cols >= j * P_) & (cols < (j + 1) * P_), xdt, 0.0).astype(BF16))
        y_ref[...] = y + _dot(jnp.concatenate(M, axis=1), jnp.concatenate(xh, axis=0))
        dte = jnp.exp(Al_e - A_e)
        hT[...] = jnp.exp(Al_e) * hprev + _dot(b_ref[...].T.astype(BF16), (xdt * dte).astype(BF16))

    nbx = TOK // N
    par = pl.BlockSpec((None, 1, LANES), lambda g, c: (g, 0, 0))
    return pl.pallas_call(
        body, name=name, grid=(G, NC),
        in_specs=[pl.BlockSpec((Q, HP), lambda g, c: (c, g)),
                  pl.BlockSpec((Q, N), lambda g, c: (c, nbx + g)),
                  pl.BlockSpec((Q, N), lambda g, c: (c, nbx + G + g)),
                  pl.BlockSpec((None, Q, LANES), lambda g, c: (g, c, 0)), par, par, par],
        out_specs=[pl.BlockSpec((Q, HP), lambda g, c: (c, g)),
                   pl.BlockSpec((None, None, N, HP), lambda g, c: (g, c, 0, 0))],
        out_shape=[jax.ShapeDtypeStruct((T, TOK), F32), jax.ShapeDtypeStruct((G, NC, N, HP), F32)],
        scratch_shapes=[pltpu.VMEM((N, HP), F32)],
        compiler_params=_cparams("parallel", "arbitrary"),
    )(act, act, act, raw_g, bias_g, alog_g, dsk_g)


def _ssd_bwd(act, raw_g, bias_g, alog_g, dsk_g, hprev, dy, TOK, name):
    T = act.shape[0]
    G, N, Q, P_ = SSD_GROUPS, SSD_STATE, SSD_CHUNK, SSD_HEAD_DIM
    HP = TOK // G
    HPG = HP // P_
    NC = T // Q

    def body(x_ref, b_ref, c_ref, raw_ref, bias_ref, alog_ref, dsk_ref, hp_ref, dy_ref,
             dx_ref, db_ref, dc_ref, draw_ref, dalog_ref, dbias_ref, ddsk_ref, dHT):
        @pl.when(pl.program_id(1) == 0)
        def _():
            dHT[...] = jnp.zeros_like(dHT)
            dalog_ref[...] = jnp.zeros_like(dalog_ref)
            dbias_ref[...] = jnp.zeros_like(dbias_ref)
            ddsk_ref[...] = jnp.zeros_like(ddsk_ref)

        xs = x_ref[...]
        dyv = dy_ref[...]
        Bm = b_ref[...]
        Cm = c_ref[...]
        Bb = Bm.astype(BF16)
        Cb = Cm.astype(BF16)
        raw_in = raw_ref[...] + bias_ref[...]
        dt, a, A, causal, E = _ssd_common(raw_ref[...], bias_ref[...], alog_ref[...], Q, HP, HPG)
        AT = A.T
        t_r = lax.broadcasted_iota(jnp.int32, (HP, LANES), 0)
        t_c = lax.broadcasted_iota(jnp.int32, (HP, LANES), 1)
        ET = ((t_r >= t_c * P_) & (t_r < (t_c + 1) * P_) & (t_c < HPG)).astype(BF16)
        dt_e = _xdot(dt, E)
        A_e = _xdot(A, E)
        dsk_e = _xdot(jnp.broadcast_to(dsk_ref[...], (Q, LANES)), E)
        rows = lax.broadcasted_iota(jnp.int32, (Q, HP), 0)
        cols = lax.broadcasted_iota(jnp.int32, (Q, HP), 1)
        last = rows == Q - 1
        Al_e = jnp.sum(jnp.where(last, A_e, 0.0), axis=0, keepdims=True)
        eA = jnp.exp(A_e)
        eAl = jnp.exp(Al_e)
        dte = jnp.exp(Al_e - A_e)
        xdt = xs * dt_e
        xdt_b = xdt.astype(BF16)
        CB = _dot_nt(Cb, Bb)
        HT = hp_ref[...]
        HTb = HT.astype(BF16)
        dH = dHT[...]
        dHb = dH.astype(BF16)
        dys = (dyv * eA).astype(BF16)
        CH = _dot(Cb, HTb)
        dC = _dot_nt(dys, HTb)
        dH_prev = _dot(Cm.T.astype(BF16), dys) + eAl * dH
        dAe = dyv * CH * eA
        dAl = eAl * jnp.sum(dH * HT, axis=0, keepdims=True)
        W = _dot(Bb, dHb)
        dxdt = W * dte
        dd = W * xdt * dte
        dB = _dot_nt((xdt * dte).astype(BF16), dHb)
        dAl = dAl + jnp.sum(dd, axis=0, keepdims=True)
        dAe = dAe - dd + jnp.where(last, dAl, 0.0)
        lane = lax.broadcasted_iota(jnp.int32, (Q, LANES), 1)
        sub = lax.broadcasted_iota(jnp.int32, (LANES, Q), 0)
        dCB = jnp.zeros((Q, Q), F32)
        dA_col = jnp.zeros((Q, LANES), F32)
        dA_row = jnp.zeros((LANES, Q), F32)
        dy_heads = jnp.concatenate(
            [jnp.where((cols >= j * P_) & (cols < (j + 1) * P_), dyv, 0.0).astype(BF16) for j in range(HPG)], axis=0)
        G_heads = _dot_nt(dy_heads, xdt_b)
        MT = []
        for j in range(HPG):
            a_col, a_row = _head_cols(A, AT, j)
            L = jnp.exp(jnp.where(causal, a_col - a_row, NEG))
            GL = G_heads[j * Q:(j + 1) * Q] * L
            dCB = dCB + GL
            dLL = GL * CB
            dA_col = dA_col + jnp.where(lane == j, jnp.sum(dLL, axis=-1, keepdims=True), 0.0)
            dA_row = dA_row + jnp.where(sub == j, jnp.sum(dLL, axis=0, keepdims=True), 0.0)
            MT.append((CB * L).T.astype(BF16))
        dxdt = dxdt + _dot(jnp.concatenate(MT, axis=1), dy_heads)
        dC = dC + _dot(dCB.astype(BF16), Bb)
        dB = dB + _dot(dCB.T.astype(BF16), Cb)
        dA = dA_col - dA_row.T + _xdot(dAe, ET)
        triu = (lax.broadcasted_iota(jnp.int32, (Q, Q), 1) >= lax.broadcasted_iota(jnp.int32, (Q, Q), 0)).astype(BF16)
        rcs = _xdot_l(triu, dA)
        ddt = a * rcs + _xdot(dxdt * xs, ET)
        draw = ddt * _sigmoid(raw_in)
        draw_ref[...] = draw
        dalog_ref[...] += jnp.sum(dt * rcs, axis=0, keepdims=True) * a
        dbias_ref[...] += jnp.sum(draw, axis=0, keepdims=True)
        ddsk_ref[...] += jnp.sum(_xdot(dyv * xs, ET), axis=0, keepdims=True)
        dx_ref[...] = dxdt * dt_e + dsk_e * dyv
        db_ref[...] = dB
        dc_ref[...] = dC
        dHT[...] = dH_prev

    nbx = TOK // N
    rv = lambda c: NC - 1 - c
    par = pl.BlockSpec((None, 1, LANES), lambda g, c: (g, 0, 0))
    xsp = pl.BlockSpec((Q, HP), lambda g, c: (rv(c), g))
    outs = pl.pallas_call(
        body, name=name, grid=(G, NC),
        in_specs=[xsp,
                  pl.BlockSpec((Q, N), lambda g, c: (rv(c), nbx + g)),
                  pl.BlockSpec((Q, N), lambda g, c: (rv(c), nbx + G + g)),
                  pl.BlockSpec((None, Q, LANES), lambda g, c: (g, rv(c), 0)), par, par, par,
                  pl.BlockSpec((None, None, N, HP), lambda g, c: (g, rv(c), 0, 0)), xsp],
        out_specs=[xsp, pl.BlockSpec((Q, N), lambda g, c: (rv(c), g)), pl.BlockSpec((Q, N), lambda g, c: (rv(c), g)),
                   pl.BlockSpec((None, Q, LANES), lambda g, c: (g, rv(c), 0)), par, par, par],
        out_shape=[jax.ShapeDtypeStruct((T, TOK), F32), jax.ShapeDtypeStruct((T, G * N), F32),
                   jax.ShapeDtypeStruct((T, G * N), F32), jax.ShapeDtypeStruct((G, T, LANES), F32),
                   jax.ShapeDtypeStruct((G, 1, LANES), F32), jax.ShapeDtypeStruct((G, 1, LANES), F32),
                   jax.ShapeDtypeStruct((G, 1, LANES), F32)],
        scratch_shapes=[pltpu.VMEM((N, HP), F32)],
        compiler_params=_cparams("parallel", "arbitrary"),
    )(act, act, act, raw_g, bias_g, alog_g, dsk_g, hprev, dy)
    return outs


def _heads_per_block(H):
    for hb in (24, 12, 8, 6, 4, 3, 2, 1):
        if H % hb == 0:
            return hb
    return 1


def _alibi_slope(head_index, n_alibi):
    c = -ALIBI_MAX_EXP * math.log(2.0) / n_alibi
    return jnp.exp(jnp.full((1, 1), c, F32) * (head_index + 1).astype(F32))


def _attn_masks(b, nb):
    Bq = ATTN_BLOCK
    iq = lax.broadcasted_iota(jnp.int32, (Bq, 2 * Bq), 0)
    jk = lax.broadcasted_iota(jnp.int32, (Bq, 2 * Bq), 1)
    rel = iq + Bq - jk
    mask = (rel >= 0) & (rel <= Bq) & (jk + jnp.where(b > 0, Bq, 0) >= Bq)
    rel_n = lax.broadcasted_iota(jnp.int32, (Bq, Bq), 0) + Bq - lax.broadcasted_iota(jnp.int32, (Bq, Bq), 1)
    mask_n = (rel_n + jnp.where(b < nb - 1, 0, 4 * Bq)) <= Bq
    return rel.astype(F32), mask, rel_n.astype(F32), mask_n


def _rows2(a, b):
    return jnp.concatenate([a, b], axis=0)


ATTN_UNITS_IN_FLIGHT = 6


def _attn_units_fwd(load, n_units, slopes, masks, scale, store):
    rel_f, mask, _, _ = masks
    for g0 in range(0, n_units, ATTN_UNITS_IN_FLIGHT):
        ids = range(g0, min(g0 + ATTN_UNITS_IN_FLIGHT, n_units))
        units = [load(i) for i in ids]
        raw = [_dot_nt(q, _rows2(kp, kc)) for q, kc, kp, vc, vp in units]
        soft = []
        for i, s_raw in zip(ids, raw):
            s = jnp.where(mask, s_raw * scale - slopes[i] * rel_f, NEG)
            m = jnp.max(s, axis=-1, keepdims=True)
            p = jnp.exp(s - m)
            den = jnp.sum(p, axis=-1, keepdims=True)
            soft.append((p.astype(BF16), den, m + jnp.log(den)))
        for i, (p, den, lse), (q, kc, kp, vc, vp) in zip(ids, soft, units):
            store(i, _dot(p, _rows2(vp, vc)) / den, lse)


def _attn_units_bwd(load, n_units, slopes, masks, scale, store):
    rel_f, mask, reln_f, mask_n = masks
    Bq = ATTN_BLOCK
    for g0 in range(0, n_units, ATTN_UNITS_IN_FLIGHT):
        ids = range(g0, min(g0 + ATTN_UNITS_IN_FLIGHT, n_units))
        units = [load(i) for i in ids]
        prods = []
        for q0, q1, kp, k0, vp, v0, do0, do1, y0, y1, lse0, lse1 in units:
            kcat = _rows2(kp, k0)
            do0b = do0.astype(BF16)
            do1b = do1.astype(BF16)
            prods.append((kcat, do0b, do1b, _dot_nt(q0, kcat), _dot_nt(do0b, _rows2(vp, v0)), _dot_nt(q1, k0),
                          _dot_nt(do1b, v0)))
        mids = []
        for i, u, (kcat, do0b, do1b, s_raw, dp_raw, sn_raw, dpn_raw) in zip(ids, units, prods):
            q0, q1, kp, k0, vp, v0, do0, do1, y0, y1, lse0, lse1 = u
            delta0 = jnp.sum(do0 * y0, axis=-1, keepdims=True)
            delta1 = jnp.sum(do1 * y1, axis=-1, keepdims=True)
            p = jnp.exp(jnp.where(mask, s_raw * scale - slopes[i] * rel_f, NEG) - lse0)
            ds = p * (dp_raw - delta0)
            p_n = jnp.exp(jnp.where(mask_n, sn_raw * scale - slopes[i] * reln_f, NEG) - lse1)
            ds_n = p_n * (dpn_raw - delta1)
            mids.append((ds.astype(BF16), _rows2(ds[:, Bq:], ds_n).T.astype(BF16),
                         _rows2(p[:, Bq:], p_n).T.astype(BF16)))
        for i, u, (kcat, do0b, do1b, *_), (dsb, dsk_t, pk_t) in zip(ids, units, prods, mids):
            store(i, scale * _dot(dsb, kcat), scale * _dot(dsk_t, _rows2(u[0], u[1])), _dot(pk_t, _rows2(do0b, do1b)))


def _attn_fwd_strided(q, k, v, gi, d, name):
    T, TOK = q.shape
    E_ = ATTN_HEAD_DIM
    H = TOK // E_
    n_alibi = len(DILATED_GROUPS) * H
    Bq = ATTN_BLOCK
    RB = Bq * d
    nb = T // RB
    scale = E_ ** -0.5

    def body(q_ref, kc_ref, kp_ref, vc_ref, vp_ref, o_ref, l_ref):
        masks = _attn_masks(pl.program_id(1), nb)
        slope = _alibi_slope(gi * H + pl.program_id(0), n_alibi) * float(d)
        rows = lambda r: pl.ds(r, Bq, stride=d)

        def load(r):
            return tuple(ref[rows(r), :].astype(BF16) for ref in (q_ref, kc_ref, kp_ref, vc_ref, vp_ref))

        def store(r, o, lse):
            o_ref[rows(r), :] = o
            l_ref[rows(r), :] = jnp.broadcast_to(lse, (Bq, E_))

        _attn_units_fwd(load, d, [slope] * d, masks, scale, store)

    cur = pl.BlockSpec((RB, E_), lambda h, b: (b, h))
    prev = pl.BlockSpec((RB, E_), lambda h, b: (jnp.maximum(b - 1, 0), h))
    sds = jax.ShapeDtypeStruct((T, TOK), F32)
    return pl.pallas_call(
        body, name=name, grid=(H, nb), in_specs=[cur, cur, prev, cur, prev], out_specs=[cur, cur],
        out_shape=[sds, sds], compiler_params=_cparams("parallel", "parallel"),
    )(q, k, k, v, v)


def _attn_bwd_strided(q, k, v, y, lse, dy, gi, d, name):
    T, TOK = q.shape
    E_ = ATTN_HEAD_DIM
    H = TOK // E_
    n_alibi = len(DILATED_GROUPS) * H
    Bq = ATTN_BLOCK
    RB = Bq * d
    nb = T // RB
    scale = E_ ** -0.5

    def body(q0_ref, q1_ref, kp_ref, k0_ref, vp_ref, v0_ref, do0_ref, do1_ref, y0_ref, y1_ref, l0_ref, l1_ref,
             dq_ref, dk_ref, dv_ref):
        masks = _attn_masks(pl.program_id(1), nb)
        slope = _alibi_slope(gi * H + pl.program_id(0), n_alibi) * float(d)
        rows = lambda r: pl.ds(r, Bq, stride=d)

        def load(r):
            return (tuple(ref[rows(r), :].astype(BF16) for ref in (q0_ref, q1_ref, kp_ref, k0_ref, vp_ref, v0_ref))
                    + tuple(ref[rows(r), :] for ref in (do0_ref, do1_ref, y0_ref, y1_ref))
                    + tuple(jnp.max(ref[rows(r), :], axis=-1, keepdims=True) for ref in (l0_ref, l1_ref)))

        def store(r, dq, dk, dv):
            dq_ref[rows(r), :] = dq
            dk_ref[rows(r), :] = dk
            dv_ref[rows(r), :] = dv

        _attn_units_bwd(load, d, [slope] * d, masks, scale, store)

    cur = pl.BlockSpec((RB, E_), lambda h, b: (b, h))
    prev = pl.BlockSpec((RB, E_), lambda h, b: (jnp.maximum(b - 1, 0), h))
    nxt = pl.BlockSpec((RB, E_), lambda h, b: (jnp.minimum(b + 1, nb - 1), h))
    sds = jax.ShapeDtypeStruct((T, TOK), F32)
    return pl.pallas_call(
        body, name=name, grid=(H, nb),
        in_specs=[cur, nxt, prev, cur, prev, cur, cur, nxt, cur, nxt, cur, nxt],
        out_specs=[cur, cur, cur], out_shape=[sds, sds, sds],
        compiler_params=_cparams("parallel", "parallel"),
    )(q, q, k, k, v, v, dy, dy, y, y, lse, lse)


def _attn_fwd(q, k, v, gi, window, d, name):
    T, TOK = q.shape
    E_ = ATTN_HEAD_DIM
    H = TOK // E_
    n_alibi = len(DILATED_GROUPS) * H
    assert window // d == ATTN_BLOCK and (T // d) % ATTN_BLOCK == 0
    if d > 1:
        return _attn_fwd_strided(q, k, v, gi, d, name)
    n_sub = T // d
    nb = n_sub // ATTN_BLOCK
    HB = _heads_per_block(H)
    NHB = H // HB
    hbw = HB * E_
    scale = E_ ** -0.5
    Bq = ATTN_BLOCK

    def body(q_ref, kc_ref, kp_ref, vc_ref, vp_ref, o_ref, l_ref):
        hb = pl.program_id(1)
        masks = _attn_masks(pl.program_id(2), nb)
        slopes = [_alibi_slope(gi * H + hb * HB + hh, n_alibi) * float(d) for hh in range(HB)]
        cols = lambda hh: slice(hh * E_, (hh + 1) * E_)

        def load(hh):
            return tuple(ref[:, cols(hh)].astype(BF16) for ref in (q_ref, kc_ref, kp_ref, vc_ref, vp_ref))

        def store(hh, o, lse):
            o_ref[:, cols(hh)] = o
            l_ref[:, cols(hh)] = jnp.broadcast_to(lse, (Bq, E_))

        _attn_units_fwd(load, HB, slopes, masks, scale, store)

    cur = pl.BlockSpec((Bq, hbw), lambda r, h, b: (b, r * NHB + h))
    prev = pl.BlockSpec((Bq, hbw), lambda r, h, b: (jnp.maximum(b - 1, 0), r * NHB + h))
    view = lambda t: t.reshape(n_sub, d * TOK)
    sds = jax.ShapeDtypeStruct((n_sub, d * TOK), F32)
    o, l = pl.pallas_call(
        body, name=name, grid=(d, NHB, nb), in_specs=[cur, cur, prev, cur, prev], out_specs=[cur, cur],
        out_shape=[sds, sds], compiler_params=_cparams("parallel", "parallel", "parallel"),
    )(view(q), view(k), view(k), view(v), view(v))
    return o.reshape(T, TOK), l.reshape(T, TOK)


def _attn_combine(os_, ls_, name):
    def fn(*v):
        n = len(v) // 2
        o, l = v[:n], v[n:]
        m = l[0]
        for t in l[1:]:
            m = jnp.maximum(m, t)
        e = [jnp.exp(t - m) for t in l]
        den = e[0]
        for t in e[1:]:
            den = den + t
        y = e[0] * o[0]
        for t, u in zip(e[1:], o[1:]):
            y = y + t * u
        return y / den, m + jnp.log(den)
    return _elementwise(fn, list(os_) + list(ls_), [], [F32, F32], name)


def _attn_bwd(q, k, v, y, lse, dy, gi, window, d, name):
    T, TOK = q.shape
    E_ = ATTN_HEAD_DIM
    H = TOK // E_
    n_alibi = len(DILATED_GROUPS) * H
    if d > 1:
        return _attn_bwd_strided(q, k, v, y, lse, dy, gi, d, name)
    n_sub = T // d
    nb = n_sub // ATTN_BLOCK
    HB = _heads_per_block(H)
    NHB = H // HB
    hbw = HB * E_
    scale = E_ ** -0.5
    Bq = ATTN_BLOCK

    def body(q0_ref, q1_ref, kp_ref, k0_ref, vp_ref, v0_ref, do0_ref, do1_ref, y0_ref, y1_ref, l0_ref, l1_ref,
             dq_ref, dk_ref, dv_ref):
        hb = pl.program_id(1)
        masks = _attn_masks(pl.program_id(2), nb)
        slopes = [_alibi_slope(gi * H + hb * HB + hh, n_alibi) * float(d) for hh in range(HB)]
        cols = lambda hh: slice(hh * E_, (hh + 1) * E_)

        def load(hh):
            return (tuple(ref[:, cols(hh)].astype(BF16) for ref in (q0_ref, q1_ref, kp_ref, k0_ref, vp_ref, v0_ref))
                    + tuple(ref[:, cols(hh)] for ref in (do0_ref, do1_ref, y0_ref, y1_ref))
                    + tuple(jnp.max(ref[:, cols(hh)], axis=-1, keepdims=True) for ref in (l0_ref, l1_ref)))

        def store(hh, dq, dk, dv):
            dq_ref[:, cols(hh)] = dq.astype(BF16)
            dk_ref[:, cols(hh)] = dk.astype(BF16)
            dv_ref[:, cols(hh)] = dv.astype(BF16)

        _attn_units_bwd(load, HB, slopes, masks, scale, store)

    cur = pl.BlockSpec((Bq, hbw), lambda r, h, b: (b, r * NHB + h))
    prev = pl.BlockSpec((Bq, hbw), lambda r, h, b: (jnp.maximum(b - 1, 0), r * NHB + h))
    nxt = pl.BlockSpec((Bq, hbw), lambda r, h, b: (jnp.minimum(b + 1, nb - 1), r * NHB + h))
    view = lambda t: t.reshape(n_sub, d * TOK)
    sds = jax.ShapeDtypeStruct((n_sub, d * TOK), BF16)
    dq, dk, dv = pl.pallas_call(
        body, name=name, grid=(d, NHB, nb),
        in_specs=[cur, nxt, prev, cur, prev, cur, cur, nxt, cur, nxt, cur, nxt],
        out_specs=[cur, cur, cur], out_shape=[sds, sds, sds],
        compiler_params=_cparams("parallel", "parallel", "parallel"),
    )(view(q), view(q), view(k), view(k), view(v), view(v), view(dy), view(dy), view(y), view(y), view(lse), view(lse))
    return dq.reshape(T, TOK), dk.reshape(T, TOK), dv.reshape(T, TOK)


_FLIPS = {
    "xy": [(1, 0, 0), (0, 1, 0), (1, 1, 0)],
    "c": [(0, 0, 1)],
    "xyc": [(dx, dy, dc) for dx in (0, 1) for dy in (0, 1) for dc in (0, 1) if (dx, dy, dc) != (0, 0, 0)],
}


def _comm_parts(group, srcs, modes, handshake):
    flips = _FLIPS[group]
    F_ = len(flips)
    P_ = F_ + 1
    n = len(srcs)

    def gidx(px, py, pc):
        if group == "xy":
            return 2 * px + py
        if group == "c":
            return pc
        return 4 * px + 2 * py + pc

    def body(*refs):
        src_refs, out_refs = refs[:n], refs[n:2 * n]
        send_sems, recv_sems, loc_sems = refs[2 * n:]
        x, y, c = lax.axis_index("x"), lax.axis_index("y"), lax.axis_index("c")
        me = gidx(x, y, c)
        peers = [(1 - x if dx else x, 1 - y if dy else y, 1 - c if dc else c) for dx, dy, dc in flips]
        if handshake:
            barrier = pltpu.get_barrier_semaphore()
            for peer in peers:
                pl.semaphore_signal(barrier, inc=1, device_id=peer, device_id_type=MESH)
            pl.semaphore_wait(barrier, F_)
        local, remote = [], []
        for i in range(n):
            mode = modes[i]
            if mode != "swap":
                mine = pltpu.make_async_copy(src_refs[i] if mode == "gather" else src_refs[i].at[me],
                                             out_refs[i].at[me], loc_sems.at[i])
                mine.start()
                local.append(mine)
            for f, peer in enumerate(peers):
                cp = pltpu.make_async_remote_copy(
                    src_ref=src_refs[i].at[gidx(*peer)] if mode == "a2a" else src_refs[i],
                    dst_ref=out_refs[i] if mode == "swap" else out_refs[i].at[me],
                    send_sem=send_sems.at[i * F_ + f], recv_sem=recv_sems.at[i * F_ + f],
                    device_id=peer, device_id_type=MESH)
                cp.start()
                remote.append(cp)
        for cp in local:
            cp.wait()
        for cp in remote:
            cp.wait()

    out_shape = []
    for s, mode in zip(srcs, modes):
        assert mode != "swap" or F_ == 1
        shp = (P_,) + tuple(s.shape) if mode == "gather" else tuple(s.shape)
        out_shape.append(jax.ShapeDtypeStruct(shp, s.dtype))
    sems = [pltpu.SemaphoreType.DMA((n * F_,)), pltpu.SemaphoreType.DMA((n * F_,)), pltpu.SemaphoreType.DMA((n,))]
    return body, out_shape, sems


def _comm(name, group, srcs, modes):
    n = len(srcs)
    body, out_shape, sems = _comm_parts(group, srcs, modes, handshake=False)
    anyspec = pl.BlockSpec(memory_space=pl.ANY)
    return pl.pallas_call(body, name=name, in_specs=[anyspec] * n, out_specs=[anyspec] * n, out_shape=out_shape,
                          scratch_shapes=sems)(*srcs)


def _comm_async(name, collective_id, group, srcs, modes):
    body, out_shape, sems = _comm_parts(group, srcs, modes, handshake=True)
    return pl.kernel(body, name=name, out_type=out_shape,
                     mesh=plsc.ScalarSubcoreMesh(axis_name="sequencer", num_cores=1), scratch_types=sems,
                     compiler_params=pltpu.CompilerParams(collective_id=collective_id))(*srcs)


def _dims(D):
    MIX = 2 * D
    MW = MIX // 4
    TOK = MIX - MW
    H = TOK // SSD_HEAD_DIM
    CONV = TOK + 2 * SSD_GROUPS * SSD_STATE
    return dict(MIX=MIX, MW=MW, TOK=TOK, H=H, CONV=CONV)


def _proj_chain(dsegs, wsegs, name):
    acc = None
    for n, (ds, ws) in enumerate(zip(dsegs, wsegs)):
        acc = _mm(ds, ws, "nt", F32, f"{name}_dh{n}", add=acc)
    return acc


def _pad_lanes(a, width=LANES):
    return jnp.pad(a, [(0, 0)] * (a.ndim - 1) + [(0, width - a.shape[-1])])


def _take_cols(parts, a, b):
    out, o = [], 0
    for part in parts:
        w = part.shape[1]
        lo, hi = max(a, o), min(b, o + w)
        if lo < hi:
            out.append(part[:, lo - o:hi - o])
        o += w
    return out[0] if len(out) == 1 else jnp.concatenate(out, axis=1)


def _heads_to_groups(a, G, HPG):
    return jnp.stack([_pad_lanes(a[:, g * HPG:(g + 1) * HPG]) for g in range(G)])


def _groups_to_heads(a, HPG):
    return jnp.concatenate([a[g, :, :HPG] for g in range(a.shape[0])], axis=1)


def _ssd_layer_fwd(x, kv, p, li):
    T, D = x.shape
    dm = _dims(D)
    TOK, MW, H, CONV = dm["TOK"], dm["MW"], dm["H"], dm["CONV"]
    G = SSD_GROUPS
    HPG = H // G
    w = p["w_in"]
    cuts = [0, CONV, CONV + H, CONV + H + MW, CONV + H + MW + TOK, CONV + H + MW + TOK + MW]
    segs = {k: _take_cols(w, cuts[n], cuts[n + 1]) for n, k in enumerate(["xbc", "dt", "qm", "zt", "zm"])}
    segs["dt"] = _pad_lanes(segs["dt"])
    h = _rms_fwd(x, p["norm_g"], f"l{li}_rms")
    pr = {k: _mm(h, ws, "nn", BF16 if k == "qm" else F32, f"l{li}_in_{k}") for k, ws in segs.items()}
    act = _conv_fwd(pr["xbc"], p["conv_w"], p["conv_b"], f"l{li}_conv")
    raw_g = _heads_to_groups(pr["dt"][:, :H], G, HPG)
    hp = lambda a: _pad_lanes(a.reshape(G, 1, HPG))
    bias_g, alog_g, dsk_g = hp(p["dt_bias"]), hp(p["a_log"]), hp(p["d_skip"])
    y, hprev = _ssd_fwd(act, raw_g, bias_g, alog_g, dsk_g, TOK, f"l{li}_ssd")
    ymem = _memattn_fwd(pr["qm"], kv, f"l{li}_mem")
    gt = _gate_norm_fwd(y, pr["zt"], p["ssd_norm_g"], G, f"l{li}_gate_tok")
    gm = _gate_fwd(ymem, pr["zm"], f"l{li}_gate_mem")
    wo = p["w_out"]
    out = _mm(gt, wo[:TOK], "nn", F32, f"l{li}_out_tok", add=x)
    out = _mm(gm, wo[TOK:], "nn", F32, f"l{li}_out_mem", add=out)
    saved = dict(x=x, h=h, pr=pr, act=act, raw_g=raw_g, par=(bias_g, alog_g, dsk_g), y=y, hprev=hprev, ymem=ymem,
                 gt=gt, gm=gm, segs=segs)
    return out, saved


def _ssd_layer_bwd(dout, doutb, kv, p, s, li):
    x = s["x"]
    T, D = x.shape
    dm = _dims(D)
    TOK, MW, H, CONV = dm["TOK"], dm["MW"], dm["H"], dm["CONV"]
    G = SSD_GROUPS
    HPG = H // G
    wo = p["w_out"]
    pr = s["pr"]
    dgt = _mm(doutb, wo[:TOK], "nt", F32, f"l{li}_dgt")
    dgm = _mm(doutb, wo[TOK:], "nt", F32, f"l{li}_dgm")
    dwo = jnp.concatenate([_mm(s["gt"], doutb, "tn", BF16, f"l{li}_dwo_tok"),
                           _mm(s["gm"], doutb, "tn", BF16, f"l{li}_dwo_mem")], axis=0)
    dy, dzt, dng = _gate_norm_bwd(s["y"], pr["zt"], p["ssd_norm_g"], dgt, G, f"l{li}_gate_tok_b")
    dymem, dzm = _gate_bwd(s["ymem"], pr["zm"], dgm, f"l{li}_gate_mem_b")
    dqm, dkv = _memattn_bwd(pr["qm"], kv, dymem, f"l{li}_mem_b")
    bias_g, alog_g, dsk_g = s["par"]
    dxs, dB, dC, draw_g, dalog, dbias, ddsk = _ssd_bwd(s["act"], s["raw_g"], bias_g, alog_g, dsk_g, s["hprev"], dy, TOK,
                                                      f"l{li}_ssd_b")
    dact = jnp.concatenate([dxs, dB, dC], axis=1)
    dpre, dconv_w, dconv_b = _conv_bwd_pre(pr["xbc"], p["conv_w"], p["conv_b"], dact, f"l{li}_conv_b1")
    dxbc = _conv_bwd_in(dpre, p["conv_w"], f"l{li}_conv_b2")
    draw = _pad_lanes(_groups_to_heads(draw_g, HPG)).astype(BF16)
    dsegs = dict(xbc=dxbc, dt=draw, qm=dqm, zt=dzt, zm=dzm)
    keys = ["xbc", "dt", "qm", "zt", "zm"]
    dh = _proj_chain([dsegs[k] for k in keys], [s["segs"][k] for k in keys], f"l{li}")
    dws = {k: _mm(s["h"], dsegs[k], "tn", BF16, f"l{li}_dwin_{k}") for k in keys}
    dws["dt"] = dws["dt"][:, :H]
    dwin = [dws[k] for k in keys]
    dh, dwin, dwo = lax.optimization_barrier((dh, dwin, dwo))
    dx, dxb, dnorm = _rms_bwd(x, p["norm_g"], dh, dout, f"l{li}_rms_b")
    unhead = lambda a: a[:, 0, :HPG].reshape(H)
    grads = dict(norm_g=dnorm, w_in=dwin, conv_w=dconv_w, conv_b=dconv_b, dt_bias=unhead(dbias), a_log=unhead(dalog),
                 d_skip=unhead(ddsk), ssd_norm_g=dng, w_out=dwo)
    return dx, dxb, dkv, grads


def _attn_layer_fwd(x, kv, p, li):
    T, D = x.shape
    dm = _dims(D)
    TOK, MW = dm["TOK"], dm["MW"]
    w = p["w_in"]
    ng = len(DILATED_GROUPS)
    segs = {}
    for g in range(ng):
        for n, nm in enumerate("qkv"):
            c0 = g * 3 * TOK + n * TOK
            segs[f"{nm}{g}"] = _take_cols(w, c0, c0 + TOK)
    c0 = ng * 3 * TOK
    segs["qm"] = _take_cols(w, c0, c0 + MW)
    segs["zt"] = _take_cols(w, c0 + MW, c0 + MW + TOK)
    segs["zm"] = _take_cols(w, c0 + MW + TOK, c0 + MW + TOK + MW)
    h = _rms_fwd(x, p["norm_g"], f"l{li}_rms")
    dense = {"qm"} | {f"{nm}{g}" for g, (_, d) in enumerate(DILATED_GROUPS) if d == 1 for nm in "qkv"}
    pr = {k: _mm(h, ws, "nn", BF16 if k in dense else F32, f"l{li}_in_{k}") for k, ws in segs.items()}
    os_, ls_ = [], []
    for g, (window, d) in enumerate(DILATED_GROUPS):
        o, l = _attn_fwd(pr[f"q{g}"], pr[f"k{g}"], pr[f"v{g}"], g, window, d, f"l{li}_attn{g}")
        os_.append(o)
        ls_.append(l)
    ytok, lse = _attn_combine(os_, ls_, f"l{li}_combine")
    ymem = _memattn_fwd(pr["qm"], kv, f"l{li}_mem")
    gt = _gate_fwd(ytok, pr["zt"], f"l{li}_gate_tok")
    gm = _gate_fwd(ymem, pr["zm"], f"l{li}_gate_mem")
    wo = p["w_out"]
    out = _mm(gt, wo[:TOK], "nn", F32, f"l{li}_out_tok", add=x)
    out = _mm(gm, wo[TOK:], "nn", F32, f"l{li}_out_mem", add=out)
    saved = dict(x=x, h=h, pr=pr, ytok=ytok, lse=lse, ymem=ymem, gt=gt, gm=gm, segs=segs)
    return out, saved


def _attn_layer_bwd(dout, doutb, kv, p, s, li):
    x = s["x"]
    T, D = x.shape
    dm = _dims(D)
    TOK, MW = dm["TOK"], dm["MW"]
    wo = p["w_out"]
    pr = s["pr"]
    dgt = _mm(doutb, wo[:TOK], "nt", F32, f"l{li}_dgt")
    dgm = _mm(doutb, wo[TOK:], "nt", F32, f"l{li}_dgm")
    dwo = jnp.concatenate([_mm(s["gt"], doutb, "tn", BF16, f"l{li}_dwo_tok"),
                           _mm(s["gm"], doutb, "tn", BF16, f"l{li}_dwo_mem")], axis=0)
    dytok, dzt = _gate_bwd(s["ytok"], pr["zt"], dgt, f"l{li}_gate_tok_b")
    dymem, dzm = _gate_bwd(s["ymem"], pr["zm"], dgm, f"l{li}_gate_mem_b")
    dqm, dkv = _memattn_bwd(pr["qm"], kv, dymem, f"l{li}_mem_b")
    dsegs = {}
    for g, (window, d) in enumerate(DILATED_GROUPS):
        dq, dk, dv = _attn_bwd(pr[f"q{g}"], pr[f"k{g}"], pr[f"v{g}"], s["ytok"], s["lse"], dytok, g, window, d,
                               f"l{li}_attn{g}_b")
        dsegs[f"q{g}"], dsegs[f"k{g}"], dsegs[f"v{g}"] = dq, dk, dv
    dsegs["qm"], dsegs["zt"], dsegs["zm"] = dqm, dzt, dzm
    keys = list(s["segs"].keys())
    dh = _proj_chain([dsegs[k] for k in keys], [s["segs"][k] for k in keys], f"l{li}")
    dwin = [_mm(s["h"], dsegs[k], "tn", BF16, f"l{li}_dwin_{k}") for k in keys]
    dh, dwin, dwo = lax.optimization_barrier((dh, dwin, dwo))
    dx, dxb, dnorm = _rms_bwd(x, p["norm_g"], dh, dout, f"l{li}_rms_b")
    return dx, dxb, dkv, dict(norm_g=dnorm, w_in=dwin, w_out=dwo)


def _local_step(x, mem, mem_n, tgt, mem_norm_g, final_norm_g, n_layers, layer_params, on_layer_grads):
    layers, kvs, saved = [], [], []
    for li in range(n_layers):
        p, x = layer_params(li, x)
        fwd = _ssd_layer_fwd if li % 2 == 0 else _attn_layer_fwd
        kv = _mm(mem_n, p["w_mem_kv"], "nn", F32, f"l{li}_kv")
        x, s = fwd(x, kv, p, li)
        layers.append(p)
        kvs.append(kv)
        saved.append(s)
    loss, dx, dxb, dfinal = _loss_head(x, final_norm_g, tgt, "loss_head")
    dmem_n = None
    for li in reversed(range(n_layers)):
        p = layers[li]
        bwd = _ssd_layer_bwd if li % 2 == 0 else _attn_layer_bwd
        dx, dxb, dkv, g = bwd(dx, dxb, kvs[li], p, saved[li], li)
        g["w_mem_kv"] = _mm(mem_n, dkv, "tn", BF16, f"l{li}_dwkv")
        dmem_n = _mm(dkv, p["w_mem_kv"], "nt", F32, f"l{li}_dmem", add=dmem_n)
        dx, dxb = on_layer_grads(li, g, dx, dxb)
    _, _, dmem_g = _rms_bwd(mem, mem_norm_g, dmem_n, None, "mem_rms_b")
    return loss, dx, dmem_g, dfinal


_SSD_SMALL = ["norm_g", "conv_w", "conv_b", "dt_bias", "a_log", "d_skip", "ssd_norm_g"]
_ATTN_SMALL = ["norm_g"]
_SSD_ORDER = ["norm_g", "w_in", "conv_w", "conv_b", "dt_bias", "a_log", "d_skip", "ssd_norm_g", "w_mem_kv", "w_out"]
_ATTN_ORDER = ["norm_g", "w_in", "w_mem_kv", "w_out"]


def _pack(arrs):
    flat = jnp.concatenate([a.reshape(-1).astype(F32) for a in arrs])
    n = flat.shape[0]
    pad = (-n) % (8 * LANES)
    return jnp.pad(flat, (0, pad)).reshape(-1, LANES)


def _unpack(mat, shapes):
    flat = mat.reshape(-1)
    out, o = [], 0
    for shp in shapes:
        n = math.prod(shp)
        out.append(flat[o:o + n].reshape(shp))
        o += n
    return out


def kernel(x, mem, mem_norm_g, final_norm_g, norm_g_0, w_in_0, conv_w_0, conv_b_0, dt_bias_0, a_log_0, d_skip_0, ssd_norm_g_0, w_mem_kv_0, w_out_0, norm_g_1, w_in_1, w_mem_kv_1, w_out_1, norm_g_2, w_in_2, conv_w_2, conv_b_2, dt_bias_2, a_log_2, d_skip_2, ssd_norm_g_2, w_mem_kv_2, w_out_2, norm_g_3, w_in_3, w_mem_kv_3, w_out_3, loss_target, m_mem_norm_g, m_final_norm_g, m_norm_g_0, m_w_in_0, m_conv_w_0, m_conv_b_0, m_dt_bias_0, m_a_log_0, m_d_skip_0, m_ssd_norm_g_0, m_w_mem_kv_0, m_w_out_0, m_norm_g_1, m_w_in_1, m_w_mem_kv_1, m_w_out_1, m_norm_g_2, m_w_in_2, m_conv_w_2, m_conv_b_2, m_dt_bias_2, m_a_log_2, m_d_skip_2, m_ssd_norm_g_2, m_w_mem_kv_2, m_w_out_2, m_norm_g_3, m_w_in_3, m_w_mem_kv_3, m_w_out_3, v_mem_norm_g, v_final_norm_g, v_norm_g_0, v_w_in_0, v_conv_w_0, v_conv_b_0, v_dt_bias_0, v_a_log_0, v_d_skip_0, v_ssd_norm_g_0, v_w_mem_kv_0, v_w_out_0, v_norm_g_1, v_w_in_1, v_w_mem_kv_1, v_w_out_1, v_norm_g_2, v_w_in_2, v_conv_w_2, v_conv_b_2, v_dt_bias_2, v_a_log_2, v_d_skip_2, v_ssd_norm_g_2, v_w_mem_kv_2, v_w_out_2, v_norm_g_3, v_w_in_3, v_w_mem_kv_3, v_w_out_3):
    a = dict(locals())
    names = ["mem_norm_g", "final_norm_g"]
    for li in range(DEPTH):
        names += [f"{k}_{li}" for k in (_SSD_ORDER if li % 2 == 0 else _ATTN_ORDER)]
    W = {n: a[n] for n in names}
    Mo = {n: a["m_" + n] for n in names}
    Vo = {n: a["v_" + n] for n in names}
    NX = 4
    chip = 2 * lax.axis_index("x") + lax.axis_index("y")

    gathered = []
    mem_n = None
    for li in range(DEPTH):
        mats = [W[f"w_in_{li}"], W[f"w_mem_kv_{li}"], W[f"w_out_{li}"]]
        if li == 1:
            mem_n = _rms_fwd(mem[0], W["mem_norm_g"], "mem_rms")
            later = [[W[f"{k}_{lj}"] for k in ("w_in", "w_mem_kv", "w_out")] for lj in range(1, DEPTH)]
            later, mem_n = lax.optimization_barrier((later, mem_n))
        if li >= 1:
            mats = later[li - 1]
        srcs = [m.astype(BF16) for m in mats]
        if li % 2 == 0:
            srcs.append(W[f"conv_w_{li}"])
        gathered.append(_comm_async(f"gather_w{li}", li, "xy", srcs, ["gather"] * len(srcs)))

    def layer_params(li, xin):
        got = gathered[li]
        if li > 0:
            got, xin = lax.optimization_barrier((got, xin))
        rows = lambda g: g.reshape((-1,) + g.shape[2:])
        parts = lambda g: [g[k] for k in range(NX)]
        p = dict(w_in=parts(got[0]), w_mem_kv=rows(got[1]), w_out=rows(got[2]), norm_g=W[f"norm_g_{li}"])
        if li % 2 == 0:
            p.update(conv_w=jnp.concatenate(parts(got[3]), axis=1), conv_b=W[f"conv_b_{li}"],
                     dt_bias=W[f"dt_bias_{li}"], a_log=W[f"a_log_{li}"], d_skip=W[f"d_skip_{li}"],
                     ssd_norm_g=W[f"ssd_norm_g_{li}"])
        return p, xin

    G, Dl, Mn, Vn = {}, {}, {}, {}
    grads = [None] * DEPTH
    in_flight = []

    def finish_exchange(li, got):
        parts = [_sum_lead(t, f"l{li}_gsum{n}") for n, t in enumerate(got)]
        theirs = _comm_async(f"swap_g{li}", 2 * DEPTH + li, "c", parts, ["swap"] * 3)
        for nm, mine, other in zip(["w_in", "w_mem_kv", "w_out"], parts, theirs):
            key = f"{nm}_{li}"
            G[key], Dl[key], Mn[key], Vn[key] = _adamw(W[key], Mo[key], Vo[key], [mine, other], f"adamw_{key}")

    def on_layer_grads(li, g, dx, dxb):
        grads[li] = g
        dwin = g["w_in"]
        cw = sum(t.shape[1] for t in dwin) // NX
        chunks = [jnp.stack([_take_cols(dwin, k * cw, (k + 1) * cw) for k in range(NX)]),
                  g["w_mem_kv"].reshape((NX, -1) + g["w_mem_kv"].shape[1:]),
                  g["w_out"].reshape((NX, -1) + g["w_out"].shape[1:])]
        prev = in_flight.pop() if in_flight else None
        pgot = prev[1] if prev else []
        chunks, pgot, dx, dxb = lax.optimization_barrier((chunks, pgot, dx, dxb))
        in_flight.append((li, _comm_async(f"xchg_g{li}", DEPTH + li, "xy", chunks, ["a2a"] * 3)))
        if prev:
            finish_exchange(prev[0], pgot)
        return dx, dxb

    loss_l, dx, dmem_g, dfinal = _local_step(x[0], mem[0], mem_n, loss_target[0], W["mem_norm_g"], W["final_norm_g"],
                                             DEPTH, layer_params, on_layer_grads)
    last_li, last_got = in_flight.pop()
    updates = (G, Dl, Mn, Vn)
    last_got, tied = lax.optimization_barrier((last_got, [dict(u) for u in updates]))
    for u, t in zip(updates, tied):
        u.update(t)
    finish_exchange(last_li, last_got)
    loss = lax.psum(loss_l, ("x", "y", "c"))

    small_names = ["mem_norm_g", "final_norm_g"]
    small_grads = [dmem_g, dfinal]
    for li in range(DEPTH):
        for k in (_SSD_SMALL if li % 2 == 0 else _ATTN_SMALL):
            small_names.append(f"{k}_{li}")
            small_grads.append(grads[li][k])
    shapes = [tuple(t.shape) for t in small_grads]
    allg = _comm("gather_small", "xyc", [_pack(small_grads)], ["gather"])[0]
    gsum = _unpack(_sum_lead(allg, "small_gsum"), shapes)
    small_w, small_m, small_v, small_g = [], [], [], []
    for nme, gv in zip(small_names, gsum):
        if nme.startswith("conv_w"):
            cw = W[nme].shape[1]
            gv = lax.dynamic_slice_in_dim(gv, chip * cw, cw, axis=1)
        small_g.append(gv)
        small_w.append(W[nme])
        small_m.append(Mo[nme])
        small_v.append(Vo[nme])
    sshapes = [tuple(t.shape) for t in small_g]
    res = _adamw(_pack(small_w), _pack(small_m), _pack(small_v), [_pack(small_g)], "adamw_small")
    for dst, mat in zip((G, Dl, Mn, Vn), res):
        for nme, t in zip(small_names, _unpack(mat, sshapes)):
            dst[nme] = t

    return (loss, dx[None], *[G[n] for n in names], *[Dl[n] for n in names], *[Mn[n] for n in names],
            *[Vn[n] for n in names])
```

```python
import functools
import math

import jax
import jax.numpy as jnp
from jax import lax
from jax.experimental import pallas as pl
from jax.experimental.pallas import tpu as pltpu
from jax.experimental.pallas import tpu_sc as plsc

F32 = jnp.float32
BF16 = jnp.bfloat16

EPS = 1e-6
MEM_HEADS = 4
SSD_HEAD_DIM = 64
SSD_GROUPS = 8
SSD_STATE = 128
SSD_CONV = 4
SSD_CHUNK = 128
ATTN_HEAD_DIM = 128
ATTN_BLOCK = 128
DILATED_GROUPS = ((128, 1), (512, 4), (2048, 16))
ALIBI_MAX_EXP = 8.0
DEPTH = 4

ADAM_LR = 0.001
ADAM_B1 = 0.9
ADAM_B2 = 0.999
ADAM_EPS = 1e-08
ADAM_WD = 0.01
ADAM_STEP = 10

LANES = 128
VMEM_LIMIT_BYTES = 48 * 1024 * 1024
NEG = -1e30
MESH = pl.DeviceIdType.MESH


def _cparams(*sem):
    return pltpu.CompilerParams(dimension_semantics=tuple(sem), vmem_limit_bytes=VMEM_LIMIT_BYTES)


def _tile(dim, pref, unit=LANES):
    if dim <= pref:
        return dim
    t = (pref // unit) * unit
    while t >= unit:
        if dim % t == 0:
            return t
        t -= unit
    return dim


def _ew_tiles(rows, cols):
    tc = cols if (cols % LANES != 0 or cols <= 2048) else _tile(cols, 2048)
    tr = rows
    while tr * tc > 256 * 1024 and tr % 2 == 0 and (tr // 2) % 8 == 0:
        tr //= 2
    return tr, tc


def _sigmoid(v):
    return 1.0 / (1.0 + jnp.exp(-v))


def _dot(a, b):
    return jnp.dot(a, b, preferred_element_type=F32)


def _dot_nt(a, b):
    return lax.dot_general(a, b, (((1,), (1,)), ((), ())), preferred_element_type=F32)


def _dot_tn(a, b):
    return lax.dot_general(a, b, (((0,), (0,)), ((), ())), preferred_element_type=F32)


def _split3(v):
    hi = v.astype(BF16)
    r = v - hi.astype(F32)
    mid = r.astype(BF16)
    lo = (r - mid.astype(F32)).astype(BF16)
    return hi, mid, lo


def _xdot(v, onehot):
    hi, mid, lo = _split3(v)
    return _dot(hi, onehot) + _dot(mid, onehot) + _dot(lo, onehot)


def _xdot_l(onehot, v):
    hi, mid, lo = _split3(v)
    return _dot(onehot, hi) + _dot(onehot, mid) + _dot(onehot, lo)


def _mm(a, b, mode, out_dtype, name, add=None):
    if mode == "nn":
        M, K = a.shape
        N = b.shape[1]
    elif mode == "nt":
        M, K = a.shape
        N = b.shape[0]
    else:
        K, M = a.shape
        N = b.shape[1]
    tm, tn, tk = (2048, 1024, 1024) if mode == "tn" else (512, 1024, 3072)
    tm, tn, tk = _tile(M, tm, 8 if M < LANES else LANES), _tile(N, tn), _tile(K, tk)
    nk = K // tk
    has_add = add is not None

    def product(a_ref, b_ref):
        av = a_ref[...].astype(BF16)
        bv = b_ref[...].astype(BF16)
        if mode == "nn":
            return _dot(av, bv)
        if mode == "nt":
            return _dot_nt(av, bv)
        return _dot_tn(av, bv)

    def body(*refs):
        a_ref, b_ref = refs[:2]
        add_ref = refs[2] if has_add else None
        o_ref = refs[3] if has_add else refs[2]

        def finish(r):
            if has_add:
                r = r + add_ref[...]
            o_ref[...] = r.astype(out_dtype)

        if nk == 1:
            finish(product(a_ref, b_ref))
            return
        acc = refs[-1]
        k = pl.program_id(2)

        @pl.when(k == 0)
        def _():
            acc[...] = product(a_ref, b_ref)

        @pl.when((k > 0) & (k < nk - 1))
        def _():
            acc[...] += product(a_ref, b_ref)

        @pl.when(k == nk - 1)
        def _():
            finish(acc[...] + product(a_ref, b_ref))

    if mode == "nn":
        a_spec = pl.BlockSpec((tm, tk), lambda j, i, k: (i, k))
        b_spec = pl.BlockSpec((tk, tn), lambda j, i, k: (k, j))
    elif mode == "nt":
        a_spec = pl.BlockSpec((tm, tk), lambda j, i, k: (i, k))
        b_spec = pl.BlockSpec((tn, tk), lambda j, i, k: (j, k))
    else:
        a_spec = pl.BlockSpec((tk, tm), lambda j, i, k: (k, i))
        b_spec = pl.BlockSpec((tk, tn), lambda j, i, k: (k, j))
    o_spec = pl.BlockSpec((tm, tn), lambda j, i, k: (i, j))
    in_specs = [a_spec, b_spec] + ([o_spec] if has_add else [])
    args = (a, b) + ((add,) if has_add else ())
    return pl.pallas_call(
        body, name=name, grid=(N // tn, M // tm, nk), in_specs=in_specs, out_specs=o_spec,
        out_shape=jax.ShapeDtypeStruct((M, N), out_dtype),
        scratch_shapes=[pltpu.VMEM((tm, tn), F32)] if nk > 1 else [],
        compiler_params=_cparams("parallel", "parallel", "arbitrary"),
    )(*args)


def _rms_fwd(x, g, name):
    R, Dm = x.shape
    tr = _tile(R, 256, 8)

    def body(x_ref, g_ref, o_ref):
        xv = x_ref[...]
        r = lax.rsqrt(jnp.mean(xv * xv, axis=-1, keepdims=True) + EPS)
        o_ref[...] = (xv * r * g_ref[...]).astype(BF16)

    return pl.pallas_call(
        body, name=name, grid=(R // tr,),
        in_specs=[pl.BlockSpec((tr, Dm), lambda i: (i, 0)), pl.BlockSpec((1, Dm), lambda i: (0, 0))],
        out_specs=pl.BlockSpec((tr, Dm), lambda i: (i, 0)),
        out_shape=jax.ShapeDtypeStruct((R, Dm), BF16),
        compiler_params=_cparams("parallel"),
    )(x, g.reshape(1, Dm))


def _rms_bwd(x, g, dh, dres, name):
    R, Dm = x.shape
    tr = _tile(R, 256, 8)
    has_res = dres is not None

    def body(*refs):
        if has_res:
            x_ref, g_ref, dh_ref, dres_ref, dx_ref, dxb_ref, dg_ref = refs
        else:
            x_ref, g_ref, dh_ref, dx_ref, dxb_ref, dg_ref = refs
        xv = x_ref[...]
        r = lax.rsqrt(jnp.mean(xv * xv, axis=-1, keepdims=True) + EPS)
        xhat = xv * r
        dhv = dh_ref[...]
        dxhat = dhv * g_ref[...]
        dx = r * (dxhat - xhat * jnp.mean(dxhat * xhat, axis=-1, keepdims=True))
        if has_res:
            dx = dx + dres_ref[...]
        dx_ref[...] = dx
        dxb_ref[...] = dx.astype(BF16)

        @pl.when(pl.program_id(0) == 0)
        def _():
            dg_ref[...] = jnp.zeros_like(dg_ref)

        dg_ref[...] += jnp.sum(dhv * xhat, axis=0, keepdims=True)

    row = pl.BlockSpec((tr, Dm), lambda i: (i, 0))
    vec = pl.BlockSpec((1, Dm), lambda i: (0, 0))
    in_specs = [row, vec, row] + ([row] if has_res else [])
    args = (x, g.reshape(1, Dm), dh) + ((dres,) if has_res else ())
    dx, dxb, dg = pl.pallas_call(
        body, name=name, grid=(R // tr,), in_specs=in_specs, out_specs=[row, row, vec],
        out_shape=[jax.ShapeDtypeStruct((R, Dm), F32), jax.ShapeDtypeStruct((R, Dm), BF16),
                   jax.ShapeDtypeStruct((1, Dm), F32)],
        compiler_params=_cparams("arbitrary"),
    )(*args)
    return dx, dxb, dg.reshape(Dm)


def _loss_head(x, g, tgt, name):
    R, Dm = x.shape
    tr = _tile(R, 256, 8)

    def body(x_ref, g_ref, t_ref, loss_ref, dx_ref, dxb_ref, dg_ref):
        xv = x_ref[...]
        gv = g_ref[...]
        r = lax.rsqrt(jnp.mean(xv * xv, axis=-1, keepdims=True) + EPS)
        xhat = xv * r
        e = xhat * gv - t_ref[...]
        part = jnp.sum(jnp.mean(e * e, axis=-1, keepdims=True), axis=0, keepdims=True) * 0.5
        dy = e * (1.0 / Dm)
        dxhat = dy * gv
        dx = r * (dxhat - xhat * jnp.mean(dxhat * xhat, axis=-1, keepdims=True))
        dx_ref[...] = dx
        dxb_ref[...] = dx.astype(BF16)

        @pl.when(pl.program_id(0) == 0)
        def _():
            dg_ref[...] = jnp.zeros_like(dg_ref)
            loss_ref[...] = jnp.zeros_like(loss_ref)

        dg_ref[...] += jnp.sum(dy * xhat, axis=0, keepdims=True)
        loss_ref[...] += jnp.broadcast_to(part, loss_ref.shape)

    row = pl.BlockSpec((tr, Dm), lambda i: (i, 0))
    vec = pl.BlockSpec((1, Dm), lambda i: (0, 0))
    lsp = pl.BlockSpec((1, LANES), lambda i: (0, 0))
    loss, dx, dxb, dg = pl.pallas_call(
        body, name=name, grid=(R // tr,), in_specs=[row, vec, row], out_specs=[lsp, row, row, vec],
        out_shape=[jax.ShapeDtypeStruct((1, LANES), F32), jax.ShapeDtypeStruct((R, Dm), F32),
                   jax.ShapeDtypeStruct((R, Dm), BF16), jax.ShapeDtypeStruct((1, Dm), F32)],
        compiler_params=_cparams("arbitrary"),
    )(x, g.reshape(1, Dm), tgt)
    return loss[0, 0], dx, dxb, dg.reshape(Dm)


def _elementwise(fn, mats, vecs, out_dtypes, name):
    R, C = mats[0].shape
    tr, tc = _ew_tiles(R, C)
    nm, nv, no = len(mats), len(vecs), len(out_dtypes)

    def body(*refs):
        ins = [r[...] for r in refs[:nm + nv]]
        outs = fn(*ins)
        for o_ref, o in zip(refs[nm + nv:], outs):
            o_ref[...] = o.astype(o_ref.dtype)

    blk = pl.BlockSpec((tr, tc), lambda i, j: (i, j))
    vblk = pl.BlockSpec((1, tc), lambda i, j: (0, j))
    res = pl.pallas_call(
        body, name=name, grid=(R // tr, C // tc),
        in_specs=[blk] * nm + [vblk] * nv, out_specs=[blk] * no,
        out_shape=[jax.ShapeDtypeStruct((R, C), dt) for dt in out_dtypes],
        compiler_params=_cparams("parallel", "parallel"),
    )(*mats, *[v.reshape(1, C) for v in vecs])
    return res


def _sum_lead(arr, name):
    P_, R, C = arr.shape
    tr, tc = _ew_tiles(R, C)

    def body(a_ref, o_ref):
        s = a_ref[0].astype(F32)
        for p in range(1, P_):
            s = s + a_ref[p].astype(F32)
        o_ref[...] = s

    return pl.pallas_call(
        body, name=name, grid=(R // tr, C // tc),
        in_specs=[pl.BlockSpec((P_, tr, tc), lambda i, j: (0, i, j))],
        out_specs=pl.BlockSpec((tr, tc), lambda i, j: (i, j)),
        out_shape=jax.ShapeDtypeStruct((R, C), F32),
        compiler_params=_cparams("parallel", "parallel"),
    )(arr)


def _adamw(w, m, v, gparts, name):
    P_ = len(gparts)
    R, C = w.shape
    tr, tc = _ew_tiles(R, C)
    c1 = 1.0 / (1.0 - ADAM_B1 ** ADAM_STEP)
    c2 = 1.0 / (1.0 - ADAM_B2 ** ADAM_STEP)

    def body(w_ref, m_ref, v_ref, *rest):
        g_refs, (go_ref, d_ref, mo_ref, vo_ref) = rest[:P_], rest[P_:]
        g = g_refs[0][...]
        for g_ref in g_refs[1:]:
            g = g + g_ref[...]
        mn = ADAM_B1 * m_ref[...] + (1.0 - ADAM_B1) * g
        vn = ADAM_B2 * v_ref[...] + (1.0 - ADAM_B2) * (g * g)
        m_hat = mn * c1
        v_hat = vn * c2
        d_ref[...] = -ADAM_LR * (m_hat / (jnp.sqrt(v_hat) + ADAM_EPS) + ADAM_WD * w_ref[...])
        go_ref[...] = g
        mo_ref[...] = mn
        vo_ref[...] = vn

    blk = pl.BlockSpec((tr, tc), lambda i, j: (i, j))
    sds = jax.ShapeDtypeStruct((R, C), F32)
    return pl.pallas_call(
        body, name=name, grid=(R // tr, C // tc),
        in_specs=[blk] * (3 + P_), out_specs=[blk] * 4, out_shape=[sds] * 4,
        compiler_params=_cparams("parallel", "parallel"),
    )(w, m, v, *gparts)


SUBLANES = 8


def _conv_pre(u, up, w_ref, b, first):
    tr = u.shape[0]
    rows = lax.broadcasted_iota(jnp.int32, u.shape, 0)
    keep = 1.0 - first.astype(F32)
    acc = b + w_ref[SSD_CONV - 1:SSD_CONV, :] * u
    shifted = []
    for j in range(1, SSD_CONV):
        su = pltpu.roll(u, j, 0)
        sp = jnp.tile(pltpu.roll(up, j, 0) * keep, (tr // SUBLANES, 1))
        sh = jnp.where(rows < j, sp, su)
        shifted.append(sh)
        acc = acc + w_ref[SSD_CONV - 1 - j:SSD_CONV - j, :] * sh
    return acc, shifted


def _conv_fwd(u, w, b, name):
    T, C = u.shape
    tr, tc = _tile(T, 256, 8), _tile(C, 1024)

    def body(u_ref, up_ref, w_ref, b_ref, o_ref):
        pre, _ = _conv_pre(u_ref[...], up_ref[...], w_ref, b_ref[...], pl.program_id(0) == 0)
        o_ref[...] = pre * _sigmoid(pre)

    return pl.pallas_call(
        body, name=name, grid=(T // tr, C // tc),
        in_specs=[pl.BlockSpec((tr, tc), lambda i, j: (i, j)),
                  pl.BlockSpec((SUBLANES, tc), lambda i, j: (jnp.maximum(i * (tr // SUBLANES) - 1, 0), j)),
                  pl.BlockSpec((SSD_CONV, tc), lambda i, j: (0, j)),
                  pl.BlockSpec((1, tc), lambda i, j: (0, j))],
        out_specs=pl.BlockSpec((tr, tc), lambda i, j: (i, j)),
        out_shape=jax.ShapeDtypeStruct((T, C), F32),
        compiler_params=_cparams("parallel", "parallel"),
    )(u, u, w, b.reshape(1, C))


def _conv_bwd_pre(u, w, b, dact, name):
    T, C = u.shape
    tr, tc = _tile(T, 256, 8), _tile(C, 1024)

    def body(u_ref, up_ref, w_ref, b_ref, da_ref, dp_ref, dw_ref, db_ref):
        i = pl.program_id(1)
        uv = u_ref[...]
        pre, shifted = _conv_pre(uv, up_ref[...], w_ref, b_ref[...], i == 0)
        sg = _sigmoid(pre)
        dpre = da_ref[...] * (sg * (1.0 + pre * (1.0 - sg)))
        dp_ref[...] = dpre

        @pl.when(i == 0)
        def _():
            dw_ref[...] = jnp.zeros_like(dw_ref)
            db_ref[...] = jnp.zeros_like(db_ref)

        db_ref[...] += jnp.sum(dpre, axis=0, keepdims=True)
        dw_ref[SSD_CONV - 1:SSD_CONV, :] += jnp.sum(dpre * uv, axis=0, keepdims=True)
        for j in range(1, SSD_CONV):
            dw_ref[SSD_CONV - 1 - j:SSD_CONV - j, :] += jnp.sum(dpre * shifted[j - 1], axis=0, keepdims=True)

    blk = pl.BlockSpec((tr, tc), lambda j, i: (i, j))
    dpre, dw, db = pl.pallas_call(
        body, name=name, grid=(C // tc, T // tr),
        in_specs=[blk, pl.BlockSpec((SUBLANES, tc), lambda j, i: (jnp.maximum(i * (tr // SUBLANES) - 1, 0), j)),
                  pl.BlockSpec((SSD_CONV, tc), lambda j, i: (0, j)),
                  pl.BlockSpec((1, tc), lambda j, i: (0, j)), blk],
        out_specs=[blk, pl.BlockSpec((SSD_CONV, tc), lambda j, i: (0, j)), pl.BlockSpec((1, tc), lambda j, i: (0, j))],
        out_shape=[jax.ShapeDtypeStruct((T, C), F32), jax.ShapeDtypeStruct((SSD_CONV, C), F32),
                   jax.ShapeDtypeStruct((1, C), F32)],
        compiler_params=_cparams("parallel", "arbitrary"),
    )(u, u, w, b.reshape(1, C), dact)
    return dpre, dw, db.reshape(C)


def _conv_bwd_in(dpre, w, name):
    T, C = dpre.shape
    tr, tc = _tile(T, 256, 8), _tile(C, 1024)
    nb = T // tr

    def body(d_ref, dn_ref, w_ref, o_ref):
        d = d_ref[...]
        keep = 1.0 - (pl.program_id(0) == nb - 1).astype(F32)
        dn = dn_ref[...] * keep
        rows = lax.broadcasted_iota(jnp.int32, d.shape, 0)
        acc = w_ref[SSD_CONV - 1:SSD_CONV, :] * d
        for j in range(1, SSD_CONV):
            sd = pltpu.roll(d, tr - j, 0)
            sn = jnp.tile(pltpu.roll(dn, SUBLANES - j, 0), (tr // SUBLANES, 1))
            acc = acc + w_ref[SSD_CONV - 1 - j:SSD_CONV - j, :] * jnp.where(rows >= tr - j, sn, sd)
        o_ref[...] = acc.astype(BF16)

    return pl.pallas_call(
        body, name=name, grid=(nb, C // tc),
        in_specs=[pl.BlockSpec((tr, tc), lambda i, j: (i, j)),
                  pl.BlockSpec((SUBLANES, tc), lambda i, j: (jnp.minimum((i + 1) * (tr // SUBLANES), T // SUBLANES - 1), j)),
                  pl.BlockSpec((SSD_CONV, tc), lambda i, j: (0, j))],
        out_specs=pl.BlockSpec((tr, tc), lambda i, j: (i, j)),
        out_shape=jax.ShapeDtypeStruct((T, C), BF16),
        compiler_params=_cparams("parallel", "parallel"),
    )(dpre, dpre, w)


def _mem_probs(qh, kh, scale):
    s = _dot_nt(qh.astype(BF16), kh.astype(BF16)) * scale
    m = jnp.max(s, axis=-1, keepdims=True)
    p = jnp.exp(s - m)
    return p / jnp.sum(p, axis=-1, keepdims=True)


def _memattn_fwd(q, kv, name):
    T, MW = q.shape
    NM = kv.shape[0]
    hd = MW // MEM_HEADS
    scale = hd ** -0.5
    tq = _tile(T, 512, 8)

    def body(q_ref, kv_ref, o_ref):
        for h in range(MEM_HEADS):
            sl = slice(h * hd, (h + 1) * hd)
            p = _mem_probs(q_ref[:, sl], kv_ref[:, sl], scale)
            vh = kv_ref[:, MW + h * hd:MW + (h + 1) * hd]
            o_ref[:, sl] = _dot(p.astype(BF16), vh.astype(BF16))

    return pl.pallas_call(
        body, name=name, grid=(T // tq,),
        in_specs=[pl.BlockSpec((tq, MW), lambda i: (i, 0)), pl.BlockSpec((NM, 2 * MW), lambda i: (0, 0))],
        out_specs=pl.BlockSpec((tq, MW), lambda i: (i, 0)),
        out_shape=jax.ShapeDtypeStruct((T, MW), F32),
        compiler_params=_cparams("parallel"),
    )(q, kv)


def _memattn_bwd(q, kv, dy, name):
    T, MW = q.shape
    NM = kv.shape[0]
    hd = MW // MEM_HEADS
    scale = hd ** -0.5
    tq = _tile(T, 512, 8)

    def body(q_ref, kv_ref, dy_ref, dq_ref, dkv_ref):
        @pl.when(pl.program_id(0) == 0)
        def _():
            dkv_ref[...] = jnp.zeros_like(dkv_ref)

        for h in range(MEM_HEADS):
            sl = slice(h * hd, (h + 1) * hd)
            vsl = slice(MW + h * hd, MW + (h + 1) * hd)
            qh = q_ref[:, sl]
            kh = kv_ref[:, sl]
            vh = kv_ref[:, vsl]
            dyh = dy_ref[:, sl].astype(BF16)
            p = _mem_probs(qh, kh, scale)
            dp = _dot_nt(dyh, vh.astype(BF16))
            ds = p * (dp - jnp.sum(dp * p, axis=-1, keepdims=True)) * scale
            dq_ref[:, sl] = _dot(ds.astype(BF16), kh.astype(BF16)).astype(BF16)
            dkv_ref[:, sl] += _dot(ds.T.astype(BF16), qh.astype(BF16))
            dkv_ref[:, vsl] += _dot(p.T.astype(BF16), dyh)

    return pl.pallas_call(
        body, name=name, grid=(T // tq,),
        in_specs=[pl.BlockSpec((tq, MW), lambda i: (i, 0)), pl.BlockSpec((NM, 2 * MW), lambda i: (0, 0)),
                  pl.BlockSpec((tq, MW), lambda i: (i, 0))],
        out_specs=[pl.BlockSpec((tq, MW), lambda i: (i, 0)), pl.BlockSpec((NM, 2 * MW), lambda i: (0, 0))],
        out_shape=[jax.ShapeDtypeStruct((T, MW), BF16), jax.ShapeDtypeStruct((NM, 2 * MW), F32)],
        compiler_params=_cparams("arbitrary"),
    )(q, kv, dy)


def _silu_parts(z):
    sg = _sigmoid(z)
    return z * sg, sg * (1.0 + z * (1.0 - sg))


def _gate_fwd(a, z, name):
    return _elementwise(lambda av, zv: (av * _silu_parts(zv)[0],), [a, z], [], [BF16], name)[0]


def _gate_bwd(a, z, d, name):
    def fn(av, zv, dv):
        s, ds = _silu_parts(zv)
        return dv * s, dv * av * ds
    return _elementwise(fn, [a, z, d], [], [F32, BF16], name)


def _gate_norm_fwd(y, z, g, groups, name):
    T, C = y.shape
    gw = C // groups
    tr = _tile(T, 128, 8)

    def body(y_ref, z_ref, g_ref, o_ref):
        for k in range(groups):
            sl = slice(k * gw, (k + 1) * gw)
            u = y_ref[:, sl] * _silu_parts(z_ref[:, sl])[0]
            r = lax.rsqrt(jnp.mean(u * u, axis=-1, keepdims=True) + EPS)
            o_ref[:, sl] = (u * r * g_ref[:, sl]).astype(BF16)

    row = pl.BlockSpec((tr, C), lambda i: (i, 0))
    return pl.pallas_call(
        body, name=name, grid=(T // tr,), in_specs=[row, row, pl.BlockSpec((1, C), lambda i: (0, 0))],
        out_specs=row, out_shape=jax.ShapeDtypeStruct((T, C), BF16),
        compiler_params=_cparams("parallel"),
    )(y, z, g.reshape(1, C))


def _gate_norm_bwd(y, z, g, d, groups, name):
    T, C = y.shape
    gw = C // groups
    tr = _tile(T, 128, 8)

    def body(y_ref, z_ref, g_ref, d_ref, dy_ref, dz_ref, dg_ref):
        @pl.when(pl.program_id(0) == 0)
        def _():
            dg_ref[...] = jnp.zeros_like(dg_ref)

        for k in range(groups):
            sl = slice(k * gw, (k + 1) * gw)
            yv = y_ref[:, sl]
            s, ds = _silu_parts(z_ref[:, sl])
            u = yv * s
            r = lax.rsqrt(jnp.mean(u * u, axis=-1, keepdims=True) + EPS)
            uhat = u * r
            dv = d_ref[:, sl]
            dg_ref[:, sl] += jnp.sum(dv * uhat, axis=0, keepdims=True)
            duhat = dv * g_ref[:, sl]
            du = r * (duhat - uhat * jnp.mean(duhat * uhat, axis=-1, keepdims=True))
            dy_ref[:, sl] = du * s
            dz_ref[:, sl] = (du * yv * ds).astype(BF16)

    row = pl.BlockSpec((tr, C), lambda i: (i, 0))
    vec = pl.BlockSpec((1, C), lambda i: (0, 0))
    dy, dz, dg = pl.pallas_call(
        body, name=name, grid=(T // tr,), in_specs=[row, row, vec, row], out_specs=[row, row, vec],
        out_shape=[jax.ShapeDtypeStruct((T, C), F32), jax.ShapeDtypeStruct((T, C), BF16),
                   jax.ShapeDtypeStruct((1, C), F32)],
        compiler_params=_cparams("arbitrary"),
    )(y, z, g.reshape(1, C), d)
    return dy, dz, dg.reshape(C)


def _ssd_common(raw, bias, alog, Q, HP, HPG):
    P_ = SSD_HEAD_DIM
    dt_in = raw + bias
    dt = jnp.maximum(dt_in, 0.0) + jnp.log(1.0 + jnp.exp(-jnp.abs(dt_in)))
    a = -jnp.exp(alog)
    r_q = lax.broadcasted_iota(jnp.int32, (Q, Q), 0)
    c_q = lax.broadcasted_iota(jnp.int32, (Q, Q), 1)
    causal = r_q >= c_q
    tril = causal.astype(BF16)
    A = _xdot_l(tril, dt * a)
    e_r = lax.broadcasted_iota(jnp.int32, (LANES, HP), 0)
    e_c = lax.broadcasted_iota(jnp.int32, (LANES, HP), 1)
    E = ((e_c >= e_r * P_) & (e_c < (e_r + 1) * P_) & (e_r < HPG)).astype(BF16)
    return dt, a, A, causal, E


def _head_cols(v, vt, j):
    lane = lax.broadcasted_iota(jnp.int32, v.shape, 1)
    sub = lax.broadcasted_iota(jnp.int32, vt.shape, 0)
    col = jnp.sum(jnp.where(lane == j, v, 0.0), axis=-1, keepdims=True)
    row = jnp.sum(jnp.where(sub == j, vt, 0.0), axis=0, keepdims=True)
    return col, row


def _ssd_fwd(act, raw_g, bias_g, alog_g, dsk_g, TOK, name):
    T = act.shape[0]
    G, N, Q, P_ = SSD_GROUPS, SSD_STATE, SSD_CHUNK, SSD_HEAD_DIM
    HP = TOK // G
    HPG = HP // P_
    NC = T // Q

    def body(x_ref, b_ref, c_ref, raw_ref, bias_ref, alog_ref, dsk_ref, y_ref, hp_ref, hT):
        @pl.when(pl.program_id(1) == 0)
        def _():
            hT[...] = jnp.zeros_like(hT)

        xs = x_ref[...]
        Bb = b_ref[...].astype(BF16)
        Cb = c_ref[...].astype(BF16)
        dt, a, A, causal, E = _ssd_common(raw_ref[...], bias_ref[...], alog_ref[...], Q, HP, HPG)
        AT = A.T
        dt_e = _xdot(dt, E)
        A_e = _xdot(A, E)
        dsk_e = _xdot(jnp.broadcast_to(dsk_ref[...], (Q, LANES)), E)
        rows = lax.broadcasted_iota(jnp.int32, (Q, HP), 0)
        cols = lax.broadcasted_iota(jnp.int32, (Q, HP), 1)
        Al_e = jnp.sum(jnp.where(rows == Q - 1, A_e, 0.0), axis=0, keepdims=True)
        xdt = xs * dt_e
        hprev = hT[...]
        hp_ref[...] = hprev
        CB = _dot_nt(Cb, Bb)
        y = _dot(Cb, hprev.astype(BF16)) * jnp.exp(A_e) + dsk_e * xs
        M, xh = [], []
        for j in range(HPG):
            a_col, a_row = _head_cols(A, AT, j)
            M.append((CB * jnp.exp(jnp.where(causal, a_col - a_row, NEG))).astype(BF16))
            xh.append(jnp.where((cols >= j * P_) & (cols < (j + 1) * P_), xdt, 0.0).astype(BF16))
        y_ref[...] = y + _dot(jnp.concatenate(M, axis=1), jnp.concatenate(xh, axis=0))
        dte = jnp.exp(Al_e - A_e)
        hT[...] = jnp.exp(Al_e) * hprev + _dot(b_ref[...].T.astype(BF16), (xdt * dte).astype(BF16))

    nbx = TOK // N
    par = pl.BlockSpec((None, 1, LANES), lambda g, c: (g, 0, 0))
    return pl.pallas_call(
        body, name=name, grid=(G, NC),
        in_specs=[pl.BlockSpec((Q, HP), lambda g, c: (c, g)),
                  pl.BlockSpec((Q, N), lambda g, c: (c, nbx + g)),
                  pl.BlockSpec((Q, N), lambda g, c: (c, nbx + G + g)),
                  pl.BlockSpec((None, Q, LANES), lambda g, c: (g, c, 0)), par, par, par],
        out_specs=[pl.BlockSpec((Q, HP), lambda g, c: (c, g)),
                   pl.BlockSpec((None, None, N, HP), lambda g, c: (g, c, 0, 0))],
        out_shape=[jax.ShapeDtypeStruct((T, TOK), F32), jax.ShapeDtypeStruct((G, NC, N, HP), F32)],
        scratch_shapes=[pltpu.VMEM((N, HP), F32)],
        compiler_params=_cparams("parallel", "arbitrary"),
    )(act, act, act, raw_g, bias_g, alog_g, dsk_g)


def _ssd_bwd(act, raw_g, bias_g, alog_g, dsk_g, hprev, dy, TOK, name):
    T = act.shape[0]
    G, N, Q, P_ = SSD_GROUPS, SSD_STATE, SSD_CHUNK, SSD_HEAD_DIM
    HP = TOK // G
    HPG = HP // P_
    NC = T // Q

    def body(x_ref, b_ref, c_ref, raw_ref, bias_ref, alog_ref, dsk_ref, hp_ref, dy_ref,
             dx_ref, db_ref, dc_ref, draw_ref, dalog_ref, dbias_ref, ddsk_ref, dHT):
        @pl.when(pl.program_id(1) == 0)
        def _():
            dHT[...] = jnp.zeros_like(dHT)
            dalog_ref[...] = jnp.zeros_like(dalog_ref)
            dbias_ref[...] = jnp.zeros_like(dbias_ref)
            ddsk_ref[...] = jnp.zeros_like(ddsk_ref)

        xs = x_ref[...]
        dyv = dy_ref[...]
        Bm = b_ref[...]
        Cm = c_ref[...]
        Bb = Bm.astype(BF16)
        Cb = Cm.astype(BF16)
        raw_in = raw_ref[...] + bias_ref[...]
        dt, a, A, causal, E = _ssd_common(raw_ref[...], bias_ref[...], alog_ref[...], Q, HP, HPG)
        AT = A.T
        t_r = lax.broadcasted_iota(jnp.int32, (HP, LANES), 0)
        t_c = lax.broadcasted_iota(jnp.int32, (HP, LANES), 1)
        ET = ((t_r >= t_c * P_) & (t_r < (t_c + 1) * P_) & (t_c < HPG)).astype(BF16)
        dt_e = _xdot(dt, E)
        A_e = _xdot(A, E)
        dsk_e = _xdot(jnp.broadcast_to(dsk_ref[...], (Q, LANES)), E)
        rows = lax.broadcasted_iota(jnp.int32, (Q, HP), 0)
        cols = lax.broadcasted_iota(jnp.int32, (Q, HP), 1)
        last = rows == Q - 1
        Al_e = jnp.sum(jnp.where(last, A_e, 0.0), axis=0, keepdims=True)
        eA = jnp.exp(A_e)
        eAl = jnp.exp(Al_e)
        dte = jnp.exp(Al_e - A_e)
        xdt = xs * dt_e
        xdt_b = xdt.astype(BF16)
        CB = _dot_nt(Cb, Bb)
        HT = hp_ref[...]
        HTb = HT.astype(BF16)
        dH = dHT[...]
        dHb = dH.astype(BF16)
        dys = (dyv * eA).astype(BF16)
        CH = _dot(Cb, HTb)
        dC = _dot_nt(dys, HTb)
        dH_prev = _dot(Cm.T.astype(BF16), dys) + eAl * dH
        dAe = dyv * CH * eA
        dAl = eAl * jnp.sum(dH * HT, axis=0, keepdims=True)
        W = _dot(Bb, dHb)
        dxdt = W * dte
        dd = W * xdt * dte
        dB = _dot_nt((xdt * dte).astype(BF16), dHb)
        dAl = dAl + jnp.sum(dd, axis=0, keepdims=True)
        dAe = dAe - dd + jnp.where(last, dAl, 0.0)
        lane = lax.broadcasted_iota(jnp.int32, (Q, LANES), 1)
        sub = lax.broadcasted_iota(jnp.int32, (LANES, Q), 0)
        dCB = jnp.zeros((Q, Q), F32)
        dA_col = jnp.zeros((Q, LANES), F32)
        dA_row = jnp.zeros((LANES, Q), F32)
        dy_heads = jnp.concatenate(
            [jnp.where((cols >= j * P_) & (cols < (j + 1) * P_), dyv, 0.0).astype(BF16) for j in range(HPG)], axis=0)
        G_heads = _dot_nt(dy_heads, xdt_b)
        MT = []
        for j in range(HPG):
            a_col, a_row = _head_cols(A, AT, j)
            L = jnp.exp(jnp.where(causal, a_col - a_row, NEG))
            GL = G_heads[j * Q:(j + 1) * Q] * L
            dCB = dCB + GL
            dLL = GL * CB
            dA_col = dA_col + jnp.where(lane == j, jnp.sum(dLL, axis=-1, keepdims=True), 0.0)
            dA_row = dA_row + jnp.where(sub == j, jnp.sum(dLL, axis=0, keepdims=True), 0.0)
            MT.append((CB * L).T.astype(BF16))
        dxdt = dxdt + _dot(jnp.concatenate(MT, axis=1), dy_heads)
        dC = dC + _dot(dCB.astype(BF16), Bb)
        dB = dB + _dot(dCB.T.astype(BF16), Cb)
        dA = dA_col - dA_row.T + _xdot(dAe, ET)
        triu = (lax.broadcasted_iota(jnp.int32, (Q, Q), 1) >= lax.broadcasted_iota(jnp.int32, (Q, Q), 0)).astype(BF16)
        rcs = _xdot_l(triu, dA)
        ddt = a * rcs + _xdot(dxdt * xs, ET)
        draw = ddt * _sigmoid(raw_in)
        draw_ref[...] = draw
        dalog_ref[...] += jnp.sum(dt * rcs, axis=0, keepdims=True) * a
        dbias_ref[...] += jnp.sum(draw, axis=0, keepdims=True)
        ddsk_ref[...] += jnp.sum(_xdot(dyv * xs, ET), axis=0, keepdims=True)
        dx_ref[...] = dxdt * dt_e + dsk_e * dyv
        db_ref[...] = dB
        dc_ref[...] = dC
        dHT[...] = dH_prev

    nbx = TOK // N
    rv = lambda c: NC - 1 - c
    par = pl.BlockSpec((None, 1, LANES), lambda g, c: (g, 0, 0))
    xsp = pl.BlockSpec((Q, HP), lambda g, c: (rv(c), g))
    outs = pl.pallas_call(
        body, name=name, grid=(G, NC),
        in_specs=[xsp,
                  pl.BlockSpec((Q, N), lambda g, c: (rv(c), nbx + g)),
                  pl.BlockSpec((Q, N), lambda g, c: (rv(c), nbx + G + g)),
                  pl.BlockSpec((None, Q, LANES), lambda g, c: (g, rv(c), 0)), par, par, par,
                  pl.BlockSpec((None, None, N, HP), lambda g, c: (g, rv(c), 0, 0)), xsp],
        out_specs=[xsp, pl.BlockSpec((Q, N), lambda g, c: (rv(c), g)), pl.BlockSpec((Q, N), lambda g, c: (rv(c), g)),
                   pl.BlockSpec((None, Q, LANES), lambda g, c: (g, rv(c), 0)), par, par, par],
        out_shape=[jax.ShapeDtypeStruct((T, TOK), F32), jax.ShapeDtypeStruct((T, G * N), F32),
                   jax.ShapeDtypeStruct((T, G * N), F32), jax.ShapeDtypeStruct((G, T, LANES), F32),
                   jax.ShapeDtypeStruct((G, 1, LANES), F32), jax.ShapeDtypeStruct((G, 1, LANES), F32),
                   jax.ShapeDtypeStruct((G, 1, LANES), F32)],
        scratch_shapes=[pltpu.VMEM((N, HP), F32)],
        compiler_params=_cparams("parallel", "arbitrary"),
    )(act, act, act, raw_g, bias_g, alog_g, dsk_g, hprev, dy)
    return outs


def _heads_per_block(H):
    for hb in (12, 8, 6, 4, 3, 2, 1):
        if H % hb == 0:
            return hb
    return 1


def _alibi_slope(head_index, n_alibi):
    c = -ALIBI_MAX_EXP * math.log(2.0) / n_alibi
    return jnp.exp(jnp.full((1, 1), c, F32) * (head_index + 1).astype(F32))


def _attn_masks(b, nb):
    Bq = ATTN_BLOCK
    iq = lax.broadcasted_iota(jnp.int32, (Bq, 2 * Bq), 0)
    jk = lax.broadcasted_iota(jnp.int32, (Bq, 2 * Bq), 1)
    rel = iq + Bq - jk
    mask = (rel >= 0) & (rel <= Bq) & (jk + jnp.where(b > 0, Bq, 0) >= Bq)
    rel_n = lax.broadcasted_iota(jnp.int32, (Bq, Bq), 0) + Bq - lax.broadcasted_iota(jnp.int32, (Bq, Bq), 1)
    mask_n = (rel_n + jnp.where(b < nb - 1, 0, 4 * Bq)) <= Bq
    return rel.astype(F32), mask, rel_n.astype(F32), mask_n


def _rows2(a, b):
    return jnp.concatenate([a, b], axis=0)


ATTN_UNITS_IN_FLIGHT = 6


def _attn_units_fwd(load, n_units, slopes, masks, scale, store):
    rel_f, mask, _, _ = masks
    for g0 in range(0, n_units, ATTN_UNITS_IN_FLIGHT):
        ids = range(g0, min(g0 + ATTN_UNITS_IN_FLIGHT, n_units))
        units = [load(i) for i in ids]
        raw = [_dot_nt(q, _rows2(kp, kc)) for q, kc, kp, vc, vp in units]
        soft = []
        for i, s_raw in zip(ids, raw):
            s = jnp.where(mask, s_raw * scale - slopes[i] * rel_f, NEG)
            m = jnp.max(s, axis=-1, keepdims=True)
            p = jnp.exp(s - m)
            den = jnp.sum(p, axis=-1, keepdims=True)
            soft.append((p.astype(BF16), den, m + jnp.log(den)))
        for i, (p, den, lse), (q, kc, kp, vc, vp) in zip(ids, soft, units):
            store(i, _dot(p, _rows2(vp, vc)) / den, lse)


def _attn_units_bwd(load, n_units, slopes, masks, scale, store):
    rel_f, mask, reln_f, mask_n = masks
    Bq = ATTN_BLOCK
    for g0 in range(0, n_units, ATTN_UNITS_IN_FLIGHT):
        ids = range(g0, min(g0 + ATTN_UNITS_IN_FLIGHT, n_units))
        units = [load(i) for i in ids]
        prods = []
        for q0, q1, kp, k0, vp, v0, do0, do1, y0, y1, lse0, lse1 in units:
            kcat = _rows2(kp, k0)
            do0b = do0.astype(BF16)
            do1b = do1.astype(BF16)
            prods.append((kcat, do0b, do1b, _dot_nt(q0, kcat), _dot_nt(do0b, _rows2(vp, v0)), _dot_nt(q1, k0),
                          _dot_nt(do1b, v0)))
        mids = []
        for i, u, (kcat, do0b, do1b, s_raw, dp_raw, sn_raw, dpn_raw) in zip(ids, units, prods):
            q0, q1, kp, k0, vp, v0, do0, do1, y0, y1, lse0, lse1 = u
            delta0 = jnp.sum(do0 * y0, axis=-1, keepdims=True)
            delta1 = jnp.sum(do1 * y1, axis=-1, keepdims=True)
            p = jnp.exp(jnp.where(mask, s_raw * scale - slopes[i] * rel_f, NEG) - lse0)
            ds = p * (dp_raw - delta0)
            p_n = jnp.exp(jnp.where(mask_n, sn_raw * scale - slopes[i] * reln_f, NEG) - lse1)
            ds_n = p_n * (dpn_raw - delta1)
            mids.append((ds.astype(BF16), _rows2(ds[:, Bq:], ds_n).T.astype(BF16),
                         _rows2(p[:, Bq:], p_n).T.astype(BF16)))
        for i, u, (kcat, do0b, do1b, *_), (dsb, dsk_t, pk_t) in zip(ids, units, prods, mids):
            store(i, scale * _dot(dsb, kcat), scale * _dot(dsk_t, _rows2(u[0], u[1])), _dot(pk_t, _rows2(do0b, do1b)))


def _attn_fwd_strided(q, k, v, gi, d, name):
    T, TOK = q.shape
    E_ = ATTN_HEAD_DIM
    H = TOK // E_
    n_alibi = len(DILATED_GROUPS) * H
    Bq = ATTN_BLOCK
    RB = Bq * d
    nb = T // RB
    scale = E_ ** -0.5

    def body(q_ref, kc_ref, kp_ref, vc_ref, vp_ref, o_ref, l_ref):
        masks = _attn_masks(pl.program_id(1), nb)
        slope = _alibi_slope(gi * H + pl.program_id(0), n_alibi) * float(d)
        rows = lambda r: pl.ds(r, Bq, stride=d)

        def load(r):
            return tuple(ref[rows(r), :].astype(BF16) for ref in (q_ref, kc_ref, kp_ref, vc_ref, vp_ref))

        def store(r, o, lse):
            o_ref[rows(r), :] = o
            l_ref[rows(r), :] = jnp.broadcast_to(lse, (Bq, E_))

        _attn_units_fwd(load, d, [slope] * d, masks, scale, store)

    cur = pl.BlockSpec((RB, E_), lambda h, b: (b, h))
    prev = pl.BlockSpec((RB, E_), lambda h, b: (jnp.maximum(b - 1, 0), h))
    sds = jax.ShapeDtypeStruct((T, TOK), F32)
    return pl.pallas_call(
        body, name=name, grid=(H, nb), in_specs=[cur, cur, prev, cur, prev], out_specs=[cur, cur],
        out_shape=[sds, sds], compiler_params=_cparams("parallel", "parallel"),
    )(q, k, k, v, v)


def _attn_bwd_strided(q, k, v, y, lse, dy, gi, d, name):
    T, TOK = q.shape
    E_ = ATTN_HEAD_DIM
    H = TOK // E_
    n_alibi = len(DILATED_GROUPS) * H
    Bq = ATTN_BLOCK
    RB = Bq * d
    nb = T // RB
    scale = E_ ** -0.5

    def body(q0_ref, q1_ref, kp_ref, k0_ref, vp_ref, v0_ref, do0_ref, do1_ref, y0_ref, y1_ref, l0_ref, l1_ref,
             dq_ref, dk_ref, dv_ref):
        masks = _attn_masks(pl.program_id(1), nb)
        slope = _alibi_slope(gi * H + pl.program_id(0), n_alibi) * float(d)
        rows = lambda r: pl.ds(r, Bq, stride=d)

        def load(r):
            return (tuple(ref[rows(r), :].astype(BF16) for ref in (q0_ref, q1_ref, kp_ref, k0_ref, vp_ref, v0_ref))
                    + tuple(ref[rows(r), :] for ref in (do0_ref, do1_ref, y0_ref, y1_ref))
                    + tuple(jnp.max(ref[rows(r), :], axis=-1, keepdims=True) for ref in (l0_ref, l1_ref)))

        def store(r, dq, dk, dv):
            dq_ref[rows(r), :] = dq
            dk_ref[rows(r), :] = dk
            dv_ref[rows(r), :] = dv

        _attn_units_bwd(load, d, [slope] * d, masks, scale, store)

    cur = pl.BlockSpec((RB, E_), lambda h, b: (b, h))
    prev = pl.BlockSpec((RB, E_), lambda h, b: (jnp.maximum(b - 1, 0), h))
    nxt = pl.BlockSpec((RB, E_), lambda h, b: (jnp.minimum(b + 1, nb - 1), h))
    sds = jax.ShapeDtypeStruct((T, TOK), F32)
    return pl.pallas_call(
        body, name=name, grid=(H, nb),
        in_specs=[cur, nxt, prev, cur, prev, cur, cur, nxt, cur, nxt, cur, nxt],
        out_specs=[cur, cur, cur], out_shape=[sds, sds, sds],
        compiler_params=_cparams("parallel", "parallel"),
    )(q, q, k, k, v, v, dy, dy, y, y, lse, lse)


def _attn_fwd(q, k, v, gi, window, d, name):
    T, TOK = q.shape
    E_ = ATTN_HEAD_DIM
    H = TOK // E_
    n_alibi = len(DILATED_GROUPS) * H
    assert window // d == ATTN_BLOCK and (T // d) % ATTN_BLOCK == 0
    if d > 1:
        return _attn_fwd_strided(q, k, v, gi, d, name)
    n_sub = T // d
    nb = n_sub // ATTN_BLOCK
    HB = _heads_per_block(H)
    NHB = H // HB
    hbw = HB * E_
    scale = E_ ** -0.5
    Bq = ATTN_BLOCK

    def body(q_ref, kc_ref, kp_ref, vc_ref, vp_ref, o_ref, l_ref):
        hb = pl.program_id(1)
        masks = _attn_masks(pl.program_id(2), nb)
        slopes = [_alibi_slope(gi * H + hb * HB + hh, n_alibi) * float(d) for hh in range(HB)]
        cols = lambda hh: slice(hh * E_, (hh + 1) * E_)

        def load(hh):
            return tuple(ref[:, cols(hh)].astype(BF16) for ref in (q_ref, kc_ref, kp_ref, vc_ref, vp_ref))

        def store(hh, o, lse):
            o_ref[:, cols(hh)] = o
            l_ref[:, cols(hh)] = jnp.broadcast_to(lse, (Bq, E_))

        _attn_units_fwd(load, HB, slopes, masks, scale, store)

    cur = pl.BlockSpec((Bq, hbw), lambda r, h, b: (b, r * NHB + h))
    prev = pl.BlockSpec((Bq, hbw), lambda r, h, b: (jnp.maximum(b - 1, 0), r * NHB + h))
    view = lambda t: t.reshape(n_sub, d * TOK)
    sds = jax.ShapeDtypeStruct((n_sub, d * TOK), F32)
    o, l = pl.pallas_call(
        body, name=name, grid=(d, NHB, nb), in_specs=[cur, cur, prev, cur, prev], out_specs=[cur, cur],
        out_shape=[sds, sds], compiler_params=_cparams("parallel", "parallel", "parallel"),
    )(view(q), view(k), view(k), view(v), view(v))
    return o.reshape(T, TOK), l.reshape(T, TOK)


def _attn_combine(os_, ls_, name):
    def fn(*v):
        n = len(v) // 2
        o, l = v[:n], v[n:]
        m = l[0]
        for t in l[1:]:
            m = jnp.maximum(m, t)
        e = [jnp.exp(t - m) for t in l]
        den = e[0]
        for t in e[1:]:
            den = den + t
        y = e[0] * o[0]
        for t, u in zip(e[1:], o[1:]):
            y = y + t * u
        return y / den, m + jnp.log(den)
    return _elementwise(fn, list(os_) + list(ls_), [], [F32, F32], name)


def _attn_bwd(q, k, v, y, lse, dy, gi, window, d, name):
    T, TOK = q.shape
    E_ = ATTN_HEAD_DIM
    H = TOK // E_
    n_alibi = len(DILATED_GROUPS) * H
    if d > 1:
        return _attn_bwd_strided(q, k, v, y, lse, dy, gi, d, name)
    n_sub = T // d
    nb = n_sub // ATTN_BLOCK
    HB = _heads_per_block(H)
    NHB = H // HB
    hbw = HB * E_
    scale = E_ ** -0.5
    Bq = ATTN_BLOCK

    def body(q0_ref, q1_ref, kp_ref, k0_ref, vp_ref, v0_ref, do0_ref, do1_ref, y0_ref, y1_ref, l0_ref, l1_ref,
             dq_ref, dk_ref, dv_ref):
        hb = pl.program_id(1)
        masks = _attn_masks(pl.program_id(2), nb)
        slopes = [_alibi_slope(gi * H + hb * HB + hh, n_alibi) * float(d) for hh in range(HB)]
        cols = lambda hh: slice(hh * E_, (hh + 1) * E_)

        def load(hh):
            return (tuple(ref[:, cols(hh)].astype(BF16) for ref in (q0_ref, q1_ref, kp_ref, k0_ref, vp_ref, v0_ref))
                    + tuple(ref[:, cols(hh)] for ref in (do0_ref, do1_ref, y0_ref, y1_ref))
                    + tuple(jnp.max(ref[:, cols(hh)], axis=-1, keepdims=True) for ref in (l0_ref, l1_ref)))

        def store(hh, dq, dk, dv):
            dq_ref[:, cols(hh)] = dq.astype(BF16)
            dk_ref[:, cols(hh)] = dk.astype(BF16)
            dv_ref[:, cols(hh)] = dv.astype(BF16)

        _attn_units_bwd(load, HB, slopes, masks, scale, store)

    cur = pl.BlockSpec((Bq, hbw), lambda r, h, b: (b, r * NHB + h))
    prev = pl.BlockSpec((Bq, hbw), lambda r, h, b: (jnp.maximum(b - 1, 0), r * NHB + h))
    nxt = pl.BlockSpec((Bq, hbw), lambda r, h, b: (jnp.minimum(b + 1, nb - 1), r * NHB + h))
    view = lambda t: t.reshape(n_sub, d * TOK)
    sds = jax.ShapeDtypeStruct((n_sub, d * TOK), BF16)
    dq, dk, dv = pl.pallas_call(
        body, name=name, grid=(d, NHB, nb),
        in_specs=[cur, nxt, prev, cur, prev, cur, cur, nxt, cur, nxt, cur, nxt],
        out_specs=[cur, cur, cur], out_shape=[sds, sds, sds],
        compiler_params=_cparams("parallel", "parallel", "parallel"),
    )(view(q), view(q), view(k), view(k), view(v), view(v), view(dy), view(dy), view(y), view(y), view(lse), view(lse))
    return dq.reshape(T, TOK), dk.reshape(T, TOK), dv.reshape(T, TOK)


_FLIPS = {
    "xy": [(1, 0, 0), (0, 1, 0), (1, 1, 0)],
    "c": [(0, 0, 1)],
    "xyc": [(dx, dy, dc) for dx in (0, 1) for dy in (0, 1) for dc in (0, 1) if (dx, dy, dc) != (0, 0, 0)],
}


def _comm_parts(group, srcs, modes, handshake):
    flips = _FLIPS[group]
    F_ = len(flips)
    P_ = F_ + 1
    n = len(srcs)

    def gidx(px, py, pc):
        if group == "xy":
            return 2 * px + py
        if group == "c":
            return pc
        return 4 * px + 2 * py + pc

    def body(*refs):
        src_refs, out_refs = refs[:n], refs[n:2 * n]
        send_sems, recv_sems, loc_sems = refs[2 * n:]
        x, y, c = lax.axis_index("x"), lax.axis_index("y"), lax.axis_index("c")
        me = gidx(x, y, c)
        peers = [(1 - x if dx else x, 1 - y if dy else y, 1 - c if dc else c) for dx, dy, dc in flips]
        if handshake:
            barrier = pltpu.get_barrier_semaphore()
            for peer in peers:
                pl.semaphore_signal(barrier, inc=1, device_id=peer, device_id_type=MESH)
            pl.semaphore_wait(barrier, F_)
        local, remote = [], []
        for i in range(n):
            mode = modes[i]
            if mode != "swap":
                mine = pltpu.make_async_copy(src_refs[i] if mode == "gather" else src_refs[i].at[me],
                                             out_refs[i].at[me], loc_sems.at[i])
                mine.start()
                local.append(mine)
            for f, peer in enumerate(peers):
                cp = pltpu.make_async_remote_copy(
                    src_ref=src_refs[i].at[gidx(*peer)] if mode == "a2a" else src_refs[i],
                    dst_ref=out_refs[i] if mode == "swap" else out_refs[i].at[me],
                    send_sem=send_sems.at[i * F_ + f], recv_sem=recv_sems.at[i * F_ + f],
                    device_id=peer, device_id_type=MESH)
                cp.start()
                remote.append(cp)
        for cp in local:
            cp.wait()
        for cp in remote:
            cp.wait()

    out_shape = []
    for s, mode in zip(srcs, modes):
        assert mode != "swap" or F_ == 1
        shp = (P_,) + tuple(s.shape) if mode == "gather" else tuple(s.shape)
        out_shape.append(jax.ShapeDtypeStruct(shp, s.dtype))
    sems = [pltpu.SemaphoreType.DMA((n * F_,)), pltpu.SemaphoreType.DMA((n * F_,)), pltpu.SemaphoreType.DMA((n,))]
    return body, out_shape, sems


def _comm(name, group, srcs, modes):
    n = len(srcs)
    body, out_shape, sems = _comm_parts(group, srcs, modes, handshake=False)
    anyspec = pl.BlockSpec(memory_space=pl.ANY)
    return pl.pallas_call(body, name=name, in_specs=[anyspec] * n, out_specs=[anyspec] * n, out_shape=out_shape,
                          scratch_shapes=sems)(*srcs)


def _comm_async(name, collective_id, group, srcs, modes):
    body, out_shape, sems = _comm_parts(group, srcs, modes, handshake=True)
    return pl.kernel(body, name=name, out_type=out_shape,
                     mesh=plsc.ScalarSubcoreMesh(axis_name="sequencer", num_cores=1), scratch_types=sems,
                     compiler_params=pltpu.CompilerParams(collective_id=collective_id))(*srcs)


def _dims(D):
    MIX = 2 * D
    MW = MIX // 4
    TOK = MIX - MW
    H = TOK // SSD_HEAD_DIM
    CONV = TOK + 2 * SSD_GROUPS * SSD_STATE
    return dict(MIX=MIX, MW=MW, TOK=TOK, H=H, CONV=CONV)


def _proj_chain(dsegs, wsegs, name):
    acc = None
    for n, (ds, ws) in enumerate(zip(dsegs, wsegs)):
        acc = _mm(ds, ws, "nt", F32, f"{name}_dh{n}", add=acc)
    return acc


def _pad_lanes(a, width=LANES):
    return jnp.pad(a, [(0, 0)] * (a.ndim - 1) + [(0, width - a.shape[-1])])


def _take_cols(parts, a, b):
    out, o = [], 0
    for part in parts:
        w = part.shape[1]
        lo, hi = max(a, o), min(b, o + w)
        if lo < hi:
            out.append(part[:, lo - o:hi - o])
        o += w
    return out[0] if len(out) == 1 else jnp.concatenate(out, axis=1)


def _heads_to_groups(a, G, HPG):
    return jnp.stack([_pad_lanes(a[:, g * HPG:(g + 1) * HPG]) for g in range(G)])


def _groups_to_heads(a, HPG):
    return jnp.concatenate([a[g, :, :HPG] for g in range(a.shape[0])], axis=1)


def _ssd_layer_fwd(x, mem_n, p, li):
    T, D = x.shape
    dm = _dims(D)
    TOK, MW, H, CONV = dm["TOK"], dm["MW"], dm["H"], dm["CONV"]
    G = SSD_GROUPS
    HPG = H // G
    w = p["w_in"]
    cuts = [0, CONV, CONV + H, CONV + H + MW, CONV + H + MW + TOK, CONV + H + MW + TOK + MW]
    segs = {k: _take_cols(w, cuts[n], cuts[n + 1]) for n, k in enumerate(["xbc", "dt", "qm", "zt", "zm"])}
    segs["dt"] = _pad_lanes(segs["dt"])
    h = _rms_fwd(x, p["norm_g"], f"l{li}_rms")
    pr = {k: _mm(h, ws, "nn", BF16 if k == "qm" else F32, f"l{li}_in_{k}") for k, ws in segs.items()}
    act = _conv_fwd(pr["xbc"], p["conv_w"], p["conv_b"], f"l{li}_conv")
    raw_g = _heads_to_groups(pr["dt"][:, :H], G, HPG)
    hp = lambda a: _pad_lanes(a.reshape(G, 1, HPG))
    bias_g, alog_g, dsk_g = hp(p["dt_bias"]), hp(p["a_log"]), hp(p["d_skip"])
    y, hprev = _ssd_fwd(act, raw_g, bias_g, alog_g, dsk_g, TOK, f"l{li}_ssd")
    kv = _mm(mem_n, p["w_mem_kv"], "nn", F32, f"l{li}_kv")
    ymem = _memattn_fwd(pr["qm"], kv, f"l{li}_mem")
    gt = _gate_norm_fwd(y, pr["zt"], p["ssd_norm_g"], G, f"l{li}_gate_tok")
    gm = _gate_fwd(ymem, pr["zm"], f"l{li}_gate_mem")
    wo = p["w_out"]
    out = _mm(gt, wo[:TOK], "nn", F32, f"l{li}_out_tok", add=x)
    out = _mm(gm, wo[TOK:], "nn", F32, f"l{li}_out_mem", add=out)
    saved = dict(x=x, h=h, kv=kv, pr=pr, act=act, raw_g=raw_g, par=(bias_g, alog_g, dsk_g), y=y, hprev=hprev, ymem=ymem,
                 gt=gt, gm=gm, segs=segs)
    return out, saved


def _ssd_layer_bwd(dout, doutb, kv, p, s, li):
    x = s["x"]
    T, D = x.shape
    dm = _dims(D)
    TOK, MW, H, CONV = dm["TOK"], dm["MW"], dm["H"], dm["CONV"]
    G = SSD_GROUPS
    HPG = H // G
    wo = p["w_out"]
    pr = s["pr"]
    dgt = _mm(doutb, wo[:TOK], "nt", F32, f"l{li}_dgt")
    dgm = _mm(doutb, wo[TOK:], "nt", F32, f"l{li}_dgm")
    dwo = jnp.concatenate([_mm(s["gt"], doutb, "tn", BF16, f"l{li}_dwo_tok"),
                           _mm(s["gm"], doutb, "tn", BF16, f"l{li}_dwo_mem")], axis=0)
    dy, dzt, dng = _gate_norm_bwd(s["y"], pr["zt"], p["ssd_norm_g"], dgt, G, f"l{li}_gate_tok_b")
    dymem, dzm = _gate_bwd(s["ymem"], pr["zm"], dgm, f"l{li}_gate_mem_b")
    dqm, dkv = _memattn_bwd(pr["qm"], kv, dymem, f"l{li}_mem_b")
    bias_g, alog_g, dsk_g = s["par"]
    dxs, dB, dC, draw_g, dalog, dbias, ddsk = _ssd_bwd(s["act"], s["raw_g"], bias_g, alog_g, dsk_g, s["hprev"], dy, TOK,
                                                      f"l{li}_ssd_b")
    dact = jnp.concatenate([dxs, dB, dC], axis=1)
    dpre, dconv_w, dconv_b = _conv_bwd_pre(pr["xbc"], p["conv_w"], p["conv_b"], dact, f"l{li}_conv_b1")
    dxbc = _conv_bwd_in(dpre, p["conv_w"], f"l{li}_conv_b2")
    draw = _pad_lanes(_groups_to_heads(draw_g, HPG)).astype(BF16)
    dsegs = dict(xbc=dxbc, dt=draw, qm=dqm, zt=dzt, zm=dzm)
    keys = ["xbc", "dt", "qm", "zt", "zm"]
    dh = _proj_chain([dsegs[k] for k in keys], [s["segs"][k] for k in keys], f"l{li}")
    dws = {k: _mm(s["h"], dsegs[k], "tn", BF16, f"l{li}_dwin_{k}") for k in keys}
    dws["dt"] = dws["dt"][:, :H]
    dwin = [dws[k] for k in keys]
    dh, dwin, dwo = lax.optimization_barrier((dh, dwin, dwo))
    dx, dxb, dnorm = _rms_bwd(x, p["norm_g"], dh, dout, f"l{li}_rms_b")
    unhead = lambda a: a[:, 0, :HPG].reshape(H)
    grads = dict(norm_g=dnorm, w_in=dwin, conv_w=dconv_w, conv_b=dconv_b, dt_bias=unhead(dbias), a_log=unhead(dalog),
                 d_skip=unhead(ddsk), ssd_norm_g=dng, w_out=dwo)
    return dx, dxb, dkv, grads


def _attn_layer_fwd(x, mem_n, p, li):
    T, D = x.shape
    dm = _dims(D)
    TOK, MW = dm["TOK"], dm["MW"]
    w = p["w_in"]
    ng = len(DILATED_GROUPS)
    segs = {}
    for g in range(ng):
        for n, nm in enumerate("qkv"):
            c0 = g * 3 * TOK + n * TOK
            segs[f"{nm}{g}"] = _take_cols(w, c0, c0 + TOK)
    c0 = ng * 3 * TOK
    segs["qm"] = _take_cols(w, c0, c0 + MW)
    segs["zt"] = _take_cols(w, c0 + MW, c0 + MW + TOK)
    segs["zm"] = _take_cols(w, c0 + MW + TOK, c0 + MW + TOK + MW)
    h = _rms_fwd(x, p["norm_g"], f"l{li}_rms")
    dense = {"qm"} | {f"{nm}{g}" for g, (_, d) in enumerate(DILATED_GROUPS) if d == 1 for nm in "qkv"}
    pr = {k: _mm(h, ws, "nn", BF16 if k in dense else F32, f"l{li}_in_{k}") for k, ws in segs.items()}
    os_, ls_ = [], []
    for g, (window, d) in enumerate(DILATED_GROUPS):
        o, l = _attn_fwd(pr[f"q{g}"], pr[f"k{g}"], pr[f"v{g}"], g, window, d, f"l{li}_attn{g}")
        os_.append(o)
        ls_.append(l)
    ytok, lse = _attn_combine(os_, ls_, f"l{li}_combine")
    kv = _mm(mem_n, p["w_mem_kv"], "nn", F32, f"l{li}_kv")
    ymem = _memattn_fwd(pr["qm"], kv, f"l{li}_mem")
    gt = _gate_fwd(ytok, pr["zt"], f"l{li}_gate_tok")
    gm = _gate_fwd(ymem, pr["zm"], f"l{li}_gate_mem")
    wo = p["w_out"]
    out = _mm(gt, wo[:TOK], "nn", F32, f"l{li}_out_tok", add=x)
    out = _mm(gm, wo[TOK:], "nn", F32, f"l{li}_out_mem", add=out)
    saved = dict(x=x, h=h, kv=kv, pr=pr, ytok=ytok, lse=lse, ymem=ymem, gt=gt, gm=gm, segs=segs)
    return out, saved


def _attn_layer_bwd(dout, doutb, kv, p, s, li):
    x = s["x"]
    T, D = x.shape
    dm = _dims(D)
    TOK, MW = dm["TOK"], dm["MW"]
    wo = p["w_out"]
    pr = s["pr"]
    dgt = _mm(doutb, wo[:TOK], "nt", F32, f"l{li}_dgt")
    dgm = _mm(doutb, wo[TOK:], "nt", F32, f"l{li}_dgm")
    dwo = jnp.concatenate([_mm(s["gt"], doutb, "tn", BF16, f"l{li}_dwo_tok"),
                           _mm(s["gm"], doutb, "tn", BF16, f"l{li}_dwo_mem")], axis=0)
    dytok, dzt = _gate_bwd(s["ytok"], pr["zt"], dgt, f"l{li}_gate_tok_b")
    dymem, dzm = _gate_bwd(s["ymem"], pr["zm"], dgm, f"l{li}_gate_mem_b")
    dqm, dkv = _memattn_bwd(pr["qm"], kv, dymem, f"l{li}_mem_b")
    dsegs = {}
    for g, (window, d) in enumerate(DILATED_GROUPS):
        dq, dk, dv = _attn_bwd(pr[f"q{g}"], pr[f"k{g}"], pr[f"v{g}"], s["ytok"], s["lse"], dytok, g, window, d,
                               f"l{li}_attn{g}_b")
        dsegs[f"q{g}"], dsegs[f"k{g}"], dsegs[f"v{g}"] = dq, dk, dv
    dsegs["qm"], dsegs["zt"], dsegs["zm"] = dqm, dzt, dzm
    keys = list(s["segs"].keys())
    dh = _proj_chain([dsegs[k] for k in keys], [s["segs"][k] for k in keys], f"l{li}")
    dwin = [_mm(s["h"], dsegs[k], "tn", BF16, f"l{li}_dwin_{k}") for k in keys]
    dh, dwin, dwo = lax.optimization_barrier((dh, dwin, dwo))
    dx, dxb, dnorm = _rms_bwd(x, p["norm_g"], dh, dout, f"l{li}_rms_b")
    return dx, dxb, dkv, dict(norm_g=dnorm, w_in=dwin, w_out=dwo)


def _local_step(x, mem, mem_n, tgt, mem_norm_g, final_norm_g, n_layers, layer_params, on_layer_grads):
    layers, kvs, saved = [], [], []
    for li in range(n_layers):
        p, x = layer_params(li, x)
        fwd = _ssd_layer_fwd if li % 2 == 0 else _attn_layer_fwd
        x, s = fwd(x, mem_n, p, li)
        kv = s["kv"]
        layers.append(p)
        kvs.append(kv)
        saved.append(s)
    loss, dx, dxb, dfinal = _loss_head(x, final_norm_g, tgt, "loss_head")
    dmem_n = None
    for li in reversed(range(n_layers)):
        p = layers[li]
        bwd = _ssd_layer_bwd if li % 2 == 0 else _attn_layer_bwd
        dx, dxb, dkv, g = bwd(dx, dxb, kvs[li], p, saved[li], li)
        g["w_mem_kv"] = _mm(mem_n, dkv, "tn", BF16, f"l{li}_dwkv")
        dmem_n = _mm(dkv, p["w_mem_kv"], "nt", F32, f"l{li}_dmem", add=dmem_n)
        dx, dxb = on_layer_grads(li, g, dx, dxb)
    _, _, dmem_g = _rms_bwd(mem, mem_norm_g, dmem_n, None, "mem_rms_b")
    return loss, dx, dmem_g, dfinal


_SSD_SMALL = ["norm_g", "conv_w", "conv_b", "dt_bias", "a_log", "d_skip", "ssd_norm_g"]
_ATTN_SMALL = ["norm_g"]
_SSD_ORDER = ["norm_g", "w_in", "conv_w", "conv_b", "dt_bias", "a_log", "d_skip", "ssd_norm_g", "w_mem_kv", "w_out"]
_ATTN_ORDER = ["norm_g", "w_in", "w_mem_kv", "w_out"]


def _pack(arrs):
    flat = jnp.concatenate([a.reshape(-1).astype(F32) for a in arrs])
    n = flat.shape[0]
    pad = (-n) % (8 * LANES)
    return jnp.pad(flat, (0, pad)).reshape(-1, LANES)


def _unpack(mat, shapes):
    flat = mat.reshape(-1)
    out, o = [], 0
    for shp in shapes:
        n = math.prod(shp)
        out.append(flat[o:o + n].reshape(shp))
        o += n
    return out


def kernel(x, mem, mem_norm_g, final_norm_g, norm_g_0, w_in_0, conv_w_0, conv_b_0, dt_bias_0, a_log_0, d_skip_0, ssd_norm_g_0, w_mem_kv_0, w_out_0, norm_g_1, w_in_1, w_mem_kv_1, w_out_1, norm_g_2, w_in_2, conv_w_2, conv_b_2, dt_bias_2, a_log_2, d_skip_2, ssd_norm_g_2, w_mem_kv_2, w_out_2, norm_g_3, w_in_3, w_mem_kv_3, w_out_3, loss_target, m_mem_norm_g, m_final_norm_g, m_norm_g_0, m_w_in_0, m_conv_w_0, m_conv_b_0, m_dt_bias_0, m_a_log_0, m_d_skip_0, m_ssd_norm_g_0, m_w_mem_kv_0, m_w_out_0, m_norm_g_1, m_w_in_1, m_w_mem_kv_1, m_w_out_1, m_norm_g_2, m_w_in_2, m_conv_w_2, m_conv_b_2, m_dt_bias_2, m_a_log_2, m_d_skip_2, m_ssd_norm_g_2, m_w_mem_kv_2, m_w_out_2, m_norm_g_3, m_w_in_3, m_w_mem_kv_3, m_w_out_3, v_mem_norm_g, v_final_norm_g, v_norm_g_0, v_w_in_0, v_conv_w_0, v_conv_b_0, v_dt_bias_0, v_a_log_0, v_d_skip_0, v_ssd_norm_g_0, v_w_mem_kv_0, v_w_out_0, v_norm_g_1, v_w_in_1, v_w_mem_kv_1, v_w_out_1, v_norm_g_2, v_w_in_2, v_conv_w_2, v_conv_b_2, v_dt_bias_2, v_a_log_2, v_d_skip_2, v_ssd_norm_g_2, v_w_mem_kv_2, v_w_out_2, v_norm_g_3, v_w_in_3, v_w_mem_kv_3, v_w_out_3):
    a = dict(locals())
    names = ["mem_norm_g", "final_norm_g"]
    for li in range(DEPTH):
        names += [f"{k}_{li}" for k in (_SSD_ORDER if li % 2 == 0 else _ATTN_ORDER)]
    W = {n: a[n] for n in names}
    Mo = {n: a["m_" + n] for n in names}
    Vo = {n: a["v_" + n] for n in names}
    NX = 4
    chip = 2 * lax.axis_index("x") + lax.axis_index("y")

    gathered = []
    mem_n = None
    for li in range(DEPTH):
        mats = [W[f"w_in_{li}"], W[f"w_mem_kv_{li}"], W[f"w_out_{li}"]]
        if li == 1:
            mem_n = _rms_fwd(mem[0], W["mem_norm_g"], "mem_rms")
            later = [[W[f"{k}_{lj}"] for k in ("w_in", "w_mem_kv", "w_out")] for lj in range(1, DEPTH)]
            later, mem_n = lax.optimization_barrier((later, mem_n))
        if li >= 1:
            mats = later[li - 1]
        srcs = [m.astype(BF16) for m in mats]
        if li % 2 == 0:
            srcs.append(W[f"conv_w_{li}"])
        if li == 0:
            first = _comm_async("gather_w0", 0, "xy", [srcs[0], srcs[3]], ["gather"] * 2)
            rest = _comm_async("gather_w0_rest", 3 * DEPTH, "xy", srcs[1:3], ["gather"] * 2)
            gathered.append([first[0], rest[0], rest[1], first[1]])
        else:
            gathered.append(_comm_async(f"gather_w{li}", li, "xy", srcs, ["gather"] * len(srcs)))

    def layer_params(li, xin):
        got = gathered[li]
        if li > 0:
            got, xin = lax.optimization_barrier((got, xin))
        rows = lambda g: g.reshape((-1,) + g.shape[2:])
        parts = lambda g: [g[k] for k in range(NX)]
        p = dict(w_in=parts(got[0]), w_mem_kv=rows(got[1]), w_out=rows(got[2]), norm_g=W[f"norm_g_{li}"])
        if li % 2 == 0:
            p.update(conv_w=jnp.concatenate(parts(got[3]), axis=1), conv_b=W[f"conv_b_{li}"],
                     dt_bias=W[f"dt_bias_{li}"], a_log=W[f"a_log_{li}"], d_skip=W[f"d_skip_{li}"],
                     ssd_norm_g=W[f"ssd_norm_g_{li}"])
        return p, xin

    G, Dl, Mn, Vn = {}, {}, {}, {}
    grads = [None] * DEPTH
    in_flight = []

    def finish_exchange(li, got):
        parts = [_sum_lead(t, f"l{li}_gsum{n}") for n, t in enumerate(got)]
        theirs = _comm_async(f"swap_g{li}", 2 * DEPTH + li, "c", parts, ["swap"] * 3)
        for nm, mine, other in zip(["w_in", "w_mem_kv", "w_out"], parts, theirs):
            key = f"{nm}_{li}"
            G[key], Dl[key], Mn[key], Vn[key] = _adamw(W[key], Mo[key], Vo[key], [mine, other], f"adamw_{key}")

    def on_layer_grads(li, g, dx, dxb):
        grads[li] = g
        dwin = g["w_in"]
        cw = sum(t.shape[1] for t in dwin) // NX
        chunks = [jnp.stack([_take_cols(dwin, k * cw, (k + 1) * cw) for k in range(NX)]),
                  g["w_mem_kv"].reshape((NX, -1) + g["w_mem_kv"].shape[1:]),
                  g["w_out"].reshape((NX, -1) + g["w_out"].shape[1:])]
        prev = in_flight.pop() if in_flight else None
        pgot = prev[1] if prev else []
        chunks, pgot, dx, dxb = lax.optimization_barrier((chunks, pgot, dx, dxb))
        in_flight.append((li, _comm_async(f"xchg_g{li}", DEPTH + li, "xy", chunks, ["a2a"] * 3)))
        if prev:
            finish_exchange(prev[0], pgot)
        return dx, dxb

    loss_l, dx, dmem_g, dfinal = _local_step(x[0], mem[0], mem_n, loss_target[0], W["mem_norm_g"], W["final_norm_g"],
                                             DEPTH, layer_params, on_layer_grads)
    last_li, last_got = in_flight.pop()
    updates = (G, Dl, Mn, Vn)
    last_got, tied = lax.optimization_barrier((last_got, [dict(u) for u in updates]))
    for u, t in zip(updates, tied):
        u.update(t)
    finish_exchange(last_li, last_got)
    loss = lax.psum(loss_l, ("x", "y", "c"))

    small_names = ["mem_norm_g", "final_norm_g"]
    small_grads = [dmem_g, dfinal]
    for li in range(DEPTH):
        for k in (_SSD_SMALL if li % 2 == 0 else _ATTN_SMALL):
            small_names.append(f"{k}_{li}")
            small_grads.append(grads[li][k])
    shapes = [tuple(t.shape) for t in small_grads]
    allg = _comm("gather_small", "xyc", [_pack(small_grads)], ["gather"])[0]
    gsum = _unpack(_sum_lead(allg, "small_gsum"), shapes)
    small_w, small_m, small_v, small_g = [], [], [], []
    for nme, gv in zip(small_names, gsum):
        if nme.startswith("conv_w"):
            cw = W[nme].shape[1]
            gv = lax.dynamic_slice_in_dim(gv, chip * cw, cw, axis=1)
        small_g.append(gv)
        small_w.append(W[nme])
        small_m.append(Mo[nme])
        small_v.append(Vo[nme])
    sshapes = [tuple(t.shape) for t in small_g]
    res = _adamw(_pack(small_w), _pack(small_m), _pack(small_v), [_pack(small_g)], "adamw_small")
    for dst, mat in zip((G, Dl, Mn, Vn), res):
        for nme, t in zip(small_names, _unpack(mat, sshapes)):
            dst[nme] = t

    return (loss, dx[None], *[G[n] for n in names], *[Dl[n] for n in names], *[Mn[n] for n in names],
            *[Vn[n] for n in names])
```

```python
import functools
import math

import jax
import jax.numpy as jnp
from jax import lax
from jax.experimental import pallas as pl
from jax.experimental.pallas import tpu as pltpu
from jax.experimental.pallas import tpu_sc as plsc

F32 = jnp.float32
BF16 = jnp.bfloat16

EPS = 1e-6
MEM_HEADS = 4
SSD_HEAD_DIM = 64
SSD_GROUPS = 8
SSD_STATE = 128
SSD_CONV = 4
SSD_CHUNK = 128
ATTN_HEAD_DIM = 128
ATTN_BLOCK = 128
DILATED_GROUPS = ((128, 1), (512, 4), (2048, 16))
ALIBI_MAX_EXP = 8.0
DEPTH = 4

ADAM_LR = 0.001
ADAM_B1 = 0.9
ADAM_B2 = 0.999
ADAM_EPS = 1e-08
ADAM_WD = 0.01
ADAM_STEP = 10

LANES = 128
VMEM_LIMIT_BYTES = 48 * 1024 * 1024
NEG = -1e30
MESH = pl.DeviceIdType.MESH


def _cparams(*sem):
    return pltpu.CompilerParams(dimension_semantics=tuple(sem), vmem_limit_bytes=VMEM_LIMIT_BYTES)


def _tile(dim, pref, unit=LANES):
    if dim <= pref:
        return dim
    t = (pref // unit) * unit
    while t >= unit:
        if dim % t == 0:
            return t
        t -= unit
    return dim


def _ew_tiles(rows, cols):
    tc = cols if (cols % LANES != 0 or cols <= 2048) else _tile(cols, 2048)
    tr = rows
    while tr * tc > 256 * 1024 and tr % 2 == 0 and (tr // 2) % 8 == 0:
        tr //= 2
    return tr, tc


def _sigmoid(v):
    return 1.0 / (1.0 + jnp.exp(-v))


def _dot(a, b):
    return jnp.dot(a, b, preferred_element_type=F32)


def _dot_nt(a, b):
    return lax.dot_general(a, b, (((1,), (1,)), ((), ())), preferred_element_type=F32)


def _dot_tn(a, b):
    return lax.dot_general(a, b, (((0,), (0,)), ((), ())), preferred_element_type=F32)


def _split3(v):
    hi = v.astype(BF16)
    r = v - hi.astype(F32)
    mid = r.astype(BF16)
    lo = (r - mid.astype(F32)).astype(BF16)
    return hi, mid, lo


def _xdot(v, onehot):
    hi, mid, lo = _split3(v)
    return _dot(hi, onehot) + _dot(mid, onehot) + _dot(lo, onehot)


def _xdot_l(onehot, v):
    hi, mid, lo = _split3(v)
    return _dot(onehot, hi) + _dot(onehot, mid) + _dot(onehot, lo)


def _mm(a, b, mode, out_dtype, name, add=None):
    if mode == "nn":
        M, K = a.shape
        N = b.shape[1]
    elif mode == "nt":
        M, K = a.shape
        N = b.shape[0]
    else:
        K, M = a.shape
        N = b.shape[1]
    tm, tn, tk = (2048, 1024, 1024) if mode == "tn" else (1024, 1024, 2048) if mode == "nn" else (512, 1024, 3072)
    tm, tn, tk = _tile(M, tm, 8 if M < LANES else LANES), _tile(N, tn), _tile(K, tk)
    nk = K // tk
    has_add = add is not None

    def product(a_ref, b_ref):
        av = a_ref[...].astype(BF16)
        bv = b_ref[...].astype(BF16)
        if mode == "nn":
            return _dot(av, bv)
        if mode == "nt":
            return _dot_nt(av, bv)
        return _dot_tn(av, bv)

    def body(*refs):
        a_ref, b_ref = refs[:2]
        add_ref = refs[2] if has_add else None
        o_ref = refs[3] if has_add else refs[2]

        def finish(r):
            if has_add:
                r = r + add_ref[...]
            o_ref[...] = r.astype(out_dtype)

        if nk == 1:
            finish(product(a_ref, b_ref))
            return
        acc = refs[-1]
        k = pl.program_id(2)

        @pl.when(k == 0)
        def _():
            acc[...] = product(a_ref, b_ref)

        @pl.when((k > 0) & (k < nk - 1))
        def _():
            acc[...] += product(a_ref, b_ref)

        @pl.when(k == nk - 1)
        def _():
            finish(acc[...] + product(a_ref, b_ref))

    if mode == "nn":
        a_spec = pl.BlockSpec((tm, tk), lambda j, i, k: (i, k))
        b_spec = pl.BlockSpec((tk, tn), lambda j, i, k: (k, j))
    elif mode == "nt":
        a_spec = pl.BlockSpec((tm, tk), lambda j, i, k: (i, k))
        b_spec = pl.BlockSpec((tn, tk), lambda j, i, k: (j, k))
    else:
        a_spec = pl.BlockSpec((tk, tm), lambda j, i, k: (k, i))
        b_spec = pl.BlockSpec((tk, tn), lambda j, i, k: (k, j))
    o_spec = pl.BlockSpec((tm, tn), lambda j, i, k: (i, j))
    in_specs = [a_spec, b_spec] + ([o_spec] if has_add else [])
    args = (a, b) + ((add,) if has_add else ())
    return pl.pallas_call(
        body, name=name, grid=(N // tn, M // tm, nk), in_specs=in_specs, out_specs=o_spec,
        out_shape=jax.ShapeDtypeStruct((M, N), out_dtype),
        scratch_shapes=[pltpu.VMEM((tm, tn), F32)] if nk > 1 else [],
        compiler_params=_cparams("parallel", "parallel", "arbitrary"),
    )(*args)


def _rms_fwd(x, g, name):
    R, Dm = x.shape
    tr = _tile(R, 256, 8)

    def body(x_ref, g_ref, o_ref):
        xv = x_ref[...]
        r = lax.rsqrt(jnp.mean(xv * xv, axis=-1, keepdims=True) + EPS)
        o_ref[...] = (xv * r * g_ref[...]).astype(BF16)

    return pl.pallas_call(
        body, name=name, grid=(R // tr,),
        in_specs=[pl.BlockSpec((tr, Dm), lambda i: (i, 0)), pl.BlockSpec((1, Dm), lambda i: (0, 0))],
        out_specs=pl.BlockSpec((tr, Dm), lambda i: (i, 0)),
        out_shape=jax.ShapeDtypeStruct((R, Dm), BF16),
        compiler_params=_cparams("parallel"),
    )(x, g.reshape(1, Dm))


def _rms_bwd(x, g, dh, dres, name):
    R, Dm = x.shape
    tr = _tile(R, 256, 8)
    has_res = dres is not None

    def body(*refs):
        if has_res:
            x_ref, g_ref, dh_ref, dres_ref, dx_ref, dxb_ref, dg_ref = refs
        else:
            x_ref, g_ref, dh_ref, dx_ref, dxb_ref, dg_ref = refs
        xv = x_ref[...]
        r = lax.rsqrt(jnp.mean(xv * xv, axis=-1, keepdims=True) + EPS)
        xhat = xv * r
        dhv = dh_ref[...]
        dxhat = dhv * g_ref[...]
        dx = r * (dxhat - xhat * jnp.mean(dxhat * xhat, axis=-1, keepdims=True))
        if has_res:
            dx = dx + dres_ref[...]
        dx_ref[...] = dx
        dxb_ref[...] = dx.astype(BF16)

        @pl.when(pl.program_id(0) == 0)
        def _():
            dg_ref[...] = jnp.zeros_like(dg_ref)

        dg_ref[...] += jnp.sum(dhv * xhat, axis=0, keepdims=True)

    row = pl.BlockSpec((tr, Dm), lambda i: (i, 0))
    vec = pl.BlockSpec((1, Dm), lambda i: (0, 0))
    in_specs = [row, vec, row] + ([row] if has_res else [])
    args = (x, g.reshape(1, Dm), dh) + ((dres,) if has_res else ())
    dx, dxb, dg = pl.pallas_call(
        body, name=name, grid=(R // tr,), in_specs=in_specs, out_specs=[row, row, vec],
        out_shape=[jax.ShapeDtypeStruct((R, Dm), F32), jax.ShapeDtypeStruct((R, Dm), BF16),
                   jax.ShapeDtypeStruct((1, Dm), F32)],
        compiler_params=_cparams("arbitrary"),
    )(*args)
    return dx, dxb, dg.reshape(Dm)


def _loss_head(x, g, tgt, name):
    R, Dm = x.shape
    tr = _tile(R, 256, 8)

    def body(x_ref, g_ref, t_ref, loss_ref, dx_ref, dxb_ref, dg_ref):
        xv = x_ref[...]
        gv = g_ref[...]
        r = lax.rsqrt(jnp.mean(xv * xv, axis=-1, keepdims=True) + EPS)
        xhat = xv * r
        e = xhat * gv - t_ref[...]
        part = jnp.sum(jnp.mean(e * e, axis=-1, keepdims=True), axis=0, keepdims=True) * 0.5
        dy = e * (1.0 / Dm)
        dxhat = dy * gv
        dx = r * (dxhat - xhat * jnp.mean(dxhat * xhat, axis=-1, keepdims=True))
        dx_ref[...] = dx
        dxb_ref[...] = dx.astype(BF16)

        @pl.when(pl.program_id(0) == 0)
        def _():
            dg_ref[...] = jnp.zeros_like(dg_ref)
            loss_ref[...] = jnp.zeros_like(loss_ref)

        dg_ref[...] += jnp.sum(dy * xhat, axis=0, keepdims=True)
        loss_ref[...] += jnp.broadcast_to(part, loss_ref.shape)

    row = pl.BlockSpec((tr, Dm), lambda i: (i, 0))
    vec = pl.BlockSpec((1, Dm), lambda i: (0, 0))
    lsp = pl.BlockSpec((1, LANES), lambda i: (0, 0))
    loss, dx, dxb, dg = pl.pallas_call(
        body, name=name, grid=(R // tr,), in_specs=[row, vec, row], out_specs=[lsp, row, row, vec],
        out_shape=[jax.ShapeDtypeStruct((1, LANES), F32), jax.ShapeDtypeStruct((R, Dm), F32),
                   jax.ShapeDtypeStruct((R, Dm), BF16), jax.ShapeDtypeStruct((1, Dm), F32)],
        compiler_params=_cparams("arbitrary"),
    )(x, g.reshape(1, Dm), tgt)
    return loss[0, 0], dx, dxb, dg.reshape(Dm)


def _elementwise(fn, mats, vecs, out_dtypes, name):
    R, C = mats[0].shape
    tr, tc = _ew_tiles(R, C)
    nm, nv, no = len(mats), len(vecs), len(out_dtypes)

    def body(*refs):
        ins = [r[...] for r in refs[:nm + nv]]
        outs = fn(*ins)
        for o_ref, o in zip(refs[nm + nv:], outs):
            o_ref[...] = o.astype(o_ref.dtype)

    blk = pl.BlockSpec((tr, tc), lambda i, j: (i, j))
    vblk = pl.BlockSpec((1, tc), lambda i, j: (0, j))
    res = pl.pallas_call(
        body, name=name, grid=(R // tr, C // tc),
        in_specs=[blk] * nm + [vblk] * nv, out_specs=[blk] * no,
        out_shape=[jax.ShapeDtypeStruct((R, C), dt) for dt in out_dtypes],
        compiler_params=_cparams("parallel", "parallel"),
    )(*mats, *[v.reshape(1, C) for v in vecs])
    return res


def _sum_lead(arr, name):
    P_, R, C = arr.shape
    tr, tc = _ew_tiles(R, C)

    def body(a_ref, o_ref):
        s = a_ref[0].astype(F32)
        for p in range(1, P_):
            s = s + a_ref[p].astype(F32)
        o_ref[...] = s

    return pl.pallas_call(
        body, name=name, grid=(R // tr, C // tc),
        in_specs=[pl.BlockSpec((P_, tr, tc), lambda i, j: (0, i, j))],
        out_specs=pl.BlockSpec((tr, tc), lambda i, j: (i, j)),
        out_shape=jax.ShapeDtypeStruct((R, C), F32),
        compiler_params=_cparams("parallel", "parallel"),
    )(arr)


def _adamw(w, m, v, gparts, name):
    P_ = len(gparts)
    R, C = w.shape
    tr, tc = _ew_tiles(R, C)
    c1 = 1.0 / (1.0 - ADAM_B1 ** ADAM_STEP)
    c2 = 1.0 / (1.0 - ADAM_B2 ** ADAM_STEP)

    def body(w_ref, m_ref, v_ref, *rest):
        g_refs, (go_ref, d_ref, mo_ref, vo_ref) = rest[:P_], rest[P_:]
        g = g_refs[0][...]
        for g_ref in g_refs[1:]:
            g = g + g_ref[...]
        mn = ADAM_B1 * m_ref[...] + (1.0 - ADAM_B1) * g
        vn = ADAM_B2 * v_ref[...] + (1.0 - ADAM_B2) * (g * g)
        m_hat = mn * c1
        v_hat = vn * c2
        d_ref[...] = -ADAM_LR * (m_hat / (jnp.sqrt(v_hat) + ADAM_EPS) + ADAM_WD * w_ref[...])
        go_ref[...] = g
        mo_ref[...] = mn
        vo_ref[...] = vn

    blk = pl.BlockSpec((tr, tc), lambda i, j: (i, j))
    sds = jax.ShapeDtypeStruct((R, C), F32)
    return pl.pallas_call(
        body, name=name, grid=(R // tr, C // tc),
        in_specs=[blk] * (3 + P_), out_specs=[blk] * 4, out_shape=[sds] * 4,
        compiler_params=_cparams("parallel", "parallel"),
    )(w, m, v, *gparts)


SUBLANES = 8


def _conv_pre(u, up, w_ref, b, first):
    tr = u.shape[0]
    rows = lax.broadcasted_iota(jnp.int32, u.shape, 0)
    keep = 1.0 - first.astype(F32)
    acc = b + w_ref[SSD_CONV - 1:SSD_CONV, :] * u
    shifted = []
    for j in range(1, SSD_CONV):
        su = pltpu.roll(u, j, 0)
        sp = jnp.tile(pltpu.roll(up, j, 0) * keep, (tr // SUBLANES, 1))
        sh = jnp.where(rows < j, sp, su)
        shifted.append(sh)
        acc = acc + w_ref[SSD_CONV - 1 - j:SSD_CONV - j, :] * sh
    return acc, shifted


def _conv_fwd(u, w, b, name):
    T, C = u.shape
    tr, tc = _tile(T, 256, 8), _tile(C, 1024)

    def body(u_ref, up_ref, w_ref, b_ref, o_ref):
        pre, _ = _conv_pre(u_ref[...], up_ref[...], w_ref, b_ref[...], pl.program_id(0) == 0)
        o_ref[...] = pre * _sigmoid(pre)

    return pl.pallas_call(
        body, name=name, grid=(T // tr, C // tc),
        in_specs=[pl.BlockSpec((tr, tc), lambda i, j: (i, j)),
                  pl.BlockSpec((SUBLANES, tc), lambda i, j: (jnp.maximum(i * (tr // SUBLANES) - 1, 0), j)),
                  pl.BlockSpec((SSD_CONV, tc), lambda i, j: (0, j)),
                  pl.BlockSpec((1, tc), lambda i, j: (0, j))],
        out_specs=pl.BlockSpec((tr, tc), lambda i, j: (i, j)),
        out_shape=jax.ShapeDtypeStruct((T, C), F32),
        compiler_params=_cparams("parallel", "parallel"),
    )(u, u, w, b.reshape(1, C))


def _conv_bwd_pre(u, w, b, dact, name):
    T, C = u.shape
    tr, tc = _tile(T, 256, 8), _tile(C, 1024)

    def body(u_ref, up_ref, w_ref, b_ref, da_ref, dp_ref, dw_ref, db_ref):
        i = pl.program_id(1)
        uv = u_ref[...]
        pre, shifted = _conv_pre(uv, up_ref[...], w_ref, b_ref[...], i == 0)
        sg = _sigmoid(pre)
        dpre = da_ref[...] * (sg * (1.0 + pre * (1.0 - sg)))
        dp_ref[...] = dpre

        @pl.when(i == 0)
        def _():
            dw_ref[...] = jnp.zeros_like(dw_ref)
            db_ref[...] = jnp.zeros_like(db_ref)

        db_ref[...] += jnp.sum(dpre, axis=0, keepdims=True)
        dw_ref[SSD_CONV - 1:SSD_CONV, :] += jnp.sum(dpre * uv, axis=0, keepdims=True)
        for j in range(1, SSD_CONV):
            dw_ref[SSD_CONV - 1 - j:SSD_CONV - j, :] += jnp.sum(dpre * shifted[j - 1], axis=0, keepdims=True)

    blk = pl.BlockSpec((tr, tc), lambda j, i: (i, j))
    dpre, dw, db = pl.pallas_call(
        body, name=name, grid=(C // tc, T // tr),
        in_specs=[blk, pl.BlockSpec((SUBLANES, tc), lambda j, i: (jnp.maximum(i * (tr // SUBLANES) - 1, 0), j)),
                  pl.BlockSpec((SSD_CONV, tc), lambda j, i: (0, j)),
                  pl.BlockSpec((1, tc), lambda j, i: (0, j)), blk],
        out_specs=[blk, pl.BlockSpec((SSD_CONV, tc), lambda j, i: (0, j)), pl.BlockSpec((1, tc), lambda j, i: (0, j))],
        out_shape=[jax.ShapeDtypeStruct((T, C), F32), jax.ShapeDtypeStruct((SSD_CONV, C), F32),
                   jax.ShapeDtypeStruct((1, C), F32)],
        compiler_params=_cparams("parallel", "arbitrary"),
    )(u, u, w, b.reshape(1, C), dact)
    return dpre, dw, db.reshape(C)


def _conv_bwd_in(dpre, w, name):
    T, C = dpre.shape
    tr, tc = _tile(T, 256, 8), _tile(C, 1024)
    nb = T // tr

    def body(d_ref, dn_ref, w_ref, o_ref):
        d = d_ref[...]
        keep = 1.0 - (pl.program_id(0) == nb - 1).astype(F32)
        dn = dn_ref[...] * keep
        rows = lax.broadcasted_iota(jnp.int32, d.shape, 0)
        acc = w_ref[SSD_CONV - 1:SSD_CONV, :] * d
        for j in range(1, SSD_CONV):
            sd = pltpu.roll(d, tr - j, 0)
            sn = jnp.tile(pltpu.roll(dn, SUBLANES - j, 0), (tr // SUBLANES, 1))
            acc = acc + w_ref[SSD_CONV - 1 - j:SSD_CONV - j, :] * jnp.where(rows >= tr - j, sn, sd)
        o_ref[...] = acc.astype(BF16)

    return pl.pallas_call(
        body, name=name, grid=(nb, C // tc),
        in_specs=[pl.BlockSpec((tr, tc), lambda i, j: (i, j)),
                  pl.BlockSpec((SUBLANES, tc), lambda i, j: (jnp.minimum((i + 1) * (tr // SUBLANES), T // SUBLANES - 1), j)),
                  pl.BlockSpec((SSD_CONV, tc), lambda i, j: (0, j))],
        out_specs=pl.BlockSpec((tr, tc), lambda i, j: (i, j)),
        out_shape=jax.ShapeDtypeStruct((T, C), BF16),
        compiler_params=_cparams("parallel", "parallel"),
    )(dpre, dpre, w)


def _mem_probs(qh, kh, scale):
    s = _dot_nt(qh.astype(BF16), kh.astype(BF16)) * scale
    m = jnp.max(s, axis=-1, keepdims=True)
    p = jnp.exp(s - m)
    return p / jnp.sum(p, axis=-1, keepdims=True)


def _memattn_fwd(q, kv, name):
    T, MW = q.shape
    NM = kv.shape[0]
    hd = MW // MEM_HEADS
    scale = hd ** -0.5
    tq = _tile(T, 512, 8)

    def body(q_ref, kv_ref, o_ref):
        for h in range(MEM_HEADS):
            sl = slice(h * hd, (h + 1) * hd)
            p = _mem_probs(q_ref[:, sl], kv_ref[:, sl], scale)
            vh = kv_ref[:, MW + h * hd:MW + (h + 1) * hd]
            o_ref[:, sl] = _dot(p.astype(BF16), vh.astype(BF16))

    return pl.pallas_call(
        body, name=name, grid=(T // tq,),
        in_specs=[pl.BlockSpec((tq, MW), lambda i: (i, 0)), pl.BlockSpec((NM, 2 * MW), lambda i: (0, 0))],
        out_specs=pl.BlockSpec((tq, MW), lambda i: (i, 0)),
        out_shape=jax.ShapeDtypeStruct((T, MW), F32),
        compiler_params=_cparams("parallel"),
    )(q, kv)


def _memattn_bwd(q, kv, dy, name):
    T, MW = q.shape
    NM = kv.shape[0]
    hd = MW // MEM_HEADS
    scale = hd ** -0.5
    tq = _tile(T, 512, 8)

    def body(q_ref, kv_ref, dy_ref, dq_ref, dkv_ref):
        @pl.when(pl.program_id(0) == 0)
        def _():
            dkv_ref[...] = jnp.zeros_like(dkv_ref)

        for h in range(MEM_HEADS):
            sl = slice(h * hd, (h + 1) * hd)
            vsl = slice(MW + h * hd, MW + (h + 1) * hd)
            qh = q_ref[:, sl]
            kh = kv_ref[:, sl]
            vh = kv_ref[:, vsl]
            dyh = dy_ref[:, sl].astype(BF16)
            p = _mem_probs(qh, kh, scale)
            dp = _dot_nt(dyh, vh.astype(BF16))
            ds = p * (dp - jnp.sum(dp * p, axis=-1, keepdims=True)) * scale
            dq_ref[:, sl] = _dot(ds.astype(BF16), kh.astype(BF16)).astype(BF16)
            dkv_ref[:, sl] += _dot(ds.T.astype(BF16), qh.astype(BF16))
            dkv_ref[:, vsl] += _dot(p.T.astype(BF16), dyh)

    return pl.pallas_call(
        body, name=name, grid=(T // tq,),
        in_specs=[pl.BlockSpec((tq, MW), lambda i: (i, 0)), pl.BlockSpec((NM, 2 * MW), lambda i: (0, 0)),
                  pl.BlockSpec((tq, MW), lambda i: (i, 0))],
        out_specs=[pl.BlockSpec((tq, MW), lambda i: (i, 0)), pl.BlockSpec((NM, 2 * MW), lambda i: (0, 0))],
        out_shape=[jax.ShapeDtypeStruct((T, MW), BF16), jax.ShapeDtypeStruct((NM, 2 * MW), F32)],
        compiler_params=_cparams("arbitrary"),
    )(q, kv, dy)


def _silu_parts(z):
    sg = _sigmoid(z)
    return z * sg, sg * (1.0 + z * (1.0 - sg))


def _gate_fwd(a, z, name):
    return _elementwise(lambda av, zv: (av * _silu_parts(zv)[0],), [a, z], [], [BF16], name)[0]


def _gate_bwd(a, z, d, name):
    def fn(av, zv, dv):
        s, ds = _silu_parts(zv)
        return dv * s, dv * av * ds
    return _elementwise(fn, [a, z, d], [], [F32, BF16], name)


def _gate_norm_fwd(y, z, g, groups, name):
    T, C = y.shape
    gw = C // groups
    tr = _tile(T, 128, 8)

    def body(y_ref, z_ref, g_ref, o_ref):
        for k in range(groups):
            sl = slice(k * gw, (k + 1) * gw)
            u = y_ref[:, sl] * _silu_parts(z_ref[:, sl])[0]
            r = lax.rsqrt(jnp.mean(u * u, axis=-1, keepdims=True) + EPS)
            o_ref[:, sl] = (u * r * g_ref[:, sl]).astype(BF16)

    row = pl.BlockSpec((tr, C), lambda i: (i, 0))
    return pl.pallas_call(
        body, name=name, grid=(T // tr,), in_specs=[row, row, pl.BlockSpec((1, C), lambda i: (0, 0))],
        out_specs=row, out_shape=jax.ShapeDtypeStruct((T, C), BF16),
        compiler_params=_cparams("parallel"),
    )(y, z, g.reshape(1, C))


def _gate_norm_bwd(y, z, g, d, groups, name):
    T, C = y.shape
    gw = C // groups
    tr = _tile(T, 128, 8)

    def body(y_ref, z_ref, g_ref, d_ref, dy_ref, dz_ref, dg_ref):
        @pl.when(pl.program_id(0) == 0)
        def _():
            dg_ref[...] = jnp.zeros_like(dg_ref)

        for k in range(groups):
            sl = slice(k * gw, (k + 1) * gw)
            yv = y_ref[:, sl]
            s, ds = _silu_parts(z_ref[:, sl])
            u = yv * s
            r = lax.rsqrt(jnp.mean(u * u, axis=-1, keepdims=True) + EPS)
            uhat = u * r
            dv = d_ref[:, sl]
            dg_ref[:, sl] += jnp.sum(dv * uhat, axis=0, keepdims=True)
            duhat = dv * g_ref[:, sl]
            du = r * (duhat - uhat * jnp.mean(duhat * uhat, axis=-1, keepdims=True))
            dy_ref[:, sl] = du * s
            dz_ref[:, sl] = (du * yv * ds).astype(BF16)

    row = pl.BlockSpec((tr, C), lambda i: (i, 0))
    vec = pl.BlockSpec((1, C), lambda i: (0, 0))
    dy, dz, dg = pl.pallas_call(
        body, name=name, grid=(T // tr,), in_specs=[row, row, vec, row], out_specs=[row, row, vec],
        out_shape=[jax.ShapeDtypeStruct((T, C), F32), jax.ShapeDtypeStruct((T, C), BF16),
                   jax.ShapeDtypeStruct((1, C), F32)],
        compiler_params=_cparams("arbitrary"),
    )(y, z, g.reshape(1, C), d)
    return dy, dz, dg.reshape(C)


def _ssd_common(raw, bias, alog, Q, HP, HPG):
    P_ = SSD_HEAD_DIM
    dt_in = raw + bias
    dt = jnp.maximum(dt_in, 0.0) + jnp.log(1.0 + jnp.exp(-jnp.abs(dt_in)))
    a = -jnp.exp(alog)
    r_q = lax.broadcasted_iota(jnp.int32, (Q, Q), 0)
    c_q = lax.broadcasted_iota(jnp.int32, (Q, Q), 1)
    causal = r_q >= c_q
    tril = causal.astype(BF16)
    A = _xdot_l(tril, dt * a)
    e_r = lax.broadcasted_iota(jnp.int32, (LANES, HP), 0)
    e_c = lax.broadcasted_iota(jnp.int32, (LANES, HP), 1)
    E = ((e_c >= e_r * P_) & (e_c < (e_r + 1) * P_) & (e_r < HPG)).astype(BF16)
    return dt, a, A, causal, E


def _head_cols(v, vt, j):
    lane = lax.broadcasted_iota(jnp.int32, v.shape, 1)
    sub = lax.broadcasted_iota(jnp.int32, vt.shape, 0)
    col = jnp.sum(jnp.where(lane == j, v, 0.0), axis=-1, keepdims=True)
    row = jnp.sum(jnp.where(sub == j, vt, 0.0), axis=0, keepdims=True)
    return col, row


def _ssd_fwd(act, raw_g, bias_g, alog_g, dsk_g, TOK, name):
    T = act.shape[0]
    G, N, Q, P_ = SSD_GROUPS, SSD_STATE, SSD_CHUNK, SSD_HEAD_DIM
    HP = TOK // G
    HPG = HP // P_
    NC = T // Q

    def body(x_ref, b_ref, c_ref, raw_ref, bias_ref, alog_ref, dsk_ref, y_ref, hp_ref, hT):
        @pl.when(pl.program_id(1) == 0)
        def _():
            hT[...] = jnp.zeros_like(hT)

        xs = x_ref[...]
        Bb = b_ref[...].astype(BF16)
        Cb = c_ref[...].astype(BF16)
        dt, a, A, causal, E = _ssd_common(raw_ref[...], bias_ref[...], alog_ref[...], Q, HP, HPG)
        AT = A.T
        dt_e = _xdot(dt, E)
        A_e = _xdot(A, E)
        dsk_e = _xdot(jnp.broadcast_to(dsk_ref[...], (Q, LANES)), E)
        rows = lax.broadcasted_iota(jnp.int32, (Q, HP), 0)
        cols = lax.broadcasted_iota(jnp.int32, (Q, HP), 1)
        Al_e = jnp.sum(jnp.where(rows == Q - 1, A_e, 0.0), axis=0, keepdims=True)
        xdt = xs * dt_e
        hprev = hT[...]
        hp_ref[...] = hprev
        CB = _dot_nt(Cb, Bb)
        y = _dot(Cb, hprev.astype(BF16)) * jnp.exp(A_e) + dsk_e * xs
        M, xh = [], []
        for j in range(HPG):
            a_col, a_row = _head_cols(A, AT, j)
            M.append((CB * jnp.exp(jnp.where(causal, a_col - a_row, NEG))).astype(BF16))
            xh.append(jnp.where((cols >= j * P_) & (cols < (j + 1) * P_), xdt, 0.0).astype(BF16))
        y_ref[...] = y + _dot(jnp.concatenate(M, axis=1), jnp.concatenate(xh, axis=0))
        dte = jnp.exp(Al_e - A_e)
        hT[...] = jnp.exp(Al_e) * hprev + _dot(b_ref[...].T.astype(BF16), (xdt * dte).astype(BF16))

    nbx = TOK // N
    par = pl.BlockSpec((None, 1, LANES), lambda g, c: (g, 0, 0))
    return pl.pallas_call(
        body, name=name, grid=(G, NC),
        in_specs=[pl.BlockSpec((Q, HP), lambda g, c: (c, g)),
                  pl.BlockSpec((Q, N), lambda g, c: (c, nbx + g)),
                  pl.BlockSpec((Q, N), lambda g, c: (c, nbx + G + g)),
                  pl.BlockSpec((None, Q, LANES), lambda g, c: (g, c, 0)), par, par, par],
        out_specs=[pl.BlockSpec((Q, HP), lambda g, c: (c, g)),
                   pl.BlockSpec((None, None, N, HP), lambda g, c: (g, c, 0, 0))],
        out_shape=[jax.ShapeDtypeStruct((T, TOK), F32), jax.ShapeDtypeStruct((G, NC, N, HP), F32)],
        scratch_shapes=[pltpu.VMEM((N, HP), F32)],
        compiler_params=_cparams("parallel", "arbitrary"),
    )(act, act, act, raw_g, bias_g, alog_g, dsk_g)


def _ssd_bwd(act, raw_g, bias_g, alog_g, dsk_g, hprev, dy, TOK, name):
    T = act.shape[0]
    G, N, Q, P_ = SSD_GROUPS, SSD_STATE, SSD_CHUNK, SSD_HEAD_DIM
    HP = TOK // G
    HPG = HP // P_
    NC = T // Q

    def body(x_ref, b_ref, c_ref, raw_ref, bias_ref, alog_ref, dsk_ref, hp_ref, dy_ref,
             dx_ref, db_ref, dc_ref, draw_ref, dalog_ref, dbias_ref, ddsk_ref, dHT):
        @pl.when(pl.program_id(1) == 0)
        def _():
            dHT[...] = jnp.zeros_like(dHT)
            dalog_ref[...] = jnp.zeros_like(dalog_ref)
            dbias_ref[...] = jnp.zeros_like(dbias_ref)
            ddsk_ref[...] = jnp.zeros_like(ddsk_ref)

        xs = x_ref[...]
        dyv = dy_ref[...]
        Bm = b_ref[...]
        Cm = c_ref[...]
        Bb = Bm.astype(BF16)
        Cb = Cm.astype(BF16)
        raw_in = raw_ref[...] + bias_ref[...]
        dt, a, A, causal, E = _ssd_common(raw_ref[...], bias_ref[...], alog_ref[...], Q, HP, HPG)
        AT = A.T
        t_r = lax.broadcasted_iota(jnp.int32, (HP, LANES), 0)
        t_c = lax.broadcasted_iota(jnp.int32, (HP, LANES), 1)
        ET = ((t_r >= t_c * P_) & (t_r < (t_c + 1) * P_) & (t_c < HPG)).astype(BF16)
        dt_e = _xdot(dt, E)
        A_e = _xdot(A, E)
        dsk_e = _xdot(jnp.broadcast_to(dsk_ref[...], (Q, LANES)), E)
        rows = lax.broadcasted_iota(jnp.int32, (Q, HP), 0)
        cols = lax.broadcasted_iota(jnp.int32, (Q, HP), 1)
        last = rows == Q - 1
        Al_e = jnp.sum(jnp.where(last, A_e, 0.0), axis=0, keepdims=True)
        eA = jnp.exp(A_e)
        eAl = jnp.exp(Al_e)
        dte = jnp.exp(Al_e - A_e)
        xdt = xs * dt_e
        xdt_b = xdt.astype(BF16)
        CB = _dot_nt(Cb, Bb)
        HT = hp_ref[...]
        HTb = HT.astype(BF16)
        dH = dHT[...]
        dHb = dH.astype(BF16)
        dys = (dyv * eA).astype(BF16)
        CH = _dot(Cb, HTb)
        dC = _dot_nt(dys, HTb)
        dH_prev = _dot(Cm.T.astype(BF16), dys) + eAl * dH
        dAe = dyv * CH * eA
        dAl = eAl * jnp.sum(dH * HT, axis=0, keepdims=True)
        W = _dot(Bb, dHb)
        dxdt = W * dte
        dd = W * xdt * dte
        dB = _dot_nt((xdt * dte).astype(BF16), dHb)
        dAl = dAl + jnp.sum(dd, axis=0, keepdims=True)
        dAe = dAe - dd + jnp.where(last, dAl, 0.0)
        lane = lax.broadcasted_iota(jnp.int32, (Q, LANES), 1)
        sub = lax.broadcasted_iota(jnp.int32, (LANES, Q), 0)
        dCB = jnp.zeros((Q, Q), F32)
        dA_col = jnp.zeros((Q, LANES), F32)
        dA_row = jnp.zeros((LANES, Q), F32)
        dy_heads = jnp.concatenate(
            [jnp.where((cols >= j * P_) & (cols < (j + 1) * P_), dyv, 0.0).astype(BF16) for j in range(HPG)], axis=0)
        G_heads = _dot_nt(dy_heads, xdt_b)
        MT = []
        for j in range(HPG):
            a_col, a_row = _head_cols(A, AT, j)
            L = jnp.exp(jnp.where(causal, a_col - a_row, NEG))
            GL = G_heads[j * Q:(j + 1) * Q] * L
            dCB = dCB + GL
            dLL = GL * CB
            dA_col = dA_col + jnp.where(lane == j, jnp.sum(dLL, axis=-1, keepdims=True), 0.0)
            dA_row = dA_row + jnp.where(sub == j, jnp.sum(dLL, axis=0, keepdims=True), 0.0)
            MT.append((CB * L).T.astype(BF16))
        dxdt = dxdt + _dot(jnp.concatenate(MT, axis=1), dy_heads)
        dC = dC + _dot(dCB.astype(BF16), Bb)
        dB = dB + _dot(dCB.T.astype(BF16), Cb)
        dA = dA_col - dA_row.T + _xdot(dAe, ET)
        triu = (lax.broadcasted_iota(jnp.int32, (Q, Q), 1) >= lax.broadcasted_iota(jnp.int32, (Q, Q), 0)).astype(BF16)
        rcs = _xdot_l(triu, dA)
        ddt = a * rcs + _xdot(dxdt * xs, ET)
        draw = ddt * _sigmoid(raw_in)
        draw_ref[...] = draw
        dalog_ref[...] += jnp.sum(dt * rcs, axis=0, keepdims=True) * a
        dbias_ref[...] += jnp.sum(draw, axis=0, keepdims=True)
        ddsk_ref[...] += jnp.sum(_xdot(dyv * xs, ET), axis=0, keepdims=True)
        dx_ref[...] = dxdt * dt_e + dsk_e * dyv
        db_ref[...] = dB
        dc_ref[...] = dC
        dHT[...] = dH_prev

    nbx = TOK // N
    rv = lambda c: NC - 1 - c
    par = pl.BlockSpec((None, 1, LANES), lambda g, c: (g, 0, 0))
    xsp = pl.BlockSpec((Q, HP), lambda g, c: (rv(c), g))
    outs = pl.pallas_call(
        body, name=name, grid=(G, NC),
        in_specs=[xsp,
                  pl.BlockSpec((Q, N), lambda g, c: (rv(c), nbx + g)),
                  pl.BlockSpec((Q, N), lambda g, c: (rv(c), nbx + G + g)),
                  pl.BlockSpec((None, Q, LANES), lambda g, c: (g, rv(c), 0)), par, par, par,
                  pl.BlockSpec((None, None, N, HP), lambda g, c: (g, rv(c), 0, 0)), xsp],
        out_specs=[xsp, pl.BlockSpec((Q, N), lambda g, c: (rv(c), g)), pl.BlockSpec((Q, N), lambda g, c: (rv(c), g)),
                   pl.BlockSpec((None, Q, LANES), lambda g, c: (g, rv(c), 0)), par, par, par],
        out_shape=[jax.ShapeDtypeStruct((T, TOK), F32), jax.ShapeDtypeStruct((T, G * N), F32),
                   jax.ShapeDtypeStruct((T, G * N), F32), jax.ShapeDtypeStruct((G, T, LANES), F32),
                   jax.ShapeDtypeStruct((G, 1, LANES), F32), jax.ShapeDtypeStruct((G, 1, LANES), F32),
                   jax.ShapeDtypeStruct((G, 1, LANES), F32)],
        scratch_shapes=[pltpu.VMEM((N, HP), F32)],
        compiler_params=_cparams("parallel", "arbitrary"),
    )(act, act, act, raw_g, bias_g, alog_g, dsk_g, hprev, dy)
    return outs


def _heads_per_block(H):
    for hb in (12, 8, 6, 4, 3, 2, 1):
        if H % hb == 0:
            return hb
    return 1


def _alibi_slope(head_index, n_alibi):
    c = -ALIBI_MAX_EXP * math.log(2.0) / n_alibi
    return jnp.exp(jnp.full((1, 1), c, F32) * (head_index + 1).astype(F32))


def _attn_masks(b, nb):
    Bq = ATTN_BLOCK
    iq = lax.broadcasted_iota(jnp.int32, (Bq, 2 * Bq), 0)
    jk = lax.broadcasted_iota(jnp.int32, (Bq, 2 * Bq), 1)
    rel = iq + Bq - jk
    mask = (rel >= 0) & (rel <= Bq) & (jk + jnp.where(b > 0, Bq, 0) >= Bq)
    rel_n = lax.broadcasted_iota(jnp.int32, (Bq, Bq), 0) + Bq - lax.broadcasted_iota(jnp.int32, (Bq, Bq), 1)
    mask_n = (rel_n + jnp.where(b < nb - 1, 0, 4 * Bq)) <= Bq
    return rel.astype(F32), mask, rel_n.astype(F32), mask_n


def _rows2(a, b):
    return jnp.concatenate([a, b], axis=0)


ATTN_UNITS_IN_FLIGHT = 6


def _attn_units_fwd(load, n_units, slopes, masks, scale, store):
    rel_f, mask, _, _ = masks
    for g0 in range(0, n_units, ATTN_UNITS_IN_FLIGHT):
        ids = range(g0, min(g0 + ATTN_UNITS_IN_FLIGHT, n_units))
        units = [load(i) for i in ids]
        raw = [_dot_nt(q, _rows2(kp, kc)) for q, kc, kp, vc, vp in units]
        soft = []
        for i, s_raw in zip(ids, raw):
            s = jnp.where(mask, s_raw * scale - slopes[i] * rel_f, NEG)
            m = jnp.max(s, axis=-1, keepdims=True)
            p = jnp.exp(s - m)
            den = jnp.sum(p, axis=-1, keepdims=True)
            soft.append((p.astype(BF16), den, m + jnp.log(den)))
        for i, (p, den, lse), (q, kc, kp, vc, vp) in zip(ids, soft, units):
            store(i, _dot(p, _rows2(vp, vc)) / den, lse)


def _attn_units_bwd(load, n_units, slopes, masks, scale, store):
    rel_f, mask, reln_f, mask_n = masks
    Bq = ATTN_BLOCK
    for g0 in range(0, n_units, ATTN_UNITS_IN_FLIGHT):
        ids = range(g0, min(g0 + ATTN_UNITS_IN_FLIGHT, n_units))
        units = [load(i) for i in ids]
        prods = []
        for q0, q1, kp, k0, vp, v0, do0, do1, y0, y1, lse0, lse1 in units:
            kcat = _rows2(kp, k0)
            do0b = do0.astype(BF16)
            do1b = do1.astype(BF16)
            prods.append((kcat, do0b, do1b, _dot_nt(q0, kcat), _dot_nt(do0b, _rows2(vp, v0)), _dot_nt(q1, k0),
                          _dot_nt(do1b, v0)))
        mids = []
        for i, u, (kcat, do0b, do1b, s_raw, dp_raw, sn_raw, dpn_raw) in zip(ids, units, prods):
            q0, q1, kp, k0, vp, v0, do0, do1, y0, y1, lse0, lse1 = u
            delta0 = jnp.sum(do0 * y0, axis=-1, keepdims=True)
            delta1 = jnp.sum(do1 * y1, axis=-1, keepdims=True)
            p = jnp.exp(jnp.where(mask, s_raw * scale - slopes[i] * rel_f, NEG) - lse0)
            ds = p * (dp_raw - delta0)
            p_n = jnp.exp(jnp.where(mask_n, sn_raw * scale - slopes[i] * reln_f, NEG) - lse1)
            ds_n = p_n * (dpn_raw - delta1)
            mids.append((ds.astype(BF16), _rows2(ds[:, Bq:], ds_n).T.astype(BF16),
                         _rows2(p[:, Bq:], p_n).T.astype(BF16)))
        for i, u, (kcat, do0b, do1b, *_), (dsb, dsk_t, pk_t) in zip(ids, units, prods, mids):
            store(i, scale * _dot(dsb, kcat), scale * _dot(dsk_t, _rows2(u[0], u[1])), _dot(pk_t, _rows2(do0b, do1b)))


def _attn_fwd_strided(q, k, v, gi, d, name):
    T, TOK = q.shape
    E_ = ATTN_HEAD_DIM
    H = TOK // E_
    n_alibi = len(DILATED_GROUPS) * H
    Bq = ATTN_BLOCK
    RB = Bq * d
    nb = T // RB
    scale = E_ ** -0.5

    def body(q_ref, kc_ref, kp_ref, vc_ref, vp_ref, o_ref, l_ref):
        masks = _attn_masks(pl.program_id(1), nb)
        slope = _alibi_slope(gi * H + pl.program_id(0), n_alibi) * float(d)
        rows = lambda r: pl.ds(r, Bq, stride=d)

        def load(r):
            return tuple(ref[rows(r), :].astype(BF16) for ref in (q_ref, kc_ref, kp_ref, vc_ref, vp_ref))

        def store(r, o, lse):
            o_ref[rows(r), :] = o
            l_ref[rows(r), :] = jnp.broadcast_to(lse, (Bq, E_))

        _attn_units_fwd(load, d, [slope] * d, masks, scale, store)

    cur = pl.BlockSpec((RB, E_), lambda h, b: (b, h))
    prev = pl.BlockSpec((RB, E_), lambda h, b: (jnp.maximum(b - 1, 0), h))
    sds = jax.ShapeDtypeStruct((T, TOK), F32)
    return pl.pallas_call(
        body, name=name, grid=(H, nb), in_specs=[cur, cur, prev, cur, prev], out_specs=[cur, cur],
        out_shape=[sds, sds], compiler_params=_cparams("parallel", "parallel"),
    )(q, k, k, v, v)


def _attn_bwd_strided(q, k, v, y, lse, dy, gi, d, name):
    T, TOK = q.shape
    E_ = ATTN_HEAD_DIM
    H = TOK // E_
    n_alibi = len(DILATED_GROUPS) * H
    Bq = ATTN_BLOCK
    RB = Bq * d
    nb = T // RB
    scale = E_ ** -0.5

    def body(q0_ref, q1_ref, kp_ref, k0_ref, vp_ref, v0_ref, do0_ref, do1_ref, y0_ref, y1_ref, l0_ref, l1_ref,
             dq_ref, dk_ref, dv_ref):
        masks = _attn_masks(pl.program_id(1), nb)
        slope = _alibi_slope(gi * H + pl.program_id(0), n_alibi) * float(d)
        rows = lambda r: pl.ds(r, Bq, stride=d)

        def load(r):
            return (tuple(ref[rows(r), :].astype(BF16) for ref in (q0_ref, q1_ref, kp_ref, k0_ref, vp_ref, v0_ref))
                    + tuple(ref[rows(r), :] for ref in (do0_ref, do1_ref, y0_ref, y1_ref))
                    + tuple(jnp.max(ref[rows(r), :], axis=-1, keepdims=True) for ref in (l0_ref, l1_ref)))

        def store(r, dq, dk, dv):
            dq_ref[rows(r), :] = dq
            dk_ref[rows(r), :] = dk
            dv_ref[rows(r), :] = dv

        _attn_units_bwd(load, d, [slope] * d, masks, scale, store)

    cur = pl.BlockSpec((RB, E_), lambda h, b: (b, h))
    prev = pl.BlockSpec((RB, E_), lambda h, b: (jnp.maximum(b - 1, 0), h))
    nxt = pl.BlockSpec((RB, E_), lambda h, b: (jnp.minimum(b + 1, nb - 1), h))
    sds = jax.ShapeDtypeStruct((T, TOK), F32)
    return pl.pallas_call(
        body, name=name, grid=(H, nb),
        in_specs=[cur, nxt, prev, cur, prev, cur, cur, nxt, cur, nxt, cur, nxt],
        out_specs=[cur, cur, cur], out_shape=[sds, sds, sds],
        compiler_params=_cparams("parallel", "parallel"),
    )(q, q, k, k, v, v, dy, dy, y, y, lse, lse)


def _attn_fwd(q, k, v, gi, window, d, name):
    T, TOK = q.shape
    E_ = ATTN_HEAD_DIM
    H = TOK // E_
    n_alibi = len(DILATED_GROUPS) * H
    assert window // d == ATTN_BLOCK and (T // d) % ATTN_BLOCK == 0
    if d > 1:
        return _attn_fwd_strided(q, k, v, gi, d, name)
    n_sub = T // d
    nb = n_sub // ATTN_BLOCK
    HB = _heads_per_block(H)
    NHB = H // HB
    hbw = HB * E_
    scale = E_ ** -0.5
    Bq = ATTN_BLOCK

    def body(q_ref, kc_ref, kp_ref, vc_ref, vp_ref, o_ref, l_ref):
        hb = pl.program_id(1)
        masks = _attn_masks(pl.program_id(2), nb)
        slopes = [_alibi_slope(gi * H + hb * HB + hh, n_alibi) * float(d) for hh in range(HB)]
        cols = lambda hh: slice(hh * E_, (hh + 1) * E_)

        def load(hh):
            return tuple(ref[:, cols(hh)].astype(BF16) for ref in (q_ref, kc_ref, kp_ref, vc_ref, vp_ref))

        def store(hh, o, lse):
            o_ref[:, cols(hh)] = o
            l_ref[:, cols(hh)] = jnp.broadcast_to(lse, (Bq, E_))

        _attn_units_fwd(load, HB, slopes, masks, scale, store)

    cur = pl.BlockSpec((Bq, hbw), lambda r, h, b: (b, r * NHB + h))
    prev = pl.BlockSpec((Bq, hbw), lambda r, h, b: (jnp.maximum(b - 1, 0), r * NHB + h))
    view = lambda t: t.reshape(n_sub, d * TOK)
    sds = jax.ShapeDtypeStruct((n_sub, d * TOK), F32)
    o, l = pl.pallas_call(
        body, name=name, grid=(d, NHB, nb), in_specs=[cur, cur, prev, cur, prev], out_specs=[cur, cur],
        out_shape=[sds, sds], compiler_params=_cparams("parallel", "parallel", "parallel"),
    )(view(q), view(k), view(k), view(v), view(v))
    return o.reshape(T, TOK), l.reshape(T, TOK)


def _attn_combine(os_, ls_, name):
    def fn(*v):
        n = len(v) // 2
        o, l = v[:n], v[n:]
        m = l[0]
        for t in l[1:]:
            m = jnp.maximum(m, t)
        e = [jnp.exp(t - m) for t in l]
        den = e[0]
        for t in e[1:]:
            den = den + t
        y = e[0] * o[0]
        for t, u in zip(e[1:], o[1:]):
            y = y + t * u
        return y / den, m + jnp.log(den)
    return _elementwise(fn, list(os_) + list(ls_), [], [F32, F32], name)


def _attn_bwd(q, k, v, y, lse, dy, gi, window, d, name):
    T, TOK = q.shape
    E_ = ATTN_HEAD_DIM
    H = TOK // E_
    n_alibi = len(DILATED_GROUPS) * H
    if d > 1:
        return _attn_bwd_strided(q, k, v, y, lse, dy, gi, d, name)
    n_sub = T // d
    nb = n_sub // ATTN_BLOCK
    HB = _heads_per_block(H)
    NHB = H // HB
    hbw = HB * E_
    scale = E_ ** -0.5
    Bq = ATTN_BLOCK

    def body(q0_ref, q1_ref, kp_ref, k0_ref, vp_ref, v0_ref, do0_ref, do1_ref, y0_ref, y1_ref, l0_ref, l1_ref,
             dq_ref, dk_ref, dv_ref):
        hb = pl.program_id(1)
        masks = _attn_masks(pl.program_id(2), nb)
        slopes = [_alibi_slope(gi * H + hb * HB + hh, n_alibi) * float(d) for hh in range(HB)]
        cols = lambda hh: slice(hh * E_, (hh + 1) * E_)

        def load(hh):
            return (tuple(ref[:, cols(hh)].astype(BF16) for ref in (q0_ref, q1_ref, kp_ref, k0_ref, vp_ref, v0_ref))
                    + tuple(ref[:, cols(hh)] for ref in (do0_ref, do1_ref, y0_ref, y1_ref))
                    + tuple(jnp.max(ref[:, cols(hh)], axis=-1, keepdims=True) for ref in (l0_ref, l1_ref)))

        def store(hh, dq, dk, dv):
            dq_ref[:, cols(hh)] = dq.astype(BF16)
            dk_ref[:, cols(hh)] = dk.astype(BF16)
            dv_ref[:, cols(hh)] = dv.astype(BF16)

        _attn_units_bwd(load, HB, slopes, masks, scale, store)

    cur = pl.BlockSpec((Bq, hbw), lambda r, h, b: (b, r * NHB + h))
    prev = pl.BlockSpec((Bq, hbw), lambda r, h, b: (jnp.maximum(b - 1, 0), r * NHB + h))
    nxt = pl.BlockSpec((Bq, hbw), lambda r, h, b: (jnp.minimum(b + 1, nb - 1), r * NHB + h))
    view = lambda t: t.reshape(n_sub, d * TOK)
    sds = jax.ShapeDtypeStruct((n_sub, d * TOK), BF16)
    dq, dk, dv = pl.pallas_call(
        body, name=name, grid=(d, NHB, nb),
        in_specs=[cur, nxt, prev, cur, prev, cur, cur, nxt, cur, nxt, cur, nxt],
        out_specs=[cur, cur, cur], out_shape=[sds, sds, sds],
        compiler_params=_cparams("parallel", "parallel", "parallel"),
    )(view(q), view(q), view(k), view(k), view(v), view(v), view(dy), view(dy), view(y), view(y), view(lse), view(lse))
    return dq.reshape(T, TOK), dk.reshape(T, TOK), dv.reshape(T, TOK)


_FLIPS = {
    "xy": [(1, 0, 0), (0, 1, 0), (1, 1, 0)],
    "c": [(0, 0, 1)],
    "xyc": [(dx, dy, dc) for dx in (0, 1) for dy in (0, 1) for dc in (0, 1) if (dx, dy, dc) != (0, 0, 0)],
}


def _comm_parts(group, srcs, modes, handshake):
    flips = _FLIPS[group]
    F_ = len(flips)
    P_ = F_ + 1
    n = len(srcs)

    def gidx(px, py, pc):
        if group == "xy":
            return 2 * px + py
        if group == "c":
            return pc
        return 4 * px + 2 * py + pc

    def body(*refs):
        src_refs, out_refs = refs[:n], refs[n:2 * n]
        send_sems, recv_sems, loc_sems = refs[2 * n:]
        x, y, c = lax.axis_index("x"), lax.axis_index("y"), lax.axis_index("c")
        me = gidx(x, y, c)
        peers = [(1 - x if dx else x, 1 - y if dy else y, 1 - c if dc else c) for dx, dy, dc in flips]
        if handshake:
            barrier = pltpu.get_barrier_semaphore()
            for peer in peers:
                pl.semaphore_signal(barrier, inc=1, device_id=peer, device_id_type=MESH)
            pl.semaphore_wait(barrier, F_)
        local, remote = [], []
        for i in range(n):
            mode = modes[i]
            if mode != "swap":
                mine = pltpu.make_async_copy(src_refs[i] if mode == "gather" else src_refs[i].at[me],
                                             out_refs[i].at[me], loc_sems.at[i])
                mine.start()
                local.append(mine)
            for f, peer in enumerate(peers):
                cp = pltpu.make_async_remote_copy(
                    src_ref=src_refs[i].at[gidx(*peer)] if mode == "a2a" else src_refs[i],
                    dst_ref=out_refs[i] if mode == "swap" else out_refs[i].at[me],
                    send_sem=send_sems.at[i * F_ + f], recv_sem=recv_sems.at[i * F_ + f],
                    device_id=peer, device_id_type=MESH)
                cp.start()
                remote.append(cp)
        for cp in local:
            cp.wait()
        for cp in remote:
            cp.wait()

    out_shape = []
    for s, mode in zip(srcs, modes):
        assert mode != "swap" or F_ == 1
        shp = (P_,) + tuple(s.shape) if mode == "gather" else tuple(s.shape)
        out_shape.append(jax.ShapeDtypeStruct(shp, s.dtype))
    sems = [pltpu.SemaphoreType.DMA((n * F_,)), pltpu.SemaphoreType.DMA((n * F_,)), pltpu.SemaphoreType.DMA((n,))]
    return body, out_shape, sems


def _comm(name, group, srcs, modes):
    n = len(srcs)
    body, out_shape, sems = _comm_parts(group, srcs, modes, handshake=False)
    anyspec = pl.BlockSpec(memory_space=pl.ANY)
    return pl.pallas_call(body, name=name, in_specs=[anyspec] * n, out_specs=[anyspec] * n, out_shape=out_shape,
                          scratch_shapes=sems)(*srcs)


def _comm_async(name, collective_id, group, srcs, modes):
    body, out_shape, sems = _comm_parts(group, srcs, modes, handshake=True)
    return pl.kernel(body, name=name, out_type=out_shape,
                     mesh=plsc.ScalarSubcoreMesh(axis_name="sequencer", num_cores=1), scratch_types=sems,
                     compiler_params=pltpu.CompilerParams(collective_id=collective_id))(*srcs)


def _dims(D):
    MIX = 2 * D
    MW = MIX // 4
    TOK = MIX - MW
    H = TOK // SSD_HEAD_DIM
    CONV = TOK + 2 * SSD_GROUPS * SSD_STATE
    return dict(MIX=MIX, MW=MW, TOK=TOK, H=H, CONV=CONV)


def _proj_chain(dsegs, wsegs, name):
    acc = None
    for n, (ds, ws) in enumerate(zip(dsegs, wsegs)):
        acc = _mm(ds, ws, "nt", F32, f"{name}_dh{n}", add=acc)
    return acc


def _pad_lanes(a, width=LANES):
    return jnp.pad(a, [(0, 0)] * (a.ndim - 1) + [(0, width - a.shape[-1])])


def _take_cols(parts, a, b):
    out, o = [], 0
    for part in parts:
        w = part.shape[1]
        lo, hi = max(a, o), min(b, o + w)
        if lo < hi:
            out.append(part[:, lo - o:hi - o])
        o += w
    return out[0] if len(out) == 1 else jnp.concatenate(out, axis=1)


def _heads_to_groups(a, G, HPG):
    return jnp.stack([_pad_lanes(a[:, g * HPG:(g + 1) * HPG]) for g in range(G)])


def _groups_to_heads(a, HPG):
    return jnp.concatenate([a[g, :, :HPG] for g in range(a.shape[0])], axis=1)


def _ssd_layer_fwd(x, kv, p, li):
    T, D = x.shape
    dm = _dims(D)
    TOK, MW, H, CONV = dm["TOK"], dm["MW"], dm["H"], dm["CONV"]
    G = SSD_GROUPS
    HPG = H // G
    w = p["w_in"]
    cuts = [0, CONV, CONV + H, CONV + H + MW, CONV + H + MW + TOK, CONV + H + MW + TOK + MW]
    segs = {k: _take_cols(w, cuts[n], cuts[n + 1]) for n, k in enumerate(["xbc", "dt", "qm", "zt", "zm"])}
    segs["dt"] = _pad_lanes(segs["dt"])
    h = _rms_fwd(x, p["norm_g"], f"l{li}_rms")
    pr = {k: _mm(h, ws, "nn", BF16 if k == "qm" else F32, f"l{li}_in_{k}") for k, ws in segs.items()}
    act = _conv_fwd(pr["xbc"], p["conv_w"], p["conv_b"], f"l{li}_conv")
    raw_g = _heads_to_groups(pr["dt"][:, :H], G, HPG)
    hp = lambda a: _pad_lanes(a.reshape(G, 1, HPG))
    bias_g, alog_g, dsk_g = hp(p["dt_bias"]), hp(p["a_log"]), hp(p["d_skip"])
    y, hprev = _ssd_fwd(act, raw_g, bias_g, alog_g, dsk_g, TOK, f"l{li}_ssd")
    ymem = _memattn_fwd(pr["qm"], kv, f"l{li}_mem")
    gt = _gate_norm_fwd(y, pr["zt"], p["ssd_norm_g"], G, f"l{li}_gate_tok")
    gm = _gate_fwd(ymem, pr["zm"], f"l{li}_gate_mem")
    wo = p["w_out"]
    out = _mm(gt, wo[:TOK], "nn", F32, f"l{li}_out_tok", add=x)
    out = _mm(gm, wo[TOK:], "nn", F32, f"l{li}_out_mem", add=out)
    saved = dict(x=x, h=h, pr=pr, act=act, raw_g=raw_g, par=(bias_g, alog_g, dsk_g), y=y, hprev=hprev, ymem=ymem,
                 gt=gt, gm=gm, segs=segs)
    return out, saved


def _ssd_layer_bwd(dout, doutb, kv, p, s, li):
    x = s["x"]
    T, D = x.shape
    dm = _dims(D)
    TOK, MW, H, CONV = dm["TOK"], dm["MW"], dm["H"], dm["CONV"]
    G = SSD_GROUPS
    HPG = H // G
    wo = p["w_out"]
    pr = s["pr"]
    dgt = _mm(doutb, wo[:TOK], "nt", F32, f"l{li}_dgt")
    dgm = _mm(doutb, wo[TOK:], "nt", F32, f"l{li}_dgm")
    dwo = jnp.concatenate([_mm(s["gt"], doutb, "tn", BF16, f"l{li}_dwo_tok"),
                           _mm(s["gm"], doutb, "tn", BF16, f"l{li}_dwo_mem")], axis=0)
    dy, dzt, dng = _gate_norm_bwd(s["y"], pr["zt"], p["ssd_norm_g"], dgt, G, f"l{li}_gate_tok_b")
    dymem, dzm = _gate_bwd(s["ymem"], pr["zm"], dgm, f"l{li}_gate_mem_b")
    dqm, dkv = _memattn_bwd(pr["qm"], kv, dymem, f"l{li}_mem_b")
    bias_g, alog_g, dsk_g = s["par"]
    dxs, dB, dC, draw_g, dalog, dbias, ddsk = _ssd_bwd(s["act"], s["raw_g"], bias_g, alog_g, dsk_g, s["hprev"], dy, TOK,
                                                      f"l{li}_ssd_b")
    dact = jnp.concatenate([dxs, dB, dC], axis=1)
    dpre, dconv_w, dconv_b = _conv_bwd_pre(pr["xbc"], p["conv_w"], p["conv_b"], dact, f"l{li}_conv_b1")
    dxbc = _conv_bwd_in(dpre, p["conv_w"], f"l{li}_conv_b2")
    draw = _pad_lanes(_groups_to_heads(draw_g, HPG)).astype(BF16)
    dsegs = dict(xbc=dxbc, dt=draw, qm=dqm, zt=dzt, zm=dzm)
    keys = ["xbc", "dt", "qm", "zt", "zm"]
    dh = _proj_chain([dsegs[k] for k in keys], [s["segs"][k] for k in keys], f"l{li}")
    dws = {k: _mm(s["h"], dsegs[k], "tn", BF16, f"l{li}_dwin_{k}") for k in keys}
    dws["dt"] = dws["dt"][:, :H]
    dwin = [dws[k] for k in keys]
    dh, dwin, dwo = lax.optimization_barrier((dh, dwin, dwo))
    dx, dxb, dnorm = _rms_bwd(x, p["norm_g"], dh, dout, f"l{li}_rms_b")
    unhead = lambda a: a[:, 0, :HPG].reshape(H)
    grads = dict(norm_g=dnorm, w_in=dwin, conv_w=dconv_w, conv_b=dconv_b, dt_bias=unhead(dbias), a_log=unhead(dalog),
                 d_skip=unhead(ddsk), ssd_norm_g=dng, w_out=dwo)
    return dx, dxb, dkv, grads


def _attn_layer_fwd(x, kv, p, li):
    T, D = x.shape
    dm = _dims(D)
    TOK, MW = dm["TOK"], dm["MW"]
    w = p["w_in"]
    ng = len(DILATED_GROUPS)
    segs = {}
    for g in range(ng):
        for n, nm in enumerate("qkv"):
            c0 = g * 3 * TOK + n * TOK
            segs[f"{nm}{g}"] = _take_cols(w, c0, c0 + TOK)
    c0 = ng * 3 * TOK
    segs["qm"] = _take_cols(w, c0, c0 + MW)
    segs["zt"] = _take_cols(w, c0 + MW, c0 + MW + TOK)
    segs["zm"] = _take_cols(w, c0 + MW + TOK, c0 + MW + TOK + MW)
    h = _rms_fwd(x, p["norm_g"], f"l{li}_rms")
    dense = {"qm"} | {f"{nm}{g}" for g, (_, d) in enumerate(DILATED_GROUPS) if d == 1 for nm in "qkv"}
    pr = {k: _mm(h, ws, "nn", BF16 if k in dense else F32, f"l{li}_in_{k}") for k, ws in segs.items()}
    os_, ls_ = [], []
    for g, (window, d) in enumerate(DILATED_GROUPS):
        o, l = _attn_fwd(pr[f"q{g}"], pr[f"k{g}"], pr[f"v{g}"], g, window, d, f"l{li}_attn{g}")
        os_.append(o)
        ls_.append(l)
    ytok, lse = _attn_combine(os_, ls_, f"l{li}_combine")
    ymem = _memattn_fwd(pr["qm"], kv, f"l{li}_mem")
    gt = _gate_fwd(ytok, pr["zt"], f"l{li}_gate_tok")
    gm = _gate_fwd(ymem, pr["zm"], f"l{li}_gate_mem")
    wo = p["w_out"]
    out = _mm(gt, wo[:TOK], "nn", F32, f"l{li}_out_tok", add=x)
    out = _mm(gm, wo[TOK:], "nn", F32, f"l{li}_out_mem", add=out)
    saved = dict(x=x, h=h, pr=pr, ytok=ytok, lse=lse, ymem=ymem, gt=gt, gm=gm, segs=segs)
    return out, saved


def _attn_layer_bwd(dout, doutb, kv, p, s, li):
    x = s["x"]
    T, D = x.shape
    dm = _dims(D)
    TOK, MW = dm["TOK"], dm["MW"]
    wo = p["w_out"]
    pr = s["pr"]
    dgt = _mm(doutb, wo[:TOK], "nt", F32, f"l{li}_dgt")
    dgm = _mm(doutb, wo[TOK:], "nt", F32, f"l{li}_dgm")
    dwo = jnp.concatenate([_mm(s["gt"], doutb, "tn", BF16, f"l{li}_dwo_tok"),
                           _mm(s["gm"], doutb, "tn", BF16, f"l{li}_dwo_mem")], axis=0)
    dytok, dzt = _gate_bwd(s["ytok"], pr["zt"], dgt, f"l{li}_gate_tok_b")
    dymem, dzm = _gate_bwd(s["ymem"], pr["zm"], dgm, f"l{li}_gate_mem_b")
    dqm, dkv = _memattn_bwd(pr["qm"], kv, dymem, f"l{li}_mem_b")
    dsegs = {}
    for g, (window, d) in enumerate(DILATED_GROUPS):
        dq, dk, dv = _attn_bwd(pr[f"q{g}"], pr[f"k{g}"], pr[f"v{g}"], s["ytok"], s["lse"], dytok, g, window, d,
                               f"l{li}_attn{g}_b")
        dsegs[f"q{g}"], dsegs[f"k{g}"], dsegs[f"v{g}"] = dq, dk, dv
    dsegs["qm"], dsegs["zt"], dsegs["zm"] = dqm, dzt, dzm
    keys = list(s["segs"].keys())
    dh = _proj_chain([dsegs[k] for k in keys], [s["segs"][k] for k in keys], f"l{li}")
    dwin = [_mm(s["h"], dsegs[k], "tn", BF16, f"l{li}_dwin_{k}") for k in keys]
    dh, dwin, dwo = lax.optimization_barrier((dh, dwin, dwo))
    dx, dxb, dnorm = _rms_bwd(x, p["norm_g"], dh, dout, f"l{li}_rms_b")
    return dx, dxb, dkv, dict(norm_g=dnorm, w_in=dwin, w_out=dwo)


def _local_step(x, mem, mem_n, tgt, mem_norm_g, final_norm_g, n_layers, layer_params, on_layer_grads):
    layers, kvs, saved = [], [], []
    for li in range(n_layers):
        p, x = layer_params(li, x)
        fwd = _ssd_layer_fwd if li % 2 == 0 else _attn_layer_fwd
        kv = _mm(mem_n, p["w_mem_kv"], "nn", F32, f"l{li}_kv")
        x, s = fwd(x, kv, p, li)
        layers.append(p)
        kvs.append(kv)
        saved.append(s)
    loss, dx, dxb, dfinal = _loss_head(x, final_norm_g, tgt, "loss_head")
    dmem_n = None
    for li in reversed(range(n_layers)):
        p = layers[li]
        bwd = _ssd_layer_bwd if li % 2 == 0 else _attn_layer_bwd
        dx, dxb, dkv, g = bwd(dx, dxb, kvs[li], p, saved[li], li)
        g["w_mem_kv"] = _mm(mem_n, dkv, "tn", BF16, f"l{li}_dwkv")
        dmem_n = _mm(dkv, p["w_mem_kv"], "nt", F32, f"l{li}_dmem", add=dmem_n)
        dx, dxb = on_layer_grads(li, g, dx, dxb)
    _, _, dmem_g = _rms_bwd(mem, mem_norm_g, dmem_n, None, "mem_rms_b")
    return loss, dx, dmem_g, dfinal


_SSD_SMALL = ["norm_g", "conv_w", "conv_b", "dt_bias", "a_log", "d_skip", "ssd_norm_g"]
_ATTN_SMALL = ["norm_g"]
_SSD_ORDER = ["norm_g", "w_in", "conv_w", "conv_b", "dt_bias", "a_log", "d_skip", "ssd_norm_g", "w_mem_kv", "w_out"]
_ATTN_ORDER = ["norm_g", "w_in", "w_mem_kv", "w_out"]


def _pack(arrs):
    flat = jnp.concatenate([a.reshape(-1).astype(F32) for a in arrs])
    n = flat.shape[0]
    pad = (-n) % (8 * LANES)
    return jnp.pad(flat, (0, pad)).reshape(-1, LANES)


def _unpack(mat, shapes):
    flat = mat.reshape(-1)
    out, o = [], 0
    for shp in shapes:
        n = math.prod(shp)
        out.append(flat[o:o + n].reshape(shp))
        o += n
    return out


def kernel(x, mem, mem_norm_g, final_norm_g, norm_g_0, w_in_0, conv_w_0, conv_b_0, dt_bias_0, a_log_0, d_skip_0, ssd_norm_g_0, w_mem_kv_0, w_out_0, norm_g_1, w_in_1, w_mem_kv_1, w_out_1, norm_g_2, w_in_2, conv_w_2, conv_b_2, dt_bias_2, a_log_2, d_skip_2, ssd_norm_g_2, w_mem_kv_2, w_out_2, norm_g_3, w_in_3, w_mem_kv_3, w_out_3, loss_target, m_mem_norm_g, m_final_norm_g, m_norm_g_0, m_w_in_0, m_conv_w_0, m_conv_b_0, m_dt_bias_0, m_a_log_0, m_d_skip_0, m_ssd_norm_g_0, m_w_mem_kv_0, m_w_out_0, m_norm_g_1, m_w_in_1, m_w_mem_kv_1, m_w_out_1, m_norm_g_2, m_w_in_2, m_conv_w_2, m_conv_b_2, m_dt_bias_2, m_a_log_2, m_d_skip_2, m_ssd_norm_g_2, m_w_mem_kv_2, m_w_out_2, m_norm_g_3, m_w_in_3, m_w_mem_kv_3, m_w_out_3, v_mem_norm_g, v_final_norm_g, v_norm_g_0, v_w_in_0, v_conv_w_0, v_conv_b_0, v_dt_bias_0, v_a_log_0, v_d_skip_0, v_ssd_norm_g_0, v_w_mem_kv_0, v_w_out_0, v_norm_g_1, v_w_in_1, v_w_mem_kv_1, v_w_out_1, v_norm_g_2, v_w_in_2, v_conv_w_2, v_conv_b_2, v_dt_bias_2, v_a_log_2, v_d_skip_2, v_ssd_norm_g_2, v_w_mem_kv_2, v_w_out_2, v_norm_g_3, v_w_in_3, v_w_mem_kv_3, v_w_out_3):
    a = dict(locals())
    names = ["mem_norm_g", "final_norm_g"]
    for li in range(DEPTH):
        names += [f"{k}_{li}" for k in (_SSD_ORDER if li % 2 == 0 else _ATTN_ORDER)]
    W = {n: a[n] for n in names}
    Mo = {n: a["m_" + n] for n in names}
    Vo = {n: a["v_" + n] for n in names}
    NX = 4
    chip = 2 * lax.axis_index("x") + lax.axis_index("y")

    gathered = []
    mem_n = None
    for li in range(DEPTH):
        mats = [W[f"w_in_{li}"], W[f"w_mem_kv_{li}"], W[f"w_out_{li}"]]
        if li == 1:
            mem_n = _rms_fwd(mem[0], W["mem_norm_g"], "mem_rms")
            later = [[W[f"{k}_{lj}"] for k in ("w_in", "w_mem_kv", "w_out")] for lj in range(1, DEPTH)]
            later, mem_n = lax.optimization_barrier((later, mem_n))
        if li >= 1:
            mats = later[li - 1]
        srcs = [m.astype(BF16) for m in mats]
        if li % 2 == 0:
            srcs.append(W[f"conv_w_{li}"])
        gathered.append(_comm_async(f"gather_w{li}", li, "xy", srcs, ["gather"] * len(srcs)))

    def layer_params(li, xin):
        got = gathered[li]
        if li > 0:
            got, xin = lax.optimization_barrier((got, xin))
        rows = lambda g: g.reshape((-1,) + g.shape[2:])
        parts = lambda g: [g[k] for k in range(NX)]
        p = dict(w_in=parts(got[0]), w_mem_kv=rows(got[1]), w_out=rows(got[2]), norm_g=W[f"norm_g_{li}"])
        if li % 2 == 0:
            p.update(conv_w=jnp.concatenate(parts(got[3]), axis=1), conv_b=W[f"conv_b_{li}"],
                     dt_bias=W[f"dt_bias_{li}"], a_log=W[f"a_log_{li}"], d_skip=W[f"d_skip_{li}"],
                     ssd_norm_g=W[f"ssd_norm_g_{li}"])
        return p, xin

    G, Dl, Mn, Vn = {}, {}, {}, {}
    grads = [None] * DEPTH
    in_flight = []

    def finish_exchange(li, got):
        parts = [_sum_lead(t, f"l{li}_gsum{n}") for n, t in enumerate(got)]
        theirs = _comm_async(f"swap_g{li}", 2 * DEPTH + li, "c", parts, ["swap"] * 3)
        for nm, mine, other in zip(["w_in", "w_mem_kv", "w_out"], parts, theirs):
            key = f"{nm}_{li}"
            G[key], Dl[key], Mn[key], Vn[key] = _adamw(W[key], Mo[key], Vo[key], [mine, other], f"adamw_{key}")

    def on_layer_grads(li, g, dx, dxb):
        grads[li] = g
        dwin = g["w_in"]
        cw = sum(t.shape[1] for t in dwin) // NX
        chunks = [jnp.stack([_take_cols(dwin, k * cw, (k + 1) * cw) for k in range(NX)]),
                  g["w_mem_kv"].reshape((NX, -1) + g["w_mem_kv"].shape[1:]),
                  g["w_out"].reshape((NX, -1) + g["w_out"].shape[1:])]
        prev = in_flight.pop() if in_flight else None
        pgot = prev[1] if prev else []
        chunks, pgot, dx, dxb = lax.optimization_barrier((chunks, pgot, dx, dxb))
        in_flight.append((li, _comm_async(f"xchg_g{li}", DEPTH + li, "xy", chunks, ["a2a"] * 3)))
        if prev:
            finish_exchange(prev[0], pgot)
        return dx, dxb

    loss_l, dx, dmem_g, dfinal = _local_step(x[0], mem[0], mem_n, loss_target[0], W["mem_norm_g"], W["final_norm_g"],
                                             DEPTH, layer_params, on_layer_grads)
    last_li, last_got = in_flight.pop()
    updates = (G, Dl, Mn, Vn)
    last_got, tied = lax.optimization_barrier((last_got, [dict(u) for u in updates]))
    for u, t in zip(updates, tied):
        u.update(t)
    finish_exchange(last_li, last_got)
    loss = lax.psum(loss_l, ("x", "y", "c"))

    small_names = ["mem_norm_g", "final_norm_g"]
    small_grads = [dmem_g, dfinal]
    for li in range(DEPTH):
        for k in (_SSD_SMALL if li % 2 == 0 else _ATTN_SMALL):
            small_names.append(f"{k}_{li}")
            small_grads.append(grads[li][k])
    shapes = [tuple(t.shape) for t in small_grads]
    allg = _comm("gather_small", "xyc", [_pack(small_grads)], ["gather"])[0]
    gsum = _unpack(_sum_lead(allg, "small_gsum"), shapes)
    small_w, small_m, small_v, small_g = [], [], [], []
    for nme, gv in zip(small_names, gsum):
        if nme.startswith("conv_w"):
            cw = W[nme].shape[1]
            gv = lax.dynamic_slice_in_dim(gv, chip * cw, cw, axis=1)
        small_g.append(gv)
        small_w.append(W[nme])
        small_m.append(Mo[nme])
        small_v.append(Vo[nme])
    sshapes = [tuple(t.shape) for t in small_g]
    res = _adamw(_pack(small_w), _pack(small_m), _pack(small_v), [_pack(small_g)], "adamw_small")
    for dst, mat in zip((G, Dl, Mn, Vn), res):
        for nme, t in zip(small_names, _unpack(mat, sshapes)):
            dst[nme] = t

    return (loss, dx[None], *[G[n] for n in names], *[Dl[n] for n in names], *[Mn[n] for n in names],
            *[Vn[n] for n in names])
```
